```python
import math
import jax
import jax.numpy as jnp
from jax import lax
import numpy as np

D_MODEL = 1024
BATCH = 8
SEQ = 2048
DEPTH = 4

CTX_LEN = 256
GRID_W = 64
N_MIXERS = 3
N_LAYERS_NA = (DEPTH + 2) // 3
N_LAYERS_DA = (DEPTH + 1) // 3
N_LAYERS_SW = DEPTH // 3
HEAD_DIM = 64
ROPE_AXIS_PAIRS = HEAD_DIM // 4
ROPE_BASE = 10000.0
NA_HEADS = D_MODEL // HEAD_DIM
NA_WIN_R = 8
NA_WIN_C = 16
DA_HEADS = D_MODEL // (2 * HEAD_DIM)
SW_Q_HEADS = D_MODEL // HEAD_DIM
SW_KV_HEADS = 4
SW_WINDOW = 128
Q_BLOCK = 128
N_GROUPS = 4
EXPERTS_PER_GROUP = 8
N_EXPERTS = N_GROUPS * EXPERTS_PER_GROUP
TOP_K = 2
D_EXPERT = 512
NORM_EPS = 1e-6
NEG_INF = -1e30

kernel_name = "hybrid_dit_natten_diff_swa_hmoe"


def _rmsnorm(x, g):
    xf = x.astype(jnp.float32)
    y = xf * lax.rsqrt(jnp.mean(xf * xf, axis=-1, keepdims=True) + NORM_EPS)
    return (y * g.astype(jnp.float32)).astype(x.dtype)


def _adaln(cond, w, b):
    m = (cond @ w + b)[:, None, :]
    return jnp.split(m, 6, axis=-1)


def _modulate(h, shift, scale):
    return h * (1.0 + scale) + shift


def _rope_2d_tables(n_tokens):
    t = jnp.arange(n_tokens, dtype=jnp.int32)
    row = (t // GRID_W).astype(jnp.float32)
    col = (t % GRID_W).astype(jnp.float32)
    inv = ROPE_BASE ** (-jnp.arange(ROPE_AXIS_PAIRS, dtype=jnp.float32) / ROPE_AXIS_PAIRS)
    ang_r = row[:, None] * inv[None, :]
    ang_c = col[:, None] * inv[None, :]
    ang = jnp.concatenate([ang_r, ang_r, ang_c, ang_c], axis=-1)
    return jnp.cos(ang)[:, None, :], jnp.sin(ang)[:, None, :]


def _apply_rope_2d(x, cos, sin):
    xr = x.reshape(x.shape[:-1] + (2, 2, ROPE_AXIS_PAIRS))
    rot = jnp.concatenate([-xr[..., 1:, :], xr[..., :1, :]], axis=-2).reshape(x.shape)
    return x * cos.astype(x.dtype) + rot * sin.astype(x.dtype)


def neighbourhood_attention(hx, hc, w_qkv, rpb, w_o, with_ctx_out):
    B, S, _ = hx.shape
    L = hc.shape[1]
    H, dh = NA_HEADS, HEAD_DIM
    rows = S // GRID_W
    win_r = min(NA_WIN_R, rows)
    scale = dh ** -0.5
    qkv_x = (hx @ w_qkv).reshape(B, S, 3, H, dh)
    qkv_c = (hc @ w_qkv).reshape(B, L, 3, H, dh)
    qc, kc, vc = qkv_c[:, :, 0], qkv_c[:, :, 1], qkv_c[:, :, 2]
    qg = qkv_x[:, :, 0].reshape(B, rows, GRID_W, H, dh)
    kg = qkv_x[:, :, 1].reshape(B, rows, GRID_W, H, dh)
    vg = qkv_x[:, :, 2].reshape(B, rows, GRID_W, H, dh)
    qcol = jnp.arange(GRID_W)
    kcol = jnp.arange(GRID_W)
    c0 = jnp.clip(qcol - NA_WIN_C // 2, 0, GRID_W - NA_WIN_C)
    col_ok = (kcol[None, :] >= c0[:, None]) & (kcol[None, :] < c0[:, None] + NA_WIN_C)
    col_idx = jnp.clip(kcol[None, :] - qcol[:, None] + NA_WIN_C - 1, 0, 2 * NA_WIN_C - 2)
    rpb_f = rpb.astype(jnp.float32)

    def row_block(r):
        r0 = jnp.clip(r - win_r // 2, 0, rows - win_r)
        q_r = lax.dynamic_index_in_dim(qg, r, axis=1, keepdims=False)
        k_r = lax.dynamic_slice_in_dim(kg, r0, win_r, axis=1)
        v_r = lax.dynamic_slice_in_dim(vg, r0, win_r, axis=1)
        s_nb = jnp.einsum('bqhd,bykhd->bhqyk', q_r, k_r).astype(jnp.float32) * scale
        row_idx = r0 + jnp.arange(win_r) - r + NA_WIN_R - 1
        bias = rpb_f[:, row_idx][:, :, col_idx].transpose(0, 2, 1, 3)
        s_nb = jnp.where(col_ok[:, None, :], s_nb + bias[None], NEG_INF)
        s_cx = jnp.einsum('bqhd,bkhd->bhqk', q_r, kc).astype(jnp.float32) * scale
        s_all = jnp.concatenate([s_cx, s_nb.reshape(B, H, GRID_W, win_r * GRID_W)], axis=-1)
        p = jax.nn.softmax(s_all, axis=-1).astype(hx.dtype)
        o = jnp.einsum('bhqk,bkhd->bqhd', p[..., :L], vc)
        o = o + jnp.einsum('bhqk,bkhd->bqhd', p[..., L:], v_r.reshape(B, win_r * GRID_W, H, dh))
        return o

    o = lax.map(row_block, jnp.arange(rows))
    ox = o.transpose(1, 0, 2, 3, 4).reshape(B, S, H * dh) @ w_o
    oc = None
    if with_ctx_out:
        s = jnp.einsum('bqhd,bkhd->bhqk', qc, kc).astype(jnp.float32) * scale
        p = jax.nn.softmax(s, axis=-1).astype(hc.dtype)
        oc = jnp.einsum('bhqk,bkhd->bqhd', p, vc).reshape(B, L, H * dh) @ w_o
    return ox, oc


def differential_attention(hx, hc, w_qkv, lam_params, subln_g, w_o, layer_idx, rope_cos, rope_sin, with_ctx_out):
    B, S, _ = hx.shape
    L = hc.shape[1]
    H, dh = DA_HEADS, HEAD_DIM
    nq = 2 * H * dh
    scale = dh ** -0.5

    def proj(h):
        t = h @ w_qkv
        q = t[..., :nq].reshape(B, -1, 2 * H, dh)
        k = t[..., nq:2 * nq].reshape(B, -1, 2 * H, dh)
        v = t[..., 2 * nq:].reshape(B, -1, H, 2 * dh)
        return q, k, v

    qx, kx, vx = proj(hx)
    qc, kc, vc = proj(hc)
    qx = _apply_rope_2d(qx, rope_cos, rope_sin)
    kx = _apply_rope_2d(kx, rope_cos, rope_sin)
    lam_init = 0.8 - 0.6 * math.exp(-0.3 * layer_idx)
    lp = lam_params.astype(jnp.float32)
    lam = jnp.exp(jnp.sum(lp[0] * lp[1])) - jnp.exp(jnp.sum(lp[2] * lp[3])) + lam_init

    def attend(q, k, v):
        n = q.shape[1]
        q = q.reshape(B, n, H, 2, dh)
        s = jnp.einsum('bqhjd,bkhjd->bhjqk', q, k).astype(jnp.float32) * scale
        p = jax.nn.softmax(s, axis=-1)
        a = (p[:, :, 0] - lam * p[:, :, 1]).astype(v.dtype)
        o = jnp.einsum('bhqk,bkhe->bqhe', a, v)
        o = _rmsnorm(o, subln_g) * (1.0 - lam_init)
        return o.reshape(B, n, H * 2 * dh)

    kc5 = kc.reshape(B, L, H, 2, dh)
    k_all = jnp.concatenate([kc5, kx.reshape(B, S, H, 2, dh)], axis=1)
    v_all = jnp.concatenate([vc, vx], axis=1)
    nblk = S // Q_BLOCK
    qb = qx.reshape(B, nblk, Q_BLOCK, 2 * H, dh).transpose(1, 0, 2, 3, 4)
    o = lax.map(lambda qq: attend(qq, k_all, v_all), qb)
    ox = o.transpose(1, 0, 2, 3).reshape(B, S, -1) @ w_o
    oc = None
    if with_ctx_out:
        oc = attend(qc, kc5, vc) @ w_o
    return ox, oc


def sliding_window_gqa(hx, hc, w_qkv, sinks, w_o, rope_cos, rope_sin, with_ctx_out):
    B, S, _ = hx.shape
    L = hc.shape[1]
    Hq, Hk, dh = SW_Q_HEADS, SW_KV_HEADS, HEAD_DIM
    G = Hq // Hk
    nq, nk = Hq * dh, Hk * dh
    scale = dh ** -0.5

    def proj(h):
        t = h @ w_qkv
        q = t[..., :nq].reshape(B, -1, Hq, dh)
        k = t[..., nq:nq + nk].reshape(B, -1, Hk, dh)
        v = t[..., nq + nk:].reshape(B, -1, Hk, dh)
        return q, k, v

    qx, kx, vx = proj(hx)
    qc, kc, vc = proj(hc)
    qx = _apply_rope_2d(qx, rope_cos, rope_sin).reshape(B, S, Hk, G, dh)
    kx = _apply_rope_2d(kx, rope_cos, rope_sin)
    qc = qc.reshape(B, L, Hk, G, dh)
    sink_logit = sinks.astype(jnp.float32).reshape(1, Hk, G, 1, 1)
    band = Q_BLOCK + 2 * SW_WINDOW
    kp = jnp.pad(kx, ((0, 0), (SW_WINDOW, SW_WINDOW), (0, 0), (0, 0)))
    vp = jnp.pad(vx, ((0, 0), (SW_WINDOW, SW_WINDOW), (0, 0), (0, 0)))
    q_local = jnp.arange(Q_BLOCK)
    k_local = jnp.arange(band) - SW_WINDOW
    in_band = jnp.abs(k_local[None, :] - q_local[:, None]) <= SW_WINDOW

    def softmax_with_sink(s_c, s_l):
        sink = jnp.broadcast_to(sink_logit, s_c.shape[:-1] + (1,))
        parts = [sink, s_c] if s_l is None else [sink, s_c, s_l]
        return jax.nn.softmax(jnp.concatenate(parts, axis=-1), axis=-1)[..., 1:]

    def block(n):
        start = n * Q_BLOCK
        q_n = lax.dynamic_slice_in_dim(qx, start, Q_BLOCK, axis=1)
        k_n = lax.dynamic_slice_in_dim(kp, start, band, axis=1)
        v_n = lax.dynamic_slice_in_dim(vp, start, band, axis=1)
        kpos = start + k_local
        valid = in_band & ((kpos >= 0) & (kpos < S))[None, :]
        s_l = jnp.einsum('bqhgd,bkhd->bhgqk', q_n, k_n).astype(jnp.float32) * scale
        s_l = jnp.where(valid, s_l, NEG_INF)
        s_c = jnp.einsum('bqhgd,bkhd->bhgqk', q_n, kc).astype(jnp.float32) * scale
        p = softmax_with_sink(s_c, s_l).astype(hx.dtype)
        o = jnp.einsum('bhgqk,bkhd->bqhgd', p[..., :L], vc)
        o = o + jnp.einsum('bhgqk,bkhd->bqhgd', p[..., L:], v_n)
        return o.reshape(B, Q_BLOCK, Hq * dh)

    o = lax.map(block, jnp.arange(S // Q_BLOCK))
    ox = o.transpose(1, 0, 2, 3).reshape(B, S, Hq * dh) @ w_o
    oc = None
    if with_ctx_out:
        s_c = jnp.einsum('bqhgd,bkhd->bhgqk', qc, kc).astype(jnp.float32) * scale
        p = softmax_with_sink(s_c, None).astype(hc.dtype)
        oc = jnp.einsum('bhgqk,bkhd->bqhgd', p, vc).reshape(B, L, Hq * dh) @ w_o
    return ox, oc


def hierarchical_moe(h, w_grp, b_grp, w_exp, b_exp, w_gate, w_up, w_down):
    T = h.shape[0]
    grp_prob = jax.nn.softmax((h @ w_grp).astype(jnp.float32) + b_grp.astype(jnp.float32), axis=-1)
    g_p, g_idx = lax.top_k(grp_prob, 1)
    exp_logits = ((h @ w_exp).astype(jnp.float32) + b_exp.astype(jnp.float32)).reshape(T, N_GROUPS, EXPERTS_PER_GROUP)
    in_grp = jnp.take_along_axis(exp_logits, g_idx[:, :, None], axis=1)[:, 0]
    top_v, top_i = lax.top_k(in_grp, TOP_K)
    top_w = jax.nn.softmax(top_v, axis=-1) * g_p
    expert_id = g_idx * EXPERTS_PER_GROUP + top_i
    combine = jnp.einsum('tk,tkn->tn', top_w, jax.nn.one_hot(expert_id, N_EXPERTS, dtype=jnp.float32)).astype(h.dtype)
    y = jnp.zeros_like(h)
    for g in range(N_GROUPS):
        sl = slice(g * EXPERTS_PER_GROUP, (g + 1) * EXPERTS_PER_GROUP)
        a = jnp.einsum('td,edf->tef', h, w_gate[sl])
        u = jnp.einsum('td,edf->tef', h, w_up[sl])
        y = y + jnp.einsum('tef,efd->td', jax.nn.silu(a) * u * combine[:, sl, None], w_down[sl])
    return y


def setup_inputs(seed: int = 0) -> dict:
    key = jax.random.key(seed)
    ks = jax.random.split(key, 32)
    D = D_MODEL
    nrm = jax.random.normal
    f32 = jnp.float32
    sw_cols = SW_Q_HEADS * HEAD_DIM + 2 * SW_KV_HEADS * HEAD_DIM
    return {
        "x": nrm(ks[0], (BATCH, SEQ, D), f32),
        "c": nrm(ks[1], (BATCH, D), f32),
        "ctx": nrm(ks[2], (BATCH, CTX_LEN, D), f32),
        "c_ctx": nrm(ks[3], (D,), f32),
        "ada_w": nrm(ks[4], (DEPTH, D, 6 * D), f32) * (0.5 * D ** -0.5),
        "ada_b": nrm(ks[5], (DEPTH, 6 * D), f32) * 0.02,
        "norm_mix_g": 1.0 + 0.02 * nrm(ks[6], (DEPTH, D), f32),
        "norm_ffn_g": 1.0 + 0.02 * nrm(ks[7], (DEPTH, D), f32),
        "final_norm_g": 1.0 + 0.02 * nrm(ks[8], (D,), f32),
        "na_w_qkv": nrm(ks[9], (N_LAYERS_NA, D, 3 * NA_HEADS * HEAD_DIM), f32) * D ** -0.5,
        "na_rpb": nrm(ks[10], (N_LAYERS_NA, NA_HEADS, 2 * NA_WIN_R - 1, 2 * NA_WIN_C - 1), f32) * 0.1,
        "na_w_o": nrm(ks[11], (N_LAYERS_NA, NA_HEADS * HEAD_DIM, D), f32) * (NA_HEADS * HEAD_DIM) ** -0.5,
        "da_w_qkv": nrm(ks[12], (N_LAYERS_DA, D, 6 * DA_HEADS * HEAD_DIM), f32) * D ** -0.5,
        "da_lambda": nrm(ks[13], (N_LAYERS_DA, 4, HEAD_DIM), f32) * 0.1,
        "da_subln_g": 1.0 + 0.02 * nrm(ks[14], (N_LAYERS_DA, 2 * HEAD_DIM), f32),
        "da_w_o": nrm(ks[15], (N_LAYERS_DA, 2 * DA_HEADS * HEAD_DIM, D), f32) * (2 * DA_HEADS * HEAD_DIM) ** -0.5,
        "sw_w_qkv": nrm(ks[16], (N_LAYERS_SW, D, sw_cols), f32) * D ** -0.5,
        "sw_sinks": nrm(ks[17], (N_LAYERS_SW, SW_Q_HEADS), f32),
        "sw_w_o": nrm(ks[18], (N_LAYERS_SW, SW_Q_HEADS * HEAD_DIM, D), f32) * (SW_Q_HEADS * HEAD_DIM) ** -0.5,
        "moe_w_grp": nrm(ks[19], (DEPTH, D, N_GROUPS), f32) * D ** -0.5,
        "moe_b_grp": nrm(ks[20], (DEPTH, N_GROUPS), f32) * 0.01,
        "moe_w_exp": nrm(ks[21], (DEPTH, D, N_EXPERTS), f32) * D ** -0.5,
        "moe_b_exp": nrm(ks[22], (DEPTH, N_EXPERTS), f32) * 0.01,
        "moe_w_gate": nrm(ks[23], (DEPTH, N_EXPERTS, D, D_EXPERT), f32) * D ** -0.5,
        "moe_w_up": nrm(ks[24], (DEPTH, N_EXPERTS, D, D_EXPERT), f32) * D ** -0.5,
        "moe_w_down": nrm(ks[25], (DEPTH, N_EXPERTS, D_EXPERT, D), f32) * D_EXPERT ** -0.5,
    }


def reference(x, c, ctx, c_ctx, ada_w, ada_b, norm_mix_g, norm_ffn_g, final_norm_g,
              na_w_qkv, na_rpb, na_w_o, da_w_qkv, da_lambda, da_subln_g, da_w_o,
              sw_w_qkv, sw_sinks, sw_w_o, moe_w_grp, moe_b_grp, moe_w_exp, moe_b_exp,
              moe_w_gate, moe_w_up, moe_w_down):
    B, S, D = x.shape
    rope_cos, rope_sin = _rope_2d_tables(S)
    cond_lat = jax.nn.silu(c)
    cond_ctx = jax.nn.silu(c_ctx)[None, :]
    xc = ctx
    for i in range(DEPTH):
        need_ctx = i < DEPTH - 1
        sh1, sc1, g1, sh2, sc2, g2 = _adaln(cond_lat, ada_w[i], ada_b[i])
        csh1, csc1, cg1, csh2, csc2, cg2 = _adaln(cond_ctx, ada_w[i], ada_b[i])
        hx = _modulate(_rmsnorm(x, norm_mix_g[i]), sh1, sc1)
        hc = _modulate(_rmsnorm(xc, norm_mix_g[i]), csh1, csc1)
        kind, j = i % N_MIXERS, i // N_MIXERS
        if kind == 0:
            ox, oc = neighbourhood_attention(hx, hc, na_w_qkv[j], na_rpb[j], na_w_o[j], need_ctx)
        elif kind == 1:
            ox, oc = differential_attention(hx, hc, da_w_qkv[j], da_lambda[j], da_subln_g[j], da_w_o[j],
                                            i, rope_cos, rope_sin, need_ctx)
        else:
            ox, oc = sliding_window_gqa(hx, hc, sw_w_qkv[j], sw_sinks[j], sw_w_o[j],
                                        rope_cos, rope_sin, need_ctx)
        x = x + g1 * ox
        hx = _modulate(_rmsnorm(x, norm_ffn_g[i]), sh2, sc2)
        if need_ctx:
            xc = xc + cg1 * oc
            hc = _modulate(_rmsnorm(xc, norm_ffn_g[i]), csh2, csc2)
            n_c = hc.shape[0] * hc.shape[1]
            y = hierarchical_moe(jnp.concatenate([hc.reshape(-1, D), hx.reshape(-1, D)], axis=0),
                                 moe_w_grp[i], moe_b_grp[i], moe_w_exp[i], moe_b_exp[i],
                                 moe_w_gate[i], moe_w_up[i], moe_w_down[i])
            xc = xc + cg2 * y[:n_c].reshape(hc.shape)
            x = x + g2 * y[n_c:].reshape(hx.shape)
        else:
            y = hierarchical_moe(hx.reshape(-1, D), moe_w_grp[i], moe_b_grp[i], moe_w_exp[i], moe_b_exp[i],
                                 moe_w_gate[i], moe_w_up[i], moe_w_down[i])
            x = x + g2 * y.reshape(hx.shape)
    return _rmsnorm(x, final_norm_g)
```

```python
import functools
import math

import jax
import jax.numpy as jnp
from jax import lax
from jax.experimental import pallas as pl
from jax.experimental.pallas import tpu as pltpu

F32 = jnp.float32
BF16 = jnp.bfloat16

D_MODEL = 1024
HEAD_DIM = 64
GRID_W = 64
ROPE_AXIS_PAIRS = HEAD_DIM // 4
ROPE_BASE = 10000.0
NA_HEADS = D_MODEL // HEAD_DIM
NA_WIN_R = 8
NA_WIN_C = 16
DA_HEADS = D_MODEL // (2 * HEAD_DIM)
SW_Q_HEADS = D_MODEL // HEAD_DIM
SW_KV_HEADS = 4
SW_WINDOW = 128
Q_BLOCK = 128
N_GROUPS = 4
EXPERTS_PER_GROUP = 8
N_EXPERTS = N_GROUPS * EXPERTS_PER_GROUP
D_EXPERT = 512
NORM_EPS = 1e-6
NEG_INF = -1e30
N_MIXERS = 3

LANES = 128
SUBLANES = 8
ROW_TILE = 256
MOD_ROWS = 16
VMEM_LIMIT = 56 * 1024 * 1024

_NT = (((1,), (1,)), ((), ()))


def _dot(a, b, **kw):
    return jnp.dot(a, b, preferred_element_type=F32, **kw)


def _dot_nt(a, b):
    return lax.dot_general(a, b, _NT, preferred_element_type=F32)


def _params(sem, vmem=VMEM_LIMIT):
    return pltpu.CompilerParams(dimension_semantics=sem, vmem_limit_bytes=vmem)


def _mod_row(b, tile, ctx_tiles):
    return jnp.where(tile < ctx_tiles, MOD_ROWS // 2, b)


def _rms(x):
    return x * lax.rsqrt(jnp.mean(x * x, axis=-1, keepdims=True) + NORM_EPS)


def _adaln_kernel(c_ref, w_ref, b_ref, o_ref):
    s = jax.nn.silu(c_ref[...])
    o_ref[...] = _dot(s, w_ref[...], precision=lax.Precision.HIGHEST) + b_ref[...]


def _adaln(cond, ada_w, ada_b):
    depth, d, n6 = ada_w.shape
    tn = 1536
    return pl.pallas_call(
        _adaln_kernel,
        grid=(depth, n6 // tn),
        in_specs=[
            pl.BlockSpec((MOD_ROWS, d), lambda l, n: (0, 0)),
            pl.BlockSpec((None, d, tn), lambda l, n: (l, 0, n)),
            pl.BlockSpec((None, 1, tn), lambda l, n: (l, 0, n)),
        ],
        out_specs=pl.BlockSpec((None, MOD_ROWS, tn), lambda l, n: (l, 0, n)),
        out_shape=jax.ShapeDtypeStruct((depth, MOD_ROWS, n6), F32),
        compiler_params=_params(("arbitrary", "arbitrary")),
        name="adaln",
    )(cond, ada_w, ada_b.reshape(depth, 1, n6))


def _qkv_kernel(x_ref, g_ref, mod_ref, w_ref, cos_ref, sin_ref, o_ref, *, rope_cols, ctx_tiles):
    d = x_ref.shape[-1]
    row = _mod_row(pl.program_id(0), pl.program_id(1), ctx_tiles)
    shift = mod_ref[pl.ds(row, 1), 0:d]
    scale = mod_ref[pl.ds(row, 1), d:2 * d]
    h = (_rms(x_ref[...]) * g_ref[...]) * (1.0 + scale) + shift
    acc = _dot(h.astype(BF16), w_ref[...])
    n = acc.shape[-1]
    if rope_cols:
        tm = acc.shape[0]
        first_half = (lax.broadcasted_iota(jnp.int32, (tm, LANES), 1) % (2 * ROPE_AXIS_PAIRS)) < ROPE_AXIS_PAIRS
        cos = cos_ref[...]
        sin = sin_ref[...]
        for c in range(rope_cols // LANES):
            t = acc[:, c * LANES:(c + 1) * LANES]
            up = pltpu.roll(t, LANES - ROPE_AXIS_PAIRS, 1)
            dn = pltpu.roll(t, ROPE_AXIS_PAIRS, 1)
            o_ref[:, c * LANES:(c + 1) * LANES] = (t * cos + jnp.where(first_half, up, dn) * sin).astype(o_ref.dtype)
    if rope_cols < n:
        o_ref[:, rope_cols:] = acc[:, rope_cols:].astype(o_ref.dtype)


def _qkv(x, g, mod, w, cos, sin, rope_cols, ctx_len):
    b, r, d = x.shape
    n = w.shape[1]
    kern = functools.partial(_qkv_kernel, rope_cols=rope_cols, ctx_tiles=ctx_len // ROW_TILE)
    return pl.pallas_call(
        kern,
        grid=(b, r // ROW_TILE),
        in_specs=[
            pl.BlockSpec((None, ROW_TILE, d), lambda bi, i: (bi, i, 0)),
            pl.BlockSpec((1, d), lambda bi, i: (0, 0)),
            pl.BlockSpec((MOD_ROWS, 6 * d), lambda bi, i: (0, 0)),
            pl.BlockSpec((d, n), lambda bi, i: (0, 0)),
            pl.BlockSpec((ROW_TILE, LANES), lambda bi, i: (i, 0)),
            pl.BlockSpec((ROW_TILE, LANES), lambda bi, i: (i, 0)),
        ],
        out_specs=pl.BlockSpec((None, ROW_TILE, n), lambda bi, i: (bi, i, 0)),
        out_shape=jax.ShapeDtypeStruct((b, r, n), BF16),
        compiler_params=_params(("arbitrary", "arbitrary")),
        name="qkv",
    )(x, g, mod, w, cos, sin)


def _rope_tables(seq, ctx_len):
    t = jnp.arange(seq, dtype=jnp.int32)
    row = (t // GRID_W).astype(F32)
    col = (t % GRID_W).astype(F32)
    inv = ROPE_BASE ** (-jnp.arange(ROPE_AXIS_PAIRS, dtype=F32) / ROPE_AXIS_PAIRS)
    ang_r = row[:, None] * inv[None, :]
    ang_c = col[:, None] * inv[None, :]
    ang = jnp.concatenate([ang_r, ang_r, ang_c, ang_c], axis=-1)
    sign = jnp.tile(jnp.concatenate([-jnp.ones((ROPE_AXIS_PAIRS,), F32), jnp.ones((ROPE_AXIS_PAIRS,), F32)]), 2)
    cos = jnp.concatenate([jnp.ones((ctx_len, HEAD_DIM), F32), jnp.cos(ang)], axis=0)
    sin = jnp.concatenate([jnp.zeros((ctx_len, HEAD_DIM), F32), jnp.sin(ang) * sign[None, :]], axis=0)
    return jnp.tile(cos, (1, LANES // HEAD_DIM)), jnp.tile(sin, (1, LANES // HEAD_DIM))


def _head_masks(rows):
    lane = lax.broadcasted_iota(jnp.int32, (rows, LANES), 1)
    lo = lane < HEAD_DIM
    return lo, jnp.logical_not(lo)


def _na_kernel(q_ref, k_ref, v_ref, pt_ref, o_ref, *, ctx_len, grid_rows, need_ctx):
    scale = HEAD_DIM ** -0.5
    kc = k_ref[0:ctx_len, :]
    vc = v_ref[0:ctx_len, :]
    win_rows = NA_WIN_R * GRID_W

    if need_ctx:
        qc = q_ref[0:ctx_len, :] * scale
        lo, hi = _head_masks(ctx_len)
        outs = []
        for m in (lo, hi):
            s = _dot_nt(jnp.where(m, qc, 0), kc)
            e = jnp.exp(s - jnp.max(s, axis=-1, keepdims=True))
            outs.append(_dot(e.astype(BF16), vc) / jnp.sum(e, axis=-1, keepdims=True))
        o_ref[0:ctx_len, :] = jnp.where(lo, outs[0], outs[1]).astype(o_ref.dtype)
    else:
        o_ref[0:ctx_len, :] = jnp.zeros((ctx_len, LANES), o_ref.dtype)

    lo, hi = _head_masks(GRID_W)

    def row_body(r, carry):
        r0 = jnp.clip(r - NA_WIN_R // 2, 0, grid_rows - NA_WIN_R)
        q0 = pl.multiple_of(ctx_len + r * GRID_W, GRID_W)
        w0 = pl.multiple_of(ctx_len + r0 * GRID_W, GRID_W)
        q_r = q_ref[pl.ds(q0, GRID_W), :] * scale
        k_w = k_ref[pl.ds(w0, win_rows), :]
        v_w = v_ref[pl.ds(w0, win_rows), :]
        dy0 = r0 - r + NA_WIN_R - 1
        outs = []
        for j, m in enumerate((lo, hi)):
            qj = jnp.where(m, q_r, 0)
            bias = jnp.concatenate([pt_ref[j, dy0 + 2 * jj] for jj in range(NA_WIN_R // 2)], axis=-1)
            s_nb = _dot_nt(qj, k_w) + bias
            s_cx = _dot_nt(qj, kc)
            mx = jnp.maximum(jnp.max(s_nb, axis=-1, keepdims=True), jnp.max(s_cx, axis=-1, keepdims=True))
            e_nb = jnp.exp(s_nb - mx)
            e_cx = jnp.exp(s_cx - mx)
            den = jnp.sum(e_nb, axis=-1, keepdims=True) + jnp.sum(e_cx, axis=-1, keepdims=True)
            outs.append((_dot(e_cx.astype(BF16), vc) + _dot(e_nb.astype(BF16), v_w)) / den)
        o_ref[pl.ds(q0, GRID_W), :] = jnp.where(lo, outs[0], outs[1]).astype(o_ref.dtype)
        return carry

    lax.fori_loop(0, grid_rows, row_body, 0)


def _na_bias_table(rpb):
    qcol = jnp.arange(GRID_W)
    kcol = jnp.arange(GRID_W)
    c0 = jnp.clip(qcol - NA_WIN_C // 2, 0, GRID_W - NA_WIN_C)
    col_ok = (kcol[None, :] >= c0[:, None]) & (kcol[None, :] < c0[:, None] + NA_WIN_C)
    col_idx = jnp.clip(kcol[None, :] - qcol[:, None] + NA_WIN_C - 1, 0, 2 * NA_WIN_C - 2)
    bt = jnp.where(col_ok[None, None], rpb.astype(F32)[:, :, col_idx], NEG_INF)
    return jnp.concatenate([bt[:, :-1], bt[:, 1:]], axis=-1)


def _na_attention(qkv, rpb, ctx_len, need_ctx):
    b, r, _ = qkv.shape
    grid_rows = (r - ctx_len) // GRID_W
    nblk = D_MODEL // LANES
    pt = _na_bias_table(rpb)
    kern = functools.partial(_na_kernel, ctx_len=ctx_len, grid_rows=grid_rows, need_ctx=need_ctx)
    return pl.pallas_call(
        kern,
        grid=(nblk, b),
        in_specs=[
            pl.BlockSpec((None, r, LANES), lambda hp, bi: (bi, 0, hp)),
            pl.BlockSpec((None, r, LANES), lambda hp, bi: (bi, 0, nblk + hp)),
            pl.BlockSpec((None, r, LANES), lambda hp, bi: (bi, 0, 2 * nblk + hp)),
            pl.BlockSpec((2, 2 * NA_WIN_R - 2, GRID_W, 2 * GRID_W), lambda hp, bi: (hp, 0, 0, 0)),
        ],
        out_specs=pl.BlockSpec((None, r, LANES), lambda hp, bi: (bi, 0, hp)),
        out_shape=jax.ShapeDtypeStruct((b, r, D_MODEL), BF16),
        compiler_params=_params(("arbitrary", "arbitrary")),
        name="na_attn",
    )(qkv, qkv, qkv, pt)


def _da_kernel(q_ref, k_ref, v_ref, lam_ref, g_ref, o_ref, *, ctx_len, lam_init):
    scale = HEAD_DIM ** -0.5
    lp = lam_ref[...]
    lam = (jnp.exp(jnp.sum(lp[0:1] * lp[1:2], axis=-1, keepdims=True))
           - jnp.exp(jnp.sum(lp[2:3] * lp[3:4], axis=-1, keepdims=True)) + lam_init)
    q = q_ref[...] * scale
    lo, hi = _head_masks(q.shape[0])

    def attend(k, v):
        probs = []
        for m in (lo, hi):
            s = _dot_nt(jnp.where(m, q, 0), k)
            e = jnp.exp(s - jnp.max(s, axis=-1, keepdims=True))
            probs.append(e * (1.0 / jnp.sum(e, axis=-1, keepdims=True)))
        a = probs[0] - lam * probs[1]
        o = _dot(a.astype(BF16), v)
        o_ref[...] = (_rms(o) * g_ref[...] * (1.0 - lam_init)).astype(o_ref.dtype)

    is_ctx = pl.program_id(2) < ctx_len // ROW_TILE

    @pl.when(is_ctx)
    def _():
        attend(k_ref[0:ctx_len, :], v_ref[0:ctx_len, :])

    @pl.when(jnp.logical_not(is_ctx))
    def _():
        attend(k_ref[...], v_ref[...])


def _da_attention(qkv, lam_params, subln_g, layer_idx, ctx_len):
    b, r, _ = qkv.shape
    nblk = D_MODEL // LANES
    lam_init = 0.8 - 0.6 * math.exp(-0.3 * layer_idx)
    kern = functools.partial(_da_kernel, ctx_len=ctx_len, lam_init=lam_init)
    return pl.pallas_call(
        kern,
        grid=(nblk, b, r // ROW_TILE),
        in_specs=[
            pl.BlockSpec((None, ROW_TILE, LANES), lambda h, bi, i: (bi, i, h)),
            pl.BlockSpec((None, r, LANES), lambda h, bi, i: (bi, 0, nblk + h)),
            pl.BlockSpec((None, r, LANES), lambda h, bi, i: (bi, 0, 2 * nblk + h)),
            pl.BlockSpec((4, HEAD_DIM), lambda h, bi, i: (0, 0)),
            pl.BlockSpec((1, 2 * HEAD_DIM), lambda h, bi, i: (0, 0)),
        ],
        out_specs=pl.BlockSpec((None, ROW_TILE, LANES), lambda h, bi, i: (bi, i, h)),
        out_shape=jax.ShapeDtypeStruct((b, r, D_MODEL), BF16),
        compiler_params=_params(("arbitrary", "arbitrary", "arbitrary")),
        name="da_attn",
    )(qkv, qkv, qkv, lam_params.astype(F32), subln_g.astype(F32).reshape(1, -1))


def _sw_kernel(sink_ref, q_ref, k_ref, v_ref, o_ref, kd_ref, vd_ref, *, ctx_len, seq, need_ctx):
    scale = HEAD_DIM ** -0.5
    kvp = pl.program_id(0)
    band = Q_BLOCK + 2 * SW_WINDOW
    heads_per_kv = SW_Q_HEADS // SW_KV_HEADS

    k = k_ref[...]
    v = v_ref[...]
    for hh in range(2):
        sl = slice(hh * HEAD_DIM, (hh + 1) * HEAD_DIM)
        kd_ref[hh] = jnp.concatenate([k[:, sl], k[:, sl]], axis=-1)
        vd_ref[hh] = jnp.concatenate([v[:, sl], v[:, sl]], axis=-1)

    def head_list():
        for hh in range(2):
            for qb in range(heads_per_kv // 2):
                for j in range(2):
                    blk = hh * (heads_per_kv // 2) + qb
                    yield hh, blk, j, 2 * blk + j

    def sink_of(local_head):
        return sink_ref[kvp * 2 * heads_per_kv + local_head]

    if need_ctx:
        lo, hi = _head_masks(ctx_len)
        for blk in range(heads_per_kv):
            hh = blk // (heads_per_kv // 2)
            qc = q_ref[0:ctx_len, blk * LANES:(blk + 1) * LANES] * scale
            kc = kd_ref[hh, 0:ctx_len, :]
            vc = vd_ref[hh, 0:ctx_len, :]
            outs = []
            for j, m in enumerate((lo, hi)):
                sink = sink_of(2 * blk + j)
                s = _dot_nt(jnp.where(m, qc, 0), kc)
                mx = jnp.maximum(jnp.max(s, axis=-1, keepdims=True), sink)
                e = jnp.exp(s - mx)
                den = jnp.sum(e, axis=-1, keepdims=True) + jnp.exp(sink - mx)
                outs.append(_dot(e.astype(BF16), vc) / den)
            o_ref[0:ctx_len, blk * LANES:(blk + 1) * LANES] = jnp.where(lo, outs[0], outs[1]).astype(o_ref.dtype)
    else:
        o_ref[0:ctx_len, :] = jnp.zeros((ctx_len, o_ref.shape[-1]), o_ref.dtype)

    lo, hi = _head_masks(Q_BLOCK)
    masks = (lo, hi)

    def block_body(n, carry):
        start = n * Q_BLOCK
        b0 = jnp.clip(start - SW_WINDOW, 0, seq - band)
        qrow = pl.multiple_of(ctx_len + start, Q_BLOCK)
        brow = pl.multiple_of(ctx_len + b0, Q_BLOCK)
        qpos = start + lax.broadcasted_iota(jnp.int32, (Q_BLOCK, band), 0)
        kpos = b0 + lax.broadcasted_iota(jnp.int32, (Q_BLOCK, band), 1)
        in_band = jnp.abs(kpos - qpos) <= SW_WINDOW
        res = {}
        for hh, blk, j, local in head_list():
            q_n = q_ref[pl.ds(qrow, Q_BLOCK), blk * LANES:(blk + 1) * LANES] * scale
            qj = jnp.where(masks[j], q_n, 0)
            sink = sink_of(local)
            s_l = jnp.where(in_band, _dot_nt(qj, kd_ref[hh, pl.ds(brow, band), :]), NEG_INF)
            s_c = _dot_nt(qj, kd_ref[hh, 0:ctx_len, :])
            mx = jnp.maximum(jnp.maximum(jnp.max(s_l, axis=-1, keepdims=True),
                                         jnp.max(s_c, axis=-1, keepdims=True)), sink)
            e_l = jnp.exp(s_l - mx)
            e_c = jnp.exp(s_c - mx)
            den = (jnp.sum(e_l, axis=-1, keepdims=True) + jnp.sum(e_c, axis=-1, keepdims=True)
                   + jnp.exp(sink - mx))
            o = (_dot(e_c.astype(BF16), vd_ref[hh, 0:ctx_len, :])
                 + _dot(e_l.astype(BF16), vd_ref[hh, pl.ds(brow, band), :])) / den
            res[(blk, j)] = o
            if j == 1:
                o_ref[pl.ds(qrow, Q_BLOCK), blk * LANES:(blk + 1) * LANES] = jnp.where(
                    lo, res[(blk, 0)], res[(blk, 1)]).astype(o_ref.dtype)
        return carry

    lax.fori_loop(0, seq // Q_BLOCK, block_body, 0)


def _sw_attention(qkv, sinks, ctx_len, need_ctx):
    b, r, _ = qkv.shape
    seq = r - ctx_len
    nq_blk = SW_Q_HEADS * HEAD_DIM // LANES
    slab = 2 * (SW_Q_HEADS // SW_KV_HEADS) * HEAD_DIM
    kern = functools.partial(_sw_kernel, ctx_len=ctx_len, seq=seq, need_ctx=need_ctx)
    n_kv_blk = SW_KV_HEADS * HEAD_DIM // LANES
    return pl.pallas_call(
        kern,
        grid=(n_kv_blk, b),
        in_specs=[
            pl.BlockSpec(memory_space=pltpu.SMEM),
            pl.BlockSpec((None, r, slab), lambda kp, bi: (bi, 0, kp)),
            pl.BlockSpec((None, r, LANES), lambda kp, bi: (bi, 0, nq_blk + kp)),
            pl.BlockSpec((None, r, LANES), lambda kp, bi: (bi, 0, nq_blk + n_kv_blk + kp)),
        ],
        out_specs=pl.BlockSpec((None, r, slab), lambda kp, bi: (bi, 0, kp)),
        out_shape=jax.ShapeDtypeStruct((b, r, D_MODEL), BF16),
        scratch_shapes=[pltpu.VMEM((2, r, LANES), BF16), pltpu.VMEM((2, r, LANES), BF16)],
        compiler_params=_params(("arbitrary", "arbitrary")),
        name="sw_attn",
    )(sinks.astype(F32), qkv, qkv, qkv)


def _post_kernel(o_ref, x_ref, wo_ref, mod_ref, g_ref, wr_ref, br_ref, xn_ref, h_ref, rt_ref, *, ctx_tiles):
    d = x_ref.shape[-1]
    tm = x_ref.shape[0]
    row = _mod_row(pl.program_id(0), pl.program_id(1), ctx_tiles)
    gate1 = mod_ref[pl.ds(row, 1), 2 * d:3 * d]
    shift2 = mod_ref[pl.ds(row, 1), 3 * d:4 * d]
    scale2 = mod_ref[pl.ds(row, 1), 4 * d:5 * d]
    xn = x_ref[...] + gate1 * _dot(o_ref[...], wo_ref[...])
    xn_ref[...] = xn
    h = (_rms(xn) * g_ref[...]) * (1.0 + scale2) + shift2
    for j in range(SUBLANES):
        h_ref[pl.ds(j, tm, stride=SUBLANES), :] = h[:, j * LANES:(j + 1) * LANES]

    logits = _dot(h, wr_ref[...], precision=lax.Precision.HIGHEST) + br_ref[...]
    lane = lax.broadcasted_iota(jnp.int32, (tm, LANES), 1).astype(F32)

    def first_argmax(vals, mx):
        return jnp.min(jnp.where(vals == mx, lane, float(LANES)), axis=-1, keepdims=True)

    gl = jnp.where(lane < N_GROUPS, logits, -jnp.inf)
    ge = jnp.exp(gl - jnp.max(gl, axis=-1, keepdims=True))
    gp = ge / jnp.sum(ge, axis=-1, keepdims=True)
    g_p = jnp.max(gp, axis=-1, keepdims=True)
    g_idx = first_argmax(gp, g_p)
    e_lo = N_GROUPS + EXPERTS_PER_GROUP * g_idx
    el = jnp.where((lane >= e_lo) & (lane < e_lo + EXPERTS_PER_GROUP), logits, -jnp.inf)
    v1 = jnp.max(el, axis=-1, keepdims=True)
    i1 = first_argmax(el, v1)
    el2 = jnp.where(lane == i1, -jnp.inf, el)
    v2 = jnp.max(el2, axis=-1, keepdims=True)
    i2 = first_argmax(el2, v2)
    e2 = jnp.exp(v2 - v1)
    w1 = g_p * (1.0 / (1.0 + e2))
    w2 = g_p * (e2 / (1.0 + e2))
    lane8 = lax.broadcasted_iota(jnp.int32, (tm, SUBLANES), 1)
    rt = jnp.where(lane8 == 0, i1 - N_GROUPS,
                   jnp.where(lane8 == 1, i2 - N_GROUPS,
                             jnp.where(lane8 == 2, w1, jnp.where(lane8 == 3, w2, 0.0))))
    rt_ref[...] = rt


def _post(o, x, wo, mod, g, wr, br, ctx_len):
    b, r, d = x.shape
    nt = r // ROW_TILE
    kern = functools.partial(_post_kernel, ctx_tiles=ctx_len // ROW_TILE)
    return pl.pallas_call(
        kern,
        grid=(b, nt),
        in_specs=[
            pl.BlockSpec((None, ROW_TILE, d), lambda bi, i: (bi, i, 0)),
            pl.BlockSpec((None, ROW_TILE, d), lambda bi, i: (bi, i, 0)),
            pl.BlockSpec((d, d), lambda bi, i: (0, 0)),
            pl.BlockSpec((MOD_ROWS, 6 * d), lambda bi, i: (0, 0)),
            pl.BlockSpec((1, d), lambda bi, i: (0, 0)),
            pl.BlockSpec((d, LANES), lambda bi, i: (0, 0)),
            pl.BlockSpec((1, LANES), lambda bi, i: (0, 0)),
        ],
        out_specs=[
            pl.BlockSpec((None, ROW_TILE, d), lambda bi, i: (bi, i, 0)),
            pl.BlockSpec((ROW_TILE * SUBLANES, LANES), lambda bi, i: (bi * nt + i, 0)),
            pl.BlockSpec((ROW_TILE, SUBLANES), lambda bi, i: (bi * nt + i, 0)),
        ],
        out_shape=[
            jax.ShapeDtypeStruct((b, r, d), F32),
            jax.ShapeDtypeStruct((b * r * SUBLANES, LANES), F32),
            jax.ShapeDtypeStruct((b * r, SUBLANES), F32),
        ],
        compiler_params=_params(("arbitrary", "arbitrary")),
        name="post_attn",
    )(o, x, wo, mod, g, wr, br)


DISPATCH_TOKENS = 512


def _row(ref, idx):
    return ref.at[pl.ds(pl.multiple_of(idx * SUBLANES, SUBLANES), SUBLANES), :]


def _dispatch_kernel(dest_ref, h_hbm, xs_hbm, sem):
    base = pl.program_id(0) * DISPATCH_TOKENS

    def issue(t, carry):
        src = _row(h_hbm, base + t)
        pltpu.make_async_copy(src, _row(xs_hbm, dest_ref[0, 0, 2 * t]), sem).start()
        pltpu.make_async_copy(src, _row(xs_hbm, dest_ref[0, 0, 2 * t + 1]), sem).start()
        return carry

    lax.fori_loop(0, DISPATCH_TOKENS, issue, 0)

    def drain(t, carry):
        pltpu.make_async_copy(_row(h_hbm, 0), _row(xs_hbm, 0), sem).wait()
        pltpu.make_async_copy(_row(h_hbm, 0), _row(xs_hbm, 0), sem).wait()
        return carry

    lax.fori_loop(0, DISPATCH_TOKENS, drain, 0)


def _dispatch(h_rows, dest):
    t = dest.shape[0] // 2
    steps = t // DISPATCH_TOKENS
    return pl.pallas_call(
        _dispatch_kernel,
        grid=(steps,),
        in_specs=[
            pl.BlockSpec((1, 1, 2 * DISPATCH_TOKENS), lambda i: (i, 0, 0), memory_space=pltpu.SMEM),
            pl.BlockSpec(memory_space=pl.ANY),
        ],
        out_specs=pl.BlockSpec(memory_space=pl.ANY),
        out_shape=jax.ShapeDtypeStruct((2 * t * SUBLANES, LANES), F32),
        scratch_shapes=[pltpu.SemaphoreType.DMA(())],
        compiler_params=_params(("arbitrary",)),
        name="moe_dispatch",
    )(dest.reshape(steps, 1, 2 * DISPATCH_TOKENS), h_rows)


def _experts_kernel(ve_ref, vt_ref, lo_ref, hi_ref, first_ref, valid_ref,
                    xs_ref, wg_ref, wu_ref, wd_ref, ys_ref, wgu_s, wd_s):
    v = pl.program_id(0)
    tm = ROW_TILE
    f = wg_ref.shape[-1]

    @pl.when(valid_ref[v] == 1)
    def _():
        changed = jnp.logical_or(v == 0, ve_ref[v] != ve_ref[jnp.maximum(v - 1, 0)])

        @pl.when(changed)
        def _():
            wgu_s[:, 0:f] = wg_ref[...].astype(BF16)
            wgu_s[:, f:2 * f] = wu_ref[...].astype(BF16)
            wd_s[...] = wd_ref[...].astype(BF16)

        x = jnp.concatenate([xs_ref[pl.ds(j, tm, stride=SUBLANES), :] for j in range(SUBLANES)], axis=-1)
        au = _dot(x.astype(BF16), wgu_s[...])
        rows = vt_ref[v] * tm + lax.broadcasted_iota(jnp.int32, (tm, 1), 0)
        mine = (rows >= lo_ref[v]) & (rows < hi_ref[v])
        mid = jnp.where(mine, jax.nn.silu(au[:, 0:f]) * au[:, f:2 * f], 0.0)
        y = _dot(mid.astype(BF16), wd_s[...])

        @pl.when(first_ref[v] == 1)
        def _():
            for j in range(SUBLANES):
                ys_ref[pl.ds(j, tm, stride=SUBLANES), :] = y[:, j * LANES:(j + 1) * LANES]

        @pl.when(first_ref[v] == 0)
        def _():
            for j in range(SUBLANES):
                ys_ref[pl.ds(j, tm, stride=SUBLANES), :] += y[:, j * LANES:(j + 1) * LANES]


def _experts(xs, sched, w_gate, w_up, w_down, layer):
    rows = xs.shape[0] // SUBLANES
    d, f = w_gate.shape[-2:]
    n_visits = sched[0].shape[0]
    blk = (ROW_TILE * SUBLANES, LANES)
    grid_spec = pltpu.PrefetchScalarGridSpec(
        num_scalar_prefetch=6,
        grid=(n_visits,),
        in_specs=[
            pl.BlockSpec(blk, lambda v, ve, vt, lo, hi, fi, va: (vt[v], 0)),
            pl.BlockSpec((None, None, d, f), lambda v, ve, vt, lo, hi, fi, va: (layer, ve[v], 0, 0)),
            pl.BlockSpec((None, None, d, f), lambda v, ve, vt, lo, hi, fi, va: (layer, ve[v], 0, 0)),
            pl.BlockSpec((None, None, f, d), lambda v, ve, vt, lo, hi, fi, va: (layer, ve[v], 0, 0)),
        ],
        out_specs=pl.BlockSpec(blk, lambda v, ve, vt, lo, hi, fi, va: (vt[v], 0)),
        scratch_shapes=[pltpu.VMEM((d, 2 * f), BF16), pltpu.VMEM((f, d), BF16)],
    )
    return pl.pallas_call(
        _experts_kernel,
        grid_spec=grid_spec,
        out_shape=jax.ShapeDtypeStruct((rows * SUBLANES, LANES), F32),
        compiler_params=_params(("arbitrary",)),
        name="moe_experts",
    )(*sched, xs, w_gate, w_up, w_down)


def _combine_kernel(dest_ref, rt_ref, x_ref, mod_ref, ys_hbm, o_ref, buf, sem, *, ctx_tiles, tiles_per_batch):
    tm = ROW_TILE
    d = x_ref.shape[-1]

    def issue(t, carry):
        for k in range(2):
            pltpu.make_async_copy(_row(ys_hbm, dest_ref[0, 0, 2 * t + k]), _row(buf, k * tm + t), sem).start()
        return carry

    lax.fori_loop(0, tm, issue, 0)

    def drain(t, carry):
        for k in range(2):
            pltpu.make_async_copy(_row(ys_hbm, 0), _row(buf, 0), sem).wait()
        return carry

    lax.fori_loop(0, tm, drain, 0)

    gi = pl.program_id(0)
    row = _mod_row(gi // tiles_per_batch, gi % tiles_per_batch, ctx_tiles)
    gate2 = mod_ref[pl.ds(row, 1), 5 * d:6 * d]
    rt = rt_ref[...]
    w = (rt[:, 2:3], rt[:, 3:4])
    for j in range(SUBLANES):
        y = (w[0] * buf[pl.ds(j, tm, stride=SUBLANES), :]
             + w[1] * buf[pl.ds(tm * SUBLANES + j, tm, stride=SUBLANES), :])
        sl = slice(j * LANES, (j + 1) * LANES)
        o_ref[:, sl] = x_ref[:, sl] + gate2[:, sl] * y


def _combine(ys, dest, rt, x, mod, ctx_len):
    b, r, d = x.shape
    t = b * r
    steps = t // ROW_TILE
    kern = functools.partial(_combine_kernel, ctx_tiles=ctx_len // ROW_TILE, tiles_per_batch=r // ROW_TILE)
    out = pl.pallas_call(
        kern,
        grid=(steps,),
        in_specs=[
            pl.BlockSpec((1, 1, 2 * ROW_TILE), lambda i: (i, 0, 0), memory_space=pltpu.SMEM),
            pl.BlockSpec((ROW_TILE, SUBLANES), lambda i: (i, 0)),
            pl.BlockSpec((ROW_TILE, d), lambda i: (i, 0)),
            pl.BlockSpec((MOD_ROWS, 6 * d), lambda i: (0, 0)),
            pl.BlockSpec(memory_space=pl.ANY),
        ],
        out_specs=pl.BlockSpec((ROW_TILE, d), lambda i: (i, 0)),
        out_shape=jax.ShapeDtypeStruct((t, d), F32),
        scratch_shapes=[pltpu.VMEM((2 * ROW_TILE * SUBLANES, LANES), F32), pltpu.SemaphoreType.DMA(())],
        compiler_params=_params(("arbitrary",)),
        name="moe_combine",
    )(dest.reshape(steps, 1, 2 * ROW_TILE), rt, x.reshape(t, d), mod, ys)
    return out.reshape(b, r, d)


def _route_plan(rt, n_tiles):
    eid = rt[:, 0:2].astype(jnp.int32).reshape(-1)
    onehot = (eid[:, None] == jnp.arange(N_EXPERTS, dtype=jnp.int32)[None, :]).astype(jnp.int32)
    csum = jnp.cumsum(onehot, axis=0)
    counts = csum[-1]
    cstart = jnp.cumsum(counts) - counts
    dest = jnp.sum(onehot * (cstart[None, :] + csum - 1), axis=1).astype(jnp.int32)

    tm = ROW_TILE
    first_tile = cstart // tm
    last_tile = (cstart + counts - 1) // tm
    nvis = jnp.where(counts > 0, last_tile - first_tile + 1, 0)
    vend = jnp.cumsum(nvis)
    vstart = vend - nvis
    total = vend[-1]
    n_visits = n_tiles + N_EXPERTS - 1
    v = jnp.arange(n_visits, dtype=jnp.int32)
    valid = v < total
    vc = jnp.minimum(v, total - 1)
    ve = jnp.sum((vc[:, None] >= vend[None, :]).astype(jnp.int32), axis=1)
    oh_e = (ve[:, None] == jnp.arange(N_EXPERTS, dtype=jnp.int32)[None, :]).astype(jnp.int32)
    pick = lambda a: jnp.sum(oh_e * a[None, :], axis=1)
    vt = pick(first_tile) + vc - pick(vstart)
    lo = pick(cstart)
    hi = jnp.where(valid, lo + pick(counts), lo)
    first = jnp.concatenate([jnp.ones((1,), jnp.int32), (vt[1:] != vt[:-1]).astype(jnp.int32)])
    sched = tuple(a.astype(jnp.int32) for a in (ve, vt, lo, hi, first, valid))
    return dest, sched


def _final_kernel(x_ref, g_ref, o_ref):
    o_ref[...] = _rms(x_ref[...]) * g_ref[...]


def _final_norm(x, g, ctx_len):
    b, r, d = x.shape
    ct = ctx_len // ROW_TILE
    return pl.pallas_call(
        _final_kernel,
        grid=(b, (r - ctx_len) // ROW_TILE),
        in_specs=[
            pl.BlockSpec((None, ROW_TILE, d), lambda bi, i: (bi, i + ct, 0)),
            pl.BlockSpec((1, d), lambda bi, i: (0, 0)),
        ],
        out_specs=pl.BlockSpec((None, ROW_TILE, d), lambda bi, i: (bi, i, 0)),
        out_shape=jax.ShapeDtypeStruct((b, r - ctx_len, d), F32),
        compiler_params=_params(("arbitrary", "arbitrary")),
        name="final_norm",
    )(x, g)


def kernel(x, c, ctx, c_ctx, ada_w, ada_b, norm_mix_g, norm_ffn_g, final_norm_g, na_w_qkv, na_rpb, na_w_o, da_w_qkv, da_lambda, da_subln_g, da_w_o, sw_w_qkv, sw_sinks, sw_w_o, moe_w_grp, moe_b_grp, moe_w_exp, moe_b_exp, moe_w_gate, moe_w_up, moe_w_down):
    b, seq, d = x.shape
    ctx_len = ctx.shape[1]
    depth = ada_w.shape[0]
    assert d == D_MODEL and ctx_len % ROW_TILE == 0 and seq % ROW_TILE == 0 and b <= MOD_ROWS // 2
    assert (b * (ctx_len + seq)) % DISPATCH_TOKENS == 0

    half = MOD_ROWS // 2
    cond = jnp.concatenate([jnp.pad(c, ((0, half - b), (0, 0))), jnp.pad(c_ctx[None, :], ((0, half - 1), (0, 0)))], axis=0)
    mod_all = _adaln(cond, ada_w, ada_b)
    cos, sin = _rope_tables(seq, ctx_len)
    stream = jnp.concatenate([ctx, x], axis=1)
    n_tiles = 2 * b * (ctx_len + seq) // ROW_TILE

    for i in range(depth):
        need_ctx = i < depth - 1
        kind, j = i % N_MIXERS, i // N_MIXERS
        mod = mod_all[i]
        g_mix = norm_mix_g[i].reshape(1, d)
        if kind == 0:
            qkv = _qkv(stream, g_mix, mod, na_w_qkv[j].astype(BF16), cos, sin, 0, ctx_len)
            o = _na_attention(qkv, na_rpb[j], ctx_len, need_ctx)
            wo = na_w_o[j]
        elif kind == 1:
            nq = 2 * DA_HEADS * HEAD_DIM
            qkv = _qkv(stream, g_mix, mod, da_w_qkv[j].astype(BF16), cos, sin, 2 * nq, ctx_len)
            o = _da_attention(qkv, da_lambda[j], da_subln_g[j], i, ctx_len)
            wo = da_w_o[j]
        else:
            rope_cols = (SW_Q_HEADS + SW_KV_HEADS) * HEAD_DIM
            qkv = _qkv(stream, g_mix, mod, sw_w_qkv[j].astype(BF16), cos, sin, rope_cols, ctx_len)
            o = _sw_attention(qkv, sw_sinks[j], ctx_len, need_ctx)
            wo = sw_w_o[j]

        pad = LANES - N_GROUPS - N_EXPERTS
        wr = jnp.pad(jnp.concatenate([moe_w_grp[i], moe_w_exp[i]], axis=1).astype(F32), ((0, 0), (0, pad)))
        br = jnp.pad(jnp.concatenate([moe_b_grp[i], moe_b_exp[i]]).astype(F32), (0, pad)).reshape(1, LANES)
        stream, h_rows, rt = _post(o, stream, wo.astype(BF16), mod, norm_ffn_g[i].reshape(1, d), wr, br, ctx_len)
        dest, sched = _route_plan(rt, n_tiles)
        xs = _dispatch(h_rows, dest)
        ys = _experts(xs, sched, moe_w_gate, moe_w_up, moe_w_down, i)
        stream = _combine(ys, dest, rt, stream, mod, ctx_len)

    return _final_norm(stream, final_norm_g.reshape(1, d), ctx_len)
```

```python
import functools
import math

import jax
import jax.numpy as jnp
from jax import lax
from jax.experimental import pallas as pl
from jax.experimental.pallas import tpu as pltpu

F32 = jnp.float32
BF16 = jnp.bfloat16

D_MODEL = 1024
HEAD_DIM = 64
GRID_W = 64
ROPE_AXIS_PAIRS = HEAD_DIM // 4
ROPE_BASE = 10000.0
NA_HEADS = D_MODEL // HEAD_DIM
NA_WIN_R = 8
NA_WIN_C = 16
NA_GROUP = 4
NA_UNION = NA_GROUP + NA_WIN_R
DA_HEADS = D_MODEL // (2 * HEAD_DIM)
SW_Q_HEADS = D_MODEL // HEAD_DIM
SW_KV_HEADS = 4
SW_WINDOW = 128
Q_BLOCK = 128
N_GROUPS = 4
EXPERTS_PER_GROUP = 8
N_EXPERTS = N_GROUPS * EXPERTS_PER_GROUP
D_EXPERT = 512
NORM_EPS = 1e-6
NEG_INF = -1e30
N_MIXERS = 3

LANES = 128
SUBLANES = 8
ROW_TILE = 256
MOD_ROWS = 16
VMEM_LIMIT = 56 * 1024 * 1024

_NT = (((1,), (1,)), ((), ()))


def _dot(a, b, **kw):
    return jnp.dot(a, b, preferred_element_type=F32, **kw)


def _dot_nt(a, b):
    return lax.dot_general(a, b, _NT, preferred_element_type=F32)


def _params(sem, vmem=VMEM_LIMIT):
    return pltpu.CompilerParams(dimension_semantics=sem, vmem_limit_bytes=vmem)


def _mod_row(b, tile, ctx_tiles):
    return jnp.where(tile < ctx_tiles, MOD_ROWS // 2, b)


def _rms(x):
    return x * lax.rsqrt(jnp.mean(x * x, axis=-1, keepdims=True) + NORM_EPS)


def _adaln_kernel(c_ref, w_ref, b_ref, o_ref):
    s = jax.nn.silu(c_ref[...])
    o_ref[...] = _dot(s, w_ref[...], precision=lax.Precision.HIGHEST) + b_ref[...]


def _adaln(cond, ada_w, ada_b):
    depth, d, n6 = ada_w.shape
    tn = 1536
    return pl.pallas_call(
        _adaln_kernel,
        grid=(depth, n6 // tn),
        in_specs=[
            pl.BlockSpec((MOD_ROWS, d), lambda l, n: (0, 0)),
            pl.BlockSpec((None, d, tn), lambda l, n: (l, 0, n)),
            pl.BlockSpec((None, 1, tn), lambda l, n: (l, 0, n)),
        ],
        out_specs=pl.BlockSpec((None, MOD_ROWS, tn), lambda l, n: (l, 0, n)),
        out_shape=jax.ShapeDtypeStruct((depth, MOD_ROWS, n6), F32),
        compiler_params=_params(("arbitrary", "arbitrary")),
        name="adaln",
    )(cond, ada_w, ada_b.reshape(depth, 1, n6))


def _qkv_kernel(x_ref, g_ref, mod_ref, w_ref, cos_ref, sin_ref, o_ref, *, rope_cols, ctx_tiles):
    d = x_ref.shape[-1]
    row = _mod_row(pl.program_id(0), pl.program_id(1), ctx_tiles)
    shift = mod_ref[pl.ds(row, 1), 0:d]
    scale = mod_ref[pl.ds(row, 1), d:2 * d]
    h = (_rms(x_ref[...]) * g_ref[...]) * (1.0 + scale) + shift
    acc = _dot(h.astype(BF16), w_ref[...])
    n = acc.shape[-1]
    if rope_cols:
        tm = acc.shape[0]
        first_half = (lax.broadcasted_iota(jnp.int32, (tm, LANES), 1) % (2 * ROPE_AXIS_PAIRS)) < ROPE_AXIS_PAIRS
        cos = cos_ref[...]
        sin = sin_ref[...]
        for c in range(rope_cols // LANES):
            t = acc[:, c * LANES:(c + 1) * LANES]
            up = pltpu.roll(t, LANES - ROPE_AXIS_PAIRS, 1)
            dn = pltpu.roll(t, ROPE_AXIS_PAIRS, 1)
            o_ref[:, c * LANES:(c + 1) * LANES] = (t * cos + jnp.where(first_half, up, dn) * sin).astype(o_ref.dtype)
    if rope_cols < n:
        o_ref[:, rope_cols:] = acc[:, rope_cols:].astype(o_ref.dtype)


def _qkv(x, g, mod, w, cos, sin, rope_cols, ctx_len):
    b, r, d = x.shape
    n = w.shape[1]
    kern = functools.partial(_qkv_kernel, rope_cols=rope_cols, ctx_tiles=ctx_len // ROW_TILE)
    return pl.pallas_call(
        kern,
        grid=(b, r // ROW_TILE),
        in_specs=[
            pl.BlockSpec((None, ROW_TILE, d), lambda bi, i: (bi, i, 0)),
            pl.BlockSpec((1, d), lambda bi, i: (0, 0)),
            pl.BlockSpec((MOD_ROWS, 6 * d), lambda bi, i: (0, 0)),
            pl.BlockSpec((d, n), lambda bi, i: (0, 0)),
            pl.BlockSpec((ROW_TILE, LANES), lambda bi, i: (i, 0)),
            pl.BlockSpec((ROW_TILE, LANES), lambda bi, i: (i, 0)),
        ],
        out_specs=pl.BlockSpec((None, ROW_TILE, n), lambda bi, i: (bi, i, 0)),
        out_shape=jax.ShapeDtypeStruct((b, r, n), BF16),
        compiler_params=_params(("arbitrary", "arbitrary")),
        name="qkv",
    )(x, g, mod, w, cos, sin)


def _rope_tables(seq, ctx_len):
    t = jnp.arange(seq, dtype=jnp.int32)
    row = (t // GRID_W).astype(F32)
    col = (t % GRID_W).astype(F32)
    inv = ROPE_BASE ** (-jnp.arange(ROPE_AXIS_PAIRS, dtype=F32) / ROPE_AXIS_PAIRS)
    ang_r = row[:, None] * inv[None, :]
    ang_c = col[:, None] * inv[None, :]
    ang = jnp.concatenate([ang_r, ang_r, ang_c, ang_c], axis=-1)
    sign = jnp.tile(jnp.concatenate([-jnp.ones((ROPE_AXIS_PAIRS,), F32), jnp.ones((ROPE_AXIS_PAIRS,), F32)]), 2)
    cos = jnp.concatenate([jnp.ones((ctx_len, HEAD_DIM), F32), jnp.cos(ang)], axis=0)
    sin = jnp.concatenate([jnp.zeros((ctx_len, HEAD_DIM), F32), jnp.sin(ang) * sign[None, :]], axis=0)
    return jnp.tile(cos, (1, LANES // HEAD_DIM)), jnp.tile(sin, (1, LANES // HEAD_DIM))


def _head_masks(rows):
    lane = lax.broadcasted_iota(jnp.int32, (rows, LANES), 1)
    lo = lane < HEAD_DIM
    return lo, jnp.logical_not(lo)


def _na_kernel(q_ref, k_ref, v_ref, pt_ref, o_ref, *, ctx_len, grid_rows, need_ctx):
    scale = HEAD_DIM ** -0.5
    kc = k_ref[0:ctx_len, :]
    vc = v_ref[0:ctx_len, :]

    if need_ctx:
        qc = q_ref[0:ctx_len, :] * scale
        lo, hi = _head_masks(ctx_len)
        outs = []
        for m in (lo, hi):
            s = _dot_nt(jnp.where(m, qc, 0), kc)
            e = jnp.exp(s - jnp.max(s, axis=-1, keepdims=True))
            outs.append(_dot(e.astype(BF16), vc) / jnp.sum(e, axis=-1, keepdims=True))
        o_ref[0:ctx_len, :] = jnp.where(lo, outs[0], outs[1]).astype(o_ref.dtype)
    else:
        o_ref[0:ctx_len, :] = jnp.zeros((ctx_len, LANES), o_ref.dtype)

    gq = NA_GROUP * GRID_W
    uk = NA_UNION * GRID_W
    lo, hi = _head_masks(gq)
    q_row = lax.broadcasted_iota(jnp.int32, (gq, 1), 0) // GRID_W
    k_row = lax.broadcasted_iota(jnp.int32, (1, uk), 1) // GRID_W

    def group_body(g, carry):
        u0 = jnp.clip(g * NA_GROUP - NA_WIN_R // 2, 0, grid_rows - NA_UNION)
        r0 = jnp.clip(g * NA_GROUP + q_row - NA_WIN_R // 2, 0, grid_rows - NA_WIN_R)
        in_win = (u0 + k_row >= r0) & (u0 + k_row < r0 + NA_WIN_R)
        q0 = pl.multiple_of(ctx_len + g * gq, gq)
        w0 = pl.multiple_of(ctx_len + u0 * GRID_W, GRID_W)
        q_g = q_ref[pl.ds(q0, gq), :] * scale
        k_u = k_ref[pl.ds(w0, uk), :]
        v_u = v_ref[pl.ds(w0, uk), :]
        outs = []
        for j, m in enumerate((lo, hi)):
            qj = jnp.where(m, q_g, 0)
            bias = jnp.concatenate([
                jnp.concatenate([
                    pt_ref[j, jnp.clip(u0 + 2 * p - (g * NA_GROUP + a) + NA_WIN_R - 1, -1, 2 * NA_WIN_R - 2) + 1]
                    for p in range(NA_UNION // 2)], axis=-1)
                for a in range(NA_GROUP)], axis=0)
            s_nb = jnp.where(in_win, _dot_nt(qj, k_u) + bias, NEG_INF)
            s_cx = _dot_nt(qj, kc)
            mx = jnp.maximum(jnp.max(s_nb, axis=-1, keepdims=True), jnp.max(s_cx, axis=-1, keepdims=True))
            e_nb = jnp.exp(s_nb - mx)
            e_cx = jnp.exp(s_cx - mx)
            den = jnp.sum(e_nb, axis=-1, keepdims=True) + jnp.sum(e_cx, axis=-1, keepdims=True)
            outs.append((_dot(e_cx.astype(BF16), vc) + _dot(e_nb.astype(BF16), v_u)) / den)
        o_ref[pl.ds(q0, gq), :] = jnp.where(lo, outs[0], outs[1]).astype(o_ref.dtype)
        return carry

    lax.fori_loop(0, grid_rows // NA_GROUP, group_body, 0)


def _na_bias_table(rpb):
    qcol = jnp.arange(GRID_W)
    kcol = jnp.arange(GRID_W)
    c0 = jnp.clip(qcol - NA_WIN_C // 2, 0, GRID_W - NA_WIN_C)
    col_ok = (kcol[None, :] >= c0[:, None]) & (kcol[None, :] < c0[:, None] + NA_WIN_C)
    col_idx = jnp.clip(kcol[None, :] - qcol[:, None] + NA_WIN_C - 1, 0, 2 * NA_WIN_C - 2)
    bt = jnp.where(col_ok[None, None], rpb.astype(F32)[:, :, col_idx], NEG_INF)
    bt = jnp.pad(bt, ((0, 0), (1, 1), (0, 0), (0, 0)))
    return jnp.concatenate([bt[:, :-1], bt[:, 1:]], axis=-1)


def _na_attention(qkv, rpb, ctx_len, need_ctx):
    b, r, _ = qkv.shape
    grid_rows = (r - ctx_len) // GRID_W
    nblk = D_MODEL // LANES
    pt = _na_bias_table(rpb)
    kern = functools.partial(_na_kernel, ctx_len=ctx_len, grid_rows=grid_rows, need_ctx=need_ctx)
    return pl.pallas_call(
        kern,
        grid=(nblk, b),
        in_specs=[
            pl.BlockSpec((None, r, LANES), lambda hp, bi: (bi, 0, hp)),
            pl.BlockSpec((None, r, LANES), lambda hp, bi: (bi, 0, nblk + hp)),
            pl.BlockSpec((None, r, LANES), lambda hp, bi: (bi, 0, 2 * nblk + hp)),
            pl.BlockSpec((2, 2 * NA_WIN_R, GRID_W, 2 * GRID_W), lambda hp, bi: (hp, 0, 0, 0)),
        ],
        out_specs=pl.BlockSpec((None, r, LANES), lambda hp, bi: (bi, 0, hp)),
        out_shape=jax.ShapeDtypeStruct((b, r, D_MODEL), BF16),
        compiler_params=_params(("arbitrary", "arbitrary")),
        name="na_attn",
    )(qkv, qkv, qkv, pt)


def _da_kernel(q_ref, k_ref, v_ref, lam_ref, g_ref, o_ref, *, ctx_len, lam_init):
    scale = HEAD_DIM ** -0.5
    lp = lam_ref[...]
    lam = (jnp.exp(jnp.sum(lp[0:1] * lp[1:2], axis=-1, keepdims=True))
           - jnp.exp(jnp.sum(lp[2:3] * lp[3:4], axis=-1, keepdims=True)) + lam_init)
    q = q_ref[...] * scale
    lo, hi = _head_masks(q.shape[0])

    def attend(k, v):
        probs = []
        for m in (lo, hi):
            s = _dot_nt(jnp.where(m, q, 0), k)
            e = jnp.exp(s - jnp.max(s, axis=-1, keepdims=True))
            probs.append(e * (1.0 / jnp.sum(e, axis=-1, keepdims=True)))
        a = probs[0] - lam * probs[1]
        o = _dot(a.astype(BF16), v)
        o_ref[...] = (_rms(o) * g_ref[...] * (1.0 - lam_init)).astype(o_ref.dtype)

    is_ctx = pl.program_id(2) < ctx_len // ROW_TILE

    @pl.when(is_ctx)
    def _():
        attend(k_ref[0:ctx_len, :], v_ref[0:ctx_len, :])

    @pl.when(jnp.logical_not(is_ctx))
    def _():
        attend(k_ref[...], v_ref[...])


def _da_attention(qkv, lam_params, subln_g, layer_idx, ctx_len):
    b, r, _ = qkv.shape
    nblk = D_MODEL // LANES
    lam_init = 0.8 - 0.6 * math.exp(-0.3 * layer_idx)
    kern = functools.partial(_da_kernel, ctx_len=ctx_len, lam_init=lam_init)
    return pl.pallas_call(
        kern,
        grid=(nblk, b, r // ROW_TILE),
        in_specs=[
            pl.BlockSpec((None, ROW_TILE, LANES), lambda h, bi, i: (bi, i, h)),
            pl.BlockSpec((None, r, LANES), lambda h, bi, i: (bi, 0, nblk + h)),
            pl.BlockSpec((None, r, LANES), lambda h, bi, i: (bi, 0, 2 * nblk + h)),
            pl.BlockSpec((4, HEAD_DIM), lambda h, bi, i: (0, 0)),
            pl.BlockSpec((1, 2 * HEAD_DIM), lambda h, bi, i: (0, 0)),
        ],
        out_specs=pl.BlockSpec((None, ROW_TILE, LANES), lambda h, bi, i: (bi, i, h)),
        out_shape=jax.ShapeDtypeStruct((b, r, D_MODEL), BF16),
        compiler_params=_params(("arbitrary", "arbitrary", "arbitrary")),
        name="da_attn",
    )(qkv, qkv, qkv, lam_params.astype(F32), subln_g.astype(F32).reshape(1, -1))


def _sw_kernel(sink_ref, q_ref, k_ref, v_ref, o_ref, kd_ref, vd_ref, *, ctx_len, seq, need_ctx):
    scale = HEAD_DIM ** -0.5
    kvp = pl.program_id(0)
    band = Q_BLOCK + 2 * SW_WINDOW
    heads_per_kv = SW_Q_HEADS // SW_KV_HEADS

    k = k_ref[...]
    v = v_ref[...]
    for hh in range(2):
        sl = slice(hh * HEAD_DIM, (hh + 1) * HEAD_DIM)
        kd_ref[hh] = jnp.concatenate([k[:, sl], k[:, sl]], axis=-1)
        vd_ref[hh] = jnp.concatenate([v[:, sl], v[:, sl]], axis=-1)

    def head_list():
        for hh in range(2):
            for qb in range(heads_per_kv // 2):
                for j in range(2):
                    blk = hh * (heads_per_kv // 2) + qb
                    yield hh, blk, j, 2 * blk + j

    def sink_of(local_head):
        return sink_ref[kvp * 2 * heads_per_kv + local_head]

    if need_ctx:
        lo, hi = _head_masks(ctx_len)
        for blk in range(heads_per_kv):
            hh = blk // (heads_per_kv // 2)
            qc = q_ref[0:ctx_len, blk * LANES:(blk + 1) * LANES] * scale
            kc = kd_ref[hh, 0:ctx_len, :]
            vc = vd_ref[hh, 0:ctx_len, :]
            outs = []
            for j, m in enumerate((lo, hi)):
                sink = sink_of(2 * blk + j)
                s = _dot_nt(jnp.where(m, qc, 0), kc)
                mx = jnp.maximum(jnp.max(s, axis=-1, keepdims=True), sink)
                e = jnp.exp(s - mx)
                den = jnp.sum(e, axis=-1, keepdims=True) + jnp.exp(sink - mx)
                outs.append(_dot(e.astype(BF16), vc) / den)
            o_ref[0:ctx_len, blk * LANES:(blk + 1) * LANES] = jnp.where(lo, outs[0], outs[1]).astype(o_ref.dtype)
    else:
        o_ref[0:ctx_len, :] = jnp.zeros((ctx_len, o_ref.shape[-1]), o_ref.dtype)

    lo, hi = _head_masks(Q_BLOCK)
    masks = (lo, hi)

    def block_body(n, carry):
        start = n * Q_BLOCK
        b0 = jnp.clip(start - SW_WINDOW, 0, seq - band)
        qrow = pl.multiple_of(ctx_len + start, Q_BLOCK)
        brow = pl.multiple_of(ctx_len + b0, Q_BLOCK)
        qpos = start + lax.broadcasted_iota(jnp.int32, (Q_BLOCK, band), 0)
        kpos = b0 + lax.broadcasted_iota(jnp.int32, (Q_BLOCK, band), 1)
        in_band = jnp.abs(kpos - qpos) <= SW_WINDOW
        res = {}
        for hh, blk, j, local in head_list():
            q_n = q_ref[pl.ds(qrow, Q_BLOCK), blk * LANES:(blk + 1) * LANES] * scale
            qj = jnp.where(masks[j], q_n, 0)
            sink = sink_of(local)
            s_l = jnp.where(in_band, _dot_nt(qj, kd_ref[hh, pl.ds(brow, band), :]), NEG_INF)
            s_c = _dot_nt(qj, kd_ref[hh, 0:ctx_len, :])
            mx = jnp.maximum(jnp.maximum(jnp.max(s_l, axis=-1, keepdims=True),
                                         jnp.max(s_c, axis=-1, keepdims=True)), sink)
            e_l = jnp.exp(s_l - mx)
            e_c = jnp.exp(s_c - mx)
            den = (jnp.sum(e_l, axis=-1, keepdims=True) + jnp.sum(e_c, axis=-1, keepdims=True)
                   + jnp.exp(sink - mx))
            o = (_dot(e_c.astype(BF16), vd_ref[hh, 0:ctx_len, :])
                 + _dot(e_l.astype(BF16), vd_ref[hh, pl.ds(brow, band), :])) / den
            res[(blk, j)] = o
            if j == 1:
                o_ref[pl.ds(qrow, Q_BLOCK), blk * LANES:(blk + 1) * LANES] = jnp.where(
                    lo, res[(blk, 0)], res[(blk, 1)]).astype(o_ref.dtype)
        return carry

    lax.fori_loop(0, seq // Q_BLOCK, block_body, 0)


def _sw_attention(qkv, sinks, ctx_len, need_ctx):
    b, r, _ = qkv.shape
    seq = r - ctx_len
    nq_blk = SW_Q_HEADS * HEAD_DIM // LANES
    slab = 2 * (SW_Q_HEADS // SW_KV_HEADS) * HEAD_DIM
    kern = functools.partial(_sw_kernel, ctx_len=ctx_len, seq=seq, need_ctx=need_ctx)
    n_kv_blk = SW_KV_HEADS * HEAD_DIM // LANES
    return pl.pallas_call(
        kern,
        grid=(n_kv_blk, b),
        in_specs=[
            pl.BlockSpec(memory_space=pltpu.SMEM),
            pl.BlockSpec((None, r, slab), lambda kp, bi: (bi, 0, kp)),
            pl.BlockSpec((None, r, LANES), lambda kp, bi: (bi, 0, nq_blk + kp)),
            pl.BlockSpec((None, r, LANES), lambda kp, bi: (bi, 0, nq_blk + n_kv_blk + kp)),
        ],
        out_specs=pl.BlockSpec((None, r, slab), lambda kp, bi: (bi, 0, kp)),
        out_shape=jax.ShapeDtypeStruct((b, r, D_MODEL), BF16),
        scratch_shapes=[pltpu.VMEM((2, r, LANES), BF16), pltpu.VMEM((2, r, LANES), BF16)],
        compiler_params=_params(("arbitrary", "arbitrary")),
        name="sw_attn",
    )(sinks.astype(F32), qkv, qkv, qkv)


def _post_kernel(o_ref, x_ref, wo_ref, mod_ref, g_ref, wrh_ref, wrl_ref, br_ref, xn_ref, h_ref, rt_ref, *, ctx_tiles):
    d = x_ref.shape[-1]
    tm = x_ref.shape[0]
    row = _mod_row(pl.program_id(0), pl.program_id(1), ctx_tiles)
    gate1 = mod_ref[pl.ds(row, 1), 2 * d:3 * d]
    shift2 = mod_ref[pl.ds(row, 1), 3 * d:4 * d]
    scale2 = mod_ref[pl.ds(row, 1), 4 * d:5 * d]
    xn = x_ref[...] + gate1 * _dot(o_ref[...], wo_ref[...])
    xn_ref[...] = xn
    h = (_rms(xn) * g_ref[...]) * (1.0 + scale2) + shift2
    for j in range(SUBLANES):
        h_ref[pl.ds(j, tm, stride=SUBLANES), :] = h[:, j * LANES:(j + 1) * LANES]

    h_hi = h.astype(BF16)
    h_lo = (h - h_hi.astype(F32)).astype(BF16)
    logits = (_dot(h_hi, wrh_ref[...]) + _dot(h_lo, wrh_ref[...]) + _dot(h_hi, wrl_ref[...])) + br_ref[...]
    lane = lax.broadcasted_iota(jnp.int32, (tm, LANES), 1).astype(F32)

    def first_argmax(vals, mx):
        return jnp.min(jnp.where(vals == mx, lane, float(LANES)), axis=-1, keepdims=True)

    gl = jnp.where(lane < N_GROUPS, logits, -jnp.inf)
    ge = jnp.exp(gl - jnp.max(gl, axis=-1, keepdims=True))
    gp = ge / jnp.sum(ge, axis=-1, keepdims=True)
    g_p = jnp.max(gp, axis=-1, keepdims=True)
    g_idx = first_argmax(gp, g_p)
    e_lo = N_GROUPS + EXPERTS_PER_GROUP * g_idx
    el = jnp.where((lane >= e_lo) & (lane < e_lo + EXPERTS_PER_GROUP), logits, -jnp.inf)
    v1 = jnp.max(el, axis=-1, keepdims=True)
    i1 = first_argmax(el, v1)
    el2 = jnp.where(lane == i1, -jnp.inf, el)
    v2 = jnp.max(el2, axis=-1, keepdims=True)
    i2 = first_argmax(el2, v2)
    e2 = jnp.exp(v2 - v1)
    w1 = g_p * (1.0 / (1.0 + e2))
    w2 = g_p * (e2 / (1.0 + e2))
    lane8 = lax.broadcasted_iota(jnp.int32, (tm, SUBLANES), 1)
    rt = jnp.where(lane8 == 0, i1 - N_GROUPS,
                   jnp.where(lane8 == 1, i2 - N_GROUPS,
                             jnp.where(lane8 == 2, w1, jnp.where(lane8 == 3, w2, 0.0))))
    rt_ref[...] = rt


def _post(o, x, wo, mod, g, wr, br, ctx_len):
    wr_hi = wr.astype(BF16)
    wr_lo = (wr - wr_hi.astype(F32)).astype(BF16)
    b, r, d = x.shape
    nt = r // ROW_TILE
    kern = functools.partial(_post_kernel, ctx_tiles=ctx_len // ROW_TILE)
    return pl.pallas_call(
        kern,
        grid=(b, nt),
        in_specs=[
            pl.BlockSpec((None, ROW_TILE, d), lambda bi, i: (bi, i, 0)),
            pl.BlockSpec((None, ROW_TILE, d), lambda bi, i: (bi, i, 0)),
            pl.BlockSpec((d, d), lambda bi, i: (0, 0)),
            pl.BlockSpec((MOD_ROWS, 6 * d), lambda bi, i: (0, 0)),
            pl.BlockSpec((1, d), lambda bi, i: (0, 0)),
            pl.BlockSpec((d, LANES), lambda bi, i: (0, 0)),
            pl.BlockSpec((d, LANES), lambda bi, i: (0, 0)),
            pl.BlockSpec((1, LANES), lambda bi, i: (0, 0)),
        ],
        out_specs=[
            pl.BlockSpec((None, ROW_TILE, d), lambda bi, i: (bi, i, 0)),
            pl.BlockSpec((ROW_TILE * SUBLANES, LANES), lambda bi, i: (bi * nt + i, 0)),
            pl.BlockSpec((ROW_TILE, SUBLANES), lambda bi, i: (bi * nt + i, 0)),
        ],
        out_shape=[
            jax.ShapeDtypeStruct((b, r, d), F32),
            jax.ShapeDtypeStruct((b * r * SUBLANES, LANES), F32),
            jax.ShapeDtypeStruct((b * r, SUBLANES), F32),
        ],
        compiler_params=_params(("arbitrary", "arbitrary")),
        name="post_attn",
    )(o, x, wo, mod, g, wr_hi, wr_lo, br)


DISPATCH_TOKENS = 512


def _row(ref, idx):
    return ref.at[pl.ds(pl.multiple_of(idx * SUBLANES, SUBLANES), SUBLANES), :]


def _dispatch_kernel(dest_ref, h_ref, xs_hbm, sem):
    def issue(t, carry):
        src = _row(h_ref, t)
        pltpu.make_async_copy(src, _row(xs_hbm, dest_ref[0, 0, 2 * t]), sem).start()
        pltpu.make_async_copy(src, _row(xs_hbm, dest_ref[0, 0, 2 * t + 1]), sem).start()
        return carry

    lax.fori_loop(0, DISPATCH_TOKENS, issue, 0)
    for _ in range(2):
        pltpu.make_async_copy(h_ref, xs_hbm.at[pl.ds(0, DISPATCH_TOKENS * SUBLANES), :], sem).wait()


def _dispatch(h_rows, dest):
    t = dest.shape[0] // 2
    steps = t // DISPATCH_TOKENS
    return pl.pallas_call(
        _dispatch_kernel,
        grid=(steps,),
        in_specs=[
            pl.BlockSpec((1, 1, 2 * DISPATCH_TOKENS), lambda i: (i, 0, 0), memory_space=pltpu.SMEM),
            pl.BlockSpec((DISPATCH_TOKENS * SUBLANES, LANES), lambda i: (i, 0)),
        ],
        out_specs=pl.BlockSpec(memory_space=pl.ANY),
        out_shape=jax.ShapeDtypeStruct((2 * t * SUBLANES, LANES), F32),
        scratch_shapes=[pltpu.SemaphoreType.DMA(())],
        compiler_params=_params(("arbitrary",)),
        name="moe_dispatch",
    )(dest.reshape(steps, 1, 2 * DISPATCH_TOKENS), h_rows)


def _experts_kernel(ve_ref, vt_ref, lo_ref, hi_ref, first_ref, valid_ref,
                    xs_ref, wg_ref, wu_ref, wd_ref, ys_ref, wgu_s, wd_s):
    v = pl.program_id(0)
    tm = ROW_TILE
    f = wg_ref.shape[-1]

    @pl.when(valid_ref[v] == 1)
    def _():
        changed = jnp.logical_or(v == 0, ve_ref[v] != ve_ref[jnp.maximum(v - 1, 0)])

        @pl.when(changed)
        def _():
            wgu_s[:, 0:f] = wg_ref[...].astype(BF16)
            wgu_s[:, f:2 * f] = wu_ref[...].astype(BF16)
            wd_s[...] = wd_ref[...].astype(BF16)

        x = jnp.concatenate([xs_ref[pl.ds(j, tm, stride=SUBLANES), :] for j in range(SUBLANES)], axis=-1)
        au = _dot(x.astype(BF16), wgu_s[...])
        rows = vt_ref[v] * tm + lax.broadcasted_iota(jnp.int32, (tm, 1), 0)
        mine = (rows >= lo_ref[v]) & (rows < hi_ref[v])
        mid = jnp.where(mine, jax.nn.silu(au[:, 0:f]) * au[:, f:2 * f], 0.0)
        y = _dot(mid.astype(BF16), wd_s[...])

        @pl.when(first_ref[v] == 1)
        def _():
            for j in range(SUBLANES):
                ys_ref[pl.ds(j, tm, stride=SUBLANES), :] = y[:, j * LANES:(j + 1) * LANES]

        @pl.when(first_ref[v] == 0)
        def _():
            for j in range(SUBLANES):
                ys_ref[pl.ds(j, tm, stride=SUBLANES), :] += y[:, j * LANES:(j + 1) * LANES]


def _experts(xs, sched, w_gate, w_up, w_down, layer):
    rows = xs.shape[0] // SUBLANES
    d, f = w_gate.shape[-2:]
    n_visits = sched[0].shape[0]
    blk = (ROW_TILE * SUBLANES, LANES)
    grid_spec = pltpu.PrefetchScalarGridSpec(
        num_scalar_prefetch=6,
        grid=(n_visits,),
        in_specs=[
            pl.BlockSpec(blk, lambda v, ve, vt, lo, hi, fi, va: (vt[v], 0)),
            pl.BlockSpec((None, None, d, f), lambda v, ve, vt, lo, hi, fi, va: (layer, ve[v], 0, 0)),
            pl.BlockSpec((None, None, d, f), lambda v, ve, vt, lo, hi, fi, va: (layer, ve[v], 0, 0)),
            pl.BlockSpec((None, None, f, d), lambda v, ve, vt, lo, hi, fi, va: (layer, ve[v], 0, 0)),
        ],
        out_specs=pl.BlockSpec(blk, lambda v, ve, vt, lo, hi, fi, va: (vt[v], 0)),
        scratch_shapes=[pltpu.VMEM((d, 2 * f), BF16), pltpu.VMEM((f, d), BF16)],
    )
    return pl.pallas_call(
        _experts_kernel,
        grid_spec=grid_spec,
        out_shape=jax.ShapeDtypeStruct((rows * SUBLANES, LANES), F32),
        compiler_params=_params(("arbitrary",)),
        name="moe_experts",
    )(*sched, xs, w_gate, w_up, w_down)


def _combine_kernel(dest_ref, rt_ref, x_ref, mod_ref, ys_hbm, o_ref, buf, sem, *, ctx_tiles, tiles_per_batch):
    tm = ROW_TILE
    d = x_ref.shape[-1]

    def issue(t, carry):
        for k in range(2):
            pltpu.make_async_copy(_row(ys_hbm, dest_ref[0, 0, 2 * t + k]), _row(buf, k * tm + t), sem).start()
        return carry

    lax.fori_loop(0, tm, issue, 0)

    for k in range(2):
        half = pl.ds(k * tm * SUBLANES, tm * SUBLANES)
        pltpu.make_async_copy(ys_hbm.at[half, :], buf.at[half, :], sem).wait()

    gi = pl.program_id(0)
    row = _mod_row(gi // tiles_per_batch, gi % tiles_per_batch, ctx_tiles)
    gate2 = mod_ref[pl.ds(row, 1), 5 * d:6 * d]
    rt = rt_ref[...]
    w = (rt[:, 2:3], rt[:, 3:4])
    for j in range(SUBLANES):
        y = (w[0] * buf[pl.ds(j, tm, stride=SUBLANES), :]
             + w[1] * buf[pl.ds(tm * SUBLANES + j, tm, stride=SUBLANES), :])
        sl = slice(j * LANES, (j + 1) * LANES)
        o_ref[:, sl] = x_ref[:, sl] + gate2[:, sl] * y


def _combine(ys, dest, rt, x, mod, ctx_len):
    b, r, d = x.shape
    t = b * r
    steps = t // ROW_TILE
    kern = functools.partial(_combine_kernel, ctx_tiles=ctx_len // ROW_TILE, tiles_per_batch=r // ROW_TILE)
    out = pl.pallas_call(
        kern,
        grid=(steps,),
        in_specs=[
            pl.BlockSpec((1, 1, 2 * ROW_TILE), lambda i: (i, 0, 0), memory_space=pltpu.SMEM),
            pl.BlockSpec((ROW_TILE, SUBLANES), lambda i: (i, 0)),
            pl.BlockSpec((ROW_TILE, d), lambda i: (i, 0)),
            pl.BlockSpec((MOD_ROWS, 6 * d), lambda i: (0, 0)),
            pl.BlockSpec(memory_space=pl.ANY),
        ],
        out_specs=pl.BlockSpec((ROW_TILE, d), lambda i: (i, 0)),
        out_shape=jax.ShapeDtypeStruct((t, d), F32),
        scratch_shapes=[pltpu.VMEM((2 * ROW_TILE * SUBLANES, LANES), F32), pltpu.SemaphoreType.DMA(())],
        compiler_params=_params(("arbitrary",)),
        name="moe_combine",
    )(dest.reshape(steps, 1, 2 * ROW_TILE), rt, x.reshape(t, d), mod, ys)
    return out.reshape(b, r, d)


def _route_plan(rt, n_tiles):
    eid = rt[:, 0:2].astype(jnp.int32).reshape(-1)
    onehot = (eid[:, None] == jnp.arange(N_EXPERTS, dtype=jnp.int32)[None, :]).astype(jnp.int32)
    csum = jnp.cumsum(onehot, axis=0)
    counts = csum[-1]
    cstart = jnp.cumsum(counts) - counts
    dest = jnp.sum(onehot * (cstart[None, :] + csum - 1), axis=1).astype(jnp.int32)

    tm = ROW_TILE
    first_tile = cstart // tm
    last_tile = (cstart + counts - 1) // tm
    nvis = jnp.where(counts > 0, last_tile - first_tile + 1, 0)
    vend = jnp.cumsum(nvis)
    vstart = vend - nvis
    total = vend[-1]
    n_visits = n_tiles + N_EXPERTS - 1
    v = jnp.arange(n_visits, dtype=jnp.int32)
    valid = v < total
    vc = jnp.minimum(v, total - 1)
    ve = jnp.sum((vc[:, None] >= vend[None, :]).astype(jnp.int32), axis=1)
    oh_e = (ve[:, None] == jnp.arange(N_EXPERTS, dtype=jnp.int32)[None, :]).astype(jnp.int32)
    pick = lambda a: jnp.sum(oh_e * a[None, :], axis=1)
    vt = pick(first_tile) + vc - pick(vstart)
    lo = pick(cstart)
    hi = jnp.where(valid, lo + pick(counts), lo)
    first = jnp.concatenate([jnp.ones((1,), jnp.int32), (vt[1:] != vt[:-1]).astype(jnp.int32)])
    sched = tuple(a.astype(jnp.int32) for a in (ve, vt, lo, hi, first, valid))
    return dest, sched


def _final_kernel(x_ref, g_ref, o_ref):
    o_ref[...] = _rms(x_ref[...]) * g_ref[...]


def _final_norm(x, g, ctx_len):
    b, r, d = x.shape
    ct = ctx_len // ROW_TILE
    return pl.pallas_call(
        _final_kernel,
        grid=(b, (r - ctx_len) // ROW_TILE),
        in_specs=[
            pl.BlockSpec((None, ROW_TILE, d), lambda bi, i: (bi, i + ct, 0)),
            pl.BlockSpec((1, d), lambda bi, i: (0, 0)),
        ],
        out_specs=pl.BlockSpec((None, ROW_TILE, d), lambda bi, i: (bi, i, 0)),
        out_shape=jax.ShapeDtypeStruct((b, r - ctx_len, d), F32),
        compiler_params=_params(("arbitrary", "arbitrary")),
        name="final_norm",
    )(x, g)


def kernel(x, c, ctx, c_ctx, ada_w, ada_b, norm_mix_g, norm_ffn_g, final_norm_g, na_w_qkv, na_rpb, na_w_o, da_w_qkv, da_lambda, da_subln_g, da_w_o, sw_w_qkv, sw_sinks, sw_w_o, moe_w_grp, moe_b_grp, moe_w_exp, moe_b_exp, moe_w_gate, moe_w_up, moe_w_down):
    b, seq, d = x.shape
    ctx_len = ctx.shape[1]
    depth = ada_w.shape[0]
    assert d == D_MODEL and ctx_len % ROW_TILE == 0 and seq % ROW_TILE == 0 and b <= MOD_ROWS // 2
    assert (b * (ctx_len + seq)) % DISPATCH_TOKENS == 0

    half = MOD_ROWS // 2
    cond = jnp.concatenate([jnp.pad(c, ((0, half - b), (0, 0))), jnp.pad(c_ctx[None, :], ((0, half - 1), (0, 0)))], axis=0)
    mod_all = _adaln(cond, ada_w, ada_b)
    cos, sin = _rope_tables(seq, ctx_len)
    stream = jnp.concatenate([ctx, x], axis=1)
    n_tiles = 2 * b * (ctx_len + seq) // ROW_TILE

    for i in range(depth):
        need_ctx = i < depth - 1
        kind, j = i % N_MIXERS, i // N_MIXERS
        mod = mod_all[i]
        g_mix = norm_mix_g[i].reshape(1, d)
        if kind == 0:
            qkv = _qkv(stream, g_mix, mod, na_w_qkv[j].astype(BF16), cos, sin, 0, ctx_len)
            o = _na_attention(qkv, na_rpb[j], ctx_len, need_ctx)
            wo = na_w_o[j]
        elif kind == 1:
            nq = 2 * DA_HEADS * HEAD_DIM
            qkv = _qkv(stream, g_mix, mod, da_w_qkv[j].astype(BF16), cos, sin, 2 * nq, ctx_len)
            o = _da_attention(qkv, da_lambda[j], da_subln_g[j], i, ctx_len)
            wo = da_w_o[j]
        else:
            rope_cols = (SW_Q_HEADS + SW_KV_HEADS) * HEAD_DIM
            qkv = _qkv(stream, g_mix, mod, sw_w_qkv[j].astype(BF16), cos, sin, rope_cols, ctx_len)
            o = _sw_attention(qkv, sw_sinks[j], ctx_len, need_ctx)
            wo = sw_w_o[j]

        pad = LANES - N_GROUPS - N_EXPERTS
        wr = jnp.pad(jnp.concatenate([moe_w_grp[i], moe_w_exp[i]], axis=1).astype(F32), ((0, 0), (0, pad)))
        br = jnp.pad(jnp.concatenate([moe_b_grp[i], moe_b_exp[i]]).astype(F32), (0, pad)).reshape(1, LANES)
        stream, h_rows, rt = _post(o, stream, wo.astype(BF16), mod, norm_ffn_g[i].reshape(1, d), wr, br, ctx_len)
        dest, sched = _route_plan(rt, n_tiles)
        xs = _dispatch(h_rows, dest)
        ys = _experts(xs, sched, moe_w_gate, moe_w_up, moe_w_down, i)
        stream = _combine(ys, dest, rt, stream, mod, ctx_len)

    return _final_norm(stream, final_norm_g.reshape(1, d), ctx_len)
```

```python
import functools
import math

import jax
import jax.numpy as jnp
from jax import lax
from jax.experimental import pallas as pl
from jax.experimental.pallas import tpu as pltpu

F32 = jnp.float32
BF16 = jnp.bfloat16

D_MODEL = 1024
HEAD_DIM = 64
GRID_W = 64
ROPE_AXIS_PAIRS = HEAD_DIM // 4
ROPE_BASE = 10000.0
NA_HEADS = D_MODEL // HEAD_DIM
NA_WIN_R = 8
NA_WIN_C = 16
NA_GROUP = 4
NA_UNION = NA_GROUP + NA_WIN_R
DA_HEADS = D_MODEL // (2 * HEAD_DIM)
SW_Q_HEADS = D_MODEL // HEAD_DIM
SW_KV_HEADS = 4
SW_WINDOW = 128
Q_BLOCK = 128
N_GROUPS = 4
EXPERTS_PER_GROUP = 8
N_EXPERTS = N_GROUPS * EXPERTS_PER_GROUP
D_EXPERT = 512
NORM_EPS = 1e-6
NEG_INF = -1e30
N_MIXERS = 3

LANES = 128
SUBLANES = 8
ROW_TILE = 256
MOD_ROWS = 16
VMEM_LIMIT = 56 * 1024 * 1024

LOG2E = math.log2(math.e)
Q_SCALE = HEAD_DIM ** -0.5 * LOG2E

_NT = (((1,), (1,)), ((), ()))


def _dot(a, b, **kw):
    return jnp.dot(a, b, preferred_element_type=F32, **kw)


def _dot_nt(a, b):
    return lax.dot_general(a, b, _NT, preferred_element_type=F32)


def _params(sem, vmem=VMEM_LIMIT):
    return pltpu.CompilerParams(dimension_semantics=sem, vmem_limit_bytes=vmem)


def _mod_row(b, tile, ctx_tiles):
    return jnp.where(tile < ctx_tiles, MOD_ROWS // 2, b)


def _rms(x):
    return x * lax.rsqrt(jnp.mean(x * x, axis=-1, keepdims=True) + NORM_EPS)


def _adaln_kernel(c_ref, w_ref, b_ref, o_ref):
    s = jax.nn.silu(c_ref[...])
    o_ref[...] = _dot(s, w_ref[...], precision=lax.Precision.HIGHEST) + b_ref[...]


def _adaln(cond, ada_w, ada_b):
    depth, d, n6 = ada_w.shape
    tn = 1536
    return pl.pallas_call(
        _adaln_kernel,
        grid=(depth, n6 // tn),
        in_specs=[
            pl.BlockSpec((MOD_ROWS, d), lambda l, n: (0, 0)),
            pl.BlockSpec((None, d, tn), lambda l, n: (l, 0, n)),
            pl.BlockSpec((None, 1, tn), lambda l, n: (l, 0, n)),
        ],
        out_specs=pl.BlockSpec((None, MOD_ROWS, tn), lambda l, n: (l, 0, n)),
        out_shape=jax.ShapeDtypeStruct((depth, MOD_ROWS, n6), F32),
        compiler_params=_params(("arbitrary", "arbitrary")),
        name="adaln",
    )(cond, ada_w, ada_b.reshape(depth, 1, n6))


def _qkv_kernel(x_ref, g_ref, mod_ref, w_ref, cos_ref, sin_ref, o_ref, *, rope_cols, q_cols, ctx_tiles):
    d = x_ref.shape[-1]
    row = _mod_row(pl.program_id(0), pl.program_id(1), ctx_tiles)
    shift = mod_ref[pl.ds(row, 1), 0:d]
    scale = mod_ref[pl.ds(row, 1), d:2 * d]
    h = (_rms(x_ref[...]) * g_ref[...]) * (1.0 + scale) + shift
    acc = _dot(h.astype(BF16), w_ref[...])
    n = acc.shape[-1]
    if rope_cols:
        tm = acc.shape[0]
        first_half = (lax.broadcasted_iota(jnp.int32, (tm, LANES), 1) % (2 * ROPE_AXIS_PAIRS)) < ROPE_AXIS_PAIRS
        for c in range(rope_cols // LANES):
            tab = slice(LANES, 2 * LANES) if c * LANES < q_cols else slice(0, LANES)
            t = acc[:, c * LANES:(c + 1) * LANES]
            up = pltpu.roll(t, LANES - ROPE_AXIS_PAIRS, 1)
            dn = pltpu.roll(t, ROPE_AXIS_PAIRS, 1)
            o_ref[:, c * LANES:(c + 1) * LANES] = (
                t * cos_ref[:, tab] + jnp.where(first_half, up, dn) * sin_ref[:, tab]).astype(o_ref.dtype)
    else:
        o_ref[:, 0:q_cols] = (acc[:, 0:q_cols] * Q_SCALE).astype(o_ref.dtype)
        rope_cols = q_cols
    if rope_cols < n:
        o_ref[:, rope_cols:] = acc[:, rope_cols:].astype(o_ref.dtype)


def _qkv(x, g, mod, w, cos, sin, rope_cols, q_cols, ctx_len):
    b, r, d = x.shape
    n = w.shape[1]
    kern = functools.partial(_qkv_kernel, rope_cols=rope_cols, q_cols=q_cols, ctx_tiles=ctx_len // ROW_TILE)
    return pl.pallas_call(
        kern,
        grid=(b, r // ROW_TILE),
        in_specs=[
            pl.BlockSpec((None, ROW_TILE, d), lambda bi, i: (bi, i, 0)),
            pl.BlockSpec((1, d), lambda bi, i: (0, 0)),
            pl.BlockSpec((MOD_ROWS, 6 * d), lambda bi, i: (0, 0)),
            pl.BlockSpec((d, n), lambda bi, i: (0, 0)),
            pl.BlockSpec((ROW_TILE, 2 * LANES), lambda bi, i: (i, 0)),
            pl.BlockSpec((ROW_TILE, 2 * LANES), lambda bi, i: (i, 0)),
        ],
        out_specs=pl.BlockSpec((None, ROW_TILE, n), lambda bi, i: (bi, i, 0)),
        out_shape=jax.ShapeDtypeStruct((b, r, n), BF16),
        compiler_params=_params(("arbitrary", "arbitrary")),
        name="qkv",
    )(x, g, mod, w, cos, sin)


def _rope_tables(seq, ctx_len):
    t = jnp.arange(seq, dtype=jnp.int32)
    row = (t // GRID_W).astype(F32)
    col = (t % GRID_W).astype(F32)
    inv = ROPE_BASE ** (-jnp.arange(ROPE_AXIS_PAIRS, dtype=F32) / ROPE_AXIS_PAIRS)
    ang_r = row[:, None] * inv[None, :]
    ang_c = col[:, None] * inv[None, :]
    ang = jnp.concatenate([ang_r, ang_r, ang_c, ang_c], axis=-1)
    sign = jnp.tile(jnp.concatenate([-jnp.ones((ROPE_AXIS_PAIRS,), F32), jnp.ones((ROPE_AXIS_PAIRS,), F32)]), 2)
    cos = jnp.concatenate([jnp.ones((ctx_len, HEAD_DIM), F32), jnp.cos(ang)], axis=0)
    sin = jnp.concatenate([jnp.zeros((ctx_len, HEAD_DIM), F32), jnp.sin(ang) * sign[None, :]], axis=0)
    cos = jnp.tile(cos, (1, LANES // HEAD_DIM))
    sin = jnp.tile(sin, (1, LANES // HEAD_DIM))
    return jnp.concatenate([cos, cos * Q_SCALE], axis=1), jnp.concatenate([sin, sin * Q_SCALE], axis=1)


def _head_masks(rows):
    lane = lax.broadcasted_iota(jnp.int32, (rows, LANES), 1)
    lo = lane < HEAD_DIM
    return lo, jnp.logical_not(lo)


def _na_kernel(q_ref, k_ref, v_ref, fb_ref, o_ref, *, ctx_len, grid_rows, need_ctx):
    kc = k_ref[0:ctx_len, :]
    vc = v_ref[0:ctx_len, :]

    if need_ctx:
        qc = q_ref[0:ctx_len, :]
        lo, hi = _head_masks(ctx_len)
        outs = []
        for m in (lo, hi):
            s = _dot_nt(jnp.where(m, qc, 0), kc)
            e = jnp.exp2(s - jnp.max(s, axis=-1, keepdims=True))
            outs.append(_dot(e.astype(BF16), vc) / jnp.sum(e, axis=-1, keepdims=True))
        o_ref[0:ctx_len, :] = jnp.where(lo, outs[0], outs[1]).astype(o_ref.dtype)
    else:
        o_ref[0:ctx_len, :] = jnp.zeros((ctx_len, LANES), o_ref.dtype)

    gq = NA_GROUP * GRID_W
    uk = NA_UNION * GRID_W
    n_groups = grid_rows // NA_GROUP
    lo, hi = _head_masks(gq)

    def group_body(g, carry):
        u0 = jnp.clip(g * NA_GROUP - NA_WIN_R // 2, 0, grid_rows - NA_UNION)
        q0 = pl.multiple_of(ctx_len + g * gq, gq)
        w0 = pl.multiple_of(ctx_len + u0 * GRID_W, GRID_W)
        pattern = jnp.where(g == 0, 0, jnp.where(g == n_groups - 1, 2, 1))
        q_g = q_ref[pl.ds(q0, gq), :]
        k_u = k_ref[pl.ds(w0, uk), :]
        v_u = v_ref[pl.ds(w0, uk), :]
        outs = []
        for j, m in enumerate((lo, hi)):
            qj = jnp.where(m, q_g, 0)
            s_nb = _dot_nt(qj, k_u) + fb_ref[j, pattern]
            s_cx = _dot_nt(qj, kc)
            mx = jnp.maximum(jnp.max(s_nb, axis=-1, keepdims=True), jnp.max(s_cx, axis=-1, keepdims=True))
            e_nb = jnp.exp2(s_nb - mx)
            e_cx = jnp.exp2(s_cx - mx)
            den = jnp.sum(e_nb, axis=-1, keepdims=True) + jnp.sum(e_cx, axis=-1, keepdims=True)
            outs.append((_dot(e_cx.astype(BF16), vc) + _dot(e_nb.astype(BF16), v_u)) / den)
        o_ref[pl.ds(q0, gq), :] = jnp.where(lo, outs[0], outs[1]).astype(o_ref.dtype)
        return carry

    lax.fori_loop(0, n_groups, group_body, 0)


def _na_bias_table(rpb, grid_rows):
    n_groups = grid_rows // NA_GROUP
    assert grid_rows % NA_GROUP == 0 and n_groups >= 3 and grid_rows >= NA_UNION
    qcol = jnp.arange(GRID_W)
    kcol = jnp.arange(GRID_W)
    c0 = jnp.clip(qcol - NA_WIN_C // 2, 0, GRID_W - NA_WIN_C)
    col_ok = (kcol[None, :] >= c0[:, None]) & (kcol[None, :] < c0[:, None] + NA_WIN_C)
    rel = kcol[None, :] - qcol[:, None] + NA_WIN_C - 1
    onehot = ((rel[:, :, None] == jnp.arange(2 * NA_WIN_C - 1)[None, None, :]) & col_ok[:, :, None]).astype(F32)
    bt = jnp.einsum('hyd,qkd->hyqk', rpb.astype(F32), onehot, precision=lax.Precision.HIGHEST)
    bt = jnp.where(col_ok[None, None], bt * LOG2E, NEG_INF)
    masked = jnp.full((rpb.shape[0], GRID_W, GRID_W), NEG_INF, F32)

    def group_table(g):
        u0 = min(max(g * NA_GROUP - NA_WIN_R // 2, 0), grid_rows - NA_UNION)
        rows = []
        for a in range(NA_GROUP):
            r = g * NA_GROUP + a
            r0 = min(max(r - NA_WIN_R // 2, 0), grid_rows - NA_WIN_R)
            blocks = [bt[:, u0 + y - r + NA_WIN_R - 1] if r0 <= u0 + y < r0 + NA_WIN_R else masked
                      for y in range(NA_UNION)]
            rows.append(jnp.concatenate(blocks, axis=-1))
        return jnp.concatenate(rows, axis=-2)

    for g in range(1, n_groups - 1):
        assert min(max(g * NA_GROUP - NA_WIN_R // 2, 0), grid_rows - NA_UNION) == g * NA_GROUP - NA_WIN_R // 2
        assert 0 <= g * NA_GROUP - NA_WIN_R // 2 and (g + 1) * NA_GROUP - 1 - NA_WIN_R // 2 <= grid_rows - NA_WIN_R
    return jnp.stack([group_table(0), group_table(1), group_table(n_groups - 1)], axis=1)


def _na_attention(qkv, rpb, ctx_len, need_ctx):
    b, r, _ = qkv.shape
    grid_rows = (r - ctx_len) // GRID_W
    nblk = D_MODEL // LANES
    fb = _na_bias_table(rpb, grid_rows)
    kern = functools.partial(_na_kernel, ctx_len=ctx_len, grid_rows=grid_rows, need_ctx=need_ctx)
    return pl.pallas_call(
        kern,
        grid=(nblk, b),
        in_specs=[
            pl.BlockSpec((None, r, LANES), lambda hp, bi: (bi, 0, hp)),
            pl.BlockSpec((None, r, LANES), lambda hp, bi: (bi, 0, nblk + hp)),
            pl.BlockSpec((None, r, LANES), lambda hp, bi: (bi, 0, 2 * nblk + hp)),
            pl.BlockSpec((2, 3, NA_GROUP * GRID_W, NA_UNION * GRID_W), lambda hp, bi: (hp, 0, 0, 0)),
        ],
        out_specs=pl.BlockSpec((None, r, LANES), lambda hp, bi: (bi, 0, hp)),
        out_shape=jax.ShapeDtypeStruct((b, r, D_MODEL), BF16),
        compiler_params=_params(("arbitrary", "arbitrary")),
        name="na_attn",
    )(qkv, qkv, qkv, fb)


def _da_kernel(q_ref, k_ref, v_ref, lam_ref, g_ref, o_ref, *, ctx_len, lam_init):
    lp = lam_ref[...]
    lam = (jnp.exp(jnp.sum(lp[0:1] * lp[1:2], axis=-1, keepdims=True))
           - jnp.exp(jnp.sum(lp[2:3] * lp[3:4], axis=-1, keepdims=True)) + lam_init)
    q = q_ref[...]
    lo, hi = _head_masks(q.shape[0])

    def attend(k, v):
        es, ls = [], []
        for m in (lo, hi):
            s = _dot_nt(jnp.where(m, q, 0), k)
            e = jnp.exp2(s - jnp.max(s, axis=-1, keepdims=True))
            es.append(e)
            ls.append(jnp.sum(e, axis=-1, keepdims=True))
        a = es[0] - (lam * ls[0] / ls[1]) * es[1]
        o = _dot(a.astype(BF16), v) * (1.0 / ls[0])
        o_ref[...] = (_rms(o) * g_ref[...] * (1.0 - lam_init)).astype(o_ref.dtype)

    is_ctx = pl.program_id(2) < ctx_len // ROW_TILE

    @pl.when(is_ctx)
    def _():
        attend(k_ref[0:ctx_len, :], v_ref[0:ctx_len, :])

    @pl.when(jnp.logical_not(is_ctx))
    def _():
        attend(k_ref[...], v_ref[...])


def _da_attention(qkv, lam_params, subln_g, layer_idx, ctx_len):
    b, r, _ = qkv.shape
    nblk = D_MODEL // LANES
    lam_init = 0.8 - 0.6 * math.exp(-0.3 * layer_idx)
    kern = functools.partial(_da_kernel, ctx_len=ctx_len, lam_init=lam_init)
    return pl.pallas_call(
        kern,
        grid=(nblk, b, r // ROW_TILE),
        in_specs=[
            pl.BlockSpec((None, ROW_TILE, LANES), lambda h, bi, i: (bi, i, h)),
            pl.BlockSpec((None, r, LANES), lambda h, bi, i: (bi, 0, nblk + h)),
            pl.BlockSpec((None, r, LANES), lambda h, bi, i: (bi, 0, 2 * nblk + h)),
            pl.BlockSpec((4, HEAD_DIM), lambda h, bi, i: (0, 0)),
            pl.BlockSpec((1, 2 * HEAD_DIM), lambda h, bi, i: (0, 0)),
        ],
        out_specs=pl.BlockSpec((None, ROW_TILE, LANES), lambda h, bi, i: (bi, i, h)),
        out_shape=jax.ShapeDtypeStruct((b, r, D_MODEL), BF16),
        compiler_params=_params(("arbitrary", "arbitrary", "arbitrary")),
        name="da_attn",
    )(qkv, qkv, qkv, lam_params.astype(F32), subln_g.astype(F32).reshape(1, -1))


def _sw_kernel(sink_ref, q_ref, k_ref, v_ref, o_ref, kd_ref, vd_ref, *, ctx_len, seq, need_ctx):
    kvp = pl.program_id(0)
    band = Q_BLOCK + 2 * SW_WINDOW
    hpk = SW_Q_HEADS // SW_KV_HEADS

    k = k_ref[...]
    v = v_ref[...]
    for hh in range(2):
        sl = slice(hh * HEAD_DIM, (hh + 1) * HEAD_DIM)
        kd_ref[hh] = jnp.concatenate([k[:, sl], k[:, sl]], axis=-1)
        vd_ref[hh] = jnp.concatenate([v[:, sl], v[:, sl]], axis=-1)

    def stacked_q(row0, rows, hh):
        lo, hi = _head_masks(rows)
        parts = []
        for qb in range(hpk // 2):
            blk = hh * (hpk // 2) + qb
            qv = q_ref[pl.ds(row0, rows), blk * LANES:(blk + 1) * LANES]
            parts += [jnp.where(lo, qv, 0), jnp.where(hi, qv, 0)]
        return jnp.concatenate(parts, axis=0)

    def sink_col(rows, hh):
        base = kvp * 2 * hpk + hh * hpk
        return jnp.concatenate([jnp.full((rows, 1), sink_ref[base + i] * LOG2E, F32) for i in range(hpk)], axis=0)

    def write_out(row0, rows, hh, o):
        lo, _ = _head_masks(rows)
        for qb in range(hpk // 2):
            blk = hh * (hpk // 2) + qb
            o_lo = o[(2 * qb) * rows:(2 * qb + 1) * rows]
            o_hi = o[(2 * qb + 1) * rows:(2 * qb + 2) * rows]
            o_ref[pl.ds(row0, rows), blk * LANES:(blk + 1) * LANES] = jnp.where(lo, o_lo, o_hi).astype(o_ref.dtype)

    if need_ctx:
        for hh in range(2):
            sink = sink_col(ctx_len, hh)
            s = _dot_nt(stacked_q(0, ctx_len, hh), kd_ref[hh, 0:ctx_len, :])
            mx = jnp.maximum(jnp.max(s, axis=-1, keepdims=True), sink)
            e = jnp.exp2(s - mx)
            den = jnp.sum(e, axis=-1, keepdims=True) + jnp.exp2(sink - mx)
            write_out(0, ctx_len, hh, _dot(e.astype(BF16), vd_ref[hh, 0:ctx_len, :]) / den)
    else:
        o_ref[0:ctx_len, :] = jnp.zeros((ctx_len, o_ref.shape[-1]), o_ref.dtype)

    def block_body(n, carry):
        start = n * Q_BLOCK
        b0 = jnp.clip(start - SW_WINDOW, 0, seq - band)
        qrow = pl.multiple_of(ctx_len + start, Q_BLOCK)
        brow = pl.multiple_of(ctx_len + b0, Q_BLOCK)
        qpos = start + (lax.broadcasted_iota(jnp.int32, (hpk * Q_BLOCK, band), 0) & (Q_BLOCK - 1))
        kpos = b0 + lax.broadcasted_iota(jnp.int32, (hpk * Q_BLOCK, band), 1)
        in_band = jnp.abs(kpos - qpos) <= SW_WINDOW
        for hh in range(2):
            sink = sink_col(Q_BLOCK, hh)
            qs = stacked_q(qrow, Q_BLOCK, hh)
            s_l = jnp.where(in_band, _dot_nt(qs, kd_ref[hh, pl.ds(brow, band), :]), NEG_INF)
            s_c = _dot_nt(qs, kd_ref[hh, 0:ctx_len, :])
            mx = jnp.maximum(jnp.maximum(jnp.max(s_l, axis=-1, keepdims=True),
                                         jnp.max(s_c, axis=-1, keepdims=True)), sink)
            e_l = jnp.exp2(s_l - mx)
            e_c = jnp.exp2(s_c - mx)
            den = (jnp.sum(e_l, axis=-1, keepdims=True) + jnp.sum(e_c, axis=-1, keepdims=True)
                   + jnp.exp2(sink - mx))
            o = (_dot(e_c.astype(BF16), vd_ref[hh, 0:ctx_len, :])
                 + _dot(e_l.astype(BF16), vd_ref[hh, pl.ds(brow, band), :])) / den
            write_out(qrow, Q_BLOCK, hh, o)
        return carry

    lax.fori_loop(0, seq // Q_BLOCK, block_body, 0)


def _sw_attention(qkv, sinks, ctx_len, need_ctx):
    b, r, _ = qkv.shape
    seq = r - ctx_len
    nq_blk = SW_Q_HEADS * HEAD_DIM // LANES
    slab = 2 * (SW_Q_HEADS // SW_KV_HEADS) * HEAD_DIM
    kern = functools.partial(_sw_kernel, ctx_len=ctx_len, seq=seq, need_ctx=need_ctx)
    n_kv_blk = SW_KV_HEADS * HEAD_DIM // LANES
    return pl.pallas_call(
        kern,
        grid=(n_kv_blk, b),
        in_specs=[
            pl.BlockSpec(memory_space=pltpu.SMEM),
            pl.BlockSpec((None, r, slab), lambda kp, bi: (bi, 0, kp)),
            pl.BlockSpec((None, r, LANES), lambda kp, bi: (bi, 0, nq_blk + kp)),
            pl.BlockSpec((None, r, LANES), lambda kp, bi: (bi, 0, nq_blk + n_kv_blk + kp)),
        ],
        out_specs=pl.BlockSpec((None, r, slab), lambda kp, bi: (bi, 0, kp)),
        out_shape=jax.ShapeDtypeStruct((b, r, D_MODEL), BF16),
        scratch_shapes=[pltpu.VMEM((2, r, LANES), BF16), pltpu.VMEM((2, r, LANES), BF16)],
        compiler_params=_params(("arbitrary", "arbitrary")),
        name="sw_attn",
    )(sinks.astype(F32), qkv, qkv, qkv)


def _post_kernel(o_ref, x_ref, wo_ref, mod_ref, g_ref, wrh_ref, wrl_ref, br_ref, xn_ref, h_ref, rt_ref, *, ctx_tiles):
    d = x_ref.shape[-1]
    tm = x_ref.shape[0]
    row = _mod_row(pl.program_id(0), pl.program_id(1), ctx_tiles)
    gate1 = mod_ref[pl.ds(row, 1), 2 * d:3 * d]
    shift2 = mod_ref[pl.ds(row, 1), 3 * d:4 * d]
    scale2 = mod_ref[pl.ds(row, 1), 4 * d:5 * d]
    xn = x_ref[...] + gate1 * _dot(o_ref[...], wo_ref[...])
    xn_ref[...] = xn
    h = (_rms(xn) * g_ref[...]) * (1.0 + scale2) + shift2
    for j in range(SUBLANES):
        h_ref[pl.ds(j, tm, stride=SUBLANES), :] = h[:, j * LANES:(j + 1) * LANES]

    h_hi = h.astype(BF16)
    h_lo = (h - h_hi.astype(F32)).astype(BF16)
    logits = (_dot(h_hi, wrh_ref[...]) + _dot(h_lo, wrh_ref[...]) + _dot(h_hi, wrl_ref[...])) + br_ref[...]
    lane = lax.broadcasted_iota(jnp.int32, (tm, LANES), 1).astype(F32)

    def first_argmax(vals, mx):
        return jnp.min(jnp.where(vals == mx, lane, float(LANES)), axis=-1, keepdims=True)

    gl = jnp.where(lane < N_GROUPS, logits, -jnp.inf)
    ge = jnp.exp(gl - jnp.max(gl, axis=-1, keepdims=True))
    gp = ge / jnp.sum(ge, axis=-1, keepdims=True)
    g_p = jnp.max(gp, axis=-1, keepdims=True)
    g_idx = first_argmax(gp, g_p)
    e_lo = N_GROUPS + EXPERTS_PER_GROUP * g_idx
    el = jnp.where((lane >= e_lo) & (lane < e_lo + EXPERTS_PER_GROUP), logits, -jnp.inf)
    v1 = jnp.max(el, axis=-1, keepdims=True)
    i1 = first_argmax(el, v1)
    el2 = jnp.where(lane == i1, -jnp.inf, el)
    v2 = jnp.max(el2, axis=-1, keepdims=True)
    i2 = first_argmax(el2, v2)
    e2 = jnp.exp(v2 - v1)
    w1 = g_p * (1.0 / (1.0 + e2))
    w2 = g_p * (e2 / (1.0 + e2))
    lane8 = lax.broadcasted_iota(jnp.int32, (tm, SUBLANES), 1)
    rt = jnp.where(lane8 == 0, i1 - N_GROUPS,
                   jnp.where(lane8 == 1, i2 - N_GROUPS,
                             jnp.where(lane8 == 2, w1, jnp.where(lane8 == 3, w2, 0.0))))
    rt_ref[...] = rt


def _post(o, x, wo, mod, g, wr, br, ctx_len):
    wr_hi = wr.astype(BF16)
    wr_lo = (wr - wr_hi.astype(F32)).astype(BF16)
    b, r, d = x.shape
    nt = r // ROW_TILE
    kern = functools.partial(_post_kernel, ctx_tiles=ctx_len // ROW_TILE)
    return pl.pallas_call(
        kern,
        grid=(b, nt),
        in_specs=[
            pl.BlockSpec((None, ROW_TILE, d), lambda bi, i: (bi, i, 0)),
            pl.BlockSpec((None, ROW_TILE, d), lambda bi, i: (bi, i, 0)),
            pl.BlockSpec((d, d), lambda bi, i: (0, 0)),
            pl.BlockSpec((MOD_ROWS, 6 * d), lambda bi, i: (0, 0)),
            pl.BlockSpec((1, d), lambda bi, i: (0, 0)),
            pl.BlockSpec((d, LANES), lambda bi, i: (0, 0)),
            pl.BlockSpec((d, LANES), lambda bi, i: (0, 0)),
            pl.BlockSpec((1, LANES), lambda bi, i: (0, 0)),
        ],
        out_specs=[
            pl.BlockSpec((None, ROW_TILE, d), lambda bi, i: (bi, i, 0)),
            pl.BlockSpec((ROW_TILE * SUBLANES, LANES), lambda bi, i: (bi * nt + i, 0)),
            pl.BlockSpec((ROW_TILE, SUBLANES), lambda bi, i: (bi * nt + i, 0)),
        ],
        out_shape=[
            jax.ShapeDtypeStruct((b, r, d), F32),
            jax.ShapeDtypeStruct((b * r * SUBLANES, LANES), F32),
            jax.ShapeDtypeStruct((b * r, SUBLANES), F32),
        ],
        compiler_params=_params(("arbitrary", "arbitrary")),
        name="post_attn",
    )(o, x, wo, mod, g, wr_hi, wr_lo, br)


DISPATCH_TOKENS = 512


def _row(ref, idx):
    return ref.at[pl.ds(pl.multiple_of(idx * SUBLANES, SUBLANES), SUBLANES), :]


def _dispatch_kernel(dest_ref, h_ref, xs_hbm, sem):
    def issue(t, carry):
        src = _row(h_ref, t)
        for k in range(2):
            pltpu.make_async_copy(src, _row(xs_hbm, dest_ref[0, 0, 2 * t + k]), sem).start(priority=k)
        return carry

    lax.fori_loop(0, DISPATCH_TOKENS, issue, 0)
    for _ in range(2):
        pltpu.make_async_copy(h_ref, xs_hbm.at[pl.ds(0, DISPATCH_TOKENS * SUBLANES), :], sem).wait()


def _dispatch(h_rows, dest):
    t = dest.shape[0] // 2
    steps = t // DISPATCH_TOKENS
    return pl.pallas_call(
        _dispatch_kernel,
        grid=(steps,),
        in_specs=[
            pl.BlockSpec((1, 1, 2 * DISPATCH_TOKENS), lambda i: (i, 0, 0), memory_space=pltpu.SMEM),
            pl.BlockSpec((DISPATCH_TOKENS * SUBLANES, LANES), lambda i: (i, 0)),
        ],
        out_specs=pl.BlockSpec(memory_space=pl.ANY),
        out_shape=jax.ShapeDtypeStruct((2 * t * SUBLANES, LANES), F32),
        scratch_shapes=[pltpu.SemaphoreType.DMA(())],
        compiler_params=_params(("arbitrary",)),
        name="moe_dispatch",
    )(dest.reshape(steps, 1, 2 * DISPATCH_TOKENS), h_rows)


def _experts_kernel(ve_ref, vt_ref, lo_ref, hi_ref, first_ref, valid_ref,
                    xs_ref, wg_ref, wu_ref, wd_ref, ys_ref, wgu_s, wd_s):
    v = pl.program_id(0)
    tm = ROW_TILE
    f = wg_ref.shape[-1]

    @pl.when(valid_ref[v] == 1)
    def _():
        changed = jnp.logical_or(v == 0, ve_ref[v] != ve_ref[jnp.maximum(v - 1, 0)])

        @pl.when(changed)
        def _():
            wgu_s[:, 0:f] = wg_ref[...].astype(BF16)
            wgu_s[:, f:2 * f] = wu_ref[...].astype(BF16)
            wd_s[...] = wd_ref[...].astype(BF16)

        x = jnp.concatenate([xs_ref[pl.ds(j, tm, stride=SUBLANES), :] for j in range(SUBLANES)], axis=-1)
        au = _dot(x.astype(BF16), wgu_s[...])
        rows = vt_ref[v] * tm + lax.broadcasted_iota(jnp.int32, (tm, 1), 0)
        mine = (rows >= lo_ref[v]) & (rows < hi_ref[v])
        mid = jnp.where(mine, jax.nn.silu(au[:, 0:f]) * au[:, f:2 * f], 0.0)
        y = _dot(mid.astype(BF16), wd_s[...])

        @pl.when(first_ref[v] == 1)
        def _():
            for j in range(SUBLANES):
                ys_ref[pl.ds(j, tm, stride=SUBLANES), :] = y[:, j * LANES:(j + 1) * LANES]

        @pl.when(first_ref[v] == 0)
        def _():
            for j in range(SUBLANES):
                ys_ref[pl.ds(j, tm, stride=SUBLANES), :] += y[:, j * LANES:(j + 1) * LANES]


def _experts(xs, sched, w_gate, w_up, w_down, layer):
    rows = xs.shape[0] // SUBLANES
    d, f = w_gate.shape[-2:]
    n_visits = sched[0].shape[0]
    blk = (ROW_TILE * SUBLANES, LANES)
    grid_spec = pltpu.PrefetchScalarGridSpec(
        num_scalar_prefetch=6,
        grid=(n_visits,),
        in_specs=[
            pl.BlockSpec(blk, lambda v, ve, vt, lo, hi, fi, va: (vt[v], 0)),
            pl.BlockSpec((None, None, d, f), lambda v, ve, vt, lo, hi, fi, va: (layer, ve[v], 0, 0)),
            pl.BlockSpec((None, None, d, f), lambda v, ve, vt, lo, hi, fi, va: (layer, ve[v], 0, 0)),
            pl.BlockSpec((None, None, f, d), lambda v, ve, vt, lo, hi, fi, va: (layer, ve[v], 0, 0)),
        ],
        out_specs=pl.BlockSpec(blk, lambda v, ve, vt, lo, hi, fi, va: (vt[v], 0)),
        scratch_shapes=[pltpu.VMEM((d, 2 * f), BF16), pltpu.VMEM((f, d), BF16)],
    )
    return pl.pallas_call(
        _experts_kernel,
        grid_spec=grid_spec,
        out_shape=jax.ShapeDtypeStruct((rows * SUBLANES, LANES), F32),
        compiler_params=_params(("arbitrary",)),
        name="moe_experts",
    )(*sched, xs, w_gate, w_up, w_down)


def _combine_kernel(dest_ref, rt_ref, x_ref, mod_ref, ys_hbm, o_ref, buf, sem, *, ctx_tiles, tiles_per_batch):
    tm = ROW_TILE
    d = x_ref.shape[-1]

    def issue(t, carry):
        for k in range(2):
            pltpu.make_async_copy(_row(ys_hbm, dest_ref[0, 0, 2 * t + k]), _row(buf, k * tm + t), sem).start(priority=k)
        return carry

    lax.fori_loop(0, tm, issue, 0)

    for k in range(2):
        half = pl.ds(k * tm * SUBLANES, tm * SUBLANES)
        pltpu.make_async_copy(ys_hbm.at[half, :], buf.at[half, :], sem).wait()

    gi = pl.program_id(0)
    row = _mod_row(gi // tiles_per_batch, gi % tiles_per_batch, ctx_tiles)
    gate2 = mod_ref[pl.ds(row, 1), 5 * d:6 * d]
    rt = rt_ref[...]
    w = (rt[:, 2:3], rt[:, 3:4])
    for j in range(SUBLANES):
        y = (w[0] * buf[pl.ds(j, tm, stride=SUBLANES), :]
             + w[1] * buf[pl.ds(tm * SUBLANES + j, tm, stride=SUBLANES), :])
        sl = slice(j * LANES, (j + 1) * LANES)
        o_ref[:, sl] = x_ref[:, sl] + gate2[:, sl] * y


def _combine(ys, dest, rt, x, mod, ctx_len):
    b, r, d = x.shape
    t = b * r
    steps = t // ROW_TILE
    kern = functools.partial(_combine_kernel, ctx_tiles=ctx_len // ROW_TILE, tiles_per_batch=r // ROW_TILE)
    out = pl.pallas_call(
        kern,
        grid=(steps,),
        in_specs=[
            pl.BlockSpec((1, 1, 2 * ROW_TILE), lambda i: (i, 0, 0), memory_space=pltpu.SMEM),
            pl.BlockSpec((ROW_TILE, SUBLANES), lambda i: (i, 0)),
            pl.BlockSpec((ROW_TILE, d), lambda i: (i, 0)),
            pl.BlockSpec((MOD_ROWS, 6 * d), lambda i: (0, 0)),
            pl.BlockSpec(memory_space=pl.ANY),
        ],
        out_specs=pl.BlockSpec((ROW_TILE, d), lambda i: (i, 0)),
        out_shape=jax.ShapeDtypeStruct((t, d), F32),
        scratch_shapes=[pltpu.VMEM((2 * ROW_TILE * SUBLANES, LANES), F32), pltpu.SemaphoreType.DMA(())],
        compiler_params=_params(("arbitrary",)),
        name="moe_combine",
    )(dest.reshape(steps, 1, 2 * ROW_TILE), rt, x.reshape(t, d), mod, ys)
    return out.reshape(b, r, d)


def _route_plan(rt, n_tiles):
    eid = rt[:, 0:2].astype(jnp.int32).reshape(-1)
    onehot = (eid[:, None] == jnp.arange(N_EXPERTS, dtype=jnp.int32)[None, :]).astype(jnp.int32)
    csum = jnp.cumsum(onehot, axis=0)
    counts = csum[-1]
    cstart = jnp.cumsum(counts) - counts
    dest = jnp.sum(onehot * (cstart[None, :] + csum - 1), axis=1).astype(jnp.int32)

    tm = ROW_TILE
    first_tile = cstart // tm
    last_tile = (cstart + counts - 1) // tm
    nvis = jnp.where(counts > 0, last_tile - first_tile + 1, 0)
    vend = jnp.cumsum(nvis)
    vstart = vend - nvis
    total = vend[-1]
    n_visits = n_tiles + N_EXPERTS - 1
    v = jnp.arange(n_visits, dtype=jnp.int32)
    valid = v < total
    vc = jnp.minimum(v, total - 1)
    ve = jnp.sum((vc[:, None] >= vend[None, :]).astype(jnp.int32), axis=1)
    oh_e = (ve[:, None] == jnp.arange(N_EXPERTS, dtype=jnp.int32)[None, :]).astype(jnp.int32)
    pick = lambda a: jnp.sum(oh_e * a[None, :], axis=1)
    vt = pick(first_tile) + vc - pick(vstart)
    lo = pick(cstart)
    hi = jnp.where(valid, lo + pick(counts), lo)
    first = jnp.concatenate([jnp.ones((1,), jnp.int32), (vt[1:] != vt[:-1]).astype(jnp.int32)])
    sched = tuple(a.astype(jnp.int32) for a in (ve, vt, lo, hi, first, valid))
    return dest, sched


def _final_kernel(x_ref, g_ref, o_ref):
    o_ref[...] = _rms(x_ref[...]) * g_ref[...]


def _final_norm(x, g, ctx_len):
    b, r, d = x.shape
    ct = ctx_len // ROW_TILE
    return pl.pallas_call(
        _final_kernel,
        grid=(b, (r - ctx_len) // ROW_TILE),
        in_specs=[
            pl.BlockSpec((None, ROW_TILE, d), lambda bi, i: (bi, i + ct, 0)),
            pl.BlockSpec((1, d), lambda bi, i: (0, 0)),
        ],
        out_specs=pl.BlockSpec((None, ROW_TILE, d), lambda bi, i: (bi, i, 0)),
        out_shape=jax.ShapeDtypeStruct((b, r - ctx_len, d), F32),
        compiler_params=_params(("arbitrary", "arbitrary")),
        name="final_norm",
    )(x, g)


def kernel(x, c, ctx, c_ctx, ada_w, ada_b, norm_mix_g, norm_ffn_g, final_norm_g, na_w_qkv, na_rpb, na_w_o, da_w_qkv, da_lambda, da_subln_g, da_w_o, sw_w_qkv, sw_sinks, sw_w_o, moe_w_grp, moe_b_grp, moe_w_exp, moe_b_exp, moe_w_gate, moe_w_up, moe_w_down):
    b, seq, d = x.shape
    ctx_len = ctx.shape[1]
    depth = ada_w.shape[0]
    assert d == D_MODEL and ctx_len % ROW_TILE == 0 and seq % ROW_TILE == 0 and b <= MOD_ROWS // 2
    assert (b * (ctx_len + seq)) % DISPATCH_TOKENS == 0

    half = MOD_ROWS // 2
    cond = jnp.concatenate([jnp.pad(c, ((0, half - b), (0, 0))), jnp.pad(c_ctx[None, :], ((0, half - 1), (0, 0)))], axis=0)
    mod_all = _adaln(cond, ada_w, ada_b)
    cos, sin = _rope_tables(seq, ctx_len)
    stream = jnp.concatenate([ctx, x], axis=1)
    n_tiles = 2 * b * (ctx_len + seq) // ROW_TILE

    for i in range(depth):
        need_ctx = i < depth - 1
        kind, j = i % N_MIXERS, i // N_MIXERS
        mod = mod_all[i]
        g_mix = norm_mix_g[i].reshape(1, d)
        if kind == 0:
            qkv = _qkv(stream, g_mix, mod, na_w_qkv[j].astype(BF16), cos, sin, 0, NA_HEADS * HEAD_DIM, ctx_len)
            o = _na_attention(qkv, na_rpb[j], ctx_len, need_ctx)
            wo = na_w_o[j]
        elif kind == 1:
            nq = 2 * DA_HEADS * HEAD_DIM
            qkv = _qkv(stream, g_mix, mod, da_w_qkv[j].astype(BF16), cos, sin, 2 * nq, nq, ctx_len)
            o = _da_attention(qkv, da_lambda[j], da_subln_g[j], i, ctx_len)
            wo = da_w_o[j]
        else:
            rope_cols = (SW_Q_HEADS + SW_KV_HEADS) * HEAD_DIM
            qkv = _qkv(stream, g_mix, mod, sw_w_qkv[j].astype(BF16), cos, sin, rope_cols, SW_Q_HEADS * HEAD_DIM, ctx_len)
            o = _sw_attention(qkv, sw_sinks[j], ctx_len, need_ctx)
            wo = sw_w_o[j]

        pad = LANES - N_GROUPS - N_EXPERTS
        wr = jnp.pad(jnp.concatenate([moe_w_grp[i], moe_w_exp[i]], axis=1).astype(F32), ((0, 0), (0, pad)))
        br = jnp.pad(jnp.concatenate([moe_b_grp[i], moe_b_exp[i]]).astype(F32), (0, pad)).reshape(1, LANES)
        stream, h_rows, rt = _post(o, stream, wo.astype(BF16), mod, norm_ffn_g[i].reshape(1, d), wr, br, ctx_len)
        dest, sched = _route_plan(rt, n_tiles)
        xs = _dispatch(h_rows, dest)
        ys = _experts(xs, sched, moe_w_gate, moe_w_up, moe_w_down, i)
        stream = _combine(ys, dest, rt, stream, mod, ctx_len)

    return _final_norm(stream, final_norm_g.reshape(1, d), ctx_len)
```

```python
import functools
import math

import jax
import jax.numpy as jnp
from jax import lax
from jax.experimental import pallas as pl
from jax.experimental.pallas import tpu as pltpu

F32 = jnp.float32
BF16 = jnp.bfloat16

D_MODEL = 1024
HEAD_DIM = 64
GRID_W = 64
ROPE_AXIS_PAIRS = HEAD_DIM // 4
ROPE_BASE = 10000.0
NA_HEADS = D_MODEL // HEAD_DIM
NA_WIN_R = 8
NA_WIN_C = 16
NA_GROUP = 4
NA_UNION = NA_GROUP + NA_WIN_R
DA_HEADS = D_MODEL // (2 * HEAD_DIM)
SW_Q_HEADS = D_MODEL // HEAD_DIM
SW_KV_HEADS = 4
SW_WINDOW = 128
Q_BLOCK = 128
N_GROUPS = 4
EXPERTS_PER_GROUP = 8
N_EXPERTS = N_GROUPS * EXPERTS_PER_GROUP
D_EXPERT = 512
NORM_EPS = 1e-6
NEG_INF = -1e30
N_MIXERS = 3

LANES = 128
SUBLANES = 8
ROW_TILE = 256
MOD_ROWS = 16
VMEM_LIMIT = 56 * 1024 * 1024

LOG2E = math.log2(math.e)
Q_SCALE = HEAD_DIM ** -0.5 * LOG2E

_NT = (((1,), (1,)), ((), ()))


def _dot(a, b, **kw):
    return jnp.dot(a, b, preferred_element_type=F32, **kw)


def _dot_nt(a, b):
    return lax.dot_general(a, b, _NT, preferred_element_type=F32)


def _params(sem, vmem=VMEM_LIMIT):
    return pltpu.CompilerParams(dimension_semantics=sem, vmem_limit_bytes=vmem)


def _mod_row(b, tile, ctx_tiles):
    return jnp.where(tile < ctx_tiles, MOD_ROWS // 2, b)


def _rms(x):
    return x * lax.rsqrt(jnp.mean(x * x, axis=-1, keepdims=True) + NORM_EPS)


def _adaln_kernel(c_ref, w_ref, b_ref, o_ref):
    s = jax.nn.silu(c_ref[...])
    o_ref[...] = _dot(s, w_ref[...], precision=lax.Precision.HIGHEST) + b_ref[...]


def _adaln(cond, ada_w, ada_b):
    depth, d, n6 = ada_w.shape
    tn = 1536
    return pl.pallas_call(
        _adaln_kernel,
        grid=(depth, n6 // tn),
        in_specs=[
            pl.BlockSpec((MOD_ROWS, d), lambda l, n: (0, 0)),
            pl.BlockSpec((None, d, tn), lambda l, n: (l, 0, n)),
            pl.BlockSpec((None, 1, tn), lambda l, n: (l, 0, n)),
        ],
        out_specs=pl.BlockSpec((None, MOD_ROWS, tn), lambda l, n: (l, 0, n)),
        out_shape=jax.ShapeDtypeStruct((depth, MOD_ROWS, n6), F32),
        compiler_params=_params(("arbitrary", "arbitrary")),
        name="adaln",
    )(cond, ada_w, ada_b.reshape(depth, 1, n6))


def _qkv_kernel(x_ref, g_ref, mod_ref, w_ref, cos_ref, sin_ref, o_ref, *, rope_cols, q_cols, ctx_tiles):
    d = x_ref.shape[-1]
    row = _mod_row(pl.program_id(0), pl.program_id(1), ctx_tiles)
    shift = mod_ref[pl.ds(row, 1), 0:d]
    scale = mod_ref[pl.ds(row, 1), d:2 * d]
    h = (_rms(x_ref[...]) * g_ref[...]) * (1.0 + scale) + shift
    acc = _dot(h.astype(BF16), w_ref[...])
    n = acc.shape[-1]
    if rope_cols:
        tm = acc.shape[0]
        first_half = (lax.broadcasted_iota(jnp.int32, (tm, LANES), 1) % (2 * ROPE_AXIS_PAIRS)) < ROPE_AXIS_PAIRS
        for c in range(rope_cols // LANES):
            tab = slice(LANES, 2 * LANES) if c * LANES < q_cols else slice(0, LANES)
            t = acc[:, c * LANES:(c + 1) * LANES]
            up = pltpu.roll(t, LANES - ROPE_AXIS_PAIRS, 1)
            dn = pltpu.roll(t, ROPE_AXIS_PAIRS, 1)
            o_ref[:, c * LANES:(c + 1) * LANES] = (
                t * cos_ref[:, tab] + jnp.where(first_half, up, dn) * sin_ref[:, tab]).astype(o_ref.dtype)
    else:
        o_ref[:, 0:q_cols] = (acc[:, 0:q_cols] * Q_SCALE).astype(o_ref.dtype)
        rope_cols = q_cols
    if rope_cols < n:
        o_ref[:, rope_cols:] = acc[:, rope_cols:].astype(o_ref.dtype)


def _qkv(x, g, mod, w, cos, sin, rope_cols, q_cols, ctx_len):
    b, r, d = x.shape
    n = w.shape[1]
    kern = functools.partial(_qkv_kernel, rope_cols=rope_cols, q_cols=q_cols, ctx_tiles=ctx_len // ROW_TILE)
    return pl.pallas_call(
        kern,
        grid=(b, r // ROW_TILE),
        in_specs=[
            pl.BlockSpec((None, ROW_TILE, d), lambda bi, i: (bi, i, 0)),
            pl.BlockSpec((1, d), lambda bi, i: (0, 0)),
            pl.BlockSpec((MOD_ROWS, 6 * d), lambda bi, i: (0, 0)),
            pl.BlockSpec((d, n), lambda bi, i: (0, 0)),
            pl.BlockSpec((ROW_TILE, 2 * LANES), lambda bi, i: (i, 0)),
            pl.BlockSpec((ROW_TILE, 2 * LANES), lambda bi, i: (i, 0)),
        ],
        out_specs=pl.BlockSpec((None, ROW_TILE, n), lambda bi, i: (bi, i, 0)),
        out_shape=jax.ShapeDtypeStruct((b, r, n), BF16),
        compiler_params=_params(("arbitrary", "arbitrary")),
        name="qkv",
    )(x, g, mod, w, cos, sin)


def _rope_tables(seq, ctx_len):
    t = jnp.arange(seq, dtype=jnp.int32)
    row = (t // GRID_W).astype(F32)
    col = (t % GRID_W).astype(F32)
    inv = ROPE_BASE ** (-jnp.arange(ROPE_AXIS_PAIRS, dtype=F32) / ROPE_AXIS_PAIRS)
    ang_r = row[:, None] * inv[None, :]
    ang_c = col[:, None] * inv[None, :]
    ang = jnp.concatenate([ang_r, ang_r, ang_c, ang_c], axis=-1)
    sign = jnp.tile(jnp.concatenate([-jnp.ones((ROPE_AXIS_PAIRS,), F32), jnp.ones((ROPE_AXIS_PAIRS,), F32)]), 2)
    cos = jnp.concatenate([jnp.ones((ctx_len, HEAD_DIM), F32), jnp.cos(ang)], axis=0)
    sin = jnp.concatenate([jnp.zeros((ctx_len, HEAD_DIM), F32), jnp.sin(ang) * sign[None, :]], axis=0)
    cos = jnp.tile(cos, (1, LANES // HEAD_DIM))
    sin = jnp.tile(sin, (1, LANES // HEAD_DIM))
    return jnp.concatenate([cos, cos * Q_SCALE], axis=1), jnp.concatenate([sin, sin * Q_SCALE], axis=1)


def _head_masks(rows):
    lane = lax.broadcasted_iota(jnp.int32, (rows, LANES), 1)
    lo = lane < HEAD_DIM
    return lo, jnp.logical_not(lo)


def _na_kernel(q_ref, k_ref, v_ref, fb_ref, o_ref, nb_a, cx_a, nb_b, cx_b, *, ctx_len, grid_rows, need_ctx):
    kc = k_ref[0:ctx_len, :]
    vc = v_ref[0:ctx_len, :]

    if need_ctx:
        qc = q_ref[0:ctx_len, :]
        lo, hi = _head_masks(ctx_len)
        outs = []
        for m in (lo, hi):
            s = _dot_nt(jnp.where(m, qc, 0), kc)
            e = jnp.exp2(s - jnp.max(s, axis=-1, keepdims=True))
            outs.append(_dot(e.astype(BF16), vc) / jnp.sum(e, axis=-1, keepdims=True))
        o_ref[0:ctx_len, :] = jnp.where(lo, outs[0], outs[1]).astype(o_ref.dtype)
    else:
        o_ref[0:ctx_len, :] = jnp.zeros((ctx_len, LANES), o_ref.dtype)

    gq = NA_GROUP * GRID_W
    uk = NA_UNION * GRID_W
    n_groups = grid_rows // NA_GROUP
    lo, hi = _head_masks(gq)

    def rows_of(g):
        u0 = jnp.clip(g * NA_GROUP - NA_WIN_R // 2, 0, grid_rows - NA_UNION)
        return pl.multiple_of(ctx_len + g * gq, gq), pl.multiple_of(ctx_len + u0 * GRID_W, GRID_W)

    def scores(g, nb_ref, cx_ref):
        q0, w0 = rows_of(g)
        pattern = jnp.where(g == 0, 0, jnp.where(g == n_groups - 1, 2, 1))
        q_g = q_ref[pl.ds(q0, gq), :]
        k_u = k_ref[pl.ds(w0, uk), :]
        for j, m in enumerate((lo, hi)):
            qj = jnp.where(m, q_g, 0)
            nb_ref[j] = _dot_nt(qj, k_u) + fb_ref[j, pattern]
            cx_ref[j] = _dot_nt(qj, kc)

    def finish(g, nb_ref, cx_ref):
        q0, w0 = rows_of(g)
        v_u = v_ref[pl.ds(w0, uk), :]
        outs = []
        for j in range(2):
            s_nb = nb_ref[j]
            s_cx = cx_ref[j]
            mx = jnp.maximum(jnp.max(s_nb, axis=-1, keepdims=True), jnp.max(s_cx, axis=-1, keepdims=True))
            e_nb = jnp.exp2(s_nb - mx)
            e_cx = jnp.exp2(s_cx - mx)
            den = jnp.sum(e_nb, axis=-1, keepdims=True) + jnp.sum(e_cx, axis=-1, keepdims=True)
            outs.append((_dot(e_cx.astype(BF16), vc) + _dot(e_nb.astype(BF16), v_u)) / den)
        o_ref[pl.ds(q0, gq), :] = jnp.where(lo, outs[0], outs[1]).astype(o_ref.dtype)

    scores(0, nb_a, cx_a)

    def pair_body(t, carry):
        g = 2 * t
        scores(g + 1, nb_b, cx_b)
        finish(g, nb_a, cx_a)
        scores(jnp.minimum(g + 2, n_groups - 1), nb_a, cx_a)
        finish(g + 1, nb_b, cx_b)
        return carry

    lax.fori_loop(0, n_groups // 2, pair_body, 0)


def _na_bias_table(rpb, grid_rows):
    n_groups = grid_rows // NA_GROUP
    assert grid_rows % (2 * NA_GROUP) == 0 and n_groups >= 3 and grid_rows >= NA_UNION
    qcol = jnp.arange(GRID_W)
    kcol = jnp.arange(GRID_W)
    c0 = jnp.clip(qcol - NA_WIN_C // 2, 0, GRID_W - NA_WIN_C)
    col_ok = (kcol[None, :] >= c0[:, None]) & (kcol[None, :] < c0[:, None] + NA_WIN_C)
    rel = kcol[None, :] - qcol[:, None] + NA_WIN_C - 1
    onehot = ((rel[:, :, None] == jnp.arange(2 * NA_WIN_C - 1)[None, None, :]) & col_ok[:, :, None]).astype(F32)
    bt = jnp.einsum('hyd,qkd->hyqk', rpb.astype(F32), onehot, precision=lax.Precision.HIGHEST)
    bt = jnp.where(col_ok[None, None], bt * LOG2E, NEG_INF)
    masked = jnp.full((rpb.shape[0], GRID_W, GRID_W), NEG_INF, F32)

    def group_table(g):
        u0 = min(max(g * NA_GROUP - NA_WIN_R // 2, 0), grid_rows - NA_UNION)
        rows = []
        for a in range(NA_GROUP):
            r = g * NA_GROUP + a
            r0 = min(max(r - NA_WIN_R // 2, 0), grid_rows - NA_WIN_R)
            blocks = [bt[:, u0 + y - r + NA_WIN_R - 1] if r0 <= u0 + y < r0 + NA_WIN_R else masked
                      for y in range(NA_UNION)]
            rows.append(jnp.concatenate(blocks, axis=-1))
        return jnp.concatenate(rows, axis=-2)

    for g in range(1, n_groups - 1):
        assert min(max(g * NA_GROUP - NA_WIN_R // 2, 0), grid_rows - NA_UNION) == g * NA_GROUP - NA_WIN_R // 2
        assert 0 <= g * NA_GROUP - NA_WIN_R // 2 and (g + 1) * NA_GROUP - 1 - NA_WIN_R // 2 <= grid_rows - NA_WIN_R
    return jnp.stack([group_table(0), group_table(1), group_table(n_groups - 1)], axis=1)


def _na_attention(qkv, rpb, ctx_len, need_ctx):
    b, r, _ = qkv.shape
    grid_rows = (r - ctx_len) // GRID_W
    nblk = D_MODEL // LANES
    fb = _na_bias_table(rpb, grid_rows)
    kern = functools.partial(_na_kernel, ctx_len=ctx_len, grid_rows=grid_rows, need_ctx=need_ctx)
    return pl.pallas_call(
        kern,
        grid=(nblk, b),
        in_specs=[
            pl.BlockSpec((None, r, LANES), lambda hp, bi: (bi, 0, hp)),
            pl.BlockSpec((None, r, LANES), lambda hp, bi: (bi, 0, nblk + hp)),
            pl.BlockSpec((None, r, LANES), lambda hp, bi: (bi, 0, 2 * nblk + hp)),
            pl.BlockSpec((2, 3, NA_GROUP * GRID_W, NA_UNION * GRID_W), lambda hp, bi: (hp, 0, 0, 0)),
        ],
        out_specs=pl.BlockSpec((None, r, LANES), lambda hp, bi: (bi, 0, hp)),
        out_shape=jax.ShapeDtypeStruct((b, r, D_MODEL), BF16),
        scratch_shapes=[pltpu.VMEM((2, NA_GROUP * GRID_W, NA_UNION * GRID_W), F32),
                        pltpu.VMEM((2, NA_GROUP * GRID_W, ctx_len), F32)] * 2,
        compiler_params=_params(("arbitrary", "arbitrary")),
        name="na_attn",
    )(qkv, qkv, qkv, fb)


def _da_kernel(qc_ref, qn_ref, k_ref, v_ref, lam_ref, g_ref, o_ref, sa_ref, sb_ref, *, ctx_len, lam_init):
    i = pl.program_id(2)
    lp = lam_ref[...]
    lam = (jnp.exp(jnp.sum(lp[0:1] * lp[1:2], axis=-1, keepdims=True))
           - jnp.exp(jnp.sum(lp[2:3] * lp[3:4], axis=-1, keepdims=True)) + lam_init)
    lo, hi = _head_masks(ROW_TILE)

    def finish(scores, v):
        es, ls = [], []
        for s in scores:
            e = jnp.exp2(s - jnp.max(s, axis=-1, keepdims=True))
            es.append(e)
            ls.append(jnp.sum(e, axis=-1, keepdims=True))
        a = es[0] - (lam * ls[0] / ls[1]) * es[1]
        o = _dot(a.astype(BF16), v) * (1.0 / ls[0])
        o_ref[...] = (_rms(o) * g_ref[...] * (1.0 - lam_init)).astype(o_ref.dtype)

    @pl.when(i == 0)
    def _():
        sb_ref[...] = jnp.zeros(sb_ref.shape, sb_ref.dtype)

    def step(store_ref, load_ref):
        qn = qn_ref[...]
        k = k_ref[...]
        for j, m in enumerate((lo, hi)):
            store_ref[j] = _dot_nt(jnp.where(m, qn, 0), k)
        finish([load_ref[0], load_ref[1]], v_ref[...])

    @pl.when(i % 2 == 0)
    def _():
        step(sa_ref, sb_ref)

    @pl.when(i % 2 == 1)
    def _():
        step(sb_ref, sa_ref)

    @pl.when(i == 0)
    def _():
        qc = qc_ref[...]
        kc = k_ref[0:ctx_len, :]
        finish([_dot_nt(jnp.where(m, qc, 0), kc) for m in (lo, hi)], v_ref[0:ctx_len, :])


def _da_attention(qkv, lam_params, subln_g, layer_idx, ctx_len):
    b, r, _ = qkv.shape
    assert ctx_len == ROW_TILE
    nblk = D_MODEL // LANES
    nt = r // ROW_TILE
    lam_init = 0.8 - 0.6 * math.exp(-0.3 * layer_idx)
    kern = functools.partial(_da_kernel, ctx_len=ctx_len, lam_init=lam_init)
    return pl.pallas_call(
        kern,
        grid=(nblk, b, nt),
        in_specs=[
            pl.BlockSpec((None, ROW_TILE, LANES), lambda h, bi, i: (bi, 0, h)),
            pl.BlockSpec((None, ROW_TILE, LANES), lambda h, bi, i: (bi, jnp.minimum(i + 1, nt - 1), h)),
            pl.BlockSpec((None, r, LANES), lambda h, bi, i: (bi, 0, nblk + h)),
            pl.BlockSpec((None, r, LANES), lambda h, bi, i: (bi, 0, 2 * nblk + h)),
            pl.BlockSpec((4, HEAD_DIM), lambda h, bi, i: (0, 0)),
            pl.BlockSpec((1, 2 * HEAD_DIM), lambda h, bi, i: (0, 0)),
        ],
        out_specs=pl.BlockSpec((None, ROW_TILE, LANES), lambda h, bi, i: (bi, i, h)),
        out_shape=jax.ShapeDtypeStruct((b, r, D_MODEL), BF16),
        scratch_shapes=[pltpu.VMEM((2, ROW_TILE, r), F32), pltpu.VMEM((2, ROW_TILE, r), F32)],
        compiler_params=_params(("arbitrary", "arbitrary", "arbitrary")),
        name="da_attn",
    )(qkv, qkv, qkv, qkv, lam_params.astype(F32), subln_g.astype(F32).reshape(1, -1))


def _sw_kernel(sink_ref, q_ref, k_ref, v_ref, o_ref, kd_ref, vd_ref, sl_a, sc_a, sl_b, sc_b, *, ctx_len, seq, need_ctx):
    kvp = pl.program_id(0)
    band = Q_BLOCK + 2 * SW_WINDOW
    hpk = SW_Q_HEADS // SW_KV_HEADS

    k = k_ref[...]
    v = v_ref[...]
    for hh in range(2):
        sl = slice(hh * HEAD_DIM, (hh + 1) * HEAD_DIM)
        kd_ref[hh] = jnp.concatenate([k[:, sl], k[:, sl]], axis=-1)
        vd_ref[hh] = jnp.concatenate([v[:, sl], v[:, sl]], axis=-1)

    n_blk = 2 * (hpk // 2)

    def stacked_q(row0, rows, c):
        lo, hi = _head_masks(rows)
        qv = q_ref[pl.ds(row0, rows), c * LANES:(c + 1) * LANES]
        return jnp.concatenate([jnp.where(lo, qv, 0), jnp.where(hi, qv, 0)], axis=0)

    def per_head(rows, c, col, fn):
        sinks = [sink_ref[kvp * 2 * hpk + 2 * c + j] * LOG2E for j in range(2)]
        return jnp.concatenate([fn(col[j * rows:(j + 1) * rows], sinks[j]) for j in range(2)], axis=0)

    def write_out(row0, rows, c, o):
        lo, _ = _head_masks(rows)
        o_ref[pl.ds(row0, rows), c * LANES:(c + 1) * LANES] = jnp.where(lo, o[0:rows], o[rows:2 * rows]).astype(o_ref.dtype)

    def softmax_pv(rows, c, parts):
        row_max = functools.reduce(jnp.maximum, [jnp.max(s, axis=-1, keepdims=True) for s, _ in parts])
        mx = per_head(rows, c, row_max, lambda m, sink: jnp.maximum(m, sink))
        es = [jnp.exp2(s - mx) for s, _ in parts]
        den = sum(jnp.sum(e, axis=-1, keepdims=True) for e in es) + per_head(rows, c, mx, lambda m, sink: jnp.exp2(sink - m))
        return sum(_dot(e.astype(BF16), val) for e, (_, val) in zip(es, parts)) / den

    if need_ctx:
        for c in range(n_blk):
            hh = c // (hpk // 2)
            s = _dot_nt(stacked_q(0, ctx_len, c), kd_ref[hh, 0:ctx_len, :])
            write_out(0, ctx_len, c, softmax_pv(ctx_len, c, [(s, vd_ref[hh, 0:ctx_len, :])]))
    else:
        o_ref[0:ctx_len, :] = jnp.zeros((ctx_len, o_ref.shape[-1]), o_ref.dtype)

    n_steps = seq // Q_BLOCK

    def rows_of(n):
        start = n * Q_BLOCK
        b0 = jnp.clip(start - SW_WINDOW, 0, seq - band)
        return start, b0, pl.multiple_of(ctx_len + start, Q_BLOCK), pl.multiple_of(ctx_len + b0, Q_BLOCK)

    def scores(n, sl_ref, sc_ref):
        start, b0, qrow, brow = rows_of(n)
        qpos = start + (lax.broadcasted_iota(jnp.int32, (2 * Q_BLOCK, band), 0) & (Q_BLOCK - 1))
        kpos = b0 + lax.broadcasted_iota(jnp.int32, (2 * Q_BLOCK, band), 1)
        in_band = jnp.abs(kpos - qpos) <= SW_WINDOW
        for c in range(n_blk):
            hh = c // (hpk // 2)
            qs = stacked_q(qrow, Q_BLOCK, c)
            sl_ref[c] = jnp.where(in_band, _dot_nt(qs, kd_ref[hh, pl.ds(brow, band), :]), NEG_INF)
            sc_ref[c] = _dot_nt(qs, kd_ref[hh, 0:ctx_len, :])

    def finish(n, sl_ref, sc_ref):
        _, _, qrow, brow = rows_of(n)
        for c in range(n_blk):
            hh = c // (hpk // 2)
            o = softmax_pv(Q_BLOCK, c, [(sc_ref[c], vd_ref[hh, 0:ctx_len, :]),
                                        (sl_ref[c], vd_ref[hh, pl.ds(brow, band), :])])
            write_out(qrow, Q_BLOCK, c, o)

    scores(0, sl_a, sc_a)

    def pair_body(t, carry):
        n = 2 * t
        scores(n + 1, sl_b, sc_b)
        finish(n, sl_a, sc_a)
        scores(jnp.minimum(n + 2, n_steps - 1), sl_a, sc_a)
        finish(n + 1, sl_b, sc_b)
        return carry

    lax.fori_loop(0, n_steps // 2, pair_body, 0)


def _sw_attention(qkv, sinks, ctx_len, need_ctx):
    b, r, _ = qkv.shape
    seq = r - ctx_len
    assert seq % (2 * Q_BLOCK) == 0
    nq_blk = SW_Q_HEADS * HEAD_DIM // LANES
    slab = 2 * (SW_Q_HEADS // SW_KV_HEADS) * HEAD_DIM
    kern = functools.partial(_sw_kernel, ctx_len=ctx_len, seq=seq, need_ctx=need_ctx)
    n_kv_blk = SW_KV_HEADS * HEAD_DIM // LANES
    return pl.pallas_call(
        kern,
        grid=(n_kv_blk, b),
        in_specs=[
            pl.BlockSpec(memory_space=pltpu.SMEM),
            pl.BlockSpec((None, r, slab), lambda kp, bi: (bi, 0, kp)),
            pl.BlockSpec((None, r, LANES), lambda kp, bi: (bi, 0, nq_blk + kp)),
            pl.BlockSpec((None, r, LANES), lambda kp, bi: (bi, 0, nq_blk + n_kv_blk + kp)),
        ],
        out_specs=pl.BlockSpec((None, r, slab), lambda kp, bi: (bi, 0, kp)),
        out_shape=jax.ShapeDtypeStruct((b, r, D_MODEL), BF16),
        scratch_shapes=[pltpu.VMEM((2, r, LANES), BF16), pltpu.VMEM((2, r, LANES), BF16)]
        + [pltpu.VMEM((slab // LANES, 2 * Q_BLOCK, Q_BLOCK + 2 * SW_WINDOW), F32),
           pltpu.VMEM((slab // LANES, 2 * Q_BLOCK, ctx_len), F32)] * 2,
        compiler_params=_params(("arbitrary", "arbitrary")),
        name="sw_attn",
    )(sinks.astype(F32), qkv, qkv, qkv)


def _post_kernel(o_ref, x_ref, wo_ref, mod_ref, g_ref, wrh_ref, wrl_ref, br_ref, xn_ref, h_ref, rt_ref, *, ctx_tiles):
    d = x_ref.shape[-1]
    tm = x_ref.shape[0]
    row = _mod_row(pl.program_id(0), pl.program_id(1), ctx_tiles)
    gate1 = mod_ref[pl.ds(row, 1), 2 * d:3 * d]
    shift2 = mod_ref[pl.ds(row, 1), 3 * d:4 * d]
    scale2 = mod_ref[pl.ds(row, 1), 4 * d:5 * d]
    xn = x_ref[...] + gate1 * _dot(o_ref[...], wo_ref[...])
    xn_ref[...] = xn
    h = (_rms(xn) * g_ref[...]) * (1.0 + scale2) + shift2
    for j in range(SUBLANES):
        h_ref[pl.ds(j, tm, stride=SUBLANES), :] = h[:, j * LANES:(j + 1) * LANES]

    h_hi = h.astype(BF16)
    h_lo = (h - h_hi.astype(F32)).astype(BF16)
    logits = (_dot(h_hi, wrh_ref[...]) + _dot(h_lo, wrh_ref[...]) + _dot(h_hi, wrl_ref[...])) + br_ref[...]
    lane = lax.broadcasted_iota(jnp.int32, (tm, LANES), 1).astype(F32)

    def first_argmax(vals, mx):
        return jnp.min(jnp.where(vals == mx, lane, float(LANES)), axis=-1, keepdims=True)

    gl = jnp.where(lane < N_GROUPS, logits, -jnp.inf)
    ge = jnp.exp(gl - jnp.max(gl, axis=-1, keepdims=True))
    gp = ge / jnp.sum(ge, axis=-1, keepdims=True)
    g_p = jnp.max(gp, axis=-1, keepdims=True)
    g_idx = first_argmax(gp, g_p)
    e_lo = N_GROUPS + EXPERTS_PER_GROUP * g_idx
    el = jnp.where((lane >= e_lo) & (lane < e_lo + EXPERTS_PER_GROUP), logits, -jnp.inf)
    v1 = jnp.max(el, axis=-1, keepdims=True)
    i1 = first_argmax(el, v1)
    el2 = jnp.where(lane == i1, -jnp.inf, el)
    v2 = jnp.max(el2, axis=-1, keepdims=True)
    i2 = first_argmax(el2, v2)
    e2 = jnp.exp(v2 - v1)
    w1 = g_p * (1.0 / (1.0 + e2))
    w2 = g_p * (e2 / (1.0 + e2))
    lane8 = lax.broadcasted_iota(jnp.int32, (tm, SUBLANES), 1)
    rt = jnp.where(lane8 == 0, i1 - N_GROUPS,
                   jnp.where(lane8 == 1, i2 - N_GROUPS,
                             jnp.where(lane8 == 2, w1, jnp.where(lane8 == 3, w2, 0.0))))
    rt_ref[...] = rt


def _post(o, x, wo, mod, g, wr, br, ctx_len):
    wr_hi = wr.astype(BF16)
    wr_lo = (wr - wr_hi.astype(F32)).astype(BF16)
    b, r, d = x.shape
    nt = r // ROW_TILE
    kern = functools.partial(_post_kernel, ctx_tiles=ctx_len // ROW_TILE)
    return pl.pallas_call(
        kern,
        grid=(b, nt),
        in_specs=[
            pl.BlockSpec((None, ROW_TILE, d), lambda bi, i: (bi, i, 0)),
            pl.BlockSpec((None, ROW_TILE, d), lambda bi, i: (bi, i, 0)),
            pl.BlockSpec((d, d), lambda bi, i: (0, 0)),
            pl.BlockSpec((MOD_ROWS, 6 * d), lambda bi, i: (0, 0)),
            pl.BlockSpec((1, d), lambda bi, i: (0, 0)),
            pl.BlockSpec((d, LANES), lambda bi, i: (0, 0)),
            pl.BlockSpec((d, LANES), lambda bi, i: (0, 0)),
            pl.BlockSpec((1, LANES), lambda bi, i: (0, 0)),
        ],
        out_specs=[
            pl.BlockSpec((None, ROW_TILE, d), lambda bi, i: (bi, i, 0)),
            pl.BlockSpec((ROW_TILE * SUBLANES, LANES), lambda bi, i: (bi * nt + i, 0)),
            pl.BlockSpec((ROW_TILE, SUBLANES), lambda bi, i: (bi * nt + i, 0)),
        ],
        out_shape=[
            jax.ShapeDtypeStruct((b, r, d), F32),
            jax.ShapeDtypeStruct((b * r * SUBLANES, LANES), F32),
            jax.ShapeDtypeStruct((b * r, SUBLANES), F32),
        ],
        compiler_params=_params(("arbitrary", "arbitrary")),
        name="post_attn",
    )(o, x, wo, mod, g, wr_hi, wr_lo, br)


DISPATCH_TOKENS = 512


def _row(ref, idx):
    return ref.at[pl.ds(pl.multiple_of(idx * SUBLANES, SUBLANES), SUBLANES), :]


def _dispatch_kernel(dest_ref, h_ref, xs_hbm, sem):
    def issue(t, carry):
        src = _row(h_ref, t)
        for k in range(2):
            pltpu.make_async_copy(src, _row(xs_hbm, dest_ref[0, 0, 2 * t + k]), sem).start(priority=k)
        return carry

    lax.fori_loop(0, DISPATCH_TOKENS, issue, 0)
    for _ in range(2):
        pltpu.make_async_copy(h_ref, xs_hbm.at[pl.ds(0, DISPATCH_TOKENS * SUBLANES), :], sem).wait()


def _dispatch(h_rows, dest):
    t = dest.shape[0] // 2
    steps = t // DISPATCH_TOKENS
    return pl.pallas_call(
        _dispatch_kernel,
        grid=(steps,),
        in_specs=[
            pl.BlockSpec((1, 1, 2 * DISPATCH_TOKENS), lambda i: (i, 0, 0), memory_space=pltpu.SMEM),
            pl.BlockSpec((DISPATCH_TOKENS * SUBLANES, LANES), lambda i: (i, 0)),
        ],
        out_specs=pl.BlockSpec(memory_space=pl.ANY),
        out_shape=jax.ShapeDtypeStruct((2 * t * SUBLANES, LANES), F32),
        scratch_shapes=[pltpu.SemaphoreType.DMA(())],
        compiler_params=_params(("arbitrary",)),
        name="moe_dispatch",
    )(dest.reshape(steps, 1, 2 * DISPATCH_TOKENS), h_rows)


def _experts_kernel(ve_ref, vt_ref, lo_ref, hi_ref, first_ref, valid_ref,
                    xs_ref, wg_ref, wu_ref, wd_ref, ys_ref, wgu_s, wd_s):
    v = pl.program_id(0)
    tm = ROW_TILE
    f = wg_ref.shape[-1]

    @pl.when(valid_ref[v] == 1)
    def _():
        changed = jnp.logical_or(v == 0, ve_ref[v] != ve_ref[jnp.maximum(v - 1, 0)])

        @pl.when(changed)
        def _():
            wgu_s[:, 0:f] = wg_ref[...].astype(BF16)
            wgu_s[:, f:2 * f] = wu_ref[...].astype(BF16)
            wd_s[...] = wd_ref[...].astype(BF16)

        x = jnp.concatenate([xs_ref[pl.ds(j, tm, stride=SUBLANES), :] for j in range(SUBLANES)], axis=-1)
        au = _dot(x.astype(BF16), wgu_s[...])
        rows = vt_ref[v] * tm + lax.broadcasted_iota(jnp.int32, (tm, 1), 0)
        mine = (rows >= lo_ref[v]) & (rows < hi_ref[v])
        mid = jnp.where(mine, jax.nn.silu(au[:, 0:f]) * au[:, f:2 * f], 0.0)
        y = _dot(mid.astype(BF16), wd_s[...])

        @pl.when(first_ref[v] == 1)
        def _():
            for j in range(SUBLANES):
                ys_ref[pl.ds(j, tm, stride=SUBLANES), :] = y[:, j * LANES:(j + 1) * LANES]

        @pl.when(first_ref[v] == 0)
        def _():
            for j in range(SUBLANES):
                ys_ref[pl.ds(j, tm, stride=SUBLANES), :] += y[:, j * LANES:(j + 1) * LANES]


def _experts(xs, sched, w_gate, w_up, w_down, layer):
    rows = xs.shape[0] // SUBLANES
    d, f = w_gate.shape[-2:]
    n_visits = sched[0].shape[0]
    blk = (ROW_TILE * SUBLANES, LANES)
    grid_spec = pltpu.PrefetchScalarGridSpec(
        num_scalar_prefetch=6,
        grid=(n_visits,),
        in_specs=[
            pl.BlockSpec(blk, lambda v, ve, vt, lo, hi, fi, va: (vt[v], 0)),
            pl.BlockSpec((None, None, d, f), lambda v, ve, vt, lo, hi, fi, va: (layer, ve[v], 0, 0)),
            pl.BlockSpec((None, None, d, f), lambda v, ve, vt, lo, hi, fi, va: (layer, ve[v], 0, 0)),
            pl.BlockSpec((None, None, f, d), lambda v, ve, vt, lo, hi, fi, va: (layer, ve[v], 0, 0)),
        ],
        out_specs=pl.BlockSpec(blk, lambda v, ve, vt, lo, hi, fi, va: (vt[v], 0)),
        scratch_shapes=[pltpu.VMEM((d, 2 * f), BF16), pltpu.VMEM((f, d), BF16)],
    )
    return pl.pallas_call(
        _experts_kernel,
        grid_spec=grid_spec,
        out_shape=jax.ShapeDtypeStruct((rows * SUBLANES, LANES), F32),
        compiler_params=_params(("arbitrary",)),
        name="moe_experts",
    )(*sched, xs, w_gate, w_up, w_down)


def _combine_kernel(dest_ref, rt_ref, x_ref, mod_ref, ys_hbm, o_ref, buf, sem, *, ctx_tiles, tiles_per_batch):
    tm = ROW_TILE
    d = x_ref.shape[-1]

    def issue(t, carry):
        for k in range(2):
            pltpu.make_async_copy(_row(ys_hbm, dest_ref[0, 0, 2 * t + k]), _row(buf, k * tm + t), sem).start(priority=k)
        return carry

    lax.fori_loop(0, tm, issue, 0)

    for k in range(2):
        half = pl.ds(k * tm * SUBLANES, tm * SUBLANES)
        pltpu.make_async_copy(ys_hbm.at[half, :], buf.at[half, :], sem).wait()

    gi = pl.program_id(0)
    row = _mod_row(gi // tiles_per_batch, gi % tiles_per_batch, ctx_tiles)
    gate2 = mod_ref[pl.ds(row, 1), 5 * d:6 * d]
    rt = rt_ref[...]
    w = (rt[:, 2:3], rt[:, 3:4])
    for j in range(SUBLANES):
        y = (w[0] * buf[pl.ds(j, tm, stride=SUBLANES), :]
             + w[1] * buf[pl.ds(tm * SUBLANES + j, tm, stride=SUBLANES), :])
        sl = slice(j * LANES, (j + 1) * LANES)
        o_ref[:, sl] = x_ref[:, sl] + gate2[:, sl] * y


def _combine(ys, dest, rt, x, mod, ctx_len):
    b, r, d = x.shape
    t = b * r
    steps = t // ROW_TILE
    kern = functools.partial(_combine_kernel, ctx_tiles=ctx_len // ROW_TILE, tiles_per_batch=r // ROW_TILE)
    out = pl.pallas_call(
        kern,
        grid=(steps,),
        in_specs=[
            pl.BlockSpec((1, 1, 2 * ROW_TILE), lambda i: (i, 0, 0), memory_space=pltpu.SMEM),
            pl.BlockSpec((ROW_TILE, SUBLANES), lambda i: (i, 0)),
            pl.BlockSpec((ROW_TILE, d), lambda i: (i, 0)),
            pl.BlockSpec((MOD_ROWS, 6 * d), lambda i: (0, 0)),
            pl.BlockSpec(memory_space=pl.ANY),
        ],
        out_specs=pl.BlockSpec((ROW_TILE, d), lambda i: (i, 0)),
        out_shape=jax.ShapeDtypeStruct((t, d), F32),
        scratch_shapes=[pltpu.VMEM((2 * ROW_TILE * SUBLANES, LANES), F32), pltpu.SemaphoreType.DMA(())],
        compiler_params=_params(("arbitrary",)),
        name="moe_combine",
    )(dest.reshape(steps, 1, 2 * ROW_TILE), rt, x.reshape(t, d), mod, ys)
    return out.reshape(b, r, d)


def _route_plan(rt, n_tiles):
    eid = rt[:, 0:2].astype(jnp.int32).reshape(-1)
    onehot = (eid[:, None] == jnp.arange(N_EXPERTS, dtype=jnp.int32)[None, :]).astype(jnp.int32)
    csum = jnp.cumsum(onehot, axis=0)
    counts = csum[-1]
    cstart = jnp.cumsum(counts) - counts
    dest = jnp.sum(onehot * (cstart[None, :] + csum - 1), axis=1).astype(jnp.int32)

    tm = ROW_TILE
    first_tile = cstart // tm
    last_tile = (cstart + counts - 1) // tm
    nvis = jnp.where(counts > 0, last_tile - first_tile + 1, 0)
    vend = jnp.cumsum(nvis)
    vstart = vend - nvis
    total = vend[-1]
    n_visits = n_tiles + N_EXPERTS - 1
    v = jnp.arange(n_visits, dtype=jnp.int32)
    valid = v < total
    vc = jnp.minimum(v, total - 1)
    ve = jnp.sum((vc[:, None] >= vend[None, :]).astype(jnp.int32), axis=1)
    oh_e = (ve[:, None] == jnp.arange(N_EXPERTS, dtype=jnp.int32)[None, :]).astype(jnp.int32)
    pick = lambda a: jnp.sum(oh_e * a[None, :], axis=1)
    vt = pick(first_tile) + vc - pick(vstart)
    lo = pick(cstart)
    hi = jnp.where(valid, lo + pick(counts), lo)
    first = jnp.concatenate([jnp.ones((1,), jnp.int32), (vt[1:] != vt[:-1]).astype(jnp.int32)])
    sched = tuple(a.astype(jnp.int32) for a in (ve, vt, lo, hi, first, valid))
    return dest, sched


def _final_kernel(x_ref, g_ref, o_ref):
    o_ref[...] = _rms(x_ref[...]) * g_ref[...]


def _final_norm(x, g, ctx_len):
    b, r, d = x.shape
    ct = ctx_len // ROW_TILE
    return pl.pallas_call(
        _final_kernel,
        grid=(b, (r - ctx_len) // ROW_TILE),
        in_specs=[
            pl.BlockSpec((None, ROW_TILE, d), lambda bi, i: (bi, i + ct, 0)),
            pl.BlockSpec((1, d), lambda bi, i: (0, 0)),
        ],
        out_specs=pl.BlockSpec((None, ROW_TILE, d), lambda bi, i: (bi, i, 0)),
        out_shape=jax.ShapeDtypeStruct((b, r - ctx_len, d), F32),
        compiler_params=_params(("arbitrary", "arbitrary")),
        name="final_norm",
    )(x, g)


def kernel(x, c, ctx, c_ctx, ada_w, ada_b, norm_mix_g, norm_ffn_g, final_norm_g, na_w_qkv, na_rpb, na_w_o, da_w_qkv, da_lambda, da_subln_g, da_w_o, sw_w_qkv, sw_sinks, sw_w_o, moe_w_grp, moe_b_grp, moe_w_exp, moe_b_exp, moe_w_gate, moe_w_up, moe_w_down):
    b, seq, d = x.shape
    ctx_len = ctx.shape[1]
    depth = ada_w.shape[0]
    assert d == D_MODEL and ctx_len % ROW_TILE == 0 and seq % ROW_TILE == 0 and b <= MOD_ROWS // 2
    assert (b * (ctx_len + seq)) % DISPATCH_TOKENS == 0

    half = MOD_ROWS // 2
    cond = jnp.concatenate([jnp.pad(c, ((0, half - b), (0, 0))), jnp.pad(c_ctx[None, :], ((0, half - 1), (0, 0)))], axis=0)
    mod_all = _adaln(cond, ada_w, ada_b)
    cos, sin = _rope_tables(seq, ctx_len)
    stream = jnp.concatenate([ctx, x], axis=1)
    n_tiles = 2 * b * (ctx_len + seq) // ROW_TILE

    for i in range(depth):
        need_ctx = i < depth - 1
        kind, j = i % N_MIXERS, i // N_MIXERS
        mod = mod_all[i]
        g_mix = norm_mix_g[i].reshape(1, d)
        if kind == 0:
            qkv = _qkv(stream, g_mix, mod, na_w_qkv[j].astype(BF16), cos, sin, 0, NA_HEADS * HEAD_DIM, ctx_len)
            o = _na_attention(qkv, na_rpb[j], ctx_len, need_ctx)
            wo = na_w_o[j]
        elif kind == 1:
            nq = 2 * DA_HEADS * HEAD_DIM
            qkv = _qkv(stream, g_mix, mod, da_w_qkv[j].astype(BF16), cos, sin, 2 * nq, nq, ctx_len)
            o = _da_attention(qkv, da_lambda[j], da_subln_g[j], i, ctx_len)
            wo = da_w_o[j]
        else:
            rope_cols = (SW_Q_HEADS + SW_KV_HEADS) * HEAD_DIM
            qkv = _qkv(stream, g_mix, mod, sw_w_qkv[j].astype(BF16), cos, sin, rope_cols, SW_Q_HEADS * HEAD_DIM, ctx_len)
            o = _sw_attention(qkv, sw_sinks[j], ctx_len, need_ctx)
            wo = sw_w_o[j]

        pad = LANES - N_GROUPS - N_EXPERTS
        wr = jnp.pad(jnp.concatenate([moe_w_grp[i], moe_w_exp[i]], axis=1).astype(F32), ((0, 0), (0, pad)))
        br = jnp.pad(jnp.concatenate([moe_b_grp[i], moe_b_exp[i]]).astype(F32), (0, pad)).reshape(1, LANES)
        stream, h_rows, rt = _post(o, stream, wo.astype(BF16), mod, norm_ffn_g[i].reshape(1, d), wr, br, ctx_len)
        dest, sched = _route_plan(rt, n_tiles)
        xs = _dispatch(h_rows, dest)
        ys = _experts(xs, sched, moe_w_gate, moe_w_up, moe_w_down, i)
        stream = _combine(ys, dest, rt, stream, mod, ctx_len)

    return _final_norm(stream, final_norm_g.reshape(1, d), ctx_len)
```

```python
import functools
import math

import jax
import jax.numpy as jnp
from jax import lax
from jax.experimental import pallas as pl
from jax.experimental.pallas import tpu as pltpu

F32 = jnp.float32
BF16 = jnp.bfloat16

D_MODEL = 1024
HEAD_DIM = 64
GRID_W = 64
ROPE_AXIS_PAIRS = HEAD_DIM // 4
ROPE_BASE = 10000.0
NA_HEADS = D_MODEL // HEAD_DIM
NA_WIN_R = 8
NA_WIN_C = 16
NA_GROUP = 4
NA_UNION = NA_GROUP + NA_WIN_R
DA_HEADS = D_MODEL // (2 * HEAD_DIM)
SW_Q_HEADS = D_MODEL // HEAD_DIM
SW_KV_HEADS = 4
SW_WINDOW = 128
Q_BLOCK = 128
N_GROUPS = 4
EXPERTS_PER_GROUP = 8
N_EXPERTS = N_GROUPS * EXPERTS_PER_GROUP
D_EXPERT = 512
NORM_EPS = 1e-6
NEG_INF = -1e30
N_MIXERS = 3

LANES = 128
SUBLANES = 8
ROW_TILE = 256
MOD_ROWS = 16
VMEM_LIMIT = 56 * 1024 * 1024

LOG2E = math.log2(math.e)
Q_SCALE = HEAD_DIM ** -0.5 * LOG2E

_NT = (((1,), (1,)), ((), ()))


def _dot(a, b, **kw):
    return jnp.dot(a, b, preferred_element_type=F32, **kw)


def _dot_nt(a, b):
    return lax.dot_general(a, b, _NT, preferred_element_type=F32)


def _params(sem, vmem=VMEM_LIMIT):
    return pltpu.CompilerParams(dimension_semantics=sem, vmem_limit_bytes=vmem)


def _mod_row(b, tile, ctx_tiles):
    return jnp.where(tile < ctx_tiles, MOD_ROWS // 2, b)


def _rms(x):
    return x * lax.rsqrt(jnp.mean(x * x, axis=-1, keepdims=True) + NORM_EPS)


def _adaln_kernel(c_ref, w_ref, b_ref, o_ref):
    s = jax.nn.silu(c_ref[...])
    o_ref[...] = _dot(s, w_ref[...], precision=lax.Precision.HIGHEST) + b_ref[...]


def _adaln(cond, ada_w, ada_b):
    depth, d, n6 = ada_w.shape
    tn = 1536
    return pl.pallas_call(
        _adaln_kernel,
        grid=(depth, n6 // tn),
        in_specs=[
            pl.BlockSpec((MOD_ROWS, d), lambda l, n: (0, 0)),
            pl.BlockSpec((None, d, tn), lambda l, n: (l, 0, n)),
            pl.BlockSpec((None, 1, tn), lambda l, n: (l, 0, n)),
        ],
        out_specs=pl.BlockSpec((None, MOD_ROWS, tn), lambda l, n: (l, 0, n)),
        out_shape=jax.ShapeDtypeStruct((depth, MOD_ROWS, n6), F32),
        compiler_params=_params(("arbitrary", "arbitrary")),
        name="adaln",
    )(cond, ada_w, ada_b.reshape(depth, 1, n6))


def _qkv_kernel(x_ref, g_ref, mod_ref, w_ref, cos_ref, sin_ref, o_ref, *, rope_cols, q_cols, ctx_tiles):
    d = x_ref.shape[-1]
    row = _mod_row(pl.program_id(0), pl.program_id(1), ctx_tiles)
    shift = mod_ref[pl.ds(row, 1), 0:d]
    scale = mod_ref[pl.ds(row, 1), d:2 * d]
    h = (_rms(x_ref[...]) * g_ref[...]) * (1.0 + scale) + shift
    acc = _dot(h.astype(BF16), w_ref[...])
    n = acc.shape[-1]
    if rope_cols:
        tm = acc.shape[0]
        first_half = (lax.broadcasted_iota(jnp.int32, (tm, LANES), 1) % (2 * ROPE_AXIS_PAIRS)) < ROPE_AXIS_PAIRS
        for c in range(rope_cols // LANES):
            tab = slice(LANES, 2 * LANES) if c * LANES < q_cols else slice(0, LANES)
            t = acc[:, c * LANES:(c + 1) * LANES]
            up = pltpu.roll(t, LANES - ROPE_AXIS_PAIRS, 1)
            dn = pltpu.roll(t, ROPE_AXIS_PAIRS, 1)
            o_ref[:, c * LANES:(c + 1) * LANES] = (
                t * cos_ref[:, tab] + jnp.where(first_half, up, dn) * sin_ref[:, tab]).astype(o_ref.dtype)
    else:
        o_ref[:, 0:q_cols] = (acc[:, 0:q_cols] * Q_SCALE).astype(o_ref.dtype)
        rope_cols = q_cols
    if rope_cols < n:
        o_ref[:, rope_cols:] = acc[:, rope_cols:].astype(o_ref.dtype)


def _qkv(x, g, mod, w, cos, sin, rope_cols, q_cols, ctx_len):
    b, r, d = x.shape
    n = w.shape[1]
    kern = functools.partial(_qkv_kernel, rope_cols=rope_cols, q_cols=q_cols, ctx_tiles=ctx_len // ROW_TILE)
    return pl.pallas_call(
        kern,
        grid=(b, r // ROW_TILE),
        in_specs=[
            pl.BlockSpec((None, ROW_TILE, d), lambda bi, i: (bi, i, 0)),
            pl.BlockSpec((1, d), lambda bi, i: (0, 0)),
            pl.BlockSpec((MOD_ROWS, 6 * d), lambda bi, i: (0, 0)),
            pl.BlockSpec((d, n), lambda bi, i: (0, 0)),
            pl.BlockSpec((ROW_TILE, 2 * LANES), lambda bi, i: (i, 0)),
            pl.BlockSpec((ROW_TILE, 2 * LANES), lambda bi, i: (i, 0)),
        ],
        out_specs=pl.BlockSpec((None, ROW_TILE, n), lambda bi, i: (bi, i, 0)),
        out_shape=jax.ShapeDtypeStruct((b, r, n), BF16),
        compiler_params=_params(("arbitrary", "arbitrary")),
        name="qkv",
    )(x, g, mod, w, cos, sin)


def _rope_tables(seq, ctx_len):
    t = jnp.arange(seq, dtype=jnp.int32)
    row = (t // GRID_W).astype(F32)
    col = (t % GRID_W).astype(F32)
    inv = ROPE_BASE ** (-jnp.arange(ROPE_AXIS_PAIRS, dtype=F32) / ROPE_AXIS_PAIRS)
    ang_r = row[:, None] * inv[None, :]
    ang_c = col[:, None] * inv[None, :]
    ang = jnp.concatenate([ang_r, ang_r, ang_c, ang_c], axis=-1)
    sign = jnp.tile(jnp.concatenate([-jnp.ones((ROPE_AXIS_PAIRS,), F32), jnp.ones((ROPE_AXIS_PAIRS,), F32)]), 2)
    cos = jnp.concatenate([jnp.ones((ctx_len, HEAD_DIM), F32), jnp.cos(ang)], axis=0)
    sin = jnp.concatenate([jnp.zeros((ctx_len, HEAD_DIM), F32), jnp.sin(ang) * sign[None, :]], axis=0)
    cos = jnp.tile(cos, (1, LANES // HEAD_DIM))
    sin = jnp.tile(sin, (1, LANES // HEAD_DIM))
    return jnp.concatenate([cos, cos * Q_SCALE], axis=1), jnp.concatenate([sin, sin * Q_SCALE], axis=1)


def _store_with_ones(dst_ref, v):
    dst_ref[:, 0:LANES] = v
    dst_ref[:, LANES:2 * LANES] = jnp.ones(v.shape, v.dtype)


def _head_masks(rows):
    lane = lax.broadcasted_iota(jnp.int32, (rows, LANES), 1)
    lo = lane < HEAD_DIM
    return lo, jnp.logical_not(lo)


def _na_kernel(q_ref, k_ref, v_ref, fb_ref, o_ref, v1_ref, nb_a, cx_a, nb_b, cx_b, *, ctx_len, grid_rows, need_ctx):
    _store_with_ones(v1_ref, v_ref[...])
    kc = k_ref[0:ctx_len, :]
    vc = v1_ref[0:ctx_len, :]

    def normalised(r):
        return r[:, 0:LANES] / r[:, LANES:2 * LANES]

    if need_ctx:
        qc = q_ref[0:ctx_len, :]
        lo, hi = _head_masks(ctx_len)
        outs = []
        for m in (lo, hi):
            s = _dot_nt(jnp.where(m, qc, 0), kc)
            e = jnp.exp2(s - jnp.max(s, axis=-1, keepdims=True))
            outs.append(normalised(_dot(e.astype(BF16), vc)))
        o_ref[0:ctx_len, :] = jnp.where(lo, outs[0], outs[1]).astype(o_ref.dtype)
    else:
        o_ref[0:ctx_len, :] = jnp.zeros((ctx_len, LANES), o_ref.dtype)

    gq = NA_GROUP * GRID_W
    uk = NA_UNION * GRID_W
    n_groups = grid_rows // NA_GROUP
    lo, hi = _head_masks(gq)

    def rows_of(g):
        u0 = jnp.clip(g * NA_GROUP - NA_WIN_R // 2, 0, grid_rows - NA_UNION)
        return pl.multiple_of(ctx_len + g * gq, gq), pl.multiple_of(ctx_len + u0 * GRID_W, GRID_W)

    def scores(g, nb_ref, cx_ref):
        q0, w0 = rows_of(g)
        pattern = jnp.where(g == 0, 0, jnp.where(g == n_groups - 1, 2, 1))
        q_g = q_ref[pl.ds(q0, gq), :]
        k_u = k_ref[pl.ds(w0, uk), :]
        for j, m in enumerate((lo, hi)):
            qj = jnp.where(m, q_g, 0)
            nb_ref[j] = _dot_nt(qj, k_u) + fb_ref[j, pattern]
            cx_ref[j] = _dot_nt(qj, kc)

    def finish(g, nb_ref, cx_ref):
        q0, w0 = rows_of(g)
        v_u = v1_ref[pl.ds(w0, uk), :]
        outs = []
        for j in range(2):
            s_nb = nb_ref[j]
            s_cx = cx_ref[j]
            mx = jnp.maximum(jnp.max(s_nb, axis=-1, keepdims=True), jnp.max(s_cx, axis=-1, keepdims=True))
            e_nb = jnp.exp2(s_nb - mx).astype(BF16)
            e_cx = jnp.exp2(s_cx - mx).astype(BF16)
            outs.append(normalised(_dot(e_cx, vc) + _dot(e_nb, v_u)))
        o_ref[pl.ds(q0, gq), :] = jnp.where(lo, outs[0], outs[1]).astype(o_ref.dtype)

    scores(0, nb_a, cx_a)

    def pair_body(t, carry):
        g = 2 * t
        scores(g + 1, nb_b, cx_b)
        finish(g, nb_a, cx_a)
        scores(jnp.minimum(g + 2, n_groups - 1), nb_a, cx_a)
        finish(g + 1, nb_b, cx_b)
        return carry

    lax.fori_loop(0, n_groups // 2, pair_body, 0)


def _na_bias_table(rpb, grid_rows):
    n_groups = grid_rows // NA_GROUP
    assert grid_rows % (2 * NA_GROUP) == 0 and n_groups >= 3 and grid_rows >= NA_UNION
    qcol = jnp.arange(GRID_W)
    kcol = jnp.arange(GRID_W)
    c0 = jnp.clip(qcol - NA_WIN_C // 2, 0, GRID_W - NA_WIN_C)
    col_ok = (kcol[None, :] >= c0[:, None]) & (kcol[None, :] < c0[:, None] + NA_WIN_C)
    rel = kcol[None, :] - qcol[:, None] + NA_WIN_C - 1
    onehot = ((rel[:, :, None] == jnp.arange(2 * NA_WIN_C - 1)[None, None, :]) & col_ok[:, :, None]).astype(F32)
    bt = jnp.einsum('hyd,qkd->hyqk', rpb.astype(F32), onehot, precision=lax.Precision.HIGHEST)
    bt = jnp.where(col_ok[None, None], bt * LOG2E, NEG_INF)
    masked = jnp.full((rpb.shape[0], GRID_W, GRID_W), NEG_INF, F32)

    def group_table(g):
        u0 = min(max(g * NA_GROUP - NA_WIN_R // 2, 0), grid_rows - NA_UNION)
        rows = []
        for a in range(NA_GROUP):
            r = g * NA_GROUP + a
            r0 = min(max(r - NA_WIN_R // 2, 0), grid_rows - NA_WIN_R)
            blocks = [bt[:, u0 + y - r + NA_WIN_R - 1] if r0 <= u0 + y < r0 + NA_WIN_R else masked
                      for y in range(NA_UNION)]
            rows.append(jnp.concatenate(blocks, axis=-1))
        return jnp.concatenate(rows, axis=-2)

    for g in range(1, n_groups - 1):
        assert min(max(g * NA_GROUP - NA_WIN_R // 2, 0), grid_rows - NA_UNION) == g * NA_GROUP - NA_WIN_R // 2
        assert 0 <= g * NA_GROUP - NA_WIN_R // 2 and (g + 1) * NA_GROUP - 1 - NA_WIN_R // 2 <= grid_rows - NA_WIN_R
    return jnp.stack([group_table(0), group_table(1), group_table(n_groups - 1)], axis=1)


def _na_attention(qkv, rpb, ctx_len, need_ctx):
    b, r, _ = qkv.shape
    grid_rows = (r - ctx_len) // GRID_W
    nblk = D_MODEL // LANES
    fb = _na_bias_table(rpb, grid_rows)
    kern = functools.partial(_na_kernel, ctx_len=ctx_len, grid_rows=grid_rows, need_ctx=need_ctx)
    return pl.pallas_call(
        kern,
        grid=(nblk, b),
        in_specs=[
            pl.BlockSpec((None, r, LANES), lambda hp, bi: (bi, 0, hp)),
            pl.BlockSpec((None, r, LANES), lambda hp, bi: (bi, 0, nblk + hp)),
            pl.BlockSpec((None, r, LANES), lambda hp, bi: (bi, 0, 2 * nblk + hp)),
            pl.BlockSpec((2, 3, NA_GROUP * GRID_W, NA_UNION * GRID_W), lambda hp, bi: (hp, 0, 0, 0)),
        ],
        out_specs=pl.BlockSpec((None, r, LANES), lambda hp, bi: (bi, 0, hp)),
        out_shape=jax.ShapeDtypeStruct((b, r, D_MODEL), BF16),
        scratch_shapes=[pltpu.VMEM((r, 2 * LANES), BF16)]
        + [pltpu.VMEM((2, NA_GROUP * GRID_W, NA_UNION * GRID_W), F32),
           pltpu.VMEM((2, NA_GROUP * GRID_W, ctx_len), F32)] * 2,
        compiler_params=_params(("arbitrary", "arbitrary")),
        name="na_attn",
    )(qkv, qkv, qkv, fb)


def _da_kernel(q_ref, k_ref, v_ref, lam_ref, g_ref, o_ref, v1_ref, *, ctx_len, lam_init):
    lp = lam_ref[...]
    lam = (jnp.exp(jnp.sum(lp[0:1] * lp[1:2], axis=-1, keepdims=True))
           - jnp.exp(jnp.sum(lp[2:3] * lp[3:4], axis=-1, keepdims=True)) + lam_init)
    q = q_ref[...]
    lo, hi = _head_masks(q.shape[0])

    @pl.when(pl.program_id(2) == 0)
    def _():
        _store_with_ones(v1_ref, v_ref[...])

    def attend(n_keys):
        k = k_ref[0:n_keys, :]
        v1 = v1_ref[0:n_keys, :]
        ps = []
        for m in (lo, hi):
            s = _dot_nt(jnp.where(m, q, 0), k)
            e = jnp.exp2(s - jnp.max(s, axis=-1, keepdims=True))
            r = _dot(e.astype(BF16), v1)
            ps.append(r[:, 0:LANES] / r[:, LANES:2 * LANES])
        o = ps[0] - lam * ps[1]
        o_ref[...] = (_rms(o) * g_ref[...] * (1.0 - lam_init)).astype(o_ref.dtype)

    is_ctx = pl.program_id(2) < ctx_len // ROW_TILE

    @pl.when(is_ctx)
    def _():
        attend(ctx_len)

    @pl.when(jnp.logical_not(is_ctx))
    def _():
        attend(k_ref.shape[0])


def _da_attention(qkv, lam_params, subln_g, layer_idx, ctx_len):
    b, r, _ = qkv.shape
    nblk = D_MODEL // LANES
    lam_init = 0.8 - 0.6 * math.exp(-0.3 * layer_idx)
    kern = functools.partial(_da_kernel, ctx_len=ctx_len, lam_init=lam_init)
    return pl.pallas_call(
        kern,
        grid=(nblk, b, r // ROW_TILE),
        in_specs=[
            pl.BlockSpec((None, ROW_TILE, LANES), lambda h, bi, i: (bi, i, h)),
            pl.BlockSpec((None, r, LANES), lambda h, bi, i: (bi, 0, nblk + h)),
            pl.BlockSpec((None, r, LANES), lambda h, bi, i: (bi, 0, 2 * nblk + h)),
            pl.BlockSpec((4, HEAD_DIM), lambda h, bi, i: (0, 0)),
            pl.BlockSpec((1, 2 * HEAD_DIM), lambda h, bi, i: (0, 0)),
        ],
        out_specs=pl.BlockSpec((None, ROW_TILE, LANES), lambda h, bi, i: (bi, i, h)),
        out_shape=jax.ShapeDtypeStruct((b, r, D_MODEL), BF16),
        scratch_shapes=[pltpu.VMEM((r, 2 * LANES), BF16)],
        compiler_params=_params(("arbitrary", "arbitrary", "arbitrary")),
        name="da_attn",
    )(qkv, qkv, qkv, lam_params.astype(F32), subln_g.astype(F32).reshape(1, -1))


def _sw_kernel(sink_ref, q_ref, k_ref, v_ref, o_ref, kd_ref, vd_ref, sl_a, sc_a, sl_b, sc_b, *, ctx_len, seq, need_ctx):
    kvp = pl.program_id(0)
    band = Q_BLOCK + 2 * SW_WINDOW
    hpk = SW_Q_HEADS // SW_KV_HEADS

    k = k_ref[...]
    v = v_ref[...]
    for hh in range(2):
        sl = slice(hh * HEAD_DIM, (hh + 1) * HEAD_DIM)
        kd_ref[hh] = jnp.concatenate([k[:, sl], k[:, sl]], axis=-1)
        _store_with_ones(vd_ref.at[hh], jnp.concatenate([v[:, sl], v[:, sl]], axis=-1))

    n_blk = 2 * (hpk // 2)

    def stacked_q(row0, rows, c):
        lo, hi = _head_masks(rows)
        qv = q_ref[pl.ds(row0, rows), c * LANES:(c + 1) * LANES]
        return jnp.concatenate([jnp.where(lo, qv, 0), jnp.where(hi, qv, 0)], axis=0)

    def per_head(rows, c, col, fn):
        sinks = [sink_ref[kvp * 2 * hpk + 2 * c + j] * LOG2E for j in range(2)]
        return jnp.concatenate([fn(col[j * rows:(j + 1) * rows], sinks[j]) for j in range(2)], axis=0)

    def write_out(row0, rows, c, o):
        lo, _ = _head_masks(rows)
        o_ref[pl.ds(row0, rows), c * LANES:(c + 1) * LANES] = jnp.where(lo, o[0:rows], o[rows:2 * rows]).astype(o_ref.dtype)

    def softmax_pv(rows, c, parts):
        row_max = functools.reduce(jnp.maximum, [jnp.max(s, axis=-1, keepdims=True) for s, _ in parts])
        mx = per_head(rows, c, row_max, lambda m, sink: jnp.maximum(m, sink))
        r = sum(_dot(jnp.exp2(s - mx).astype(BF16), val) for s, val in parts)
        den = r[:, LANES:2 * LANES] + per_head(rows, c, mx, lambda m, sink: jnp.exp2(sink - m))
        return r[:, 0:LANES] / den

    if need_ctx:
        for c in range(n_blk):
            hh = c // (hpk // 2)
            s = _dot_nt(stacked_q(0, ctx_len, c), kd_ref[hh, 0:ctx_len, :])
            write_out(0, ctx_len, c, softmax_pv(ctx_len, c, [(s, vd_ref[hh, 0:ctx_len, :])]))
    else:
        o_ref[0:ctx_len, :] = jnp.zeros((ctx_len, o_ref.shape[-1]), o_ref.dtype)

    n_steps = seq // Q_BLOCK

    def rows_of(n):
        start = n * Q_BLOCK
        b0 = jnp.clip(start - SW_WINDOW, 0, seq - band)
        return start, b0, pl.multiple_of(ctx_len + start, Q_BLOCK), pl.multiple_of(ctx_len + b0, Q_BLOCK)

    def scores(n, sl_ref, sc_ref):
        start, b0, qrow, brow = rows_of(n)
        qpos = start + (lax.broadcasted_iota(jnp.int32, (2 * Q_BLOCK, band), 0) & (Q_BLOCK - 1))
        kpos = b0 + lax.broadcasted_iota(jnp.int32, (2 * Q_BLOCK, band), 1)
        in_band = jnp.abs(kpos - qpos) <= SW_WINDOW
        for c in range(n_blk):
            hh = c // (hpk // 2)
            qs = stacked_q(qrow, Q_BLOCK, c)
            sl_ref[c] = jnp.where(in_band, _dot_nt(qs, kd_ref[hh, pl.ds(brow, band), :]), NEG_INF)
            sc_ref[c] = _dot_nt(qs, kd_ref[hh, 0:ctx_len, :])

    def finish(n, sl_ref, sc_ref):
        _, _, qrow, brow = rows_of(n)
        for c in range(n_blk):
            hh = c // (hpk // 2)
            o = softmax_pv(Q_BLOCK, c, [(sc_ref[c], vd_ref[hh, 0:ctx_len, :]),
                                        (sl_ref[c], vd_ref[hh, pl.ds(brow, band), :])])
            write_out(qrow, Q_BLOCK, c, o)

    scores(0, sl_a, sc_a)

    def pair_body(t, carry):
        n = 2 * t
        scores(n + 1, sl_b, sc_b)
        finish(n, sl_a, sc_a)
        scores(jnp.minimum(n + 2, n_steps - 1), sl_a, sc_a)
        finish(n + 1, sl_b, sc_b)
        return carry

    lax.fori_loop(0, n_steps // 2, pair_body, 0)


def _sw_attention(qkv, sinks, ctx_len, need_ctx):
    b, r, _ = qkv.shape
    seq = r - ctx_len
    assert seq % (2 * Q_BLOCK) == 0
    nq_blk = SW_Q_HEADS * HEAD_DIM // LANES
    slab = 2 * (SW_Q_HEADS // SW_KV_HEADS) * HEAD_DIM
    kern = functools.partial(_sw_kernel, ctx_len=ctx_len, seq=seq, need_ctx=need_ctx)
    n_kv_blk = SW_KV_HEADS * HEAD_DIM // LANES
    return pl.pallas_call(
        kern,
        grid=(n_kv_blk, b),
        in_specs=[
            pl.BlockSpec(memory_space=pltpu.SMEM),
            pl.BlockSpec((None, r, slab), lambda kp, bi: (bi, 0, kp)),
            pl.BlockSpec((None, r, LANES), lambda kp, bi: (bi, 0, nq_blk + kp)),
            pl.BlockSpec((None, r, LANES), lambda kp, bi: (bi, 0, nq_blk + n_kv_blk + kp)),
        ],
        out_specs=pl.BlockSpec((None, r, slab), lambda kp, bi: (bi, 0, kp)),
        out_shape=jax.ShapeDtypeStruct((b, r, D_MODEL), BF16),
        scratch_shapes=[pltpu.VMEM((2, r, LANES), BF16), pltpu.VMEM((2, r, 2 * LANES), BF16)]
        + [pltpu.VMEM((slab // LANES, 2 * Q_BLOCK, Q_BLOCK + 2 * SW_WINDOW), F32),
           pltpu.VMEM((slab // LANES, 2 * Q_BLOCK, ctx_len), F32)] * 2,
        compiler_params=_params(("arbitrary", "arbitrary")),
        name="sw_attn",
    )(sinks.astype(F32), qkv, qkv, qkv)


def _post_kernel(o_ref, x_ref, wo_ref, mod_ref, g_ref, wrh_ref, wrl_ref, br_ref, xn_ref, h_ref, rt_ref, *, ctx_tiles):
    d = x_ref.shape[-1]
    tm = x_ref.shape[0]
    row = _mod_row(pl.program_id(0), pl.program_id(1), ctx_tiles)
    gate1 = mod_ref[pl.ds(row, 1), 2 * d:3 * d]
    shift2 = mod_ref[pl.ds(row, 1), 3 * d:4 * d]
    scale2 = mod_ref[pl.ds(row, 1), 4 * d:5 * d]
    xn = x_ref[...] + gate1 * _dot(o_ref[...], wo_ref[...])
    xn_ref[...] = xn
    h = (_rms(xn) * g_ref[...]) * (1.0 + scale2) + shift2
    for j in range(SUBLANES):
        h_ref[pl.ds(j, tm, stride=SUBLANES), :] = h[:, j * LANES:(j + 1) * LANES]

    h_hi = h.astype(BF16)
    h_lo = (h - h_hi.astype(F32)).astype(BF16)
    logits = (_dot(h_hi, wrh_ref[...]) + _dot(h_lo, wrh_ref[...]) + _dot(h_hi, wrl_ref[...])) + br_ref[...]
    lane = lax.broadcasted_iota(jnp.int32, (tm, LANES), 1).astype(F32)

    def first_argmax(vals, mx):
        return jnp.min(jnp.where(vals == mx, lane, float(LANES)), axis=-1, keepdims=True)

    gl = jnp.where(lane < N_GROUPS, logits, -jnp.inf)
    ge = jnp.exp(gl - jnp.max(gl, axis=-1, keepdims=True))
    gp = ge / jnp.sum(ge, axis=-1, keepdims=True)
    g_p = jnp.max(gp, axis=-1, keepdims=True)
    g_idx = first_argmax(gp, g_p)
    e_lo = N_GROUPS + EXPERTS_PER_GROUP * g_idx
    el = jnp.where((lane >= e_lo) & (lane < e_lo + EXPERTS_PER_GROUP), logits, -jnp.inf)
    v1 = jnp.max(el, axis=-1, keepdims=True)
    i1 = first_argmax(el, v1)
    el2 = jnp.where(lane == i1, -jnp.inf, el)
    v2 = jnp.max(el2, axis=-1, keepdims=True)
    i2 = first_argmax(el2, v2)
    e2 = jnp.exp(v2 - v1)
    w1 = g_p * (1.0 / (1.0 + e2))
    w2 = g_p * (e2 / (1.0 + e2))
    lane8 = lax.broadcasted_iota(jnp.int32, (tm, SUBLANES), 1)
    rt = jnp.where(lane8 == 0, i1 - N_GROUPS,
                   jnp.where(lane8 == 1, i2 - N_GROUPS,
                             jnp.where(lane8 == 2, w1, jnp.where(lane8 == 3, w2, 0.0))))
    rt_ref[...] = rt


def _post(o, x, wo, mod, g, wr, br, ctx_len):
    wr_hi = wr.astype(BF16)
    wr_lo = (wr - wr_hi.astype(F32)).astype(BF16)
    b, r, d = x.shape
    nt = r // ROW_TILE
    kern = functools.partial(_post_kernel, ctx_tiles=ctx_len // ROW_TILE)
    return pl.pallas_call(
        kern,
        grid=(b, nt),
        in_specs=[
            pl.BlockSpec((None, ROW_TILE, d), lambda bi, i: (bi, i, 0)),
            pl.BlockSpec((None, ROW_TILE, d), lambda bi, i: (bi, i, 0)),
            pl.BlockSpec((d, d), lambda bi, i: (0, 0)),
            pl.BlockSpec((MOD_ROWS, 6 * d), lambda bi, i: (0, 0)),
            pl.BlockSpec((1, d), lambda bi, i: (0, 0)),
            pl.BlockSpec((d, LANES), lambda bi, i: (0, 0)),
            pl.BlockSpec((d, LANES), lambda bi, i: (0, 0)),
            pl.BlockSpec((1, LANES), lambda bi, i: (0, 0)),
        ],
        out_specs=[
            pl.BlockSpec((None, ROW_TILE, d), lambda bi, i: (bi, i, 0)),
            pl.BlockSpec((ROW_TILE * SUBLANES, LANES), lambda bi, i: (bi * nt + i, 0)),
            pl.BlockSpec((ROW_TILE, SUBLANES), lambda bi, i: (bi * nt + i, 0)),
        ],
        out_shape=[
            jax.ShapeDtypeStruct((b, r, d), F32),
            jax.ShapeDtypeStruct((b * r * SUBLANES, LANES), F32),
            jax.ShapeDtypeStruct((b * r, SUBLANES), F32),
        ],
        compiler_params=_params(("arbitrary", "arbitrary")),
        name="post_attn",
    )(o, x, wo, mod, g, wr_hi, wr_lo, br)


DISPATCH_TOKENS = 512


def _row(ref, idx):
    return ref.at[pl.ds(pl.multiple_of(idx * SUBLANES, SUBLANES), SUBLANES), :]


def _dispatch_kernel(dest_ref, h_ref, xs_hbm, sem):
    def issue(t, carry):
        src = _row(h_ref, t)
        for k in range(2):
            pltpu.make_async_copy(src, _row(xs_hbm, dest_ref[0, 0, 2 * t + k]), sem).start(priority=k)
        return carry

    lax.fori_loop(0, DISPATCH_TOKENS, issue, 0)
    for _ in range(2):
        pltpu.make_async_copy(h_ref, xs_hbm.at[pl.ds(0, DISPATCH_TOKENS * SUBLANES), :], sem).wait()


def _dispatch(h_rows, dest):
    t = dest.shape[0] // 2
    steps = t // DISPATCH_TOKENS
    return pl.pallas_call(
        _dispatch_kernel,
        grid=(steps,),
        in_specs=[
            pl.BlockSpec((1, 1, 2 * DISPATCH_TOKENS), lambda i: (i, 0, 0), memory_space=pltpu.SMEM),
            pl.BlockSpec((DISPATCH_TOKENS * SUBLANES, LANES), lambda i: (i, 0)),
        ],
        out_specs=pl.BlockSpec(memory_space=pl.ANY),
        out_shape=jax.ShapeDtypeStruct((2 * t * SUBLANES, LANES), F32),
        scratch_shapes=[pltpu.SemaphoreType.DMA(())],
        compiler_params=_params(("arbitrary",)),
        name="moe_dispatch",
    )(dest.reshape(steps, 1, 2 * DISPATCH_TOKENS), h_rows)


def _experts_kernel(ve_ref, vt_ref, lo_ref, hi_ref, first_ref, valid_ref,
                    xs_ref, wg_ref, wu_ref, wd_ref, ys_ref, wgu_s, wd_s):
    v = pl.program_id(0)
    tm = ROW_TILE
    f = wg_ref.shape[-1]

    @pl.when(valid_ref[v] == 1)
    def _():
        changed = jnp.logical_or(v == 0, ve_ref[v] != ve_ref[jnp.maximum(v - 1, 0)])

        @pl.when(changed)
        def _():
            wgu_s[:, 0:f] = wg_ref[...].astype(BF16)
            wgu_s[:, f:2 * f] = wu_ref[...].astype(BF16)
            wd_s[...] = wd_ref[...].astype(BF16)

        x = jnp.concatenate([xs_ref[pl.ds(j, tm, stride=SUBLANES), :] for j in range(SUBLANES)], axis=-1)
        au = _dot(x.astype(BF16), wgu_s[...])
        rows = vt_ref[v] * tm + lax.broadcasted_iota(jnp.int32, (tm, 1), 0)
        mine = (rows >= lo_ref[v]) & (rows < hi_ref[v])
        mid = jnp.where(mine, jax.nn.silu(au[:, 0:f]) * au[:, f:2 * f], 0.0)
        y = _dot(mid.astype(BF16), wd_s[...])

        @pl.when(first_ref[v] == 1)
        def _():
            for j in range(SUBLANES):
                ys_ref[pl.ds(j, tm, stride=SUBLANES), :] = y[:, j * LANES:(j + 1) * LANES]

        @pl.when(first_ref[v] == 0)
        def _():
            for j in range(SUBLANES):
                ys_ref[pl.ds(j, tm, stride=SUBLANES), :] += y[:, j * LANES:(j + 1) * LANES]


def _experts(xs, sched, w_gate, w_up, w_down, layer):
    rows = xs.shape[0] // SUBLANES
    d, f = w_gate.shape[-2:]
    n_visits = sched[0].shape[0]
    blk = (ROW_TILE * SUBLANES, LANES)
    grid_spec = pltpu.PrefetchScalarGridSpec(
        num_scalar_prefetch=6,
        grid=(n_visits,),
        in_specs=[
            pl.BlockSpec(blk, lambda v, ve, vt, lo, hi, fi, va: (vt[v], 0)),
            pl.BlockSpec((None, None, d, f), lambda v, ve, vt, lo, hi, fi, va: (layer, ve[v], 0, 0)),
            pl.BlockSpec((None, None, d, f), lambda v, ve, vt, lo, hi, fi, va: (layer, ve[v], 0, 0)),
            pl.BlockSpec((None, None, f, d), lambda v, ve, vt, lo, hi, fi, va: (layer, ve[v], 0, 0)),
        ],
        out_specs=pl.BlockSpec(blk, lambda v, ve, vt, lo, hi, fi, va: (vt[v], 0)),
        scratch_shapes=[pltpu.VMEM((d, 2 * f), BF16), pltpu.VMEM((f, d), BF16)],
    )
    return pl.pallas_call(
        _experts_kernel,
        grid_spec=grid_spec,
        out_shape=jax.ShapeDtypeStruct((rows * SUBLANES, LANES), F32),
        compiler_params=_params(("arbitrary",)),
        name="moe_experts",
    )(*sched, xs, w_gate, w_up, w_down)


def _combine_kernel(dest_ref, next_ref, rt_ref, x_ref, mod_ref, ys_hbm, o_ref, buf, sem, *, ctx_tiles, tiles_per_batch):
    tm = ROW_TILE
    d = x_ref.shape[-1]
    s = pl.program_id(0)
    slot = s % 2
    half_rows = tm * SUBLANES

    def gather(idx_ref, slot_):
        def issue(t, carry):
            for k in range(2):
                pltpu.make_async_copy(_row(ys_hbm, idx_ref[0, 0, 2 * t + k]),
                                      _row(buf.at[slot_], k * tm + t), sem.at[slot_]).start(priority=k)
            return carry

        lax.fori_loop(0, tm, issue, 0)

    @pl.when(s == 0)
    def _():
        gather(dest_ref, 0)

    @pl.when(s + 1 < pl.num_programs(0))
    def _():
        gather(next_ref, 1 - slot)

    for k in range(2):
        part = pl.ds(k * half_rows, half_rows)
        pltpu.make_async_copy(ys_hbm.at[part, :], buf.at[slot].at[part, :], sem.at[slot]).wait()

    row = _mod_row(s // tiles_per_batch, s % tiles_per_batch, ctx_tiles)
    gate2 = mod_ref[pl.ds(row, 1), 5 * d:6 * d]
    rt = rt_ref[...]
    w = (rt[:, 2:3], rt[:, 3:4])
    rows = buf.at[slot]
    for j in range(SUBLANES):
        y = (w[0] * rows[pl.ds(j, tm, stride=SUBLANES), :]
             + w[1] * rows[pl.ds(half_rows + j, tm, stride=SUBLANES), :])
        sl = slice(j * LANES, (j + 1) * LANES)
        o_ref[:, sl] = x_ref[:, sl] + gate2[:, sl] * y


def _combine(ys, dest, rt, x, mod, ctx_len):
    b, r, d = x.shape
    t = b * r
    steps = t // ROW_TILE
    kern = functools.partial(_combine_kernel, ctx_tiles=ctx_len // ROW_TILE, tiles_per_batch=r // ROW_TILE)
    dest3 = dest.reshape(steps, 1, 2 * ROW_TILE)
    out = pl.pallas_call(
        kern,
        grid=(steps,),
        in_specs=[
            pl.BlockSpec((1, 1, 2 * ROW_TILE), lambda i: (i, 0, 0), memory_space=pltpu.SMEM),
            pl.BlockSpec((1, 1, 2 * ROW_TILE), lambda i: (jnp.minimum(i + 1, steps - 1), 0, 0), memory_space=pltpu.SMEM),
            pl.BlockSpec((ROW_TILE, SUBLANES), lambda i: (i, 0)),
            pl.BlockSpec((ROW_TILE, d), lambda i: (i, 0)),
            pl.BlockSpec((MOD_ROWS, 6 * d), lambda i: (0, 0)),
            pl.BlockSpec(memory_space=pl.ANY),
        ],
        out_specs=pl.BlockSpec((ROW_TILE, d), lambda i: (i, 0)),
        out_shape=jax.ShapeDtypeStruct((t, d), F32),
        scratch_shapes=[pltpu.VMEM((2, 2 * ROW_TILE * SUBLANES, LANES), F32), pltpu.SemaphoreType.DMA((2,))],
        compiler_params=_params(("arbitrary",)),
        name="moe_combine",
    )(dest3, dest3, rt, x.reshape(t, d), mod, ys)
    return out.reshape(b, r, d)


def _route_plan(rt, n_tiles):
    eid = rt[:, 0:2].astype(jnp.int32).reshape(-1)
    onehot = (eid[:, None] == jnp.arange(N_EXPERTS, dtype=jnp.int32)[None, :]).astype(jnp.int32)
    csum = jnp.cumsum(onehot, axis=0)
    counts = csum[-1]
    cstart = jnp.cumsum(counts) - counts
    dest = jnp.sum(onehot * (cstart[None, :] + csum - 1), axis=1).astype(jnp.int32)

    tm = ROW_TILE
    first_tile = cstart // tm
    last_tile = (cstart + counts - 1) // tm
    nvis = jnp.where(counts > 0, last_tile - first_tile + 1, 0)
    vend = jnp.cumsum(nvis)
    vstart = vend - nvis
    total = vend[-1]
    n_visits = n_tiles + N_EXPERTS - 1
    v = jnp.arange(n_visits, dtype=jnp.int32)
    valid = v < total
    vc = jnp.minimum(v, total - 1)
    ve = jnp.sum((vc[:, None] >= vend[None, :]).astype(jnp.int32), axis=1)
    oh_e = (ve[:, None] == jnp.arange(N_EXPERTS, dtype=jnp.int32)[None, :]).astype(jnp.int32)
    pick = lambda a: jnp.sum(oh_e * a[None, :], axis=1)
    vt = pick(first_tile) + vc - pick(vstart)
    lo = pick(cstart)
    hi = jnp.where(valid, lo + pick(counts), lo)
    first = jnp.concatenate([jnp.ones((1,), jnp.int32), (vt[1:] != vt[:-1]).astype(jnp.int32)])
    sched = tuple(a.astype(jnp.int32) for a in (ve, vt, lo, hi, first, valid))
    return dest, sched


def _final_kernel(x_ref, g_ref, o_ref):
    o_ref[...] = _rms(x_ref[...]) * g_ref[...]


def _final_norm(x, g, ctx_len):
    b, r, d = x.shape
    ct = ctx_len // ROW_TILE
    return pl.pallas_call(
        _final_kernel,
        grid=(b, (r - ctx_len) // ROW_TILE),
        in_specs=[
            pl.BlockSpec((None, ROW_TILE, d), lambda bi, i: (bi, i + ct, 0)),
            pl.BlockSpec((1, d), lambda bi, i: (0, 0)),
        ],
        out_specs=pl.BlockSpec((None, ROW_TILE, d), lambda bi, i: (bi, i, 0)),
        out_shape=jax.ShapeDtypeStruct((b, r - ctx_len, d), F32),
        compiler_params=_params(("arbitrary", "arbitrary")),
        name="final_norm",
    )(x, g)


def kernel(x, c, ctx, c_ctx, ada_w, ada_b, norm_mix_g, norm_ffn_g, final_norm_g, na_w_qkv, na_rpb, na_w_o, da_w_qkv, da_lambda, da_subln_g, da_w_o, sw_w_qkv, sw_sinks, sw_w_o, moe_w_grp, moe_b_grp, moe_w_exp, moe_b_exp, moe_w_gate, moe_w_up, moe_w_down):
    b, seq, d = x.shape
    ctx_len = ctx.shape[1]
    depth = ada_w.shape[0]
    assert d == D_MODEL and ctx_len % ROW_TILE == 0 and seq % ROW_TILE == 0 and b <= MOD_ROWS // 2
    assert (b * (ctx_len + seq)) % DISPATCH_TOKENS == 0

    half = MOD_ROWS // 2
    cond = jnp.concatenate([jnp.pad(c, ((0, half - b), (0, 0))), jnp.pad(c_ctx[None, :], ((0, half - 1), (0, 0)))], axis=0)
    mod_all = _adaln(cond, ada_w, ada_b)
    cos, sin = _rope_tables(seq, ctx_len)
    stream = jnp.concatenate([ctx, x], axis=1)
    n_tiles = 2 * b * (ctx_len + seq) // ROW_TILE

    for i in range(depth):
        need_ctx = i < depth - 1
        kind, j = i % N_MIXERS, i // N_MIXERS
        mod = mod_all[i]
        g_mix = norm_mix_g[i].reshape(1, d)
        if kind == 0:
            qkv = _qkv(stream, g_mix, mod, na_w_qkv[j].astype(BF16), cos, sin, 0, NA_HEADS * HEAD_DIM, ctx_len)
            o = _na_attention(qkv, na_rpb[j], ctx_len, need_ctx)
            wo = na_w_o[j]
        elif kind == 1:
            nq = 2 * DA_HEADS * HEAD_DIM
            qkv = _qkv(stream, g_mix, mod, da_w_qkv[j].astype(BF16), cos, sin, 2 * nq, nq, ctx_len)
            o = _da_attention(qkv, da_lambda[j], da_subln_g[j], i, ctx_len)
            wo = da_w_o[j]
        else:
            rope_cols = (SW_Q_HEADS + SW_KV_HEADS) * HEAD_DIM
            qkv = _qkv(stream, g_mix, mod, sw_w_qkv[j].astype(BF16), cos, sin, rope_cols, SW_Q_HEADS * HEAD_DIM, ctx_len)
            o = _sw_attention(qkv, sw_sinks[j], ctx_len, need_ctx)
            wo = sw_w_o[j]

        pad = LANES - N_GROUPS - N_EXPERTS
        wr = jnp.pad(jnp.concatenate([moe_w_grp[i], moe_w_exp[i]], axis=1).astype(F32), ((0, 0), (0, pad)))
        br = jnp.pad(jnp.concatenate([moe_b_grp[i], moe_b_exp[i]]).astype(F32), (0, pad)).reshape(1, LANES)
        stream, h_rows, rt = _post(o, stream, wo.astype(BF16), mod, norm_ffn_g[i].reshape(1, d), wr, br, ctx_len)
        dest, sched = _route_plan(rt, n_tiles)
        xs = _dispatch(h_rows, dest)
        ys = _experts(xs, sched, moe_w_gate, moe_w_up, moe_w_down, i)
        stream = _combine(ys, dest, rt, stream, mod, ctx_len)

    return _final_norm(stream, final_norm_g.reshape(1, d), ctx_len)
```

```python
import functools
import math

import jax
import jax.numpy as jnp
from jax import lax
from jax.experimental import pallas as pl
from jax.experimental.pallas import tpu as pltpu

F32 = jnp.float32
BF16 = jnp.bfloat16

D_MODEL = 1024
HEAD_DIM = 64
GRID_W = 64
ROPE_AXIS_PAIRS = HEAD_DIM // 4
ROPE_BASE = 10000.0
NA_HEADS = D_MODEL // HEAD_DIM
NA_WIN_R = 8
NA_WIN_C = 16
NA_GROUP = 4
NA_UNION = NA_GROUP + NA_WIN_R
DA_HEADS = D_MODEL // (2 * HEAD_DIM)
SW_Q_HEADS = D_MODEL // HEAD_DIM
SW_KV_HEADS = 4
SW_WINDOW = 128
Q_BLOCK = 128
N_GROUPS = 4
EXPERTS_PER_GROUP = 8
N_EXPERTS = N_GROUPS * EXPERTS_PER_GROUP
D_EXPERT = 512
NORM_EPS = 1e-6
NEG_INF = -1e30
N_MIXERS = 3

LANES = 128
SUBLANES = 8
ROW_TILE = 256
MOD_ROWS = 16
VMEM_LIMIT = 56 * 1024 * 1024

LOG2E = math.log2(math.e)
Q_SCALE = HEAD_DIM ** -0.5 * LOG2E

_NT = (((1,), (1,)), ((), ()))


def _dot(a, b, **kw):
    return jnp.dot(a, b, preferred_element_type=F32, **kw)


def _dot_nt(a, b):
    return lax.dot_general(a, b, _NT, preferred_element_type=F32)


def _params(sem, vmem=VMEM_LIMIT):
    return pltpu.CompilerParams(dimension_semantics=sem, vmem_limit_bytes=vmem)


def _mod_row(b, tile, ctx_tiles):
    return jnp.where(tile < ctx_tiles, MOD_ROWS // 2, b)


def _rms(x):
    return x * lax.rsqrt(jnp.mean(x * x, axis=-1, keepdims=True) + NORM_EPS)


def _adaln_kernel(c_ref, w_ref, b_ref, o_ref):
    s = jax.nn.silu(c_ref[...])
    o_ref[...] = _dot(s, w_ref[...], precision=lax.Precision.HIGHEST) + b_ref[...]


def _adaln(cond, ada_w, ada_b):
    depth, d, n6 = ada_w.shape
    tn = 1536
    return pl.pallas_call(
        _adaln_kernel,
        grid=(depth, n6 // tn),
        in_specs=[
            pl.BlockSpec((MOD_ROWS, d), lambda l, n: (0, 0)),
            pl.BlockSpec((None, d, tn), lambda l, n: (l, 0, n)),
            pl.BlockSpec((None, 1, tn), lambda l, n: (l, 0, n)),
        ],
        out_specs=pl.BlockSpec((None, MOD_ROWS, tn), lambda l, n: (l, 0, n)),
        out_shape=jax.ShapeDtypeStruct((depth, MOD_ROWS, n6), F32),
        compiler_params=_params(("arbitrary", "arbitrary")),
        name="adaln",
    )(cond, ada_w, ada_b.reshape(depth, 1, n6))


def _qkv_kernel(x_ref, g_ref, mod_ref, w_ref, cos_ref, sin_ref, o_ref, *, rope_cols, q_cols, ctx_tiles):
    d = x_ref.shape[-1]
    row = _mod_row(pl.program_id(0), pl.program_id(1), ctx_tiles)
    shift = mod_ref[pl.ds(row, 1), 0:d]
    scale = mod_ref[pl.ds(row, 1), d:2 * d]
    h = (_rms(x_ref[...]) * g_ref[...]) * (1.0 + scale) + shift
    acc = _dot(h.astype(BF16), w_ref[...])
    n = acc.shape[-1]
    if rope_cols:
        tm = acc.shape[0]
        first_half = (lax.broadcasted_iota(jnp.int32, (tm, LANES), 1) % (2 * ROPE_AXIS_PAIRS)) < ROPE_AXIS_PAIRS
        for c in range(rope_cols // LANES):
            tab = slice(LANES, 2 * LANES) if c * LANES < q_cols else slice(0, LANES)
            t = acc[:, c * LANES:(c + 1) * LANES]
            up = pltpu.roll(t, LANES - ROPE_AXIS_PAIRS, 1)
            dn = pltpu.roll(t, ROPE_AXIS_PAIRS, 1)
            o_ref[:, c * LANES:(c + 1) * LANES] = (
                t * cos_ref[:, tab] + jnp.where(first_half, up, dn) * sin_ref[:, tab]).astype(o_ref.dtype)
    else:
        o_ref[:, 0:q_cols] = (acc[:, 0:q_cols] * Q_SCALE).astype(o_ref.dtype)
        rope_cols = q_cols
    if rope_cols < n:
        o_ref[:, rope_cols:] = acc[:, rope_cols:].astype(o_ref.dtype)


def _qkv(x, g, mod, w, cos, sin, rope_cols, q_cols, ctx_len):
    b, r, d = x.shape
    n = w.shape[1]
    kern = functools.partial(_qkv_kernel, rope_cols=rope_cols, q_cols=q_cols, ctx_tiles=ctx_len // ROW_TILE)
    return pl.pallas_call(
        kern,
        grid=(b, r // ROW_TILE),
        in_specs=[
            pl.BlockSpec((None, ROW_TILE, d), lambda bi, i: (bi, i, 0)),
            pl.BlockSpec((1, d), lambda bi, i: (0, 0)),
            pl.BlockSpec((MOD_ROWS, 6 * d), lambda bi, i: (0, 0)),
            pl.BlockSpec((d, n), lambda bi, i: (0, 0)),
            pl.BlockSpec((ROW_TILE, 2 * LANES), lambda bi, i: (i, 0)),
            pl.BlockSpec((ROW_TILE, 2 * LANES), lambda bi, i: (i, 0)),
        ],
        out_specs=pl.BlockSpec((None, ROW_TILE, n), lambda bi, i: (bi, i, 0)),
        out_shape=jax.ShapeDtypeStruct((b, r, n), BF16),
        compiler_params=_params(("arbitrary", "arbitrary")),
        name="qkv",
    )(x, g, mod, w, cos, sin)


def _rope_tables(seq, ctx_len):
    t = jnp.arange(seq, dtype=jnp.int32)
    row = (t // GRID_W).astype(F32)
    col = (t % GRID_W).astype(F32)
    inv = ROPE_BASE ** (-jnp.arange(ROPE_AXIS_PAIRS, dtype=F32) / ROPE_AXIS_PAIRS)
    ang_r = row[:, None] * inv[None, :]
    ang_c = col[:, None] * inv[None, :]
    ang = jnp.concatenate([ang_r, ang_r, ang_c, ang_c], axis=-1)
    sign = jnp.tile(jnp.concatenate([-jnp.ones((ROPE_AXIS_PAIRS,), F32), jnp.ones((ROPE_AXIS_PAIRS,), F32)]), 2)
    cos = jnp.concatenate([jnp.ones((ctx_len, HEAD_DIM), F32), jnp.cos(ang)], axis=0)
    sin = jnp.concatenate([jnp.zeros((ctx_len, HEAD_DIM), F32), jnp.sin(ang) * sign[None, :]], axis=0)
    cos = jnp.tile(cos, (1, LANES // HEAD_DIM))
    sin = jnp.tile(sin, (1, LANES // HEAD_DIM))
    return jnp.concatenate([cos, cos * Q_SCALE], axis=1), jnp.concatenate([sin, sin * Q_SCALE], axis=1)


def _store_with_ones(dst_ref, v):
    dst_ref[:, 0:LANES] = v
    dst_ref[:, LANES:2 * LANES] = jnp.ones(v.shape, v.dtype)


def _head_masks(rows):
    lane = lax.broadcasted_iota(jnp.int32, (rows, LANES), 1)
    lo = lane < HEAD_DIM
    return lo, jnp.logical_not(lo)


def _na_kernel(q_ref, k_ref, v_ref, fb_ref, o_ref, v1_ref, nb_a, cx_a, nb_b, cx_b, *, ctx_len, grid_rows, need_ctx):
    _store_with_ones(v1_ref, v_ref[...])
    kc = k_ref[0:ctx_len, :]
    vc = v1_ref[0:ctx_len, :]

    def normalised(r):
        return r[:, 0:LANES] / r[:, LANES:2 * LANES]

    if need_ctx:
        qc = q_ref[0:ctx_len, :]
        lo, hi = _head_masks(ctx_len)
        outs = []
        for m in (lo, hi):
            s = _dot_nt(jnp.where(m, qc, 0), kc)
            e = jnp.exp2(s - jnp.max(s, axis=-1, keepdims=True))
            outs.append(normalised(_dot(e.astype(BF16), vc)))
        o_ref[0:ctx_len, :] = jnp.where(lo, outs[0], outs[1]).astype(o_ref.dtype)
    else:
        o_ref[0:ctx_len, :] = jnp.zeros((ctx_len, LANES), o_ref.dtype)

    gq = NA_GROUP * GRID_W
    uk = NA_UNION * GRID_W
    n_groups = grid_rows // NA_GROUP
    lo, hi = _head_masks(gq)

    def rows_of(g):
        u0 = jnp.clip(g * NA_GROUP - NA_WIN_R // 2, 0, grid_rows - NA_UNION)
        return pl.multiple_of(ctx_len + g * gq, gq), pl.multiple_of(ctx_len + u0 * GRID_W, GRID_W)

    def scores(g, nb_ref, cx_ref):
        q0, w0 = rows_of(g)
        pattern = jnp.where(g == 0, 0, jnp.where(g == n_groups - 1, 2, 1))
        q_g = q_ref[pl.ds(q0, gq), :]
        k_u = k_ref[pl.ds(w0, uk), :]
        for j, m in enumerate((lo, hi)):
            qj = jnp.where(m, q_g, 0)
            nb_ref[j] = _dot_nt(qj, k_u) + fb_ref[j, pattern]
            cx_ref[j] = _dot_nt(qj, kc)

    def finish(g, nb_ref, cx_ref):
        q0, w0 = rows_of(g)
        v_u = v1_ref[pl.ds(w0, uk), :]
        outs = []
        for j in range(2):
            s_nb = nb_ref[j]
            s_cx = cx_ref[j]
            mx = jnp.maximum(jnp.max(s_nb, axis=-1, keepdims=True), jnp.max(s_cx, axis=-1, keepdims=True))
            e_nb = jnp.exp2(s_nb - mx).astype(BF16)
            e_cx = jnp.exp2(s_cx - mx).astype(BF16)
            outs.append(normalised(_dot(e_cx, vc) + _dot(e_nb, v_u)))
        o_ref[pl.ds(q0, gq), :] = jnp.where(lo, outs[0], outs[1]).astype(o_ref.dtype)

    scores(0, nb_a, cx_a)

    def pair_body(t, carry):
        g = 2 * t
        scores(g + 1, nb_b, cx_b)
        finish(g, nb_a, cx_a)
        scores(jnp.minimum(g + 2, n_groups - 1), nb_a, cx_a)
        finish(g + 1, nb_b, cx_b)
        return carry

    lax.fori_loop(0, n_groups // 2, pair_body, 0)


def _na_bias_table(rpb, grid_rows):
    n_groups = grid_rows // NA_GROUP
    assert grid_rows % (2 * NA_GROUP) == 0 and n_groups >= 3 and grid_rows >= NA_UNION
    qcol = jnp.arange(GRID_W)
    kcol = jnp.arange(GRID_W)
    c0 = jnp.clip(qcol - NA_WIN_C // 2, 0, GRID_W - NA_WIN_C)
    col_ok = (kcol[None, :] >= c0[:, None]) & (kcol[None, :] < c0[:, None] + NA_WIN_C)
    rel = kcol[None, :] - qcol[:, None] + NA_WIN_C - 1
    onehot = ((rel[:, :, None] == jnp.arange(2 * NA_WIN_C - 1)[None, None, :]) & col_ok[:, :, None]).astype(F32)
    bt = jnp.einsum('hyd,qkd->hyqk', rpb.astype(F32), onehot, precision=lax.Precision.HIGHEST)
    bt = jnp.where(col_ok[None, None], bt * LOG2E, NEG_INF)
    masked = jnp.full((rpb.shape[0], GRID_W, GRID_W), NEG_INF, F32)

    def group_table(g):
        u0 = min(max(g * NA_GROUP - NA_WIN_R // 2, 0), grid_rows - NA_UNION)
        rows = []
        for a in range(NA_GROUP):
            r = g * NA_GROUP + a
            r0 = min(max(r - NA_WIN_R // 2, 0), grid_rows - NA_WIN_R)
            blocks = [bt[:, u0 + y - r + NA_WIN_R - 1] if r0 <= u0 + y < r0 + NA_WIN_R else masked
                      for y in range(NA_UNION)]
            rows.append(jnp.concatenate(blocks, axis=-1))
        return jnp.concatenate(rows, axis=-2)

    for g in range(1, n_groups - 1):
        assert min(max(g * NA_GROUP - NA_WIN_R // 2, 0), grid_rows - NA_UNION) == g * NA_GROUP - NA_WIN_R // 2
        assert 0 <= g * NA_GROUP - NA_WIN_R // 2 and (g + 1) * NA_GROUP - 1 - NA_WIN_R // 2 <= grid_rows - NA_WIN_R
    return jnp.stack([group_table(0), group_table(1), group_table(n_groups - 1)], axis=1)


def _na_attention(qkv, rpb, ctx_len, need_ctx):
    b, r, _ = qkv.shape
    grid_rows = (r - ctx_len) // GRID_W
    nblk = D_MODEL // LANES
    fb = _na_bias_table(rpb, grid_rows)
    kern = functools.partial(_na_kernel, ctx_len=ctx_len, grid_rows=grid_rows, need_ctx=need_ctx)
    return pl.pallas_call(
        kern,
        grid=(nblk, b),
        in_specs=[
            pl.BlockSpec((None, r, LANES), lambda hp, bi: (bi, 0, hp)),
            pl.BlockSpec((None, r, LANES), lambda hp, bi: (bi, 0, nblk + hp)),
            pl.BlockSpec((None, r, LANES), lambda hp, bi: (bi, 0, 2 * nblk + hp)),
            pl.BlockSpec((2, 3, NA_GROUP * GRID_W, NA_UNION * GRID_W), lambda hp, bi: (hp, 0, 0, 0)),
        ],
        out_specs=pl.BlockSpec((None, r, LANES), lambda hp, bi: (bi, 0, hp)),
        out_shape=jax.ShapeDtypeStruct((b, r, D_MODEL), BF16),
        scratch_shapes=[pltpu.VMEM((r, 2 * LANES), BF16)]
        + [pltpu.VMEM((2, NA_GROUP * GRID_W, NA_UNION * GRID_W), F32),
           pltpu.VMEM((2, NA_GROUP * GRID_W, ctx_len), F32)] * 2,
        compiler_params=_params(("arbitrary", "arbitrary")),
        name="na_attn",
    )(qkv, qkv, qkv, fb)


def _da_kernel(q_ref, k_ref, v_ref, lam_ref, g_ref, o_ref, s_a, s_b, *, ctx_len, lam_init):
    lp = lam_ref[...]
    lam = (jnp.exp(jnp.sum(lp[0:1] * lp[1:2], axis=-1, keepdims=True))
           - jnp.exp(jnp.sum(lp[2:3] * lp[3:4], axis=-1, keepdims=True)) + lam_init)
    tq = ROW_TILE
    n_tiles = (q_ref.shape[0] - ctx_len) // tq
    lo, hi = _head_masks(tq)

    def finish(scores, v, row0, rows):
        es, ls = [], []
        for s in scores:
            e = jnp.exp2(s - jnp.max(s, axis=-1, keepdims=True))
            es.append(e)
            ls.append(jnp.sum(e, axis=-1, keepdims=True))
        a = es[0] - (lam * ls[0] / ls[1]) * es[1]
        o = _dot(a.astype(BF16), v) * (1.0 / ls[0])
        o_ref[pl.ds(row0, rows), :] = (_rms(o) * g_ref[...] * (1.0 - lam_init)).astype(o_ref.dtype)

    qc = q_ref[0:ctx_len, :]
    lo_c, hi_c = _head_masks(ctx_len)
    finish([_dot_nt(jnp.where(m, qc, 0), k_ref[0:ctx_len, :]) for m in (lo_c, hi_c)], v_ref[0:ctx_len, :], 0, ctx_len)

    def row_of(t):
        return pl.multiple_of(ctx_len + t * tq, tq)

    def scores(t, s_ref):
        q = q_ref[pl.ds(row_of(t), tq), :]
        for j, m in enumerate((lo, hi)):
            s_ref[j] = _dot_nt(jnp.where(m, q, 0), k_ref[...])

    scores(0, s_a)

    def pair_body(p, carry):
        t = 2 * p
        scores(t + 1, s_b)
        finish([s_a[0], s_a[1]], v_ref[...], row_of(t), tq)
        scores(jnp.minimum(t + 2, n_tiles - 1), s_a)
        finish([s_b[0], s_b[1]], v_ref[...], row_of(t + 1), tq)
        return carry

    lax.fori_loop(0, n_tiles // 2, pair_body, 0)


def _da_attention(qkv, lam_params, subln_g, layer_idx, ctx_len):
    b, r, _ = qkv.shape
    assert (r - ctx_len) % (2 * ROW_TILE) == 0
    nblk = D_MODEL // LANES
    lam_init = 0.8 - 0.6 * math.exp(-0.3 * layer_idx)
    kern = functools.partial(_da_kernel, ctx_len=ctx_len, lam_init=lam_init)
    return pl.pallas_call(
        kern,
        grid=(nblk, b),
        in_specs=[
            pl.BlockSpec((None, r, LANES), lambda h, bi: (bi, 0, h)),
            pl.BlockSpec((None, r, LANES), lambda h, bi: (bi, 0, nblk + h)),
            pl.BlockSpec((None, r, LANES), lambda h, bi: (bi, 0, 2 * nblk + h)),
            pl.BlockSpec((4, HEAD_DIM), lambda h, bi: (0, 0)),
            pl.BlockSpec((1, 2 * HEAD_DIM), lambda h, bi: (0, 0)),
        ],
        out_specs=pl.BlockSpec((None, r, LANES), lambda h, bi: (bi, 0, h)),
        out_shape=jax.ShapeDtypeStruct((b, r, D_MODEL), BF16),
        scratch_shapes=[pltpu.VMEM((2, ROW_TILE, r), F32), pltpu.VMEM((2, ROW_TILE, r), F32)],
        compiler_params=_params(("arbitrary", "arbitrary")),
        name="da_attn",
    )(qkv, qkv, qkv, lam_params.astype(F32), subln_g.astype(F32).reshape(1, -1))


def _sw_kernel(sink_ref, q_ref, k_ref, v_ref, o_ref, kd_ref, vd_ref, sl_a, sc_a, sl_b, sc_b, *, ctx_len, seq, need_ctx):
    kvp = pl.program_id(0)
    band = Q_BLOCK + 2 * SW_WINDOW
    hpk = SW_Q_HEADS // SW_KV_HEADS

    k = k_ref[...]
    v = v_ref[...]
    for hh in range(2):
        sl = slice(hh * HEAD_DIM, (hh + 1) * HEAD_DIM)
        kd_ref[hh] = jnp.concatenate([k[:, sl], k[:, sl]], axis=-1)
        _store_with_ones(vd_ref.at[hh], jnp.concatenate([v[:, sl], v[:, sl]], axis=-1))

    n_blk = 2 * (hpk // 2)

    def stacked_q(row0, rows, c):
        lo, hi = _head_masks(rows)
        qv = q_ref[pl.ds(row0, rows), c * LANES:(c + 1) * LANES]
        return jnp.concatenate([jnp.where(lo, qv, 0), jnp.where(hi, qv, 0)], axis=0)

    def per_head(rows, c, col, fn):
        sinks = [sink_ref[kvp * 2 * hpk + 2 * c + j] * LOG2E for j in range(2)]
        return jnp.concatenate([fn(col[j * rows:(j + 1) * rows], sinks[j]) for j in range(2)], axis=0)

    def write_out(row0, rows, c, o):
        lo, _ = _head_masks(rows)
        o_ref[pl.ds(row0, rows), c * LANES:(c + 1) * LANES] = jnp.where(lo, o[0:rows], o[rows:2 * rows]).astype(o_ref.dtype)

    def softmax_pv(rows, c, parts):
        row_max = functools.reduce(jnp.maximum, [jnp.max(s, axis=-1, keepdims=True) for s, _ in parts])
        mx = per_head(rows, c, row_max, lambda m, sink: jnp.maximum(m, sink))
        r = sum(_dot(jnp.exp2(s - mx).astype(BF16), val) for s, val in parts)
        den = r[:, LANES:2 * LANES] + per_head(rows, c, mx, lambda m, sink: jnp.exp2(sink - m))
        return r[:, 0:LANES] / den

    if need_ctx:
        for c in range(n_blk):
            hh = c // (hpk // 2)
            s = _dot_nt(stacked_q(0, ctx_len, c), kd_ref[hh, 0:ctx_len, :])
            write_out(0, ctx_len, c, softmax_pv(ctx_len, c, [(s, vd_ref[hh, 0:ctx_len, :])]))
    else:
        o_ref[0:ctx_len, :] = jnp.zeros((ctx_len, o_ref.shape[-1]), o_ref.dtype)

    n_steps = seq // Q_BLOCK

    def rows_of(n):
        start = n * Q_BLOCK
        b0 = jnp.clip(start - SW_WINDOW, 0, seq - band)
        return start, b0, pl.multiple_of(ctx_len + start, Q_BLOCK), pl.multiple_of(ctx_len + b0, Q_BLOCK)

    def scores(n, sl_ref, sc_ref):
        start, b0, qrow, brow = rows_of(n)
        qpos = start + (lax.broadcasted_iota(jnp.int32, (2 * Q_BLOCK, band), 0) & (Q_BLOCK - 1))
        kpos = b0 + lax.broadcasted_iota(jnp.int32, (2 * Q_BLOCK, band), 1)
        in_band = jnp.abs(kpos - qpos) <= SW_WINDOW
        for c in range(n_blk):
            hh = c // (hpk // 2)
            qs = stacked_q(qrow, Q_BLOCK, c)
            sl_ref[c] = jnp.where(in_band, _dot_nt(qs, kd_ref[hh, pl.ds(brow, band), :]), NEG_INF)
            sc_ref[c] = _dot_nt(qs, kd_ref[hh, 0:ctx_len, :])

    def finish(n, sl_ref, sc_ref):
        _, _, qrow, brow = rows_of(n)
        for c in range(n_blk):
            hh = c // (hpk // 2)
            o = softmax_pv(Q_BLOCK, c, [(sc_ref[c], vd_ref[hh, 0:ctx_len, :]),
                                        (sl_ref[c], vd_ref[hh, pl.ds(brow, band), :])])
            write_out(qrow, Q_BLOCK, c, o)

    scores(0, sl_a, sc_a)

    def pair_body(t, carry):
        n = 2 * t
        scores(n + 1, sl_b, sc_b)
        finish(n, sl_a, sc_a)
        scores(jnp.minimum(n + 2, n_steps - 1), sl_a, sc_a)
        finish(n + 1, sl_b, sc_b)
        return carry

    lax.fori_loop(0, n_steps // 2, pair_body, 0)


def _sw_attention(qkv, sinks, ctx_len, need_ctx):
    b, r, _ = qkv.shape
    seq = r - ctx_len
    assert seq % (2 * Q_BLOCK) == 0
    nq_blk = SW_Q_HEADS * HEAD_DIM // LANES
    slab = 2 * (SW_Q_HEADS // SW_KV_HEADS) * HEAD_DIM
    kern = functools.partial(_sw_kernel, ctx_len=ctx_len, seq=seq, need_ctx=need_ctx)
    n_kv_blk = SW_KV_HEADS * HEAD_DIM // LANES
    return pl.pallas_call(
        kern,
        grid=(n_kv_blk, b),
        in_specs=[
            pl.BlockSpec(memory_space=pltpu.SMEM),
            pl.BlockSpec((None, r, slab), lambda kp, bi: (bi, 0, kp)),
            pl.BlockSpec((None, r, LANES), lambda kp, bi: (bi, 0, nq_blk + kp)),
            pl.BlockSpec((None, r, LANES), lambda kp, bi: (bi, 0, nq_blk + n_kv_blk + kp)),
        ],
        out_specs=pl.BlockSpec((None, r, slab), lambda kp, bi: (bi, 0, kp)),
        out_shape=jax.ShapeDtypeStruct((b, r, D_MODEL), BF16),
        scratch_shapes=[pltpu.VMEM((2, r, LANES), BF16), pltpu.VMEM((2, r, 2 * LANES), BF16)]
        + [pltpu.VMEM((slab // LANES, 2 * Q_BLOCK, Q_BLOCK + 2 * SW_WINDOW), F32),
           pltpu.VMEM((slab // LANES, 2 * Q_BLOCK, ctx_len), F32)] * 2,
        compiler_params=_params(("arbitrary", "arbitrary")),
        name="sw_attn",
    )(sinks.astype(F32), qkv, qkv, qkv)


def _post_kernel(o_ref, x_ref, wo_ref, mod_ref, g_ref, wrh_ref, wrl_ref, br_ref, xn_ref, h_ref, rt_ref, *, ctx_tiles):
    d = x_ref.shape[-1]
    tm = x_ref.shape[0]
    row = _mod_row(pl.program_id(0), pl.program_id(1), ctx_tiles)
    gate1 = mod_ref[pl.ds(row, 1), 2 * d:3 * d]
    shift2 = mod_ref[pl.ds(row, 1), 3 * d:4 * d]
    scale2 = mod_ref[pl.ds(row, 1), 4 * d:5 * d]
    xn = x_ref[...] + gate1 * _dot(o_ref[...], wo_ref[...])
    xn_ref[...] = xn
    h = (_rms(xn) * g_ref[...]) * (1.0 + scale2) + shift2
    for j in range(SUBLANES):
        h_ref[pl.ds(j, tm, stride=SUBLANES), :] = h[:, j * LANES:(j + 1) * LANES]

    h_hi = h.astype(BF16)
    h_lo = (h - h_hi.astype(F32)).astype(BF16)
    logits = (_dot(h_hi, wrh_ref[...]) + _dot(h_lo, wrh_ref[...]) + _dot(h_hi, wrl_ref[...])) + br_ref[...]
    lane = lax.broadcasted_iota(jnp.int32, (tm, LANES), 1).astype(F32)

    def first_argmax(vals, mx):
        return jnp.min(jnp.where(vals == mx, lane, float(LANES)), axis=-1, keepdims=True)

    gl = jnp.where(lane < N_GROUPS, logits, -jnp.inf)
    ge = jnp.exp(gl - jnp.max(gl, axis=-1, keepdims=True))
    gp = ge / jnp.sum(ge, axis=-1, keepdims=True)
    g_p = jnp.max(gp, axis=-1, keepdims=True)
    g_idx = first_argmax(gp, g_p)
    e_lo = N_GROUPS + EXPERTS_PER_GROUP * g_idx
    el = jnp.where((lane >= e_lo) & (lane < e_lo + EXPERTS_PER_GROUP), logits, -jnp.inf)
    v1 = jnp.max(el, axis=-1, keepdims=True)
    i1 = first_argmax(el, v1)
    el2 = jnp.where(lane == i1, -jnp.inf, el)
    v2 = jnp.max(el2, axis=-1, keepdims=True)
    i2 = first_argmax(el2, v2)
    e2 = jnp.exp(v2 - v1)
    w1 = g_p * (1.0 / (1.0 + e2))
    w2 = g_p * (e2 / (1.0 + e2))
    lane8 = lax.broadcasted_iota(jnp.int32, (tm, SUBLANES), 1)
    rt = jnp.where(lane8 == 0, i1 - N_GROUPS,
                   jnp.where(lane8 == 1, i2 - N_GROUPS,
                             jnp.where(lane8 == 2, w1, jnp.where(lane8 == 3, w2, 0.0))))
    rt_ref[...] = rt


def _post(o, x, wo, mod, g, wr, br, ctx_len):
    wr_hi = wr.astype(BF16)
    wr_lo = (wr - wr_hi.astype(F32)).astype(BF16)
    b, r, d = x.shape
    nt = r // ROW_TILE
    kern = functools.partial(_post_kernel, ctx_tiles=ctx_len // ROW_TILE)
    return pl.pallas_call(
        kern,
        grid=(b, nt),
        in_specs=[
            pl.BlockSpec((None, ROW_TILE, d), lambda bi, i: (bi, i, 0)),
            pl.BlockSpec((None, ROW_TILE, d), lambda bi, i: (bi, i, 0)),
            pl.BlockSpec((d, d), lambda bi, i: (0, 0)),
            pl.BlockSpec((MOD_ROWS, 6 * d), lambda bi, i: (0, 0)),
            pl.BlockSpec((1, d), lambda bi, i: (0, 0)),
            pl.BlockSpec((d, LANES), lambda bi, i: (0, 0)),
            pl.BlockSpec((d, LANES), lambda bi, i: (0, 0)),
            pl.BlockSpec((1, LANES), lambda bi, i: (0, 0)),
        ],
        out_specs=[
            pl.BlockSpec((None, ROW_TILE, d), lambda bi, i: (bi, i, 0)),
            pl.BlockSpec((ROW_TILE * SUBLANES, LANES), lambda bi, i: (bi * nt + i, 0)),
            pl.BlockSpec((ROW_TILE, SUBLANES), lambda bi, i: (bi * nt + i, 0)),
        ],
        out_shape=[
            jax.ShapeDtypeStruct((b, r, d), F32),
            jax.ShapeDtypeStruct((b * r * SUBLANES, LANES), F32),
            jax.ShapeDtypeStruct((b * r, SUBLANES), F32),
        ],
        compiler_params=_params(("arbitrary", "arbitrary")),
        name="post_attn",
    )(o, x, wo, mod, g, wr_hi, wr_lo, br)


DISPATCH_TOKENS = 512


def _row(ref, idx):
    return ref.at[pl.ds(pl.multiple_of(idx * SUBLANES, SUBLANES), SUBLANES), :]


def _dispatch_kernel(dest_ref, h_ref, xs_hbm, sem):
    def issue(t, carry):
        src = _row(h_ref, t)
        for k in range(2):
            pltpu.make_async_copy(src, _row(xs_hbm, dest_ref[0, 0, 2 * t + k]), sem).start(priority=k)
        return carry

    lax.fori_loop(0, DISPATCH_TOKENS, issue, 0)
    for _ in range(2):
        pltpu.make_async_copy(h_ref, xs_hbm.at[pl.ds(0, DISPATCH_TOKENS * SUBLANES), :], sem).wait()


def _dispatch(h_rows, dest):
    t = dest.shape[0] // 2
    steps = t // DISPATCH_TOKENS
    return pl.pallas_call(
        _dispatch_kernel,
        grid=(steps,),
        in_specs=[
            pl.BlockSpec((1, 1, 2 * DISPATCH_TOKENS), lambda i: (i, 0, 0), memory_space=pltpu.SMEM),
            pl.BlockSpec((DISPATCH_TOKENS * SUBLANES, LANES), lambda i: (i, 0)),
        ],
        out_specs=pl.BlockSpec(memory_space=pl.ANY),
        out_shape=jax.ShapeDtypeStruct((2 * t * SUBLANES, LANES), F32),
        scratch_shapes=[pltpu.SemaphoreType.DMA(())],
        compiler_params=_params(("arbitrary",)),
        name="moe_dispatch",
    )(dest.reshape(steps, 1, 2 * DISPATCH_TOKENS), h_rows)


def _experts_kernel(ve_ref, vt_ref, lo_ref, hi_ref, first_ref, valid_ref,
                    xs_ref, wg_ref, wu_ref, wd_ref, ys_ref, wgu_s, wd_s, acc_s):
    v = pl.program_id(0)
    tm = ROW_TILE
    f = wg_ref.shape[-1]

    @pl.when(v == 0)
    def _():
        acc_s[...] = jnp.zeros(acc_s.shape, acc_s.dtype)

    @pl.when(valid_ref[v] == 1)
    def _():
        changed = jnp.logical_or(v == 0, ve_ref[v] != ve_ref[jnp.maximum(v - 1, 0)])

        @pl.when(changed)
        def _():
            wgu_s[:, 0:f] = wg_ref[...].astype(BF16)
            wgu_s[:, f:2 * f] = wu_ref[...].astype(BF16)
            wd_s[...] = wd_ref[...].astype(BF16)

        x = jnp.concatenate([xs_ref[pl.ds(j, tm, stride=SUBLANES), :] for j in range(SUBLANES)], axis=-1)
        au = _dot(x.astype(BF16), wgu_s[...])
        rows = vt_ref[v] * tm + lax.broadcasted_iota(jnp.int32, (tm, 1), 0)
        mine = (rows >= lo_ref[v]) & (rows < hi_ref[v])
        mid = jnp.where(mine, jax.nn.silu(au[:, 0:f]) * au[:, f:2 * f], 0.0)
        y = _dot(mid.astype(BF16), wd_s[...])
        y = jnp.where(first_ref[v] == 1, y, acc_s[...] + y)
        acc_s[...] = y
        for j in range(SUBLANES):
            ys_ref[pl.ds(j, tm, stride=SUBLANES), :] = y[:, j * LANES:(j + 1) * LANES]


def _experts(xs, sched, w_gate, w_up, w_down, layer):
    rows = xs.shape[0] // SUBLANES
    d, f = w_gate.shape[-2:]
    n_visits = sched[0].shape[0]
    blk = (ROW_TILE * SUBLANES, LANES)
    grid_spec = pltpu.PrefetchScalarGridSpec(
        num_scalar_prefetch=6,
        grid=(n_visits,),
        in_specs=[
            pl.BlockSpec(blk, lambda v, ve, vt, lo, hi, fi, va: (vt[v], 0)),
            pl.BlockSpec((None, None, d, f), lambda v, ve, vt, lo, hi, fi, va: (layer, ve[v], 0, 0)),
            pl.BlockSpec((None, None, d, f), lambda v, ve, vt, lo, hi, fi, va: (layer, ve[v], 0, 0)),
            pl.BlockSpec((None, None, f, d), lambda v, ve, vt, lo, hi, fi, va: (layer, ve[v], 0, 0)),
        ],
        out_specs=pl.BlockSpec(blk, lambda v, ve, vt, lo, hi, fi, va: (vt[v], 0)),
        scratch_shapes=[pltpu.VMEM((d, 2 * f), BF16), pltpu.VMEM((f, d), BF16), pltpu.VMEM((ROW_TILE, d), F32)],
    )
    return pl.pallas_call(
        _experts_kernel,
        grid_spec=grid_spec,
        out_shape=jax.ShapeDtypeStruct((rows * SUBLANES, LANES), F32),
        compiler_params=_params(("arbitrary",)),
        name="moe_experts",
    )(*sched, xs, w_gate, w_up, w_down)


def _combine_kernel(dest_ref, next_ref, rt_ref, x_ref, mod_ref, g_ref, ys_hbm, o_ref, buf, sem, *,
                    ctx_tiles, tiles_per_batch, final):
    tm = ROW_TILE
    d = x_ref.shape[-1]
    s = pl.program_id(0)
    slot = s % 2
    half_rows = tm * SUBLANES

    def gather(idx_ref, slot_):
        def issue(t, carry):
            for k in range(2):
                pltpu.make_async_copy(_row(ys_hbm, idx_ref[0, 0, 2 * t + k]),
                                      _row(buf.at[slot_], k * tm + t), sem.at[slot_]).start(priority=k)
            return carry

        lax.fori_loop(0, tm, issue, 0)

    @pl.when(s == 0)
    def _():
        gather(dest_ref, 0)

    @pl.when(s + 1 < pl.num_programs(0))
    def _():
        gather(next_ref, 1 - slot)

    for k in range(2):
        part = pl.ds(k * half_rows, half_rows)
        pltpu.make_async_copy(ys_hbm.at[part, :], buf.at[slot].at[part, :], sem.at[slot]).wait()

    row = _mod_row(s // tiles_per_batch, s % tiles_per_batch, ctx_tiles)
    gate2 = mod_ref[pl.ds(row, 1), 5 * d:6 * d]
    rt = rt_ref[...]
    w = (rt[:, 2:3], rt[:, 3:4])
    rows = buf.at[slot]
    pieces = []
    for j in range(SUBLANES):
        y = (w[0] * rows[pl.ds(j, tm, stride=SUBLANES), :]
             + w[1] * rows[pl.ds(half_rows + j, tm, stride=SUBLANES), :])
        sl = slice(j * LANES, (j + 1) * LANES)
        pieces.append(x_ref[:, sl] + gate2[:, sl] * y)
    if final:
        o_ref[...] = _rms(jnp.concatenate(pieces, axis=-1)) * g_ref[...]
    else:
        for j in range(SUBLANES):
            o_ref[:, j * LANES:(j + 1) * LANES] = pieces[j]


def _combine(ys, dest, rt, x, mod, ctx_len, final_g=None):
    b, r, d = x.shape
    t = b * r
    steps = t // ROW_TILE
    ct, tpb = ctx_len // ROW_TILE, r // ROW_TILE
    final = final_g is not None
    kern = functools.partial(_combine_kernel, ctx_tiles=ct, tiles_per_batch=tpb, final=final)
    if final:
        out_rows = t - b * ctx_len
        out_map = lambda i: ((i // tpb) * (tpb - ct) + jnp.maximum(i % tpb - ct, 0), 0)
    else:
        out_rows = t
        out_map = lambda i: (i, 0)
        final_g = jnp.zeros((1, d), F32)
    dest3 = dest.reshape(steps, 1, 2 * ROW_TILE)
    out = pl.pallas_call(
        kern,
        grid=(steps,),
        in_specs=[
            pl.BlockSpec((1, 1, 2 * ROW_TILE), lambda i: (i, 0, 0), memory_space=pltpu.SMEM),
            pl.BlockSpec((1, 1, 2 * ROW_TILE), lambda i: (jnp.minimum(i + 1, steps - 1), 0, 0), memory_space=pltpu.SMEM),
            pl.BlockSpec((ROW_TILE, SUBLANES), lambda i: (i, 0)),
            pl.BlockSpec((ROW_TILE, d), lambda i: (i, 0)),
            pl.BlockSpec((MOD_ROWS, 6 * d), lambda i: (0, 0)),
            pl.BlockSpec((1, d), lambda i: (0, 0)),
            pl.BlockSpec(memory_space=pl.ANY),
        ],
        out_specs=pl.BlockSpec((ROW_TILE, d), out_map),
        out_shape=jax.ShapeDtypeStruct((out_rows, d), F32),
        scratch_shapes=[pltpu.VMEM((2, 2 * ROW_TILE * SUBLANES, LANES), F32), pltpu.SemaphoreType.DMA((2,))],
        compiler_params=_params(("arbitrary",)),
        name="moe_combine",
    )(dest3, dest3, rt, x.reshape(t, d), mod, final_g, ys)
    return out.reshape(b, out_rows // b, d)


def _route_plan(rt, n_tiles):
    eid = rt[:, 0:2].astype(jnp.int32).reshape(-1)
    onehot = (eid[:, None] == jnp.arange(N_EXPERTS, dtype=jnp.int32)[None, :]).astype(jnp.int32)
    csum = jnp.cumsum(onehot, axis=0)
    counts = csum[-1]
    cstart = jnp.cumsum(counts) - counts
    dest = jnp.sum(onehot * (cstart[None, :] + csum - 1), axis=1).astype(jnp.int32)

    tm = ROW_TILE
    first_tile = cstart // tm
    last_tile = (cstart + counts - 1) // tm
    nvis = jnp.where(counts > 0, last_tile - first_tile + 1, 0)
    vend = jnp.cumsum(nvis)
    vstart = vend - nvis
    total = vend[-1]
    n_visits = n_tiles + N_EXPERTS - 1
    v = jnp.arange(n_visits, dtype=jnp.int32)
    valid = v < total
    vc = jnp.minimum(v, total - 1)
    ve = jnp.sum((vc[:, None] >= vend[None, :]).astype(jnp.int32), axis=1)
    oh_e = (ve[:, None] == jnp.arange(N_EXPERTS, dtype=jnp.int32)[None, :]).astype(jnp.int32)
    pick = lambda a: jnp.sum(oh_e * a[None, :], axis=1)
    vt = pick(first_tile) + vc - pick(vstart)
    lo = pick(cstart)
    hi = jnp.where(valid, lo + pick(counts), lo)
    first = jnp.concatenate([jnp.ones((1,), jnp.int32), (vt[1:] != vt[:-1]).astype(jnp.int32)])
    sched = tuple(a.astype(jnp.int32) for a in (ve, vt, lo, hi, first, valid))
    return dest, sched


def kernel(x, c, ctx, c_ctx, ada_w, ada_b, norm_mix_g, norm_ffn_g, final_norm_g, na_w_qkv, na_rpb, na_w_o, da_w_qkv, da_lambda, da_subln_g, da_w_o, sw_w_qkv, sw_sinks, sw_w_o, moe_w_grp, moe_b_grp, moe_w_exp, moe_b_exp, moe_w_gate, moe_w_up, moe_w_down):
    b, seq, d = x.shape
    ctx_len = ctx.shape[1]
    depth = ada_w.shape[0]
    assert d == D_MODEL and ctx_len % ROW_TILE == 0 and seq % ROW_TILE == 0 and b <= MOD_ROWS // 2
    assert (b * (ctx_len + seq)) % DISPATCH_TOKENS == 0

    half = MOD_ROWS // 2
    cond = jnp.concatenate([jnp.pad(c, ((0, half - b), (0, 0))), jnp.pad(c_ctx[None, :], ((0, half - 1), (0, 0)))], axis=0)
    mod_all = _adaln(cond, ada_w, ada_b)
    cos, sin = _rope_tables(seq, ctx_len)
    stream = jnp.concatenate([ctx, x], axis=1)
    n_tiles = 2 * b * (ctx_len + seq) // ROW_TILE

    for i in range(depth):
        need_ctx = i < depth - 1
        kind, j = i % N_MIXERS, i // N_MIXERS
        mod = mod_all[i]
        g_mix = norm_mix_g[i].reshape(1, d)
        if kind == 0:
            qkv = _qkv(stream, g_mix, mod, na_w_qkv[j].astype(BF16), cos, sin, 0, NA_HEADS * HEAD_DIM, ctx_len)
            o = _na_attention(qkv, na_rpb[j], ctx_len, need_ctx)
            wo = na_w_o[j]
        elif kind == 1:
            nq = 2 * DA_HEADS * HEAD_DIM
            qkv = _qkv(stream, g_mix, mod, da_w_qkv[j].astype(BF16), cos, sin, 2 * nq, nq, ctx_len)
            o = _da_attention(qkv, da_lambda[j], da_subln_g[j], i, ctx_len)
            wo = da_w_o[j]
        else:
            rope_cols = (SW_Q_HEADS + SW_KV_HEADS) * HEAD_DIM
            qkv = _qkv(stream, g_mix, mod, sw_w_qkv[j].astype(BF16), cos, sin, rope_cols, SW_Q_HEADS * HEAD_DIM, ctx_len)
            o = _sw_attention(qkv, sw_sinks[j], ctx_len, need_ctx)
            wo = sw_w_o[j]

        pad = LANES - N_GROUPS - N_EXPERTS
        wr = jnp.pad(jnp.concatenate([moe_w_grp[i], moe_w_exp[i]], axis=1).astype(F32), ((0, 0), (0, pad)))
        br = jnp.pad(jnp.concatenate([moe_b_grp[i], moe_b_exp[i]]).astype(F32), (0, pad)).reshape(1, LANES)
        stream, h_rows, rt = _post(o, stream, wo.astype(BF16), mod, norm_ffn_g[i].reshape(1, d), wr, br, ctx_len)
        dest, sched = _route_plan(rt, n_tiles)
        xs = _dispatch(h_rows, dest)
        ys = _experts(xs, sched, moe_w_gate, moe_w_up, moe_w_down, i)
        last = i == depth - 1
        stream = _combine(ys, dest, rt, stream, mod, ctx_len, final_norm_g.reshape(1, d).astype(F32) if last else None)

    return stream
```

```python
import functools
import math

import jax
import jax.numpy as jnp
from jax import lax
from jax.experimental import pallas as pl
from jax.experimental.pallas import tpu as pltpu

F32 = jnp.float32
BF16 = jnp.bfloat16

D_MODEL = 1024
HEAD_DIM = 64
GRID_W = 64
ROPE_AXIS_PAIRS = HEAD_DIM // 4
ROPE_BASE = 10000.0
NA_HEADS = D_MODEL // HEAD_DIM
NA_WIN_R = 8
NA_WIN_C = 16
NA_GROUP = 4
NA_UNION = NA_GROUP + NA_WIN_R
DA_HEADS = D_MODEL // (2 * HEAD_DIM)
SW_Q_HEADS = D_MODEL // HEAD_DIM
SW_KV_HEADS = 4
SW_WINDOW = 128
Q_BLOCK = 128
N_GROUPS = 4
EXPERTS_PER_GROUP = 8
N_EXPERTS = N_GROUPS * EXPERTS_PER_GROUP
D_EXPERT = 512
NORM_EPS = 1e-6
NEG_INF = -1e30
N_MIXERS = 3

LANES = 128
SUBLANES = 8
ROW_TILE = 256
MOD_ROWS = 16
VMEM_LIMIT = 56 * 1024 * 1024

LOG2E = math.log2(math.e)
Q_SCALE = HEAD_DIM ** -0.5 * LOG2E

_NT = (((1,), (1,)), ((), ()))


def _dot(a, b, **kw):
    return jnp.dot(a, b, preferred_element_type=F32, **kw)


def _dot_nt(a, b):
    return lax.dot_general(a, b, _NT, preferred_element_type=F32)


def _params(sem, vmem=VMEM_LIMIT):
    return pltpu.CompilerParams(dimension_semantics=sem, vmem_limit_bytes=vmem)


def _mod_row(b, tile, ctx_tiles):
    return jnp.where(tile < ctx_tiles, MOD_ROWS // 2, b)


def _rms(x):
    return x * lax.rsqrt(jnp.mean(x * x, axis=-1, keepdims=True) + NORM_EPS)


def _adaln_kernel(c_ref, w_ref, b_ref, o_ref):
    s = jax.nn.silu(c_ref[...])
    o_ref[...] = _dot(s, w_ref[...], precision=lax.Precision.HIGHEST) + b_ref[...]


def _adaln(cond, ada_w, ada_b):
    depth, d, n6 = ada_w.shape
    tn = 1536
    return pl.pallas_call(
        _adaln_kernel,
        grid=(depth, n6 // tn),
        in_specs=[
            pl.BlockSpec((MOD_ROWS, d), lambda l, n: (0, 0)),
            pl.BlockSpec((None, d, tn), lambda l, n: (l, 0, n)),
            pl.BlockSpec((None, 1, tn), lambda l, n: (l, 0, n)),
        ],
        out_specs=pl.BlockSpec((None, MOD_ROWS, tn), lambda l, n: (l, 0, n)),
        out_shape=jax.ShapeDtypeStruct((depth, MOD_ROWS, n6), F32),
        compiler_params=_params(("arbitrary", "arbitrary")),
        name="adaln",
    )(cond, ada_w, ada_b.reshape(depth, 1, n6))


def _qkv_kernel(x_ref, g_ref, mod_ref, w_ref, cos_ref, sin_ref, o_ref, *, rope_cols, q_cols, ctx_tiles):
    d = x_ref.shape[-1]
    row = _mod_row(pl.program_id(0), pl.program_id(1), ctx_tiles)
    shift = mod_ref[pl.ds(row, 1), 0:d]
    scale = mod_ref[pl.ds(row, 1), d:2 * d]
    h = (_rms(x_ref[...]) * g_ref[...]) * (1.0 + scale) + shift
    acc = _dot(h.astype(BF16), w_ref[...])
    n = acc.shape[-1]
    if rope_cols:
        tm = acc.shape[0]
        first_half = (lax.broadcasted_iota(jnp.int32, (tm, LANES), 1) % (2 * ROPE_AXIS_PAIRS)) < ROPE_AXIS_PAIRS
        for c in range(rope_cols // LANES):
            tab = slice(LANES, 2 * LANES) if c * LANES < q_cols else slice(0, LANES)
            t = acc[:, c * LANES:(c + 1) * LANES]
            up = pltpu.roll(t, LANES - ROPE_AXIS_PAIRS, 1)
            dn = pltpu.roll(t, ROPE_AXIS_PAIRS, 1)
            o_ref[:, c * LANES:(c + 1) * LANES] = (
                t * cos_ref[:, tab] + jnp.where(first_half, up, dn) * sin_ref[:, tab]).astype(o_ref.dtype)
    else:
        o_ref[:, 0:q_cols] = (acc[:, 0:q_cols] * Q_SCALE).astype(o_ref.dtype)
        rope_cols = q_cols
    if rope_cols < n:
        o_ref[:, rope_cols:] = acc[:, rope_cols:].astype(o_ref.dtype)


def _qkv(x, g, mod, w, cos, sin, rope_cols, q_cols, ctx_len):
    b, r, d = x.shape
    n = w.shape[1]
    kern = functools.partial(_qkv_kernel, rope_cols=rope_cols, q_cols=q_cols, ctx_tiles=ctx_len // ROW_TILE)
    return pl.pallas_call(
        kern,
        grid=(b, r // ROW_TILE),
        in_specs=[
            pl.BlockSpec((None, ROW_TILE, d), lambda bi, i: (bi, i, 0)),
            pl.BlockSpec((1, d), lambda bi, i: (0, 0)),
            pl.BlockSpec((MOD_ROWS, 6 * d), lambda bi, i: (0, 0)),
            pl.BlockSpec((d, n), lambda bi, i: (0, 0)),
            pl.BlockSpec((ROW_TILE, 2 * LANES), lambda bi, i: (i, 0)),
            pl.BlockSpec((ROW_TILE, 2 * LANES), lambda bi, i: (i, 0)),
        ],
        out_specs=pl.BlockSpec((None, ROW_TILE, n), lambda bi, i: (bi, i, 0)),
        out_shape=jax.ShapeDtypeStruct((b, r, n), BF16),
        compiler_params=_params(("arbitrary", "arbitrary")),
        name="qkv",
    )(x, g, mod, w, cos, sin)


def _rope_tables(seq, ctx_len):
    t = jnp.arange(seq, dtype=jnp.int32)
    row = (t // GRID_W).astype(F32)
    col = (t % GRID_W).astype(F32)
    inv = ROPE_BASE ** (-jnp.arange(ROPE_AXIS_PAIRS, dtype=F32) / ROPE_AXIS_PAIRS)
    ang_r = row[:, None] * inv[None, :]
    ang_c = col[:, None] * inv[None, :]
    ang = jnp.concatenate([ang_r, ang_r, ang_c, ang_c], axis=-1)
    sign = jnp.tile(jnp.concatenate([-jnp.ones((ROPE_AXIS_PAIRS,), F32), jnp.ones((ROPE_AXIS_PAIRS,), F32)]), 2)
    cos = jnp.concatenate([jnp.ones((ctx_len, HEAD_DIM), F32), jnp.cos(ang)], axis=0)
    sin = jnp.concatenate([jnp.zeros((ctx_len, HEAD_DIM), F32), jnp.sin(ang) * sign[None, :]], axis=0)
    cos = jnp.tile(cos, (1, LANES // HEAD_DIM))
    sin = jnp.tile(sin, (1, LANES // HEAD_DIM))
    return jnp.concatenate([cos, cos * Q_SCALE], axis=1), jnp.concatenate([sin, sin * Q_SCALE], axis=1)


def _store_with_ones(dst_ref, v):
    dst_ref[:, 0:LANES] = v
    dst_ref[:, LANES:2 * LANES] = jnp.ones(v.shape, v.dtype)


def _head_masks(rows):
    lane = lax.broadcasted_iota(jnp.int32, (rows, LANES), 1)
    lo = lane < HEAD_DIM
    return lo, jnp.logical_not(lo)


def _na_group_geometry(g, grid_rows):
    u0 = min(max(g * NA_GROUP - NA_WIN_R // 2, 0), grid_rows - NA_UNION)
    idx = []
    for a in range(NA_GROUP):
        r = g * NA_GROUP + a
        r0 = min(max(r - NA_WIN_R // 2, 0), grid_rows - NA_WIN_R)
        idx.append([u0 + y - r + NA_WIN_R - 1 if r0 <= u0 + y < r0 + NA_WIN_R else 2 * NA_WIN_R - 1
                    for y in range(NA_UNION)])
    return u0, idx


def _na_kernel(q_ref, k_ref, v_ref, bt_ref, o_ref, fb_ref, v1_ref, nb_a, cx_a, nb_b, cx_b, *, ctx_len, grid_rows, need_ctx):
    n_groups = grid_rows // NA_GROUP

    @pl.when(pl.program_id(1) == 0)
    def _():
        for j in range(2):
            for pattern, g in enumerate((0, 1, n_groups - 1)):
                _, idx = _na_group_geometry(g, grid_rows)
                for a in range(NA_GROUP):
                    for p in range(NA_UNION // 2):
                        fb_ref[j, pattern, a * GRID_W:(a + 1) * GRID_W, 2 * p * GRID_W:(2 * p + 2) * GRID_W] = (
                            jnp.concatenate([bt_ref[j, idx[a][2 * p]], bt_ref[j, idx[a][2 * p + 1]]], axis=-1))

    _store_with_ones(v1_ref, v_ref[...])
    kc = k_ref[0:ctx_len, :]
    vc = v1_ref[0:ctx_len, :]

    def normalised(r):
        return r[:, 0:LANES] / r[:, LANES:2 * LANES]

    if need_ctx:
        qc = q_ref[0:ctx_len, :]
        lo, hi = _head_masks(ctx_len)
        outs = []
        for m in (lo, hi):
            s = _dot_nt(jnp.where(m, qc, 0), kc)
            e = jnp.exp2(s - jnp.max(s, axis=-1, keepdims=True))
            outs.append(normalised(_dot(e.astype(BF16), vc)))
        o_ref[0:ctx_len, :] = jnp.where(lo, outs[0], outs[1]).astype(o_ref.dtype)
    else:
        o_ref[0:ctx_len, :] = jnp.zeros((ctx_len, LANES), o_ref.dtype)

    gq = NA_GROUP * GRID_W
    uk = NA_UNION * GRID_W
    lo, hi = _head_masks(gq)

    def rows_of(g):
        u0 = jnp.clip(g * NA_GROUP - NA_WIN_R // 2, 0, grid_rows - NA_UNION)
        return pl.multiple_of(ctx_len + g * gq, gq), pl.multiple_of(ctx_len + u0 * GRID_W, GRID_W)

    def scores(g, nb_ref, cx_ref):
        q0, w0 = rows_of(g)
        pattern = jnp.where(g == 0, 0, jnp.where(g == n_groups - 1, 2, 1))
        q_g = q_ref[pl.ds(q0, gq), :]
        k_u = k_ref[pl.ds(w0, uk), :]
        for j, m in enumerate((lo, hi)):
            qj = jnp.where(m, q_g, 0)
            nb_ref[j] = _dot_nt(qj, k_u) + fb_ref[j, pattern]
            cx_ref[j] = _dot_nt(qj, kc)

    def finish(g, nb_ref, cx_ref):
        q0, w0 = rows_of(g)
        v_u = v1_ref[pl.ds(w0, uk), :]
        outs = []
        for j in range(2):
            s_nb = nb_ref[j]
            s_cx = cx_ref[j]
            mx = jnp.maximum(jnp.max(s_nb, axis=-1, keepdims=True), jnp.max(s_cx, axis=-1, keepdims=True))
            e_nb = jnp.exp2(s_nb - mx).astype(BF16)
            e_cx = jnp.exp2(s_cx - mx).astype(BF16)
            outs.append(normalised(_dot(e_cx, vc) + _dot(e_nb, v_u)))
        o_ref[pl.ds(q0, gq), :] = jnp.where(lo, outs[0], outs[1]).astype(o_ref.dtype)

    scores(0, nb_a, cx_a)

    def pair_body(t, carry):
        g = 2 * t
        scores(g + 1, nb_b, cx_b)
        finish(g, nb_a, cx_a)
        scores(jnp.minimum(g + 2, n_groups - 1), nb_a, cx_a)
        finish(g + 1, nb_b, cx_b)
        return carry

    lax.fori_loop(0, n_groups // 2, pair_body, 0)


def _na_bias_table(rpb, grid_rows):
    n_groups = grid_rows // NA_GROUP
    assert grid_rows % (2 * NA_GROUP) == 0 and n_groups >= 3 and grid_rows >= NA_UNION
    for g in range(2, n_groups - 1):
        u1, idx1 = _na_group_geometry(1, grid_rows)
        ug, idxg = _na_group_geometry(g, grid_rows)
        assert idxg == idx1 and ug - g * NA_GROUP == u1 - NA_GROUP
    qcol = jnp.arange(GRID_W)
    kcol = jnp.arange(GRID_W)
    c0 = jnp.clip(qcol - NA_WIN_C // 2, 0, GRID_W - NA_WIN_C)
    col_ok = (kcol[None, :] >= c0[:, None]) & (kcol[None, :] < c0[:, None] + NA_WIN_C)
    rel = kcol[None, :] - qcol[:, None] + NA_WIN_C - 1
    onehot = ((rel[:, :, None] == jnp.arange(2 * NA_WIN_C - 1)[None, None, :]) & col_ok[:, :, None]).astype(F32)
    bt = jnp.einsum('hyd,qkd->hyqk', rpb.astype(F32), onehot, precision=lax.Precision.HIGHEST)
    bt = jnp.where(col_ok[None, None], bt * LOG2E, NEG_INF)
    return jnp.pad(bt, ((0, 0), (0, 1), (0, 0), (0, 0)), constant_values=NEG_INF)


def _na_attention(qkv, rpb, ctx_len, need_ctx):
    b, r, _ = qkv.shape
    grid_rows = (r - ctx_len) // GRID_W
    nblk = D_MODEL // LANES
    bt = _na_bias_table(rpb, grid_rows)
    kern = functools.partial(_na_kernel, ctx_len=ctx_len, grid_rows=grid_rows, need_ctx=need_ctx)
    return pl.pallas_call(
        kern,
        grid=(nblk, b),
        in_specs=[
            pl.BlockSpec((None, r, LANES), lambda hp, bi: (bi, 0, hp)),
            pl.BlockSpec((None, r, LANES), lambda hp, bi: (bi, 0, nblk + hp)),
            pl.BlockSpec((None, r, LANES), lambda hp, bi: (bi, 0, 2 * nblk + hp)),
            pl.BlockSpec((2, 2 * NA_WIN_R, GRID_W, GRID_W), lambda hp, bi: (hp, 0, 0, 0)),
        ],
        out_specs=pl.BlockSpec((None, r, LANES), lambda hp, bi: (bi, 0, hp)),
        out_shape=jax.ShapeDtypeStruct((b, r, D_MODEL), BF16),
        scratch_shapes=[pltpu.VMEM((2, 3, NA_GROUP * GRID_W, NA_UNION * GRID_W), F32), pltpu.VMEM((r, 2 * LANES), BF16)]
        + [pltpu.VMEM((2, NA_GROUP * GRID_W, NA_UNION * GRID_W), F32),
           pltpu.VMEM((2, NA_GROUP * GRID_W, ctx_len), F32)] * 2,
        compiler_params=_params(("arbitrary", "arbitrary")),
        name="na_attn",
    )(qkv, qkv, qkv, bt)


def _da_kernel(q_ref, k_ref, v_ref, lam_ref, g_ref, o_ref, s_a, s_b, *, ctx_len, lam_init):
    lp = lam_ref[...]
    lam = (jnp.exp(jnp.sum(lp[0:1] * lp[1:2], axis=-1, keepdims=True))
           - jnp.exp(jnp.sum(lp[2:3] * lp[3:4], axis=-1, keepdims=True)) + lam_init)
    tq = ROW_TILE
    n_tiles = (q_ref.shape[0] - ctx_len) // tq
    lo, hi = _head_masks(tq)

    def finish(scores, v, row0, rows):
        es, ls = [], []
        for s in scores:
            e = jnp.exp2(s - jnp.max(s, axis=-1, keepdims=True))
            es.append(e)
            ls.append(jnp.sum(e, axis=-1, keepdims=True))
        a = es[0] - (lam * ls[0] / ls[1]) * es[1]
        o = _dot(a.astype(BF16), v) * (1.0 / ls[0])
        o_ref[pl.ds(row0, rows), :] = (_rms(o) * g_ref[...] * (1.0 - lam_init)).astype(o_ref.dtype)

    qc = q_ref[0:ctx_len, :]
    lo_c, hi_c = _head_masks(ctx_len)
    finish([_dot_nt(jnp.where(m, qc, 0), k_ref[0:ctx_len, :]) for m in (lo_c, hi_c)], v_ref[0:ctx_len, :], 0, ctx_len)

    def row_of(t):
        return pl.multiple_of(ctx_len + t * tq, tq)

    def scores(t, s_ref):
        q = q_ref[pl.ds(row_of(t), tq), :]
        for j, m in enumerate((lo, hi)):
            s_ref[j] = _dot_nt(jnp.where(m, q, 0), k_ref[...])

    scores(0, s_a)

    def pair_body(p, carry):
        t = 2 * p
        scores(t + 1, s_b)
        finish([s_a[0], s_a[1]], v_ref[...], row_of(t), tq)
        scores(jnp.minimum(t + 2, n_tiles - 1), s_a)
        finish([s_b[0], s_b[1]], v_ref[...], row_of(t + 1), tq)
        return carry

    lax.fori_loop(0, n_tiles // 2, pair_body, 0)


def _da_attention(qkv, lam_params, subln_g, layer_idx, ctx_len):
    b, r, _ = qkv.shape
    assert (r - ctx_len) % (2 * ROW_TILE) == 0
    nblk = D_MODEL // LANES
    lam_init = 0.8 - 0.6 * math.exp(-0.3 * layer_idx)
    kern = functools.partial(_da_kernel, ctx_len=ctx_len, lam_init=lam_init)
    return pl.pallas_call(
        kern,
        grid=(nblk, b),
        in_specs=[
            pl.BlockSpec((None, r, LANES), lambda h, bi: (bi, 0, h)),
            pl.BlockSpec((None, r, LANES), lambda h, bi: (bi, 0, nblk + h)),
            pl.BlockSpec((None, r, LANES), lambda h, bi: (bi, 0, 2 * nblk + h)),
            pl.BlockSpec((4, HEAD_DIM), lambda h, bi: (0, 0)),
            pl.BlockSpec((1, 2 * HEAD_DIM), lambda h, bi: (0, 0)),
        ],
        out_specs=pl.BlockSpec((None, r, LANES), lambda h, bi: (bi, 0, h)),
        out_shape=jax.ShapeDtypeStruct((b, r, D_MODEL), BF16),
        scratch_shapes=[pltpu.VMEM((2, ROW_TILE, r), F32), pltpu.VMEM((2, ROW_TILE, r), F32)],
        compiler_params=_params(("arbitrary", "arbitrary")),
        name="da_attn",
    )(qkv, qkv, qkv, lam_params.astype(F32), subln_g.astype(F32).reshape(1, -1))


def _sw_kernel(sink_ref, q_ref, k_ref, v_ref, o_ref, kd_ref, vd_ref, sl_a, sc_a, sl_b, sc_b, *, ctx_len, seq, need_ctx):
    kvp = pl.program_id(0)
    band = Q_BLOCK + 2 * SW_WINDOW
    hpk = SW_Q_HEADS // SW_KV_HEADS

    k = k_ref[...]
    v = v_ref[...]
    for hh in range(2):
        sl = slice(hh * HEAD_DIM, (hh + 1) * HEAD_DIM)
        kd_ref[hh] = jnp.concatenate([k[:, sl], k[:, sl]], axis=-1)
        _store_with_ones(vd_ref.at[hh], jnp.concatenate([v[:, sl], v[:, sl]], axis=-1))

    n_blk = 2 * (hpk // 2)

    def stacked_q(row0, rows, c):
        lo, hi = _head_masks(rows)
        qv = q_ref[pl.ds(row0, rows), c * LANES:(c + 1) * LANES]
        return jnp.concatenate([jnp.where(lo, qv, 0), jnp.where(hi, qv, 0)], axis=0)

    def per_head(rows, c, col, fn):
        sinks = [sink_ref[kvp * 2 * hpk + 2 * c + j] * LOG2E for j in range(2)]
        return jnp.concatenate([fn(col[j * rows:(j + 1) * rows], sinks[j]) for j in range(2)], axis=0)

    def write_out(row0, rows, c, o):
        lo, _ = _head_masks(rows)
        o_ref[pl.ds(row0, rows), c * LANES:(c + 1) * LANES] = jnp.where(lo, o[0:rows], o[rows:2 * rows]).astype(o_ref.dtype)

    def softmax_pv(rows, c, parts):
        row_max = functools.reduce(jnp.maximum, [jnp.max(s, axis=-1, keepdims=True) for s, _ in parts])
        mx = per_head(rows, c, row_max, lambda m, sink: jnp.maximum(m, sink))
        r = sum(_dot(jnp.exp2(s - mx).astype(BF16), val) for s, val in parts)
        den = r[:, LANES:2 * LANES] + per_head(rows, c, mx, lambda m, sink: jnp.exp2(sink - m))
        return r[:, 0:LANES] / den

    if need_ctx:
        for c in range(n_blk):
            hh = c // (hpk // 2)
            s = _dot_nt(stacked_q(0, ctx_len, c), kd_ref[hh, 0:ctx_len, :])
            write_out(0, ctx_len, c, softmax_pv(ctx_len, c, [(s, vd_ref[hh, 0:ctx_len, :])]))
    else:
        o_ref[0:ctx_len, :] = jnp.zeros((ctx_len, o_ref.shape[-1]), o_ref.dtype)

    n_steps = seq // Q_BLOCK

    def rows_of(n):
        start = n * Q_BLOCK
        b0 = jnp.clip(start - SW_WINDOW, 0, seq - band)
        return start, b0, pl.multiple_of(ctx_len + start, Q_BLOCK), pl.multiple_of(ctx_len + b0, Q_BLOCK)

    def scores(n, sl_ref, sc_ref):
        start, b0, qrow, brow = rows_of(n)
        qpos = start + (lax.broadcasted_iota(jnp.int32, (2 * Q_BLOCK, band), 0) & (Q_BLOCK - 1))
        kpos = b0 + lax.broadcasted_iota(jnp.int32, (2 * Q_BLOCK, band), 1)
        in_band = jnp.abs(kpos - qpos) <= SW_WINDOW
        for c in range(n_blk):
            hh = c // (hpk // 2)
            qs = stacked_q(qrow, Q_BLOCK, c)
            sl_ref[c] = jnp.where(in_band, _dot_nt(qs, kd_ref[hh, pl.ds(brow, band), :]), NEG_INF)
            sc_ref[c] = _dot_nt(qs, kd_ref[hh, 0:ctx_len, :])

    def finish(n, sl_ref, sc_ref):
        _, _, qrow, brow = rows_of(n)
        for c in range(n_blk):
            hh = c // (hpk // 2)
            o = softmax_pv(Q_BLOCK, c, [(sc_ref[c], vd_ref[hh, 0:ctx_len, :]),
                                        (sl_ref[c], vd_ref[hh, pl.ds(brow, band), :])])
            write_out(qrow, Q_BLOCK, c, o)

    scores(0, sl_a, sc_a)

    def pair_body(t, carry):
        n = 2 * t
        scores(n + 1, sl_b, sc_b)
        finish(n, sl_a, sc_a)
        scores(jnp.minimum(n + 2, n_steps - 1), sl_a, sc_a)
        finish(n + 1, sl_b, sc_b)
        return carry

    lax.fori_loop(0, n_steps // 2, pair_body, 0)


def _sw_attention(qkv, sinks, ctx_len, need_ctx):
    b, r, _ = qkv.shape
    seq = r - ctx_len
    assert seq % (2 * Q_BLOCK) == 0
    nq_blk = SW_Q_HEADS * HEAD_DIM // LANES
    slab = 2 * (SW_Q_HEADS // SW_KV_HEADS) * HEAD_DIM
    kern = functools.partial(_sw_kernel, ctx_len=ctx_len, seq=seq, need_ctx=need_ctx)
    n_kv_blk = SW_KV_HEADS * HEAD_DIM // LANES
    return pl.pallas_call(
        kern,
        grid=(n_kv_blk, b),
        in_specs=[
            pl.BlockSpec(memory_space=pltpu.SMEM),
            pl.BlockSpec((None, r, slab), lambda kp, bi: (bi, 0, kp)),
            pl.BlockSpec((None, r, LANES), lambda kp, bi: (bi, 0, nq_blk + kp)),
            pl.BlockSpec((None, r, LANES), lambda kp, bi: (bi, 0, nq_blk + n_kv_blk + kp)),
        ],
        out_specs=pl.BlockSpec((None, r, slab), lambda kp, bi: (bi, 0, kp)),
        out_shape=jax.ShapeDtypeStruct((b, r, D_MODEL), BF16),
        scratch_shapes=[pltpu.VMEM((2, r, LANES), BF16), pltpu.VMEM((2, r, 2 * LANES), BF16)]
        + [pltpu.VMEM((slab // LANES, 2 * Q_BLOCK, Q_BLOCK + 2 * SW_WINDOW), F32),
           pltpu.VMEM((slab // LANES, 2 * Q_BLOCK, ctx_len), F32)] * 2,
        compiler_params=_params(("arbitrary", "arbitrary")),
        name="sw_attn",
    )(sinks.astype(F32), qkv, qkv, qkv)


def _post_kernel(o_ref, x_ref, wo_ref, mod_ref, g_ref, wrh_ref, wrl_ref, br_ref, xn_ref, h_ref, rt_ref, *, ctx_tiles):
    d = x_ref.shape[-1]
    tm = x_ref.shape[0]
    row = _mod_row(pl.program_id(0), pl.program_id(1), ctx_tiles)
    gate1 = mod_ref[pl.ds(row, 1), 2 * d:3 * d]
    shift2 = mod_ref[pl.ds(row, 1), 3 * d:4 * d]
    scale2 = mod_ref[pl.ds(row, 1), 4 * d:5 * d]
    xn = x_ref[...] + gate1 * _dot(o_ref[...], wo_ref[...])
    xn_ref[...] = xn
    h = (_rms(xn) * g_ref[...]) * (1.0 + scale2) + shift2
    for j in range(SUBLANES):
        h_ref[pl.ds(j, tm, stride=SUBLANES), :] = h[:, j * LANES:(j + 1) * LANES]

    h_hi = h.astype(BF16)
    h_lo = (h - h_hi.astype(F32)).astype(BF16)
    logits = (_dot(h_hi, wrh_ref[...]) + _dot(h_lo, wrh_ref[...]) + _dot(h_hi, wrl_ref[...])) + br_ref[...]
    lane = lax.broadcasted_iota(jnp.int32, (tm, LANES), 1).astype(F32)

    def first_argmax(vals, mx):
        return jnp.min(jnp.where(vals == mx, lane, float(LANES)), axis=-1, keepdims=True)

    gl = jnp.where(lane < N_GROUPS, logits, -jnp.inf)
    ge = jnp.exp(gl - jnp.max(gl, axis=-1, keepdims=True))
    gp = ge / jnp.sum(ge, axis=-1, keepdims=True)
    g_p = jnp.max(gp, axis=-1, keepdims=True)
    g_idx = first_argmax(gp, g_p)
    e_lo = N_GROUPS + EXPERTS_PER_GROUP * g_idx
    el = jnp.where((lane >= e_lo) & (lane < e_lo + EXPERTS_PER_GROUP), logits, -jnp.inf)
    v1 = jnp.max(el, axis=-1, keepdims=True)
    i1 = first_argmax(el, v1)
    el2 = jnp.where(lane == i1, -jnp.inf, el)
    v2 = jnp.max(el2, axis=-1, keepdims=True)
    i2 = first_argmax(el2, v2)
    e2 = jnp.exp(v2 - v1)
    w1 = g_p * (1.0 / (1.0 + e2))
    w2 = g_p * (e2 / (1.0 + e2))
    lane8 = lax.broadcasted_iota(jnp.int32, (tm, SUBLANES), 1)
    rt = jnp.where(lane8 == 0, i1 - N_GROUPS,
                   jnp.where(lane8 == 1, i2 - N_GROUPS,
                             jnp.where(lane8 == 2, w1, jnp.where(lane8 == 3, w2, 0.0))))
    rt_ref[...] = rt


def _post(o, x, wo, mod, g, wr, br, ctx_len):
    wr_hi = wr.astype(BF16)
    wr_lo = (wr - wr_hi.astype(F32)).astype(BF16)
    b, r, d = x.shape
    nt = r // ROW_TILE
    kern = functools.partial(_post_kernel, ctx_tiles=ctx_len // ROW_TILE)
    return pl.pallas_call(
        kern,
        grid=(b, nt),
        in_specs=[
            pl.BlockSpec((None, ROW_TILE, d), lambda bi, i: (bi, i, 0)),
            pl.BlockSpec((None, ROW_TILE, d), lambda bi, i: (bi, i, 0)),
            pl.BlockSpec((d, d), lambda bi, i: (0, 0)),
            pl.BlockSpec((MOD_ROWS, 6 * d), lambda bi, i: (0, 0)),
            pl.BlockSpec((1, d), lambda bi, i: (0, 0)),
            pl.BlockSpec((d, LANES), lambda bi, i: (0, 0)),
            pl.BlockSpec((d, LANES), lambda bi, i: (0, 0)),
            pl.BlockSpec((1, LANES), lambda bi, i: (0, 0)),
        ],
        out_specs=[
            pl.BlockSpec((None, ROW_TILE, d), lambda bi, i: (bi, i, 0)),
            pl.BlockSpec((ROW_TILE * SUBLANES, LANES), lambda bi, i: (bi * nt + i, 0)),
            pl.BlockSpec((ROW_TILE, SUBLANES), lambda bi, i: (bi * nt + i, 0)),
        ],
        out_shape=[
            jax.ShapeDtypeStruct((b, r, d), F32),
            jax.ShapeDtypeStruct((b * r * SUBLANES, LANES), F32),
            jax.ShapeDtypeStruct((b * r, SUBLANES), F32),
        ],
        compiler_params=_params(("arbitrary", "arbitrary")),
        name="post_attn",
    )(o, x, wo, mod, g, wr_hi, wr_lo, br)


DISPATCH_TOKENS = 512


def _row(ref, idx):
    return ref.at[pl.ds(pl.multiple_of(idx * SUBLANES, SUBLANES), SUBLANES), :]


def _dispatch_kernel(dest_ref, h_ref, xs_hbm, sem):
    def issue(t, carry):
        src = _row(h_ref, t)
        for k in range(2):
            pltpu.make_async_copy(src, _row(xs_hbm, dest_ref[0, 0, 2 * t + k]), sem).start(priority=k)
        return carry

    lax.fori_loop(0, DISPATCH_TOKENS, issue, 0)
    for _ in range(2):
        pltpu.make_async_copy(h_ref, xs_hbm.at[pl.ds(0, DISPATCH_TOKENS * SUBLANES), :], sem).wait()


def _dispatch(h_rows, dest):
    t = dest.shape[0] // 2
    steps = t // DISPATCH_TOKENS
    return pl.pallas_call(
        _dispatch_kernel,
        grid=(steps,),
        in_specs=[
            pl.BlockSpec((1, 1, 2 * DISPATCH_TOKENS), lambda i: (i, 0, 0), memory_space=pltpu.SMEM),
            pl.BlockSpec((DISPATCH_TOKENS * SUBLANES, LANES), lambda i: (i, 0)),
        ],
        out_specs=pl.BlockSpec(memory_space=pl.ANY),
        out_shape=jax.ShapeDtypeStruct((2 * t * SUBLANES, LANES), F32),
        scratch_shapes=[pltpu.SemaphoreType.DMA(())],
        compiler_params=_params(("arbitrary",)),
        name="moe_dispatch",
    )(dest.reshape(steps, 1, 2 * DISPATCH_TOKENS), h_rows)


def _experts_kernel(ve_ref, vt_ref, lo_ref, hi_ref, first_ref, valid_ref,
                    xs_ref, wg_ref, wu_ref, wd_ref, ys_ref, wgu_s, wd_s, acc_s):
    v = pl.program_id(0)
    tm = ROW_TILE
    f = wg_ref.shape[-1]

    @pl.when(v == 0)
    def _():
        acc_s[...] = jnp.zeros(acc_s.shape, acc_s.dtype)

    @pl.when(valid_ref[v] == 1)
    def _():
        changed = jnp.logical_or(v == 0, ve_ref[v] != ve_ref[jnp.maximum(v - 1, 0)])

        @pl.when(changed)
        def _():
            wgu_s[:, 0:f] = wg_ref[...].astype(BF16)
            wgu_s[:, f:2 * f] = wu_ref[...].astype(BF16)
            wd_s[...] = wd_ref[...].astype(BF16)

        x = jnp.concatenate([xs_ref[pl.ds(j, tm, stride=SUBLANES), :] for j in range(SUBLANES)], axis=-1)
        au = _dot(x.astype(BF16), wgu_s[...])
        rows = vt_ref[v] * tm + lax.broadcasted_iota(jnp.int32, (tm, 1), 0)
        mine = (rows >= lo_ref[v]) & (rows < hi_ref[v])
        mid = jnp.where(mine, jax.nn.silu(au[:, 0:f]) * au[:, f:2 * f], 0.0)
        y = _dot(mid.astype(BF16), wd_s[...])
        y = jnp.where(first_ref[v] == 1, y, acc_s[...] + y)
        acc_s[...] = y
        for j in range(SUBLANES):
            ys_ref[pl.ds(j, tm, stride=SUBLANES), :] = y[:, j * LANES:(j + 1) * LANES]


def _experts(xs, sched, w_gate, w_up, w_down, layer):
    rows = xs.shape[0] // SUBLANES
    d, f = w_gate.shape[-2:]
    n_visits = sched[0].shape[0]
    blk = (ROW_TILE * SUBLANES, LANES)
    grid_spec = pltpu.PrefetchScalarGridSpec(
        num_scalar_prefetch=6,
        grid=(n_visits,),
        in_specs=[
            pl.BlockSpec(blk, lambda v, ve, vt, lo, hi, fi, va: (vt[v], 0)),
            pl.BlockSpec((None, None, d, f), lambda v, ve, vt, lo, hi, fi, va: (layer, ve[v], 0, 0)),
            pl.BlockSpec((None, None, d, f), lambda v, ve, vt, lo, hi, fi, va: (layer, ve[v], 0, 0)),
            pl.BlockSpec((None, None, f, d), lambda v, ve, vt, lo, hi, fi, va: (layer, ve[v], 0, 0)),
        ],
        out_specs=pl.BlockSpec(blk, lambda v, ve, vt, lo, hi, fi, va: (vt[v], 0)),
        scratch_shapes=[pltpu.VMEM((d, 2 * f), BF16), pltpu.VMEM((f, d), BF16), pltpu.VMEM((ROW_TILE, d), F32)],
    )
    return pl.pallas_call(
        _experts_kernel,
        grid_spec=grid_spec,
        out_shape=jax.ShapeDtypeStruct((rows * SUBLANES, LANES), F32),
        compiler_params=_params(("arbitrary",)),
        name="moe_experts",
    )(*sched, xs, w_gate, w_up, w_down)


COMBINE_UNROLL = 4


def _combine_kernel(dest_ref, next_ref, rt_ref, x_ref, mod_ref, g_ref, ys_hbm, o_ref, buf, sem, *,
                    ctx_tiles, tiles_per_batch, final):
    tm = ROW_TILE
    d = x_ref.shape[-1]
    s = pl.program_id(0)
    slot = s % 2
    half_rows = tm * SUBLANES

    def gather(idx_ref, slot_):
        def issue(t4, carry):
            for u in range(COMBINE_UNROLL):
                t = t4 * COMBINE_UNROLL + u
                for k in range(2):
                    pltpu.make_async_copy(_row(ys_hbm, idx_ref[0, 0, 2 * t + k]),
                                          _row(buf.at[slot_], k * tm + t), sem.at[slot_]).start(priority=k)
            return carry

        lax.fori_loop(0, tm // COMBINE_UNROLL, issue, 0)

    @pl.when(s == 0)
    def _():
        gather(dest_ref, 0)

    for parity in range(2):
        @pl.when(jnp.logical_and(s + 1 < pl.num_programs(0), slot == parity))
        def _():
            gather(next_ref, 1 - parity)

    for k in range(2):
        part = pl.ds(k * half_rows, half_rows)
        pltpu.make_async_copy(ys_hbm.at[part, :], buf.at[slot].at[part, :], sem.at[slot]).wait()

    row = _mod_row(s // tiles_per_batch, s % tiles_per_batch, ctx_tiles)
    gate2 = mod_ref[pl.ds(row, 1), 5 * d:6 * d]
    rt = rt_ref[...]
    w = (rt[:, 2:3], rt[:, 3:4])
    rows = buf.at[slot]
    pieces = []
    for j in range(SUBLANES):
        y = (w[0] * rows[pl.ds(j, tm, stride=SUBLANES), :]
             + w[1] * rows[pl.ds(half_rows + j, tm, stride=SUBLANES), :])
        sl = slice(j * LANES, (j + 1) * LANES)
        pieces.append(x_ref[:, sl] + gate2[:, sl] * y)
    if final:
        o_ref[...] = _rms(jnp.concatenate(pieces, axis=-1)) * g_ref[...]
    else:
        for j in range(SUBLANES):
            o_ref[:, j * LANES:(j + 1) * LANES] = pieces[j]


def _combine(ys, dest, rt, x, mod, ctx_len, final_g=None):
    b, r, d = x.shape
    t = b * r
    steps = t // ROW_TILE
    ct, tpb = ctx_len // ROW_TILE, r // ROW_TILE
    final = final_g is not None
    kern = functools.partial(_combine_kernel, ctx_tiles=ct, tiles_per_batch=tpb, final=final)
    if final:
        out_rows = t - b * ctx_len
        out_map = lambda i: ((i // tpb) * (tpb - ct) + jnp.maximum(i % tpb - ct, 0), 0)
    else:
        out_rows = t
        out_map = lambda i: (i, 0)
        final_g = jnp.zeros((1, d), F32)
    dest3 = dest.reshape(steps, 1, 2 * ROW_TILE)
    out = pl.pallas_call(
        kern,
        grid=(steps,),
        in_specs=[
            pl.BlockSpec((1, 1, 2 * ROW_TILE), lambda i: (i, 0, 0), memory_space=pltpu.SMEM),
            pl.BlockSpec((1, 1, 2 * ROW_TILE), lambda i: (jnp.minimum(i + 1, steps - 1), 0, 0), memory_space=pltpu.SMEM),
            pl.BlockSpec((ROW_TILE, SUBLANES), lambda i: (i, 0)),
            pl.BlockSpec((ROW_TILE, d), lambda i: (i, 0)),
            pl.BlockSpec((MOD_ROWS, 6 * d), lambda i: (0, 0)),
            pl.BlockSpec((1, d), lambda i: (0, 0)),
            pl.BlockSpec(memory_space=pl.ANY),
        ],
        out_specs=pl.BlockSpec((ROW_TILE, d), out_map),
        out_shape=jax.ShapeDtypeStruct((out_rows, d), F32),
        scratch_shapes=[pltpu.VMEM((2, 2 * ROW_TILE * SUBLANES, LANES), F32), pltpu.SemaphoreType.DMA((2,))],
        compiler_params=_params(("arbitrary",)),
        name="moe_combine",
    )(dest3, dest3, rt, x.reshape(t, d), mod, final_g, ys)
    return out.reshape(b, out_rows // b, d)


def _route_plan(rt, n_tiles):
    eid = rt[:, 0:2].astype(jnp.int32).reshape(-1)
    onehot = (eid[:, None] == jnp.arange(N_EXPERTS, dtype=jnp.int32)[None, :]).astype(jnp.int32)
    csum = jnp.cumsum(onehot, axis=0)
    counts = csum[-1]
    cstart = jnp.cumsum(counts) - counts
    dest = jnp.sum(onehot * (cstart[None, :] + csum - 1), axis=1).astype(jnp.int32)

    tm = ROW_TILE
    first_tile = cstart // tm
    last_tile = (cstart + counts - 1) // tm
    nvis = jnp.where(counts > 0, last_tile - first_tile + 1, 0)
    vend = jnp.cumsum(nvis)
    vstart = vend - nvis
    total = vend[-1]
    n_visits = n_tiles + N_EXPERTS - 1
    v = jnp.arange(n_visits, dtype=jnp.int32)
    valid = v < total
    vc = jnp.minimum(v, total - 1)
    ve = jnp.sum((vc[:, None] >= vend[None, :]).astype(jnp.int32), axis=1)
    oh_e = (ve[:, None] == jnp.arange(N_EXPERTS, dtype=jnp.int32)[None, :]).astype(jnp.int32)
    pick = lambda a: jnp.sum(oh_e * a[None, :], axis=1)
    vt = pick(first_tile) + vc - pick(vstart)
    lo = pick(cstart)
    hi = jnp.where(valid, lo + pick(counts), lo)
    first = jnp.concatenate([jnp.ones((1,), jnp.int32), (vt[1:] != vt[:-1]).astype(jnp.int32)])
    sched = tuple(a.astype(jnp.int32) for a in (ve, vt, lo, hi, first, valid))
    return dest, sched


def kernel(x, c, ctx, c_ctx, ada_w, ada_b, norm_mix_g, norm_ffn_g, final_norm_g, na_w_qkv, na_rpb, na_w_o, da_w_qkv, da_lambda, da_subln_g, da_w_o, sw_w_qkv, sw_sinks, sw_w_o, moe_w_grp, moe_b_grp, moe_w_exp, moe_b_exp, moe_w_gate, moe_w_up, moe_w_down):
    b, seq, d = x.shape
    ctx_len = ctx.shape[1]
    depth = ada_w.shape[0]
    assert d == D_MODEL and ctx_len % ROW_TILE == 0 and seq % ROW_TILE == 0 and b <= MOD_ROWS // 2
    assert (b * (ctx_len + seq)) % DISPATCH_TOKENS == 0

    half = MOD_ROWS // 2
    cond = jnp.concatenate([jnp.pad(c, ((0, half - b), (0, 0))), jnp.pad(c_ctx[None, :], ((0, half - 1), (0, 0)))], axis=0)
    mod_all = _adaln(cond, ada_w, ada_b)
    cos, sin = _rope_tables(seq, ctx_len)
    stream = jnp.concatenate([ctx, x], axis=1)
    n_tiles = 2 * b * (ctx_len + seq) // ROW_TILE

    for i in range(depth):
        need_ctx = i < depth - 1
        kind, j = i % N_MIXERS, i // N_MIXERS
        mod = mod_all[i]
        g_mix = norm_mix_g[i].reshape(1, d)
        if kind == 0:
            qkv = _qkv(stream, g_mix, mod, na_w_qkv[j].astype(BF16), cos, sin, 0, NA_HEADS * HEAD_DIM, ctx_len)
            o = _na_attention(qkv, na_rpb[j], ctx_len, need_ctx)
            wo = na_w_o[j]
        elif kind == 1:
            nq = 2 * DA_HEADS * HEAD_DIM
            qkv = _qkv(stream, g_mix, mod, da_w_qkv[j].astype(BF16), cos, sin, 2 * nq, nq, ctx_len)
            o = _da_attention(qkv, da_lambda[j], da_subln_g[j], i, ctx_len)
            wo = da_w_o[j]
        else:
            rope_cols = (SW_Q_HEADS + SW_KV_HEADS) * HEAD_DIM
            qkv = _qkv(stream, g_mix, mod, sw_w_qkv[j].astype(BF16), cos, sin, rope_cols, SW_Q_HEADS * HEAD_DIM, ctx_len)
            o = _sw_attention(qkv, sw_sinks[j], ctx_len, need_ctx)
            wo = sw_w_o[j]

        pad = LANES - N_GROUPS - N_EXPERTS
        wr = jnp.pad(jnp.concatenate([moe_w_grp[i], moe_w_exp[i]], axis=1).astype(F32), ((0, 0), (0, pad)))
        br = jnp.pad(jnp.concatenate([moe_b_grp[i], moe_b_exp[i]]).astype(F32), (0, pad)).reshape(1, LANES)
        stream, h_rows, rt = _post(o, stream, wo.astype(BF16), mod, norm_ffn_g[i].reshape(1, d), wr, br, ctx_len)
        dest, sched = _route_plan(rt, n_tiles)
        xs = _dispatch(h_rows, dest)
        ys = _experts(xs, sched, moe_w_gate, moe_w_up, moe_w_down, i)
        last = i == depth - 1
        stream = _combine(ys, dest, rt, stream, mod, ctx_len, final_norm_g.reshape(1, d).astype(F32) if last else None)

    return stream
```

```python
import functools
import math

import jax
import jax.numpy as jnp
from jax import lax
from jax.experimental import pallas as pl
from jax.experimental.pallas import tpu as pltpu

F32 = jnp.float32
BF16 = jnp.bfloat16

D_MODEL = 1024
HEAD_DIM = 64
GRID_W = 64
ROPE_AXIS_PAIRS = HEAD_DIM // 4
ROPE_BASE = 10000.0
NA_HEADS = D_MODEL // HEAD_DIM
NA_WIN_R = 8
NA_WIN_C = 16
NA_GROUP = 4
NA_UNION = NA_GROUP + NA_WIN_R
DA_HEADS = D_MODEL // (2 * HEAD_DIM)
SW_Q_HEADS = D_MODEL // HEAD_DIM
SW_KV_HEADS = 4
SW_WINDOW = 128
Q_BLOCK = 128
N_GROUPS = 4
EXPERTS_PER_GROUP = 8
N_EXPERTS = N_GROUPS * EXPERTS_PER_GROUP
D_EXPERT = 512
NORM_EPS = 1e-6
NEG_INF = -1e30
N_MIXERS = 3

LANES = 128
SUBLANES = 8
ROW_TILE = 256
MOD_ROWS = 16
VMEM_LIMIT = 56 * 1024 * 1024

LOG2E = math.log2(math.e)
Q_SCALE = HEAD_DIM ** -0.5 * LOG2E

_NT = (((1,), (1,)), ((), ()))


def _dot(a, b, **kw):
    return jnp.dot(a, b, preferred_element_type=F32, **kw)


def _dot_nt(a, b):
    return lax.dot_general(a, b, _NT, preferred_element_type=F32)


def _params(sem, vmem=VMEM_LIMIT):
    return pltpu.CompilerParams(dimension_semantics=sem, vmem_limit_bytes=vmem)


def _mod_row(b, tile, ctx_tiles):
    return jnp.where(tile < ctx_tiles, MOD_ROWS // 2, b)


def _rms(x):
    return x * lax.rsqrt(jnp.mean(x * x, axis=-1, keepdims=True) + NORM_EPS)


def _adaln_kernel(c_ref, w_ref, b_ref, o_ref):
    s = jax.nn.silu(c_ref[...])
    o_ref[...] = _dot(s, w_ref[...], precision=lax.Precision.HIGHEST) + b_ref[...]


def _adaln(cond, ada_w, ada_b):
    depth, d, n6 = ada_w.shape
    tn = 1536
    return pl.pallas_call(
        _adaln_kernel,
        grid=(depth, n6 // tn),
        in_specs=[
            pl.BlockSpec((MOD_ROWS, d), lambda l, n: (0, 0)),
            pl.BlockSpec((None, d, tn), lambda l, n: (l, 0, n)),
            pl.BlockSpec((None, 1, tn), lambda l, n: (l, 0, n)),
        ],
        out_specs=pl.BlockSpec((None, MOD_ROWS, tn), lambda l, n: (l, 0, n)),
        out_shape=jax.ShapeDtypeStruct((depth, MOD_ROWS, n6), F32),
        compiler_params=_params(("arbitrary", "arbitrary")),
        name="adaln",
    )(cond, ada_w, ada_b.reshape(depth, 1, n6))


def _qkv_tile(x, row, g_ref, mod_ref, w_ref, cos_ref, sin_ref, o_ref, rope_cols, q_cols):
    d = x.shape[-1]
    shift = mod_ref[pl.ds(row, 1), 0:d]
    scale = mod_ref[pl.ds(row, 1), d:2 * d]
    h = (_rms(x) * g_ref[...]) * (1.0 + scale) + shift
    acc = _dot(h.astype(BF16), w_ref[...])
    n = acc.shape[-1]
    if rope_cols:
        tm = acc.shape[0]
        first_half = (lax.broadcasted_iota(jnp.int32, (tm, LANES), 1) % (2 * ROPE_AXIS_PAIRS)) < ROPE_AXIS_PAIRS
        for c in range(rope_cols // LANES):
            tab = slice(LANES, 2 * LANES) if c * LANES < q_cols else slice(0, LANES)
            t = acc[:, c * LANES:(c + 1) * LANES]
            up = pltpu.roll(t, LANES - ROPE_AXIS_PAIRS, 1)
            dn = pltpu.roll(t, ROPE_AXIS_PAIRS, 1)
            o_ref[:, c * LANES:(c + 1) * LANES] = (
                t * cos_ref[:, tab] + jnp.where(first_half, up, dn) * sin_ref[:, tab]).astype(o_ref.dtype)
    else:
        o_ref[:, 0:q_cols] = (acc[:, 0:q_cols] * Q_SCALE).astype(o_ref.dtype)
        rope_cols = q_cols
    if rope_cols < n:
        o_ref[:, rope_cols:] = acc[:, rope_cols:].astype(o_ref.dtype)


def _qkv_kernel(x_ref, g_ref, mod_ref, w_ref, cos_ref, sin_ref, o_ref, *, rope_cols, q_cols, ctx_tiles):
    row = _mod_row(pl.program_id(0), pl.program_id(1), ctx_tiles)
    _qkv_tile(x_ref[...], row, g_ref, mod_ref, w_ref, cos_ref, sin_ref, o_ref, rope_cols, q_cols)


def _qkv(x, g, mod, w, cos, sin, rope_cols, q_cols, ctx_len):
    b, r, d = x.shape
    n = w.shape[1]
    kern = functools.partial(_qkv_kernel, rope_cols=rope_cols, q_cols=q_cols, ctx_tiles=ctx_len // ROW_TILE)
    return pl.pallas_call(
        kern,
        grid=(b, r // ROW_TILE),
        in_specs=[
            pl.BlockSpec((None, ROW_TILE, d), lambda bi, i: (bi, i, 0)),
            pl.BlockSpec((1, d), lambda bi, i: (0, 0)),
            pl.BlockSpec((MOD_ROWS, 6 * d), lambda bi, i: (0, 0)),
            pl.BlockSpec((d, n), lambda bi, i: (0, 0)),
            pl.BlockSpec((ROW_TILE, 2 * LANES), lambda bi, i: (i, 0)),
            pl.BlockSpec((ROW_TILE, 2 * LANES), lambda bi, i: (i, 0)),
        ],
        out_specs=pl.BlockSpec((None, ROW_TILE, n), lambda bi, i: (bi, i, 0)),
        out_shape=jax.ShapeDtypeStruct((b, r, n), BF16),
        compiler_params=_params(("arbitrary", "arbitrary")),
        name="qkv",
    )(x, g, mod, w, cos, sin)


def _rope_tables(seq, ctx_len):
    t = jnp.arange(seq, dtype=jnp.int32)
    row = (t // GRID_W).astype(F32)
    col = (t % GRID_W).astype(F32)
    inv = ROPE_BASE ** (-jnp.arange(ROPE_AXIS_PAIRS, dtype=F32) / ROPE_AXIS_PAIRS)
    ang_r = row[:, None] * inv[None, :]
    ang_c = col[:, None] * inv[None, :]
    ang = jnp.concatenate([ang_r, ang_r, ang_c, ang_c], axis=-1)
    sign = jnp.tile(jnp.concatenate([-jnp.ones((ROPE_AXIS_PAIRS,), F32), jnp.ones((ROPE_AXIS_PAIRS,), F32)]), 2)
    cos = jnp.concatenate([jnp.ones((ctx_len, HEAD_DIM), F32), jnp.cos(ang)], axis=0)
    sin = jnp.concatenate([jnp.zeros((ctx_len, HEAD_DIM), F32), jnp.sin(ang) * sign[None, :]], axis=0)
    cos = jnp.tile(cos, (1, LANES // HEAD_DIM))
    sin = jnp.tile(sin, (1, LANES // HEAD_DIM))
    return jnp.concatenate([cos, cos * Q_SCALE], axis=1), jnp.concatenate([sin, sin * Q_SCALE], axis=1)


def _store_with_ones(dst_ref, v):
    dst_ref[:, 0:LANES] = v
    dst_ref[:, LANES:2 * LANES] = jnp.ones(v.shape, v.dtype)


def _head_masks(rows):
    lane = lax.broadcasted_iota(jnp.int32, (rows, LANES), 1)
    lo = lane < HEAD_DIM
    return lo, jnp.logical_not(lo)


def _na_group_geometry(g, grid_rows):
    u0 = min(max(g * NA_GROUP - NA_WIN_R // 2, 0), grid_rows - NA_UNION)
    idx = []
    for a in range(NA_GROUP):
        r = g * NA_GROUP + a
        r0 = min(max(r - NA_WIN_R // 2, 0), grid_rows - NA_WIN_R)
        idx.append([u0 + y - r + NA_WIN_R - 1 if r0 <= u0 + y < r0 + NA_WIN_R else 2 * NA_WIN_R - 1
                    for y in range(NA_UNION)])
    return u0, idx


def _na_kernel(q_ref, k_ref, v_ref, bt_ref, o_ref, fb_ref, v1_ref, nb_a, cx_a, nb_b, cx_b, *, ctx_len, grid_rows, need_ctx):
    n_groups = grid_rows // NA_GROUP

    @pl.when(pl.program_id(1) == 0)
    def _():
        for j in range(2):
            for pattern, g in enumerate((0, 1, n_groups - 1)):
                _, idx = _na_group_geometry(g, grid_rows)
                for a in range(NA_GROUP):
                    for p in range(NA_UNION // 2):
                        fb_ref[j, pattern, a * GRID_W:(a + 1) * GRID_W, 2 * p * GRID_W:(2 * p + 2) * GRID_W] = (
                            jnp.concatenate([bt_ref[j, idx[a][2 * p]], bt_ref[j, idx[a][2 * p + 1]]], axis=-1))

    _store_with_ones(v1_ref, v_ref[...])
    kc = k_ref[0:ctx_len, :]
    vc = v1_ref[0:ctx_len, :]

    def normalised(r):
        return r[:, 0:LANES] / r[:, LANES:2 * LANES]

    if need_ctx:
        qc = q_ref[0:ctx_len, :]
        lo, hi = _head_masks(ctx_len)
        outs = []
        for m in (lo, hi):
            s = _dot_nt(jnp.where(m, qc, 0), kc)
            e = jnp.exp2(s - jnp.max(s, axis=-1, keepdims=True))
            outs.append(normalised(_dot(e.astype(BF16), vc)))
        o_ref[0:ctx_len, :] = jnp.where(lo, outs[0], outs[1]).astype(o_ref.dtype)
    else:
        o_ref[0:ctx_len, :] = jnp.zeros((ctx_len, LANES), o_ref.dtype)

    gq = NA_GROUP * GRID_W
    uk = NA_UNION * GRID_W
    lo, hi = _head_masks(gq)

    def rows_of(g):
        u0 = jnp.clip(g * NA_GROUP - NA_WIN_R // 2, 0, grid_rows - NA_UNION)
        return pl.multiple_of(ctx_len + g * gq, gq), pl.multiple_of(ctx_len + u0 * GRID_W, GRID_W)

    def scores(g, nb_ref, cx_ref):
        q0, w0 = rows_of(g)
        pattern = jnp.where(g == 0, 0, jnp.where(g == n_groups - 1, 2, 1))
        q_g = q_ref[pl.ds(q0, gq), :]
        k_u = k_ref[pl.ds(w0, uk), :]
        for j, m in enumerate((lo, hi)):
            qj = jnp.where(m, q_g, 0)
            nb_ref[j] = _dot_nt(qj, k_u) + fb_ref[j, pattern]
            cx_ref[j] = _dot_nt(qj, kc)

    def finish(g, nb_ref, cx_ref):
        q0, w0 = rows_of(g)
        v_u = v1_ref[pl.ds(w0, uk), :]
        outs = []
        for j in range(2):
            s_nb = nb_ref[j]
            s_cx = cx_ref[j]
            mx = jnp.maximum(jnp.max(s_nb, axis=-1, keepdims=True), jnp.max(s_cx, axis=-1, keepdims=True))
            e_nb = jnp.exp2(s_nb - mx).astype(BF16)
            e_cx = jnp.exp2(s_cx - mx).astype(BF16)
            outs.append(normalised(_dot(e_cx, vc) + _dot(e_nb, v_u)))
        o_ref[pl.ds(q0, gq), :] = jnp.where(lo, outs[0], outs[1]).astype(o_ref.dtype)

    scores(0, nb_a, cx_a)

    def pair_body(t, carry):
        g = 2 * t
        scores(g + 1, nb_b, cx_b)
        finish(g, nb_a, cx_a)
        scores(jnp.minimum(g + 2, n_groups - 1), nb_a, cx_a)
        finish(g + 1, nb_b, cx_b)
        return carry

    lax.fori_loop(0, n_groups // 2, pair_body, 0)


def _na_bias_table(rpb, grid_rows):
    n_groups = grid_rows // NA_GROUP
    assert grid_rows % (2 * NA_GROUP) == 0 and n_groups >= 3 and grid_rows >= NA_UNION
    for g in range(2, n_groups - 1):
        u1, idx1 = _na_group_geometry(1, grid_rows)
        ug, idxg = _na_group_geometry(g, grid_rows)
        assert idxg == idx1 and ug - g * NA_GROUP == u1 - NA_GROUP
    qcol = jnp.arange(GRID_W)
    kcol = jnp.arange(GRID_W)
    c0 = jnp.clip(qcol - NA_WIN_C // 2, 0, GRID_W - NA_WIN_C)
    col_ok = (kcol[None, :] >= c0[:, None]) & (kcol[None, :] < c0[:, None] + NA_WIN_C)
    rel = kcol[None, :] - qcol[:, None] + NA_WIN_C - 1
    onehot = ((rel[:, :, None] == jnp.arange(2 * NA_WIN_C - 1)[None, None, :]) & col_ok[:, :, None]).astype(F32)
    bt = jnp.einsum('hyd,qkd->hyqk', rpb.astype(F32), onehot, precision=lax.Precision.HIGHEST)
    bt = jnp.where(col_ok[None, None], bt * LOG2E, NEG_INF)
    return jnp.pad(bt, ((0, 0), (0, 1), (0, 0), (0, 0)), constant_values=NEG_INF)


def _na_attention(qkv, rpb, ctx_len, need_ctx):
    b, r, _ = qkv.shape
    grid_rows = (r - ctx_len) // GRID_W
    nblk = D_MODEL // LANES
    bt = _na_bias_table(rpb, grid_rows)
    kern = functools.partial(_na_kernel, ctx_len=ctx_len, grid_rows=grid_rows, need_ctx=need_ctx)
    return pl.pallas_call(
        kern,
        grid=(nblk, b),
        in_specs=[
            pl.BlockSpec((None, r, LANES), lambda hp, bi: (bi, 0, hp)),
            pl.BlockSpec((None, r, LANES), lambda hp, bi: (bi, 0, nblk + hp)),
            pl.BlockSpec((None, r, LANES), lambda hp, bi: (bi, 0, 2 * nblk + hp)),
            pl.BlockSpec((2, 2 * NA_WIN_R, GRID_W, GRID_W), lambda hp, bi: (hp, 0, 0, 0)),
        ],
        out_specs=pl.BlockSpec((None, r, LANES), lambda hp, bi: (bi, 0, hp)),
        out_shape=jax.ShapeDtypeStruct((b, r, D_MODEL), BF16),
        scratch_shapes=[pltpu.VMEM((2, 3, NA_GROUP * GRID_W, NA_UNION * GRID_W), F32), pltpu.VMEM((r, 2 * LANES), BF16)]
        + [pltpu.VMEM((2, NA_GROUP * GRID_W, NA_UNION * GRID_W), F32),
           pltpu.VMEM((2, NA_GROUP * GRID_W, ctx_len), F32)] * 2,
        compiler_params=_params(("arbitrary", "arbitrary")),
        name="na_attn",
    )(qkv, qkv, qkv, bt)


def _da_kernel(q_ref, k_ref, v_ref, lam_ref, g_ref, o_ref, v1_ref, s_a, s_b, *, ctx_len, lam_init):
    lp = lam_ref[...]
    lam = (jnp.exp(jnp.sum(lp[0:1] * lp[1:2], axis=-1, keepdims=True))
           - jnp.exp(jnp.sum(lp[2:3] * lp[3:4], axis=-1, keepdims=True)) + lam_init)
    tq = ROW_TILE
    n_tiles = (q_ref.shape[0] - ctx_len) // tq
    lo, hi = _head_masks(tq)

    _store_with_ones(v1_ref, v_ref[...])

    def finish(scores, n_keys, row0, rows):
        ps = []
        for s in scores:
            e = jnp.exp2(s - jnp.max(s, axis=-1, keepdims=True))
            r = _dot(e.astype(BF16), v1_ref[0:n_keys, :])
            ps.append(r[:, 0:LANES] / r[:, LANES:2 * LANES])
        o = ps[0] - lam * ps[1]
        o_ref[pl.ds(row0, rows), :] = (_rms(o) * g_ref[...] * (1.0 - lam_init)).astype(o_ref.dtype)

    qc = q_ref[0:ctx_len, :]
    lo_c, hi_c = _head_masks(ctx_len)
    finish([_dot_nt(jnp.where(m, qc, 0), k_ref[0:ctx_len, :]) for m in (lo_c, hi_c)], ctx_len, 0, ctx_len)

    def row_of(t):
        return pl.multiple_of(ctx_len + t * tq, tq)

    def scores(t, s_ref):
        q = q_ref[pl.ds(row_of(t), tq), :]
        for j, m in enumerate((lo, hi)):
            s_ref[j] = _dot_nt(jnp.where(m, q, 0), k_ref[...])

    scores(0, s_a)

    def pair_body(p, carry):
        t = 2 * p
        scores(t + 1, s_b)
        finish([s_a[0], s_a[1]], k_ref.shape[0], row_of(t), tq)
        scores(jnp.minimum(t + 2, n_tiles - 1), s_a)
        finish([s_b[0], s_b[1]], k_ref.shape[0], row_of(t + 1), tq)
        return carry

    lax.fori_loop(0, n_tiles // 2, pair_body, 0)


def _da_attention(qkv, lam_params, subln_g, layer_idx, ctx_len):
    b, r, _ = qkv.shape
    assert (r - ctx_len) % (2 * ROW_TILE) == 0
    nblk = D_MODEL // LANES
    lam_init = 0.8 - 0.6 * math.exp(-0.3 * layer_idx)
    kern = functools.partial(_da_kernel, ctx_len=ctx_len, lam_init=lam_init)
    return pl.pallas_call(
        kern,
        grid=(nblk, b),
        in_specs=[
            pl.BlockSpec((None, r, LANES), lambda h, bi: (bi, 0, h)),
            pl.BlockSpec((None, r, LANES), lambda h, bi: (bi, 0, nblk + h)),
            pl.BlockSpec((None, r, LANES), lambda h, bi: (bi, 0, 2 * nblk + h)),
            pl.BlockSpec((4, HEAD_DIM), lambda h, bi: (0, 0)),
            pl.BlockSpec((1, 2 * HEAD_DIM), lambda h, bi: (0, 0)),
        ],
        out_specs=pl.BlockSpec((None, r, LANES), lambda h, bi: (bi, 0, h)),
        out_shape=jax.ShapeDtypeStruct((b, r, D_MODEL), BF16),
        scratch_shapes=[pltpu.VMEM((r, 2 * LANES), BF16), pltpu.VMEM((2, ROW_TILE, r), F32), pltpu.VMEM((2, ROW_TILE, r), F32)],
        compiler_params=_params(("arbitrary", "arbitrary")),
        name="da_attn",
    )(qkv, qkv, qkv, lam_params.astype(F32), subln_g.astype(F32).reshape(1, -1))


def _sw_kernel(sink_ref, q_ref, k_ref, v_ref, o_ref, kd_ref, vd_ref, sl_a, sc_a, sl_b, sc_b, *, ctx_len, seq, need_ctx):
    kvp = pl.program_id(0)
    band = Q_BLOCK + 2 * SW_WINDOW
    hpk = SW_Q_HEADS // SW_KV_HEADS

    k = k_ref[...]
    v = v_ref[...]
    for hh in range(2):
        sl = slice(hh * HEAD_DIM, (hh + 1) * HEAD_DIM)
        kd_ref[hh] = jnp.concatenate([k[:, sl], k[:, sl]], axis=-1)
        _store_with_ones(vd_ref.at[hh], jnp.concatenate([v[:, sl], v[:, sl]], axis=-1))

    n_blk = 2 * (hpk // 2)

    def stacked_q(row0, rows, c):
        lo, hi = _head_masks(rows)
        qv = q_ref[pl.ds(row0, rows), c * LANES:(c + 1) * LANES]
        return jnp.concatenate([jnp.where(lo, qv, 0), jnp.where(hi, qv, 0)], axis=0)

    def per_head(rows, c, col, fn):
        sinks = [sink_ref[kvp * 2 * hpk + 2 * c + j] * LOG2E for j in range(2)]
        return jnp.concatenate([fn(col[j * rows:(j + 1) * rows], sinks[j]) for j in range(2)], axis=0)

    def write_out(row0, rows, c, o):
        lo, _ = _head_masks(rows)
        o_ref[pl.ds(row0, rows), c * LANES:(c + 1) * LANES] = jnp.where(lo, o[0:rows], o[rows:2 * rows]).astype(o_ref.dtype)

    def softmax_pv(rows, c, parts):
        row_max = functools.reduce(jnp.maximum, [jnp.max(s, axis=-1, keepdims=True) for s, _ in parts])
        mx = per_head(rows, c, row_max, lambda m, sink: jnp.maximum(m, sink))
        r = sum(_dot(jnp.exp2(s - mx).astype(BF16), val) for s, val in parts)
        den = r[:, LANES:2 * LANES] + per_head(rows, c, mx, lambda m, sink: jnp.exp2(sink - m))
        return r[:, 0:LANES] / den

    if need_ctx:
        for c in range(n_blk):
            hh = c // (hpk // 2)
            s = _dot_nt(stacked_q(0, ctx_len, c), kd_ref[hh, 0:ctx_len, :])
            write_out(0, ctx_len, c, softmax_pv(ctx_len, c, [(s, vd_ref[hh, 0:ctx_len, :])]))
    else:
        o_ref[0:ctx_len, :] = jnp.zeros((ctx_len, o_ref.shape[-1]), o_ref.dtype)

    n_steps = seq // Q_BLOCK

    def rows_of(n):
        start = n * Q_BLOCK
        b0 = jnp.clip(start - SW_WINDOW, 0, seq - band)
        return start, b0, pl.multiple_of(ctx_len + start, Q_BLOCK), pl.multiple_of(ctx_len + b0, Q_BLOCK)

    def scores(n, sl_ref, sc_ref):
        start, b0, qrow, brow = rows_of(n)
        qpos = start + (lax.broadcasted_iota(jnp.int32, (2 * Q_BLOCK, band), 0) & (Q_BLOCK - 1))
        kpos = b0 + lax.broadcasted_iota(jnp.int32, (2 * Q_BLOCK, band), 1)
        in_band = jnp.abs(kpos - qpos) <= SW_WINDOW
        for c in range(n_blk):
            hh = c // (hpk // 2)
            qs = stacked_q(qrow, Q_BLOCK, c)
            sl_ref[c] = jnp.where(in_band, _dot_nt(qs, kd_ref[hh, pl.ds(brow, band), :]), NEG_INF)
            sc_ref[c] = _dot_nt(qs, kd_ref[hh, 0:ctx_len, :])

    def finish(n, sl_ref, sc_ref):
        _, _, qrow, brow = rows_of(n)
        for c in range(n_blk):
            hh = c // (hpk // 2)
            o = softmax_pv(Q_BLOCK, c, [(sc_ref[c], vd_ref[hh, 0:ctx_len, :]),
                                        (sl_ref[c], vd_ref[hh, pl.ds(brow, band), :])])
            write_out(qrow, Q_BLOCK, c, o)

    scores(0, sl_a, sc_a)

    def pair_body(t, carry):
        n = 2 * t
        scores(n + 1, sl_b, sc_b)
        finish(n, sl_a, sc_a)
        scores(jnp.minimum(n + 2, n_steps - 1), sl_a, sc_a)
        finish(n + 1, sl_b, sc_b)
        return carry

    lax.fori_loop(0, n_steps // 2, pair_body, 0)


def _sw_attention(qkv, sinks, ctx_len, need_ctx):
    b, r, _ = qkv.shape
    seq = r - ctx_len
    assert seq % (2 * Q_BLOCK) == 0
    nq_blk = SW_Q_HEADS * HEAD_DIM // LANES
    slab = 2 * (SW_Q_HEADS // SW_KV_HEADS) * HEAD_DIM
    kern = functools.partial(_sw_kernel, ctx_len=ctx_len, seq=seq, need_ctx=need_ctx)
    n_kv_blk = SW_KV_HEADS * HEAD_DIM // LANES
    return pl.pallas_call(
        kern,
        grid=(n_kv_blk, b),
        in_specs=[
            pl.BlockSpec(memory_space=pltpu.SMEM),
            pl.BlockSpec((None, r, slab), lambda kp, bi: (bi, 0, kp)),
            pl.BlockSpec((None, r, LANES), lambda kp, bi: (bi, 0, nq_blk + kp)),
            pl.BlockSpec((None, r, LANES), lambda kp, bi: (bi, 0, nq_blk + n_kv_blk + kp)),
        ],
        out_specs=pl.BlockSpec((None, r, slab), lambda kp, bi: (bi, 0, kp)),
        out_shape=jax.ShapeDtypeStruct((b, r, D_MODEL), BF16),
        scratch_shapes=[pltpu.VMEM((2, r, LANES), BF16), pltpu.VMEM((2, r, 2 * LANES), BF16)]
        + [pltpu.VMEM((slab // LANES, 2 * Q_BLOCK, Q_BLOCK + 2 * SW_WINDOW), F32),
           pltpu.VMEM((slab // LANES, 2 * Q_BLOCK, ctx_len), F32)] * 2,
        compiler_params=_params(("arbitrary", "arbitrary")),
        name="sw_attn",
    )(sinks.astype(F32), qkv, qkv, qkv)


def _post_kernel(o_ref, x_ref, wo_ref, mod_ref, g_ref, wrh_ref, wrl_ref, br_ref, xn_ref, h_ref, rt_ref, *, ctx_tiles):
    d = x_ref.shape[-1]
    tm = x_ref.shape[0]
    row = _mod_row(pl.program_id(0), pl.program_id(1), ctx_tiles)
    gate1 = mod_ref[pl.ds(row, 1), 2 * d:3 * d]
    shift2 = mod_ref[pl.ds(row, 1), 3 * d:4 * d]
    scale2 = mod_ref[pl.ds(row, 1), 4 * d:5 * d]
    xn = x_ref[...] + gate1 * _dot(o_ref[...], wo_ref[...])
    xn_ref[...] = xn
    h = (_rms(xn) * g_ref[...]) * (1.0 + scale2) + shift2
    for j in range(SUBLANES):
        h_ref[pl.ds(j, tm, stride=SUBLANES), :] = h[:, j * LANES:(j + 1) * LANES]

    h_hi = h.astype(BF16)
    h_lo = (h - h_hi.astype(F32)).astype(BF16)
    logits = (_dot(h_hi, wrh_ref[...]) + _dot(h_lo, wrh_ref[...]) + _dot(h_hi, wrl_ref[...])) + br_ref[...]
    lane = lax.broadcasted_iota(jnp.int32, (tm, LANES), 1).astype(F32)

    def first_argmax(vals, mx):
        return jnp.min(jnp.where(vals == mx, lane, float(LANES)), axis=-1, keepdims=True)

    gl = jnp.where(lane < N_GROUPS, logits, -jnp.inf)
    ge = jnp.exp(gl - jnp.max(gl, axis=-1, keepdims=True))
    gp = ge / jnp.sum(ge, axis=-1, keepdims=True)
    g_p = jnp.max(gp, axis=-1, keepdims=True)
    g_idx = first_argmax(gp, g_p)
    e_lo = N_GROUPS + EXPERTS_PER_GROUP * g_idx
    el = jnp.where((lane >= e_lo) & (lane < e_lo + EXPERTS_PER_GROUP), logits, -jnp.inf)
    v1 = jnp.max(el, axis=-1, keepdims=True)
    i1 = first_argmax(el, v1)
    el2 = jnp.where(lane == i1, -jnp.inf, el)
    v2 = jnp.max(el2, axis=-1, keepdims=True)
    i2 = first_argmax(el2, v2)
    e2 = jnp.exp(v2 - v1)
    w1 = g_p * (1.0 / (1.0 + e2))
    w2 = g_p * (e2 / (1.0 + e2))
    lane8 = lax.broadcasted_iota(jnp.int32, (tm, SUBLANES), 1)
    rt = jnp.where(lane8 == 0, i1 - N_GROUPS,
                   jnp.where(lane8 == 1, i2 - N_GROUPS,
                             jnp.where(lane8 == 2, w1, jnp.where(lane8 == 3, w2, 0.0))))
    rt_ref[...] = rt


def _post(o, x, wo, mod, g, wr, br, ctx_len):
    wr_hi = wr.astype(BF16)
    wr_lo = (wr - wr_hi.astype(F32)).astype(BF16)
    b, r, d = x.shape
    nt = r // ROW_TILE
    kern = functools.partial(_post_kernel, ctx_tiles=ctx_len // ROW_TILE)
    return pl.pallas_call(
        kern,
        grid=(b, nt),
        in_specs=[
            pl.BlockSpec((None, ROW_TILE, d), lambda bi, i: (bi, i, 0)),
            pl.BlockSpec((None, ROW_TILE, d), lambda bi, i: (bi, i, 0)),
            pl.BlockSpec((d, d), lambda bi, i: (0, 0)),
            pl.BlockSpec((MOD_ROWS, 6 * d), lambda bi, i: (0, 0)),
            pl.BlockSpec((1, d), lambda bi, i: (0, 0)),
            pl.BlockSpec((d, LANES), lambda bi, i: (0, 0)),
            pl.BlockSpec((d, LANES), lambda bi, i: (0, 0)),
            pl.BlockSpec((1, LANES), lambda bi, i: (0, 0)),
        ],
        out_specs=[
            pl.BlockSpec((None, ROW_TILE, d), lambda bi, i: (bi, i, 0)),
            pl.BlockSpec((ROW_TILE * SUBLANES, LANES), lambda bi, i: (bi * nt + i, 0)),
            pl.BlockSpec((ROW_TILE, SUBLANES), lambda bi, i: (bi * nt + i, 0)),
        ],
        out_shape=[
            jax.ShapeDtypeStruct((b, r, d), F32),
            jax.ShapeDtypeStruct((b * r * SUBLANES, LANES), F32),
            jax.ShapeDtypeStruct((b * r, SUBLANES), F32),
        ],
        compiler_params=_params(("arbitrary", "arbitrary")),
        name="post_attn",
    )(o, x, wo, mod, g, wr_hi, wr_lo, br)


DISPATCH_TOKENS = 512


def _row(ref, idx):
    return ref.at[pl.ds(pl.multiple_of(idx * SUBLANES, SUBLANES), SUBLANES), :]


def _dispatch_kernel(dest_ref, h_ref, xs_hbm, sem):
    def issue(t, carry):
        src = _row(h_ref, t)
        for k in range(2):
            pltpu.make_async_copy(src, _row(xs_hbm, dest_ref[0, 0, 2 * t + k]), sem).start(priority=k)
        return carry

    lax.fori_loop(0, DISPATCH_TOKENS, issue, 0)
    for _ in range(2):
        pltpu.make_async_copy(h_ref, xs_hbm.at[pl.ds(0, DISPATCH_TOKENS * SUBLANES), :], sem).wait()


def _dispatch(h_rows, dest):
    t = dest.shape[0] // 2
    steps = t // DISPATCH_TOKENS
    return pl.pallas_call(
        _dispatch_kernel,
        grid=(steps,),
        in_specs=[
            pl.BlockSpec((1, 1, 2 * DISPATCH_TOKENS), lambda i: (i, 0, 0), memory_space=pltpu.SMEM),
            pl.BlockSpec((DISPATCH_TOKENS * SUBLANES, LANES), lambda i: (i, 0)),
        ],
        out_specs=pl.BlockSpec(memory_space=pl.ANY),
        out_shape=jax.ShapeDtypeStruct((2 * t * SUBLANES, LANES), F32),
        scratch_shapes=[pltpu.SemaphoreType.DMA(())],
        compiler_params=_params(("arbitrary",)),
        name="moe_dispatch",
    )(dest.reshape(steps, 1, 2 * DISPATCH_TOKENS), h_rows)


def _experts_kernel(ve_ref, vt_ref, lo_ref, hi_ref, first_ref, valid_ref,
                    xs_ref, wg_ref, wu_ref, wd_ref, ys_ref, wgu_s, wd_s, acc_s):
    v = pl.program_id(0)
    tm = ROW_TILE
    f = wg_ref.shape[-1]

    @pl.when(v == 0)
    def _():
        acc_s[...] = jnp.zeros(acc_s.shape, acc_s.dtype)

    @pl.when(valid_ref[v] == 1)
    def _():
        changed = jnp.logical_or(v == 0, ve_ref[v] != ve_ref[jnp.maximum(v - 1, 0)])

        @pl.when(changed)
        def _():
            wgu_s[:, 0:f] = wg_ref[...].astype(BF16)
            wgu_s[:, f:2 * f] = wu_ref[...].astype(BF16)
            wd_s[...] = wd_ref[...].astype(BF16)

        x = jnp.concatenate([xs_ref[pl.ds(j, tm, stride=SUBLANES), :] for j in range(SUBLANES)], axis=-1)
        au = _dot(x.astype(BF16), wgu_s[...])
        rows = vt_ref[v] * tm + lax.broadcasted_iota(jnp.int32, (tm, 1), 0)
        mine = (rows >= lo_ref[v]) & (rows < hi_ref[v])
        mid = jnp.where(mine, jax.nn.silu(au[:, 0:f]) * au[:, f:2 * f], 0.0)
        y = _dot(mid.astype(BF16), wd_s[...])
        y = jnp.where(first_ref[v] == 1, y, acc_s[...] + y)
        acc_s[...] = y
        for j in range(SUBLANES):
            ys_ref[pl.ds(j, tm, stride=SUBLANES), :] = y[:, j * LANES:(j + 1) * LANES]


def _experts(xs, sched, w_gate, w_up, w_down, layer):
    rows = xs.shape[0] // SUBLANES
    d, f = w_gate.shape[-2:]
    n_visits = sched[0].shape[0]
    blk = (ROW_TILE * SUBLANES, LANES)
    grid_spec = pltpu.PrefetchScalarGridSpec(
        num_scalar_prefetch=6,
        grid=(n_visits,),
        in_specs=[
            pl.BlockSpec(blk, lambda v, ve, vt, lo, hi, fi, va: (vt[v], 0)),
            pl.BlockSpec((None, None, d, f), lambda v, ve, vt, lo, hi, fi, va: (layer, ve[v], 0, 0)),
            pl.BlockSpec((None, None, d, f), lambda v, ve, vt, lo, hi, fi, va: (layer, ve[v], 0, 0)),
            pl.BlockSpec((None, None, f, d), lambda v, ve, vt, lo, hi, fi, va: (layer, ve[v], 0, 0)),
        ],
        out_specs=pl.BlockSpec(blk, lambda v, ve, vt, lo, hi, fi, va: (vt[v], 0)),
        scratch_shapes=[pltpu.VMEM((d, 2 * f), BF16), pltpu.VMEM((f, d), BF16), pltpu.VMEM((ROW_TILE, d), F32)],
    )
    return pl.pallas_call(
        _experts_kernel,
        grid_spec=grid_spec,
        out_shape=jax.ShapeDtypeStruct((rows * SUBLANES, LANES), F32),
        compiler_params=_params(("arbitrary",)),
        name="moe_experts",
    )(*sched, xs, w_gate, w_up, w_down)


COMBINE_UNROLL = 4


def _combine_kernel(dest_ref, next_ref, rt_ref, x_ref, mod_ref, g_ref, *rest, ctx_tiles, tiles_per_batch, final, next_qkv):
    if final:
        ys_hbm, o_ref, buf, sem = rest
    else:
        qg_ref, qmod_ref, qw_ref, cos_ref, sin_ref, ys_hbm, o_ref, qkv_ref, buf, sem = rest
    tm = ROW_TILE
    d = x_ref.shape[-1]
    s = pl.program_id(0)
    slot = s % 2
    half_rows = tm * SUBLANES

    def gather(idx_ref, slot_):
        def issue(t4, carry):
            for u in range(COMBINE_UNROLL):
                t = t4 * COMBINE_UNROLL + u
                for k in range(2):
                    pltpu.make_async_copy(_row(ys_hbm, idx_ref[0, 0, 2 * t + k]),
                                          _row(buf.at[slot_], k * tm + t), sem.at[slot_]).start(priority=k)
            return carry

        lax.fori_loop(0, tm // COMBINE_UNROLL, issue, 0)

    @pl.when(s == 0)
    def _():
        gather(dest_ref, 0)

    for parity in range(2):
        @pl.when(jnp.logical_and(s + 1 < pl.num_programs(0), slot == parity))
        def _():
            gather(next_ref, 1 - parity)

    for k in range(2):
        part = pl.ds(k * half_rows, half_rows)
        pltpu.make_async_copy(ys_hbm.at[part, :], buf.at[slot].at[part, :], sem.at[slot]).wait()

    row = _mod_row(s // tiles_per_batch, s % tiles_per_batch, ctx_tiles)
    gate2 = mod_ref[pl.ds(row, 1), 5 * d:6 * d]
    rt = rt_ref[...]
    w = (rt[:, 2:3], rt[:, 3:4])
    rows = buf.at[slot]
    pieces = []
    for j in range(SUBLANES):
        y = (w[0] * rows[pl.ds(j, tm, stride=SUBLANES), :]
             + w[1] * rows[pl.ds(half_rows + j, tm, stride=SUBLANES), :])
        sl = slice(j * LANES, (j + 1) * LANES)
        pieces.append(x_ref[:, sl] + gate2[:, sl] * y)
    xn = jnp.concatenate(pieces, axis=-1)
    if final:
        o_ref[...] = _rms(xn) * g_ref[...]
    else:
        o_ref[...] = xn
        _qkv_tile(xn, row, qg_ref, qmod_ref, qw_ref, cos_ref, sin_ref, qkv_ref, *next_qkv)


def _combine(ys, dest, rt, x, mod, ctx_len, final_g=None, next_layer=None):
    b, r, d = x.shape
    t = b * r
    steps = t // ROW_TILE
    ct, tpb = ctx_len // ROW_TILE, r // ROW_TILE
    final = final_g is not None
    const = lambda i: (0, 0)
    if final:
        out_rows = t - b * ctx_len
        out_specs = pl.BlockSpec((ROW_TILE, d), lambda i: ((i // tpb) * (tpb - ct) + jnp.maximum(i % tpb - ct, 0), 0))
        out_shape = jax.ShapeDtypeStruct((out_rows, d), F32)
        extra_in, extra_specs, next_qkv = [], [], None
    else:
        out_rows = t
        final_g = jnp.zeros((1, d), F32)
        qg, qmod, qw, cos, sin, rope_cols, q_cols = next_layer
        n = qw.shape[1]
        next_qkv = (rope_cols, q_cols)
        extra_in = [qg, qmod, qw, cos, sin]
        extra_specs = [pl.BlockSpec((1, d), const), pl.BlockSpec((MOD_ROWS, 6 * d), const), pl.BlockSpec((d, n), const),
                       pl.BlockSpec((ROW_TILE, 2 * LANES), lambda i: (i % tpb, 0)),
                       pl.BlockSpec((ROW_TILE, 2 * LANES), lambda i: (i % tpb, 0))]
        out_specs = [pl.BlockSpec((ROW_TILE, d), lambda i: (i, 0)), pl.BlockSpec((ROW_TILE, n), lambda i: (i, 0))]
        out_shape = [jax.ShapeDtypeStruct((t, d), F32), jax.ShapeDtypeStruct((t, n), BF16)]
    kern = functools.partial(_combine_kernel, ctx_tiles=ct, tiles_per_batch=tpb, final=final, next_qkv=next_qkv)
    dest3 = dest.reshape(steps, 1, 2 * ROW_TILE)
    out = pl.pallas_call(
        kern,
        grid=(steps,),
        in_specs=[
            pl.BlockSpec((1, 1, 2 * ROW_TILE), lambda i: (i, 0, 0), memory_space=pltpu.SMEM),
            pl.BlockSpec((1, 1, 2 * ROW_TILE), lambda i: (jnp.minimum(i + 1, steps - 1), 0, 0), memory_space=pltpu.SMEM),
            pl.BlockSpec((ROW_TILE, SUBLANES), lambda i: (i, 0)),
            pl.BlockSpec((ROW_TILE, d), lambda i: (i, 0)),
            pl.BlockSpec((MOD_ROWS, 6 * d), const),
            pl.BlockSpec((1, d), const),
        ] + extra_specs + [pl.BlockSpec(memory_space=pl.ANY)],
        out_specs=out_specs,
        out_shape=out_shape,
        scratch_shapes=[pltpu.VMEM((2, 2 * ROW_TILE * SUBLANES, LANES), F32), pltpu.SemaphoreType.DMA((2,))],
        compiler_params=_params(("arbitrary",)),
        name="moe_combine",
    )(dest3, dest3, rt, x.reshape(t, d), mod, final_g, *extra_in, ys)
    if final:
        return out.reshape(b, out_rows // b, d), None
    return out[0].reshape(b, r, d), out[1].reshape(b, r, -1)


def _route_plan(rt, n_tiles):
    eid = rt[:, 0:2].astype(jnp.int32).reshape(-1)
    onehot = (eid[:, None] == jnp.arange(N_EXPERTS, dtype=jnp.int32)[None, :]).astype(jnp.int32)
    csum = jnp.cumsum(onehot, axis=0)
    counts = csum[-1]
    cstart = jnp.cumsum(counts) - counts
    dest = jnp.sum(onehot * (cstart[None, :] + csum - 1), axis=1).astype(jnp.int32)

    tm = ROW_TILE
    first_tile = cstart // tm
    last_tile = (cstart + counts - 1) // tm
    nvis = jnp.where(counts > 0, last_tile - first_tile + 1, 0)
    vend = jnp.cumsum(nvis)
    vstart = vend - nvis
    total = vend[-1]
    n_visits = n_tiles + N_EXPERTS - 1
    v = jnp.arange(n_visits, dtype=jnp.int32)
    valid = v < total
    vc = jnp.minimum(v, total - 1)
    ve = jnp.sum((vc[:, None] >= vend[None, :]).astype(jnp.int32), axis=1)
    oh_e = (ve[:, None] == jnp.arange(N_EXPERTS, dtype=jnp.int32)[None, :]).astype(jnp.int32)
    pick = lambda a: jnp.sum(oh_e * a[None, :], axis=1)
    vt = pick(first_tile) + vc - pick(vstart)
    lo = pick(cstart)
    hi = jnp.where(valid, lo + pick(counts), lo)
    first = jnp.concatenate([jnp.ones((1,), jnp.int32), (vt[1:] != vt[:-1]).astype(jnp.int32)])
    sched = tuple(a.astype(jnp.int32) for a in (ve, vt, lo, hi, first, valid))
    return dest, sched


def kernel(x, c, ctx, c_ctx, ada_w, ada_b, norm_mix_g, norm_ffn_g, final_norm_g, na_w_qkv, na_rpb, na_w_o, da_w_qkv, da_lambda, da_subln_g, da_w_o, sw_w_qkv, sw_sinks, sw_w_o, moe_w_grp, moe_b_grp, moe_w_exp, moe_b_exp, moe_w_gate, moe_w_up, moe_w_down):
    b, seq, d = x.shape
    ctx_len = ctx.shape[1]
    depth = ada_w.shape[0]
    assert d == D_MODEL and ctx_len % ROW_TILE == 0 and seq % ROW_TILE == 0 and b <= MOD_ROWS // 2
    assert (b * (ctx_len + seq)) % DISPATCH_TOKENS == 0

    half = MOD_ROWS // 2
    cond = jnp.concatenate([jnp.pad(c, ((0, half - b), (0, 0))), jnp.pad(c_ctx[None, :], ((0, half - 1), (0, 0)))], axis=0)
    mod_all = _adaln(cond, ada_w, ada_b)
    cos, sin = _rope_tables(seq, ctx_len)
    stream = jnp.concatenate([ctx, x], axis=1)
    n_tiles = 2 * b * (ctx_len + seq) // ROW_TILE

    def qkv_args(i):
        kind, j = i % N_MIXERS, i // N_MIXERS
        if kind == 0:
            w, rope_cols, q_cols = na_w_qkv[j], 0, NA_HEADS * HEAD_DIM
        elif kind == 1:
            nq = 2 * DA_HEADS * HEAD_DIM
            w, rope_cols, q_cols = da_w_qkv[j], 2 * nq, nq
        else:
            w, rope_cols, q_cols = sw_w_qkv[j], (SW_Q_HEADS + SW_KV_HEADS) * HEAD_DIM, SW_Q_HEADS * HEAD_DIM
        return norm_mix_g[i].reshape(1, d), mod_all[i], w.astype(BF16), cos, sin, rope_cols, q_cols

    qkv = _qkv(stream, *qkv_args(0), ctx_len)
    for i in range(depth):
        need_ctx = i < depth - 1
        kind, j = i % N_MIXERS, i // N_MIXERS
        mod = mod_all[i]
        if kind == 0:
            o = _na_attention(qkv, na_rpb[j], ctx_len, need_ctx)
            wo = na_w_o[j]
        elif kind == 1:
            o = _da_attention(qkv, da_lambda[j], da_subln_g[j], i, ctx_len)
            wo = da_w_o[j]
        else:
            o = _sw_attention(qkv, sw_sinks[j], ctx_len, need_ctx)
            wo = sw_w_o[j]

        pad = LANES - N_GROUPS - N_EXPERTS
        wr = jnp.pad(jnp.concatenate([moe_w_grp[i], moe_w_exp[i]], axis=1).astype(F32), ((0, 0), (0, pad)))
        br = jnp.pad(jnp.concatenate([moe_b_grp[i], moe_b_exp[i]]).astype(F32), (0, pad)).reshape(1, LANES)
        stream, h_rows, rt = _post(o, stream, wo.astype(BF16), mod, norm_ffn_g[i].reshape(1, d), wr, br, ctx_len)
        dest, sched = _route_plan(rt, n_tiles)
        xs = _dispatch(h_rows, dest)
        ys = _experts(xs, sched, moe_w_gate, moe_w_up, moe_w_down, i)
        if i == depth - 1:
            stream, qkv = _combine(ys, dest, rt, stream, mod, ctx_len, final_g=final_norm_g.reshape(1, d).astype(F32))
        else:
            stream, qkv = _combine(ys, dest, rt, stream, mod, ctx_len, next_layer=qkv_args(i + 1))

    return stream
```

```python
import functools
import math

import jax
import jax.numpy as jnp
from jax import lax
from jax.experimental import pallas as pl
from jax.experimental.pallas import tpu as pltpu

F32 = jnp.float32
BF16 = jnp.bfloat16

D_MODEL = 1024
HEAD_DIM = 64
GRID_W = 64
ROPE_AXIS_PAIRS = HEAD_DIM // 4
ROPE_BASE = 10000.0
NA_HEADS = D_MODEL // HEAD_DIM
NA_WIN_R = 8
NA_WIN_C = 16
NA_GROUP = 4
NA_UNION = NA_GROUP + NA_WIN_R
DA_HEADS = D_MODEL // (2 * HEAD_DIM)
SW_Q_HEADS = D_MODEL // HEAD_DIM
SW_KV_HEADS = 4
SW_WINDOW = 128
Q_BLOCK = 128
N_GROUPS = 4
EXPERTS_PER_GROUP = 8
N_EXPERTS = N_GROUPS * EXPERTS_PER_GROUP
D_EXPERT = 512
NORM_EPS = 1e-6
NEG_INF = -1e30
N_MIXERS = 3

LANES = 128
SUBLANES = 8
ROW_TILE = 256
MOD_ROWS = 16
VMEM_LIMIT = 56 * 1024 * 1024

LOG2E = math.log2(math.e)
Q_SCALE = HEAD_DIM ** -0.5 * LOG2E

_NT = (((1,), (1,)), ((), ()))


def _dot(a, b, **kw):
    return jnp.dot(a, b, preferred_element_type=F32, **kw)


def _dot_nt(a, b):
    return lax.dot_general(a, b, _NT, preferred_element_type=F32)


def _params(sem, vmem=VMEM_LIMIT):
    return pltpu.CompilerParams(dimension_semantics=sem, vmem_limit_bytes=vmem)


def _mod_row(b, tile, ctx_tiles):
    return jnp.where(tile < ctx_tiles, MOD_ROWS // 2, b)


def _rms(x):
    return x * lax.rsqrt(jnp.mean(x * x, axis=-1, keepdims=True) + NORM_EPS)


def _adaln_kernel(c_ref, w_ref, b_ref, o_ref):
    s = jax.nn.silu(c_ref[...])
    o_ref[...] = _dot(s, w_ref[...], precision=lax.Precision.HIGHEST) + b_ref[...]


def _adaln(cond, ada_w, ada_b):
    depth, d, n6 = ada_w.shape
    tn = 1536
    return pl.pallas_call(
        _adaln_kernel,
        grid=(depth, n6 // tn),
        in_specs=[
            pl.BlockSpec((MOD_ROWS, d), lambda l, n: (0, 0)),
            pl.BlockSpec((None, d, tn), lambda l, n: (l, 0, n)),
            pl.BlockSpec((None, 1, tn), lambda l, n: (l, 0, n)),
        ],
        out_specs=pl.BlockSpec((None, MOD_ROWS, tn), lambda l, n: (l, 0, n)),
        out_shape=jax.ShapeDtypeStruct((depth, MOD_ROWS, n6), F32),
        compiler_params=_params(("arbitrary", "arbitrary")),
        name="adaln",
    )(cond, ada_w, ada_b.reshape(depth, 1, n6))


def _qkv_tile(x, row, g_ref, mod_ref, w_ref, cos_ref, sin_ref, o_ref, rope_cols, q_cols):
    d = x.shape[-1]
    shift = mod_ref[pl.ds(row, 1), 0:d]
    scale = mod_ref[pl.ds(row, 1), d:2 * d]
    h = (_rms(x) * g_ref[...]) * (1.0 + scale) + shift
    acc = _dot(h.astype(BF16), w_ref[...])
    n = acc.shape[-1]
    if rope_cols:
        tm = acc.shape[0]
        first_half = (lax.broadcasted_iota(jnp.int32, (tm, LANES), 1) % (2 * ROPE_AXIS_PAIRS)) < ROPE_AXIS_PAIRS
        for c in range(rope_cols // LANES):
            tab = slice(LANES, 2 * LANES) if c * LANES < q_cols else slice(0, LANES)
            t = acc[:, c * LANES:(c + 1) * LANES]
            up = pltpu.roll(t, LANES - ROPE_AXIS_PAIRS, 1)
            dn = pltpu.roll(t, ROPE_AXIS_PAIRS, 1)
            o_ref[:, c * LANES:(c + 1) * LANES] = (
                t * cos_ref[:, tab] + jnp.where(first_half, up, dn) * sin_ref[:, tab]).astype(o_ref.dtype)
    else:
        o_ref[:, 0:q_cols] = (acc[:, 0:q_cols] * Q_SCALE).astype(o_ref.dtype)
        rope_cols = q_cols
    if rope_cols < n:
        o_ref[:, rope_cols:] = acc[:, rope_cols:].astype(o_ref.dtype)


def _qkv_kernel(x_ref, g_ref, mod_ref, w_ref, cos_ref, sin_ref, o_ref, *, rope_cols, q_cols, ctx_tiles):
    row = _mod_row(pl.program_id(0), pl.program_id(1), ctx_tiles)
    _qkv_tile(x_ref[...], row, g_ref, mod_ref, w_ref, cos_ref, sin_ref, o_ref, rope_cols, q_cols)


def _qkv(x, g, mod, w, cos, sin, rope_cols, q_cols, ctx_len):
    b, r, d = x.shape
    n = w.shape[1]
    kern = functools.partial(_qkv_kernel, rope_cols=rope_cols, q_cols=q_cols, ctx_tiles=ctx_len // ROW_TILE)
    return pl.pallas_call(
        kern,
        grid=(b, r // ROW_TILE),
        in_specs=[
            pl.BlockSpec((None, ROW_TILE, d), lambda bi, i: (bi, i, 0)),
            pl.BlockSpec((1, d), lambda bi, i: (0, 0)),
            pl.BlockSpec((MOD_ROWS, 6 * d), lambda bi, i: (0, 0)),
            pl.BlockSpec((d, n), lambda bi, i: (0, 0)),
            pl.BlockSpec((ROW_TILE, 2 * LANES), lambda bi, i: (i, 0)),
            pl.BlockSpec((ROW_TILE, 2 * LANES), lambda bi, i: (i, 0)),
        ],
        out_specs=pl.BlockSpec((None, ROW_TILE, n), lambda bi, i: (bi, i, 0)),
        out_shape=jax.ShapeDtypeStruct((b, r, n), BF16),
        compiler_params=_params(("arbitrary", "arbitrary")),
        name="qkv",
    )(x, g, mod, w, cos, sin)


def _rope_tables(seq, ctx_len):
    t = jnp.arange(seq, dtype=jnp.int32)
    row = (t // GRID_W).astype(F32)
    col = (t % GRID_W).astype(F32)
    inv = ROPE_BASE ** (-jnp.arange(ROPE_AXIS_PAIRS, dtype=F32) / ROPE_AXIS_PAIRS)
    ang_r = row[:, None] * inv[None, :]
    ang_c = col[:, None] * inv[None, :]
    ang = jnp.concatenate([ang_r, ang_r, ang_c, ang_c], axis=-1)
    sign = jnp.tile(jnp.concatenate([-jnp.ones((ROPE_AXIS_PAIRS,), F32), jnp.ones((ROPE_AXIS_PAIRS,), F32)]), 2)
    cos = jnp.concatenate([jnp.ones((ctx_len, HEAD_DIM), F32), jnp.cos(ang)], axis=0)
    sin = jnp.concatenate([jnp.zeros((ctx_len, HEAD_DIM), F32), jnp.sin(ang) * sign[None, :]], axis=0)
    cos = jnp.tile(cos, (1, LANES // HEAD_DIM))
    sin = jnp.tile(sin, (1, LANES // HEAD_DIM))
    return jnp.concatenate([cos, cos * Q_SCALE], axis=1), jnp.concatenate([sin, sin * Q_SCALE], axis=1)


def _store_with_ones(dst_ref, v):
    dst_ref[:, 0:LANES] = v
    dst_ref[:, LANES:2 * LANES] = jnp.ones(v.shape, v.dtype)


def _head_masks(rows):
    lane = lax.broadcasted_iota(jnp.int32, (rows, LANES), 1)
    lo = lane < HEAD_DIM
    return lo, jnp.logical_not(lo)


def _na_group_geometry(g, grid_rows):
    u0 = min(max(g * NA_GROUP - NA_WIN_R // 2, 0), grid_rows - NA_UNION)
    idx = []
    for a in range(NA_GROUP):
        r = g * NA_GROUP + a
        r0 = min(max(r - NA_WIN_R // 2, 0), grid_rows - NA_WIN_R)
        idx.append([u0 + y - r + NA_WIN_R - 1 if r0 <= u0 + y < r0 + NA_WIN_R else 2 * NA_WIN_R - 1
                    for y in range(NA_UNION)])
    return u0, idx


def _na_kernel(q_ref, k_ref, v_ref, bt_ref, o_ref, fb_ref, v1_ref, nb_a, cx_a, nb_b, cx_b, *, ctx_len, grid_rows, need_ctx):
    n_groups = grid_rows // NA_GROUP

    @pl.when(pl.program_id(1) == 0)
    def _():
        for j in range(2):
            for pattern, g in enumerate((0, 1, n_groups - 1)):
                _, idx = _na_group_geometry(g, grid_rows)
                for a in range(NA_GROUP):
                    for p in range(NA_UNION // 2):
                        fb_ref[j, pattern, a * GRID_W:(a + 1) * GRID_W, 2 * p * GRID_W:(2 * p + 2) * GRID_W] = (
                            jnp.concatenate([bt_ref[j, idx[a][2 * p]], bt_ref[j, idx[a][2 * p + 1]]], axis=-1))

    _store_with_ones(v1_ref, v_ref[...])
    kc = k_ref[0:ctx_len, :]
    vc = v1_ref[0:ctx_len, :]

    def normalised(r):
        return r[:, 0:LANES] / r[:, LANES:2 * LANES]

    if need_ctx:
        qc = q_ref[0:ctx_len, :]
        lo, hi = _head_masks(ctx_len)
        outs = []
        for m in (lo, hi):
            s = _dot_nt(jnp.where(m, qc, 0), kc)
            e = jnp.exp2(s - jnp.max(s, axis=-1, keepdims=True))
            outs.append(normalised(_dot(e.astype(BF16), vc)))
        o_ref[0:ctx_len, :] = jnp.where(lo, outs[0], outs[1]).astype(o_ref.dtype)
    else:
        o_ref[0:ctx_len, :] = jnp.zeros((ctx_len, LANES), o_ref.dtype)

    gq = NA_GROUP * GRID_W
    uk = NA_UNION * GRID_W
    lo, hi = _head_masks(gq)

    def rows_of(g):
        u0 = jnp.clip(g * NA_GROUP - NA_WIN_R // 2, 0, grid_rows - NA_UNION)
        return pl.multiple_of(ctx_len + g * gq, gq), pl.multiple_of(ctx_len + u0 * GRID_W, GRID_W)

    def scores(g, nb_ref, cx_ref):
        q0, w0 = rows_of(g)
        pattern = jnp.where(g == 0, 0, jnp.where(g == n_groups - 1, 2, 1))
        q_g = q_ref[pl.ds(q0, gq), :]
        k_u = k_ref[pl.ds(w0, uk), :]
        for j, m in enumerate((lo, hi)):
            qj = jnp.where(m, q_g, 0)
            nb_ref[j] = _dot_nt(qj, k_u) + fb_ref[j, pattern]
            cx_ref[j] = _dot_nt(qj, kc)

    def finish(g, nb_ref, cx_ref):
        q0, w0 = rows_of(g)
        v_u = v1_ref[pl.ds(w0, uk), :]
        outs = []
        for j in range(2):
            s_nb = nb_ref[j]
            s_cx = cx_ref[j]
            mx = jnp.maximum(jnp.max(s_nb, axis=-1, keepdims=True), jnp.max(s_cx, axis=-1, keepdims=True))
            e_nb = jnp.exp2(s_nb - mx).astype(BF16)
            e_cx = jnp.exp2(s_cx - mx).astype(BF16)
            outs.append(normalised(_dot(e_cx, vc) + _dot(e_nb, v_u)))
        o_ref[pl.ds(q0, gq), :] = jnp.where(lo, outs[0], outs[1]).astype(o_ref.dtype)

    scores(0, nb_a, cx_a)

    def pair_body(t, carry):
        g = 2 * t
        scores(g + 1, nb_b, cx_b)
        finish(g, nb_a, cx_a)
        scores(jnp.minimum(g + 2, n_groups - 1), nb_a, cx_a)
        finish(g + 1, nb_b, cx_b)
        return carry

    lax.fori_loop(0, n_groups // 2, pair_body, 0)


def _na_bias_table(rpb, grid_rows):
    n_groups = grid_rows // NA_GROUP
    assert grid_rows % (2 * NA_GROUP) == 0 and n_groups >= 3 and grid_rows >= NA_UNION
    for g in range(2, n_groups - 1):
        u1, idx1 = _na_group_geometry(1, grid_rows)
        ug, idxg = _na_group_geometry(g, grid_rows)
        assert idxg == idx1 and ug - g * NA_GROUP == u1 - NA_GROUP
    qcol = jnp.arange(GRID_W)
    kcol = jnp.arange(GRID_W)
    c0 = jnp.clip(qcol - NA_WIN_C // 2, 0, GRID_W - NA_WIN_C)
    col_ok = (kcol[None, :] >= c0[:, None]) & (kcol[None, :] < c0[:, None] + NA_WIN_C)
    rel = kcol[None, :] - qcol[:, None] + NA_WIN_C - 1
    onehot = ((rel[:, :, None] == jnp.arange(2 * NA_WIN_C - 1)[None, None, :]) & col_ok[:, :, None]).astype(F32)
    bt = jnp.einsum('hyd,qkd->hyqk', rpb.astype(F32), onehot, precision=lax.Precision.HIGHEST)
    bt = jnp.where(col_ok[None, None], bt * LOG2E, NEG_INF)
    return jnp.pad(bt, ((0, 0), (0, 1), (0, 0), (0, 0)), constant_values=NEG_INF)


def _na_attention(qkv, rpb, ctx_len, need_ctx):
    b, r, _ = qkv.shape
    grid_rows = (r - ctx_len) // GRID_W
    nblk = D_MODEL // LANES
    bt = _na_bias_table(rpb, grid_rows)
    kern = functools.partial(_na_kernel, ctx_len=ctx_len, grid_rows=grid_rows, need_ctx=need_ctx)
    return pl.pallas_call(
        kern,
        grid=(nblk, b),
        in_specs=[
            pl.BlockSpec((None, r, LANES), lambda hp, bi: (bi, 0, hp)),
            pl.BlockSpec((None, r, LANES), lambda hp, bi: (bi, 0, nblk + hp)),
            pl.BlockSpec((None, r, LANES), lambda hp, bi: (bi, 0, 2 * nblk + hp)),
            pl.BlockSpec((2, 2 * NA_WIN_R, GRID_W, GRID_W), lambda hp, bi: (hp, 0, 0, 0)),
        ],
        out_specs=pl.BlockSpec((None, r, LANES), lambda hp, bi: (bi, 0, hp)),
        out_shape=jax.ShapeDtypeStruct((b, r, D_MODEL), BF16),
        scratch_shapes=[pltpu.VMEM((2, 3, NA_GROUP * GRID_W, NA_UNION * GRID_W), F32), pltpu.VMEM((r, 2 * LANES), BF16)]
        + [pltpu.VMEM((2, NA_GROUP * GRID_W, NA_UNION * GRID_W), F32),
           pltpu.VMEM((2, NA_GROUP * GRID_W, ctx_len), F32)] * 2,
        compiler_params=_params(("arbitrary", "arbitrary")),
        name="na_attn",
    )(qkv, qkv, qkv, bt)


def _da_kernel(q_ref, k_ref, v_ref, lam_ref, g_ref, o_ref, v1_ref, s_a, s_b, *, ctx_len, lam_init):
    lp = lam_ref[...]
    lam = (jnp.exp(jnp.sum(lp[0:1] * lp[1:2], axis=-1, keepdims=True))
           - jnp.exp(jnp.sum(lp[2:3] * lp[3:4], axis=-1, keepdims=True)) + lam_init)
    tq = ROW_TILE
    n_tiles = (q_ref.shape[0] - ctx_len) // tq
    lo, hi = _head_masks(tq)

    _store_with_ones(v1_ref, v_ref[...])

    def finish(scores, n_keys, row0, rows):
        ps = []
        for s in scores:
            e = jnp.exp2(s - jnp.max(s, axis=-1, keepdims=True))
            r = _dot(e.astype(BF16), v1_ref[0:n_keys, :])
            ps.append(r[:, 0:LANES] / r[:, LANES:2 * LANES])
        o = ps[0] - lam * ps[1]
        o_ref[pl.ds(row0, rows), :] = (_rms(o) * g_ref[...] * (1.0 - lam_init)).astype(o_ref.dtype)

    qc = q_ref[0:ctx_len, :]
    lo_c, hi_c = _head_masks(ctx_len)
    finish([_dot_nt(jnp.where(m, qc, 0), k_ref[0:ctx_len, :]) for m in (lo_c, hi_c)], ctx_len, 0, ctx_len)

    def row_of(t):
        return pl.multiple_of(ctx_len + t * tq, tq)

    def scores(t, s_ref):
        q = q_ref[pl.ds(row_of(t), tq), :]
        for j, m in enumerate((lo, hi)):
            s_ref[j] = _dot_nt(jnp.where(m, q, 0), k_ref[...])

    scores(0, s_a)

    def pair_body(p, carry):
        t = 2 * p
        scores(t + 1, s_b)
        finish([s_a[0], s_a[1]], k_ref.shape[0], row_of(t), tq)
        scores(jnp.minimum(t + 2, n_tiles - 1), s_a)
        finish([s_b[0], s_b[1]], k_ref.shape[0], row_of(t + 1), tq)
        return carry

    lax.fori_loop(0, n_tiles // 2, pair_body, 0)


def _da_attention(qkv, lam_params, subln_g, layer_idx, ctx_len):
    b, r, _ = qkv.shape
    assert (r - ctx_len) % (2 * ROW_TILE) == 0
    nblk = D_MODEL // LANES
    lam_init = 0.8 - 0.6 * math.exp(-0.3 * layer_idx)
    kern = functools.partial(_da_kernel, ctx_len=ctx_len, lam_init=lam_init)
    return pl.pallas_call(
        kern,
        grid=(nblk, b),
        in_specs=[
            pl.BlockSpec((None, r, LANES), lambda h, bi: (bi, 0, h)),
            pl.BlockSpec((None, r, LANES), lambda h, bi: (bi, 0, nblk + h)),
            pl.BlockSpec((None, r, LANES), lambda h, bi: (bi, 0, 2 * nblk + h)),
            pl.BlockSpec((4, HEAD_DIM), lambda h, bi: (0, 0)),
            pl.BlockSpec((1, 2 * HEAD_DIM), lambda h, bi: (0, 0)),
        ],
        out_specs=pl.BlockSpec((None, r, LANES), lambda h, bi: (bi, 0, h)),
        out_shape=jax.ShapeDtypeStruct((b, r, D_MODEL), BF16),
        scratch_shapes=[pltpu.VMEM((r, 2 * LANES), BF16), pltpu.VMEM((2, ROW_TILE, r), F32), pltpu.VMEM((2, ROW_TILE, r), F32)],
        compiler_params=_params(("arbitrary", "arbitrary")),
        name="da_attn",
    )(qkv, qkv, qkv, lam_params.astype(F32), subln_g.astype(F32).reshape(1, -1))


def _sw_kernel(sink_ref, q_ref, k_ref, v_ref, o_ref, kd_ref, vd_ref, sl_a, sc_a, sl_b, sc_b, *, ctx_len, seq, need_ctx):
    kvp = pl.program_id(0)
    band = Q_BLOCK + 2 * SW_WINDOW
    hpk = SW_Q_HEADS // SW_KV_HEADS

    k = k_ref[...]
    v = v_ref[...]
    for hh in range(2):
        sl = slice(hh * HEAD_DIM, (hh + 1) * HEAD_DIM)
        kd_ref[hh] = jnp.concatenate([k[:, sl], k[:, sl]], axis=-1)
        _store_with_ones(vd_ref.at[hh], jnp.concatenate([v[:, sl], v[:, sl]], axis=-1))

    n_blk = 2 * (hpk // 2)

    def stacked_q(row0, rows, c):
        lo, hi = _head_masks(rows)
        qv = q_ref[pl.ds(row0, rows), c * LANES:(c + 1) * LANES]
        return jnp.concatenate([jnp.where(lo, qv, 0), jnp.where(hi, qv, 0)], axis=0)

    def per_head(rows, c, col, fn):
        sinks = [sink_ref[kvp * 2 * hpk + 2 * c + j] * LOG2E for j in range(2)]
        return jnp.concatenate([fn(col[j * rows:(j + 1) * rows], sinks[j]) for j in range(2)], axis=0)

    def write_out(row0, rows, c, o):
        lo, _ = _head_masks(rows)
        o_ref[pl.ds(row0, rows), c * LANES:(c + 1) * LANES] = jnp.where(lo, o[0:rows], o[rows:2 * rows]).astype(o_ref.dtype)

    def softmax_pv(rows, c, parts):
        row_max = functools.reduce(jnp.maximum, [jnp.max(s, axis=-1, keepdims=True) for s, _ in parts])
        mx = per_head(rows, c, row_max, lambda m, sink: jnp.maximum(m, sink))
        r = sum(_dot(jnp.exp2(s - mx).astype(BF16), val) for s, val in parts)
        den = r[:, LANES:2 * LANES] + per_head(rows, c, mx, lambda m, sink: jnp.exp2(sink - m))
        return r[:, 0:LANES] / den

    if need_ctx:
        for c in range(n_blk):
            hh = c // (hpk // 2)
            s = _dot_nt(stacked_q(0, ctx_len, c), kd_ref[hh, 0:ctx_len, :])
            write_out(0, ctx_len, c, softmax_pv(ctx_len, c, [(s, vd_ref[hh, 0:ctx_len, :])]))
    else:
        o_ref[0:ctx_len, :] = jnp.zeros((ctx_len, o_ref.shape[-1]), o_ref.dtype)

    n_steps = seq // Q_BLOCK

    def rows_of(n):
        start = n * Q_BLOCK
        b0 = jnp.clip(start - SW_WINDOW, 0, seq - band)
        return start, b0, pl.multiple_of(ctx_len + start, Q_BLOCK), pl.multiple_of(ctx_len + b0, Q_BLOCK)

    def scores(n, sl_ref, sc_ref):
        start, b0, qrow, brow = rows_of(n)
        qpos = start + (lax.broadcasted_iota(jnp.int32, (2 * Q_BLOCK, band), 0) & (Q_BLOCK - 1))
        kpos = b0 + lax.broadcasted_iota(jnp.int32, (2 * Q_BLOCK, band), 1)
        in_band = jnp.abs(kpos - qpos) <= SW_WINDOW
        for c in range(n_blk):
            hh = c // (hpk // 2)
            qs = stacked_q(qrow, Q_BLOCK, c)
            sl_ref[c] = jnp.where(in_band, _dot_nt(qs, kd_ref[hh, pl.ds(brow, band), :]), NEG_INF)
            sc_ref[c] = _dot_nt(qs, kd_ref[hh, 0:ctx_len, :])

    def finish(n, sl_ref, sc_ref):
        _, _, qrow, brow = rows_of(n)
        for c in range(n_blk):
            hh = c // (hpk // 2)
            o = softmax_pv(Q_BLOCK, c, [(sc_ref[c], vd_ref[hh, 0:ctx_len, :]),
                                        (sl_ref[c], vd_ref[hh, pl.ds(brow, band), :])])
            write_out(qrow, Q_BLOCK, c, o)

    scores(0, sl_a, sc_a)

    def pair_body(t, carry):
        n = 2 * t
        scores(n + 1, sl_b, sc_b)
        finish(n, sl_a, sc_a)
        scores(jnp.minimum(n + 2, n_steps - 1), sl_a, sc_a)
        finish(n + 1, sl_b, sc_b)
        return carry

    lax.fori_loop(0, n_steps // 2, pair_body, 0)


def _sw_attention(qkv, sinks, ctx_len, need_ctx):
    b, r, _ = qkv.shape
    seq = r - ctx_len
    assert seq % (2 * Q_BLOCK) == 0
    nq_blk = SW_Q_HEADS * HEAD_DIM // LANES
    slab = 2 * (SW_Q_HEADS // SW_KV_HEADS) * HEAD_DIM
    kern = functools.partial(_sw_kernel, ctx_len=ctx_len, seq=seq, need_ctx=need_ctx)
    n_kv_blk = SW_KV_HEADS * HEAD_DIM // LANES
    return pl.pallas_call(
        kern,
        grid=(n_kv_blk, b),
        in_specs=[
            pl.BlockSpec(memory_space=pltpu.SMEM),
            pl.BlockSpec((None, r, slab), lambda kp, bi: (bi, 0, kp)),
            pl.BlockSpec((None, r, LANES), lambda kp, bi: (bi, 0, nq_blk + kp)),
            pl.BlockSpec((None, r, LANES), lambda kp, bi: (bi, 0, nq_blk + n_kv_blk + kp)),
        ],
        out_specs=pl.BlockSpec((None, r, slab), lambda kp, bi: (bi, 0, kp)),
        out_shape=jax.ShapeDtypeStruct((b, r, D_MODEL), BF16),
        scratch_shapes=[pltpu.VMEM((2, r, LANES), BF16), pltpu.VMEM((2, r, 2 * LANES), BF16)]
        + [pltpu.VMEM((slab // LANES, 2 * Q_BLOCK, Q_BLOCK + 2 * SW_WINDOW), F32),
           pltpu.VMEM((slab // LANES, 2 * Q_BLOCK, ctx_len), F32)] * 2,
        compiler_params=_params(("arbitrary", "arbitrary")),
        name="sw_attn",
    )(sinks.astype(F32), qkv, qkv, qkv)


def _post_kernel(o_ref, x_ref, wo_ref, mod_ref, g_ref, wr_ref, br_ref, xn_ref, h_ref, rt_ref, *, ctx_tiles):
    d = x_ref.shape[-1]
    tm = x_ref.shape[0]
    row = _mod_row(pl.program_id(0), pl.program_id(1), ctx_tiles)
    gate1 = mod_ref[pl.ds(row, 1), 2 * d:3 * d]
    shift2 = mod_ref[pl.ds(row, 1), 3 * d:4 * d]
    scale2 = mod_ref[pl.ds(row, 1), 4 * d:5 * d]
    xn = x_ref[...] + gate1 * _dot(o_ref[...], wo_ref[...])
    xn_ref[...] = xn
    h = (_rms(xn) * g_ref[...]) * (1.0 + scale2) + shift2
    for j in range(SUBLANES):
        h_ref[pl.ds(j, tm, stride=SUBLANES), :] = h[:, j * LANES:(j + 1) * LANES]

    h_hi = h.astype(BF16)
    h_lo = (h - h_hi.astype(F32)).astype(BF16)
    hi_terms = _dot(h_hi, wr_ref[...])
    logits = (hi_terms[:, 0:LANES] + hi_terms[:, LANES:2 * LANES] + _dot(h_lo, wr_ref[:, 0:LANES])) + br_ref[...]

    lt = logits.T
    sub = lax.broadcasted_iota(jnp.int32, (SUBLANES, tm), 0).astype(F32)

    def first_argmax(vals, mx):
        return jnp.min(jnp.where(vals == mx, sub, float(SUBLANES)), axis=0, keepdims=True)

    gl = jnp.where(sub < N_GROUPS, lt[0:SUBLANES, :], -jnp.inf)
    ge = jnp.exp(gl - jnp.max(gl, axis=0, keepdims=True))
    gp = ge / jnp.sum(ge, axis=0, keepdims=True)
    g_p = jnp.max(gp, axis=0, keepdims=True)
    g_idx = first_argmax(gp, g_p)
    el = lt[SUBLANES:2 * SUBLANES, :]
    for g in range(1, N_GROUPS):
        el = jnp.where(g_idx == g, lt[(g + 1) * SUBLANES:(g + 2) * SUBLANES, :], el)
    v1 = jnp.max(el, axis=0, keepdims=True)
    i1 = first_argmax(el, v1)
    el2 = jnp.where(sub == i1, -jnp.inf, el)
    v2 = jnp.max(el2, axis=0, keepdims=True)
    i2 = first_argmax(el2, v2)
    e2 = jnp.exp(v2 - v1)
    w1 = g_p * (1.0 / (1.0 + e2))
    w2 = g_p * (e2 / (1.0 + e2))
    base = g_idx * EXPERTS_PER_GROUP
    rt_t = jnp.concatenate([base + i1, base + i2, w1, w2, jnp.zeros((LANES - 4, tm), F32)], axis=0)
    rt_ref[...] = rt_t.T[:, 0:SUBLANES]


def _post(o, x, wo, mod, g, wr, br, ctx_len):
    wr_hi = wr.astype(BF16)
    wr_cat = jnp.concatenate([wr_hi, (wr - wr_hi.astype(F32)).astype(BF16)], axis=1)
    b, r, d = x.shape
    nt = r // ROW_TILE
    kern = functools.partial(_post_kernel, ctx_tiles=ctx_len // ROW_TILE)
    return pl.pallas_call(
        kern,
        grid=(b, nt),
        in_specs=[
            pl.BlockSpec((None, ROW_TILE, d), lambda bi, i: (bi, i, 0)),
            pl.BlockSpec((None, ROW_TILE, d), lambda bi, i: (bi, i, 0)),
            pl.BlockSpec((d, d), lambda bi, i: (0, 0)),
            pl.BlockSpec((MOD_ROWS, 6 * d), lambda bi, i: (0, 0)),
            pl.BlockSpec((1, d), lambda bi, i: (0, 0)),
            pl.BlockSpec((d, 2 * LANES), lambda bi, i: (0, 0)),
            pl.BlockSpec((1, LANES), lambda bi, i: (0, 0)),
        ],
        out_specs=[
            pl.BlockSpec((None, ROW_TILE, d), lambda bi, i: (bi, i, 0)),
            pl.BlockSpec((ROW_TILE * SUBLANES, LANES), lambda bi, i: (bi * nt + i, 0)),
            pl.BlockSpec((ROW_TILE, SUBLANES), lambda bi, i: (bi * nt + i, 0)),
        ],
        out_shape=[
            jax.ShapeDtypeStruct((b, r, d), F32),
            jax.ShapeDtypeStruct((b * r * SUBLANES, LANES), F32),
            jax.ShapeDtypeStruct((b * r, SUBLANES), F32),
        ],
        compiler_params=_params(("arbitrary", "arbitrary")),
        name="post_attn",
    )(o, x, wo, mod, g, wr_cat, br)


DISPATCH_TOKENS = 512


def _row(ref, idx):
    return ref.at[pl.ds(pl.multiple_of(idx * SUBLANES, SUBLANES), SUBLANES), :]


def _dispatch_kernel(dest_ref, h_ref, xs_hbm, sem):
    def issue(t, carry):
        src = _row(h_ref, t)
        for k in range(2):
            pltpu.make_async_copy(src, _row(xs_hbm, dest_ref[0, 0, 2 * t + k]), sem).start(priority=k)
        return carry

    lax.fori_loop(0, DISPATCH_TOKENS, issue, 0)
    for _ in range(2):
        pltpu.make_async_copy(h_ref, xs_hbm.at[pl.ds(0, DISPATCH_TOKENS * SUBLANES), :], sem).wait()


def _dispatch(h_rows, dest):
    t = dest.shape[0] // 2
    steps = t // DISPATCH_TOKENS
    return pl.pallas_call(
        _dispatch_kernel,
        grid=(steps,),
        in_specs=[
            pl.BlockSpec((1, 1, 2 * DISPATCH_TOKENS), lambda i: (i, 0, 0), memory_space=pltpu.SMEM),
            pl.BlockSpec((DISPATCH_TOKENS * SUBLANES, LANES), lambda i: (i, 0)),
        ],
        out_specs=pl.BlockSpec(memory_space=pl.ANY),
        out_shape=jax.ShapeDtypeStruct((2 * t * SUBLANES, LANES), F32),
        scratch_shapes=[pltpu.SemaphoreType.DMA(())],
        compiler_params=_params(("arbitrary",)),
        name="moe_dispatch",
    )(dest.reshape(steps, 1, 2 * DISPATCH_TOKENS), h_rows)


def _experts_kernel(ve_ref, vt_ref, lo_ref, hi_ref, first_ref, valid_ref, change_ref, next_ref, slot_ref,
                    xs_ref, wg_hbm, wu_hbm, wd_hbm, ys_ref, wgu_s, wd_s, acc_s, stg_g, stg_u, stg_d, sem, *, layer):
    v = pl.program_id(0)
    tm = ROW_TILE
    f = stg_g.shape[-1]
    streams = ((wg_hbm, stg_g), (wu_hbm, stg_u), (wd_hbm, stg_d))

    def fetch(expert, slot):
        for src, dst in streams:
            pltpu.make_async_copy(src.at[layer, expert], dst.at[slot], sem.at[slot]).start()

    @pl.when(v == 0)
    def _():
        acc_s[...] = jnp.zeros(acc_s.shape, acc_s.dtype)
        fetch(ve_ref[0], 0)

    @pl.when(valid_ref[v] == 1)
    def _():
        @pl.when(change_ref[v] == 1)
        def _():
            slot = slot_ref[v]
            for src, dst in streams:
                pltpu.make_async_copy(src.at[layer, 0], dst.at[slot], sem.at[slot]).wait()
            wgu_s[:, 0:f] = stg_g[slot].astype(BF16)
            wgu_s[:, f:2 * f] = stg_u[slot].astype(BF16)
            wd_s[...] = stg_d[slot].astype(BF16)

            @pl.when(next_ref[v] >= 0)
            def _():
                fetch(next_ref[v], 1 - slot)

        x = jnp.concatenate([xs_ref[pl.ds(j, tm, stride=SUBLANES), :] for j in range(SUBLANES)], axis=-1)
        au = _dot(x.astype(BF16), wgu_s[...])
        rows = vt_ref[v] * tm + lax.broadcasted_iota(jnp.int32, (tm, 1), 0)
        mine = (rows >= lo_ref[v]) & (rows < hi_ref[v])
        mid = jnp.where(mine, jax.nn.silu(au[:, 0:f]) * au[:, f:2 * f], 0.0)
        y = _dot(mid.astype(BF16), wd_s[...])
        y = jnp.where(first_ref[v] == 1, y, acc_s[...] + y)
        acc_s[...] = y
        for j in range(SUBLANES):
            ys_ref[pl.ds(j, tm, stride=SUBLANES), :] = y[:, j * LANES:(j + 1) * LANES]


def _experts(xs, sched, w_gate, w_up, w_down, layer):
    rows = xs.shape[0] // SUBLANES
    d, f = w_gate.shape[-2:]
    n_visits = sched[0].shape[0]
    blk = (ROW_TILE * SUBLANES, LANES)
    tile_map = lambda v, ve, vt, *_: (vt[v], 0)
    grid_spec = pltpu.PrefetchScalarGridSpec(
        num_scalar_prefetch=len(sched),
        grid=(n_visits,),
        in_specs=[pl.BlockSpec(blk, tile_map)] + [pl.BlockSpec(memory_space=pl.ANY)] * 3,
        out_specs=pl.BlockSpec(blk, tile_map),
        scratch_shapes=[pltpu.VMEM((d, 2 * f), BF16), pltpu.VMEM((f, d), BF16), pltpu.VMEM((ROW_TILE, d), F32),
                        pltpu.VMEM((2, d, f), F32), pltpu.VMEM((2, d, f), F32), pltpu.VMEM((2, f, d), F32),
                        pltpu.SemaphoreType.DMA((2,))],
    )
    return pl.pallas_call(
        functools.partial(_experts_kernel, layer=layer),
        grid_spec=grid_spec,
        out_shape=jax.ShapeDtypeStruct((rows * SUBLANES, LANES), F32),
        compiler_params=_params(("arbitrary",)),
        name="moe_experts",
    )(*sched, xs, w_gate, w_up, w_down)


COMBINE_UNROLL = 4


def _combine_kernel(dest_ref, next_ref, rt_ref, x_ref, mod_ref, g_ref, *rest, ctx_tiles, tiles_per_batch, final, next_qkv):
    if final:
        ys_hbm, o_ref, buf, sem = rest
    else:
        qg_ref, qmod_ref, qw_ref, cos_ref, sin_ref, ys_hbm, o_ref, qkv_ref, buf, sem = rest
    tm = ROW_TILE
    d = x_ref.shape[-1]
    s = pl.program_id(0)
    slot = s % 2
    half_rows = tm * SUBLANES

    def gather(idx_ref, slot_):
        def issue(t4, carry):
            for u in range(COMBINE_UNROLL):
                t = t4 * COMBINE_UNROLL + u
                for k in range(2):
                    pltpu.make_async_copy(_row(ys_hbm, idx_ref[0, 0, 2 * t + k]),
                                          _row(buf.at[slot_], k * tm + t), sem.at[slot_]).start(priority=k)
            return carry

        lax.fori_loop(0, tm // COMBINE_UNROLL, issue, 0)

    @pl.when(s == 0)
    def _():
        gather(dest_ref, 0)

    for parity in range(2):
        @pl.when(jnp.logical_and(s + 1 < pl.num_programs(0), slot == parity))
        def _():
            gather(next_ref, 1 - parity)

    for k in range(2):
        part = pl.ds(k * half_rows, half_rows)
        pltpu.make_async_copy(ys_hbm.at[part, :], buf.at[slot].at[part, :], sem.at[slot]).wait()

    row = _mod_row(s // tiles_per_batch, s % tiles_per_batch, ctx_tiles)
    gate2 = mod_ref[pl.ds(row, 1), 5 * d:6 * d]
    rt = rt_ref[...]
    w = (rt[:, 2:3], rt[:, 3:4])
    rows = buf.at[slot]
    pieces = []
    for j in range(SUBLANES):
        y = (w[0] * rows[pl.ds(j, tm, stride=SUBLANES), :]
             + w[1] * rows[pl.ds(half_rows + j, tm, stride=SUBLANES), :])
        sl = slice(j * LANES, (j + 1) * LANES)
        pieces.append(x_ref[:, sl] + gate2[:, sl] * y)
    xn = jnp.concatenate(pieces, axis=-1)
    if final:
        o_ref[...] = _rms(xn) * g_ref[...]
    else:
        o_ref[...] = xn
        _qkv_tile(xn, row, qg_ref, qmod_ref, qw_ref, cos_ref, sin_ref, qkv_ref, *next_qkv)


def _combine(ys, dest, rt, x, mod, ctx_len, final_g=None, next_layer=None):
    b, r, d = x.shape
    t = b * r
    steps = t // ROW_TILE
    ct, tpb = ctx_len // ROW_TILE, r // ROW_TILE
    final = final_g is not None
    const = lambda i: (0, 0)
    if final:
        out_rows = t - b * ctx_len
        out_specs = pl.BlockSpec((ROW_TILE, d), lambda i: ((i // tpb) * (tpb - ct) + jnp.maximum(i % tpb - ct, 0), 0))
        out_shape = jax.ShapeDtypeStruct((out_rows, d), F32)
        extra_in, extra_specs, next_qkv = [], [], None
    else:
        out_rows = t
        final_g = jnp.zeros((1, d), F32)
        qg, qmod, qw, cos, sin, rope_cols, q_cols = next_layer
        n = qw.shape[1]
        next_qkv = (rope_cols, q_cols)
        extra_in = [qg, qmod, qw, cos, sin]
        extra_specs = [pl.BlockSpec((1, d), const), pl.BlockSpec((MOD_ROWS, 6 * d), const), pl.BlockSpec((d, n), const),
                       pl.BlockSpec((ROW_TILE, 2 * LANES), lambda i: (i % tpb, 0)),
                       pl.BlockSpec((ROW_TILE, 2 * LANES), lambda i: (i % tpb, 0))]
        out_specs = [pl.BlockSpec((ROW_TILE, d), lambda i: (i, 0)), pl.BlockSpec((ROW_TILE, n), lambda i: (i, 0))]
        out_shape = [jax.ShapeDtypeStruct((t, d), F32), jax.ShapeDtypeStruct((t, n), BF16)]
    kern = functools.partial(_combine_kernel, ctx_tiles=ct, tiles_per_batch=tpb, final=final, next_qkv=next_qkv)
    dest3 = dest.reshape(steps, 1, 2 * ROW_TILE)
    out = pl.pallas_call(
        kern,
        grid=(steps,),
        in_specs=[
            pl.BlockSpec((1, 1, 2 * ROW_TILE), lambda i: (i, 0, 0), memory_space=pltpu.SMEM),
            pl.BlockSpec((1, 1, 2 * ROW_TILE), lambda i: (jnp.minimum(i + 1, steps - 1), 0, 0), memory_space=pltpu.SMEM),
            pl.BlockSpec((ROW_TILE, SUBLANES), lambda i: (i, 0)),
            pl.BlockSpec((ROW_TILE, d), lambda i: (i, 0)),
            pl.BlockSpec((MOD_ROWS, 6 * d), const),
            pl.BlockSpec((1, d), const),
        ] + extra_specs + [pl.BlockSpec(memory_space=pl.ANY)],
        out_specs=out_specs,
        out_shape=out_shape,
        scratch_shapes=[pltpu.VMEM((2, 2 * ROW_TILE * SUBLANES, LANES), F32), pltpu.SemaphoreType.DMA((2,))],
        compiler_params=_params(("arbitrary",)),
        name="moe_combine",
    )(dest3, dest3, rt, x.reshape(t, d), mod, final_g, *extra_in, ys)
    if final:
        return out.reshape(b, out_rows // b, d), None
    return out[0].reshape(b, r, d), out[1].reshape(b, r, -1)


def _route_plan(rt, n_tiles):
    eid = rt[:, 0:2].astype(jnp.int32).reshape(-1)
    onehot = (eid[:, None] == jnp.arange(N_EXPERTS, dtype=jnp.int32)[None, :]).astype(jnp.int32)
    csum = jnp.cumsum(onehot, axis=0)
    counts = csum[-1]
    cstart = jnp.cumsum(counts) - counts
    dest = jnp.sum(onehot * (cstart[None, :] + csum - 1), axis=1).astype(jnp.int32)

    tm = ROW_TILE
    first_tile = cstart // tm
    last_tile = (cstart + counts - 1) // tm
    nvis = jnp.where(counts > 0, last_tile - first_tile + 1, 0)
    vend = jnp.cumsum(nvis)
    vstart = vend - nvis
    total = vend[-1]
    n_visits = n_tiles + N_EXPERTS - 1
    v = jnp.arange(n_visits, dtype=jnp.int32)
    valid = v < total
    vc = jnp.minimum(v, total - 1)
    ve = jnp.sum((vc[:, None] >= vend[None, :]).astype(jnp.int32), axis=1)
    oh_e = (ve[:, None] == jnp.arange(N_EXPERTS, dtype=jnp.int32)[None, :]).astype(jnp.int32)
    pick = lambda a: jnp.sum(oh_e * a[None, :], axis=1)
    vt = pick(first_tile) + vc - pick(vstart)
    lo = pick(cstart)
    hi = jnp.where(valid, lo + pick(counts), lo)
    first = jnp.concatenate([jnp.ones((1,), jnp.int32), (vt[1:] != vt[:-1]).astype(jnp.int32)])
    change = jnp.concatenate([jnp.ones((1,), jnp.int32), (ve[1:] != ve[:-1]).astype(jnp.int32)]) * valid
    e_ids = jnp.arange(N_EXPERTS, dtype=jnp.int32)
    used = jnp.where(counts > 0, e_ids, N_EXPERTS)
    following = jnp.concatenate([lax.cummin(used, reverse=True)[1:], jnp.full((1,), N_EXPERTS, jnp.int32)])
    nxt = pick(jnp.where(following < N_EXPERTS, following, -1))
    slot = (jnp.cumsum(change) - 1) % 2
    sched = tuple(a.astype(jnp.int32) for a in (ve, vt, lo, hi, first, valid, change, nxt, slot))
    return dest, sched


def kernel(x, c, ctx, c_ctx, ada_w, ada_b, norm_mix_g, norm_ffn_g, final_norm_g, na_w_qkv, na_rpb, na_w_o, da_w_qkv, da_lambda, da_subln_g, da_w_o, sw_w_qkv, sw_sinks, sw_w_o, moe_w_grp, moe_b_grp, moe_w_exp, moe_b_exp, moe_w_gate, moe_w_up, moe_w_down):
    b, seq, d = x.shape
    ctx_len = ctx.shape[1]
    depth = ada_w.shape[0]
    assert d == D_MODEL and ctx_len % ROW_TILE == 0 and seq % ROW_TILE == 0 and b <= MOD_ROWS // 2
    assert EXPERTS_PER_GROUP == SUBLANES and N_GROUPS <= SUBLANES
    assert (b * (ctx_len + seq)) % DISPATCH_TOKENS == 0

    half = MOD_ROWS // 2
    cond = jnp.concatenate([jnp.pad(c, ((0, half - b), (0, 0))), jnp.pad(c_ctx[None, :], ((0, half - 1), (0, 0)))], axis=0)
    mod_all = _adaln(cond, ada_w, ada_b)
    cos, sin = _rope_tables(seq, ctx_len)
    stream = jnp.concatenate([ctx, x], axis=1)
    n_tiles = 2 * b * (ctx_len + seq) // ROW_TILE

    def qkv_args(i):
        kind, j = i % N_MIXERS, i // N_MIXERS
        if kind == 0:
            w, rope_cols, q_cols = na_w_qkv[j], 0, NA_HEADS * HEAD_DIM
        elif kind == 1:
            nq = 2 * DA_HEADS * HEAD_DIM
            w, rope_cols, q_cols = da_w_qkv[j], 2 * nq, nq
        else:
            w, rope_cols, q_cols = sw_w_qkv[j], (SW_Q_HEADS + SW_KV_HEADS) * HEAD_DIM, SW_Q_HEADS * HEAD_DIM
        return norm_mix_g[i].reshape(1, d), mod_all[i], w.astype(BF16), cos, sin, rope_cols, q_cols

    qkv = _qkv(stream, *qkv_args(0), ctx_len)
    for i in range(depth):
        need_ctx = i < depth - 1
        kind, j = i % N_MIXERS, i // N_MIXERS
        mod = mod_all[i]
        if kind == 0:
            o = _na_attention(qkv, na_rpb[j], ctx_len, need_ctx)
            wo = na_w_o[j]
        elif kind == 1:
            o = _da_attention(qkv, da_lambda[j], da_subln_g[j], i, ctx_len)
            wo = da_w_o[j]
        else:
            o = _sw_attention(qkv, sw_sinks[j], ctx_len, need_ctx)
            wo = sw_w_o[j]

        gpad, pad = SUBLANES - N_GROUPS, LANES - SUBLANES - N_EXPERTS
        wr = jnp.concatenate([jnp.pad(moe_w_grp[i].astype(F32), ((0, 0), (0, gpad))),
                              jnp.pad(moe_w_exp[i].astype(F32), ((0, 0), (0, pad)))], axis=1)
        br = jnp.concatenate([jnp.pad(moe_b_grp[i].astype(F32), (0, gpad)),
                              jnp.pad(moe_b_exp[i].astype(F32), (0, pad))]).reshape(1, LANES)
        stream, h_rows, rt = _post(o, stream, wo.astype(BF16), mod, norm_ffn_g[i].reshape(1, d), wr, br, ctx_len)
        dest, sched = _route_plan(rt, n_tiles)
        xs = _dispatch(h_rows, dest)
        ys = _experts(xs, sched, moe_w_gate, moe_w_up, moe_w_down, i)
        if i == depth - 1:
            stream, qkv = _combine(ys, dest, rt, stream, mod, ctx_len, final_g=final_norm_g.reshape(1, d).astype(F32))
        else:
            stream, qkv = _combine(ys, dest, rt, stream, mod, ctx_len, next_layer=qkv_args(i + 1))

    return stream
```

```python
import functools
import math

import jax
import jax.numpy as jnp
from jax import lax
from jax.experimental import pallas as pl
from jax.experimental.pallas import tpu as pltpu

F32 = jnp.float32
BF16 = jnp.bfloat16

D_MODEL = 1024
HEAD_DIM = 64
GRID_W = 64
ROPE_AXIS_PAIRS = HEAD_DIM // 4
ROPE_BASE = 10000.0
NA_HEADS = D_MODEL // HEAD_DIM
NA_WIN_R = 8
NA_WIN_C = 16
NA_GROUP = 4
NA_UNION = NA_GROUP + NA_WIN_R
DA_HEADS = D_MODEL // (2 * HEAD_DIM)
SW_Q_HEADS = D_MODEL // HEAD_DIM
SW_KV_HEADS = 4
SW_WINDOW = 128
Q_BLOCK = 128
N_GROUPS = 4
EXPERTS_PER_GROUP = 8
N_EXPERTS = N_GROUPS * EXPERTS_PER_GROUP
D_EXPERT = 512
NORM_EPS = 1e-6
NEG_INF = -1e30
N_MIXERS = 3

LANES = 128
SUBLANES = 8
ROW_TILE = 256
MOD_ROWS = 16
VMEM_LIMIT = 56 * 1024 * 1024

LOG2E = math.log2(math.e)
Q_SCALE = HEAD_DIM ** -0.5 * LOG2E

_NT = (((1,), (1,)), ((), ()))


def _dot(a, b, **kw):
    return jnp.dot(a, b, preferred_element_type=F32, **kw)


def _dot_nt(a, b):
    return lax.dot_general(a, b, _NT, preferred_element_type=F32)


def _params(sem, vmem=VMEM_LIMIT):
    return pltpu.CompilerParams(dimension_semantics=sem, vmem_limit_bytes=vmem)


def _mod_row(b, tile, ctx_tiles):
    return jnp.where(tile < ctx_tiles, MOD_ROWS // 2, b)


def _rms(x):
    return x * lax.rsqrt(jnp.mean(x * x, axis=-1, keepdims=True) + NORM_EPS)


def _adaln_kernel(c_ref, w_ref, b_ref, o_ref):
    s = jax.nn.silu(c_ref[...])
    o_ref[...] = _dot(s, w_ref[...], precision=lax.Precision.HIGHEST) + b_ref[...]


def _adaln(cond, ada_w, ada_b):
    depth, d, n6 = ada_w.shape
    tn = 1536
    return pl.pallas_call(
        _adaln_kernel,
        grid=(depth, n6 // tn),
        in_specs=[
            pl.BlockSpec((MOD_ROWS, d), lambda l, n: (0, 0)),
            pl.BlockSpec((None, d, tn), lambda l, n: (l, 0, n)),
            pl.BlockSpec((None, 1, tn), lambda l, n: (l, 0, n)),
        ],
        out_specs=pl.BlockSpec((None, MOD_ROWS, tn), lambda l, n: (l, 0, n)),
        out_shape=jax.ShapeDtypeStruct((depth, MOD_ROWS, n6), F32),
        compiler_params=_params(("arbitrary", "arbitrary")),
        name="adaln",
    )(cond, ada_w, ada_b.reshape(depth, 1, n6))


def _qkv_tile(x, row, g_ref, mod_ref, w_ref, cos_ref, sin_ref, o_ref, rope_cols, q_cols):
    d = x.shape[-1]
    shift = mod_ref[pl.ds(row, 1), 0:d]
    scale = mod_ref[pl.ds(row, 1), d:2 * d]
    h = (_rms(x) * g_ref[...]) * (1.0 + scale) + shift
    acc = _dot(h.astype(BF16), w_ref[...])
    n = acc.shape[-1]
    if rope_cols:
        tm = acc.shape[0]
        first_half = (lax.broadcasted_iota(jnp.int32, (tm, LANES), 1) % (2 * ROPE_AXIS_PAIRS)) < ROPE_AXIS_PAIRS
        for c in range(rope_cols // LANES):
            tab = slice(LANES, 2 * LANES) if c * LANES < q_cols else slice(0, LANES)
            t = acc[:, c * LANES:(c + 1) * LANES]
            up = pltpu.roll(t, LANES - ROPE_AXIS_PAIRS, 1)
            dn = pltpu.roll(t, ROPE_AXIS_PAIRS, 1)
            o_ref[:, c * LANES:(c + 1) * LANES] = (
                t * cos_ref[:, tab] + jnp.where(first_half, up, dn) * sin_ref[:, tab]).astype(o_ref.dtype)
    else:
        o_ref[:, 0:q_cols] = (acc[:, 0:q_cols] * Q_SCALE).astype(o_ref.dtype)
        rope_cols = q_cols
    if rope_cols < n:
        o_ref[:, rope_cols:] = acc[:, rope_cols:].astype(o_ref.dtype)


def _qkv_kernel(x_ref, g_ref, mod_ref, w_ref, cos_ref, sin_ref, o_ref, *, rope_cols, q_cols, ctx_tiles):
    row = _mod_row(pl.program_id(0), pl.program_id(1), ctx_tiles)
    _qkv_tile(x_ref[...], row, g_ref, mod_ref, w_ref, cos_ref, sin_ref, o_ref, rope_cols, q_cols)


def _qkv(x, g, mod, w, cos, sin, rope_cols, q_cols, ctx_len):
    b, r, d = x.shape
    n = w.shape[1]
    kern = functools.partial(_qkv_kernel, rope_cols=rope_cols, q_cols=q_cols, ctx_tiles=ctx_len // ROW_TILE)
    return pl.pallas_call(
        kern,
        grid=(b, r // ROW_TILE),
        in_specs=[
            pl.BlockSpec((None, ROW_TILE, d), lambda bi, i: (bi, i, 0)),
            pl.BlockSpec((1, d), lambda bi, i: (0, 0)),
            pl.BlockSpec((MOD_ROWS, 6 * d), lambda bi, i: (0, 0)),
            pl.BlockSpec((d, n), lambda bi, i: (0, 0)),
            pl.BlockSpec((ROW_TILE, 2 * LANES), lambda bi, i: (i, 0)),
            pl.BlockSpec((ROW_TILE, 2 * LANES), lambda bi, i: (i, 0)),
        ],
        out_specs=pl.BlockSpec((None, ROW_TILE, n), lambda bi, i: (bi, i, 0)),
        out_shape=jax.ShapeDtypeStruct((b, r, n), BF16),
        compiler_params=_params(("arbitrary", "arbitrary")),
        name="qkv",
    )(x, g, mod, w, cos, sin)


def _rope_tables(seq, ctx_len):
    t = jnp.arange(seq, dtype=jnp.int32)
    row = (t // GRID_W).astype(F32)
    col = (t % GRID_W).astype(F32)
    inv = ROPE_BASE ** (-jnp.arange(ROPE_AXIS_PAIRS, dtype=F32) / ROPE_AXIS_PAIRS)
    ang_r = row[:, None] * inv[None, :]
    ang_c = col[:, None] * inv[None, :]
    ang = jnp.concatenate([ang_r, ang_r, ang_c, ang_c], axis=-1)
    sign = jnp.tile(jnp.concatenate([-jnp.ones((ROPE_AXIS_PAIRS,), F32), jnp.ones((ROPE_AXIS_PAIRS,), F32)]), 2)
    cos = jnp.concatenate([jnp.ones((ctx_len, HEAD_DIM), F32), jnp.cos(ang)], axis=0)
    sin = jnp.concatenate([jnp.zeros((ctx_len, HEAD_DIM), F32), jnp.sin(ang) * sign[None, :]], axis=0)
    cos = jnp.tile(cos, (1, LANES // HEAD_DIM))
    sin = jnp.tile(sin, (1, LANES // HEAD_DIM))
    return jnp.concatenate([cos, cos * Q_SCALE], axis=1), jnp.concatenate([sin, sin * Q_SCALE], axis=1)


def _store_with_ones(dst_ref, v):
    dst_ref[:, 0:LANES] = v
    dst_ref[:, LANES:2 * LANES] = jnp.ones(v.shape, v.dtype)


def _head_masks(rows):
    lane = lax.broadcasted_iota(jnp.int32, (rows, LANES), 1)
    lo = lane < HEAD_DIM
    return lo, jnp.logical_not(lo)


def _na_group_geometry(g, grid_rows):
    u0 = min(max(g * NA_GROUP - NA_WIN_R // 2, 0), grid_rows - NA_UNION)
    idx = []
    for a in range(NA_GROUP):
        r = g * NA_GROUP + a
        r0 = min(max(r - NA_WIN_R // 2, 0), grid_rows - NA_WIN_R)
        idx.append([u0 + y - r + NA_WIN_R - 1 if r0 <= u0 + y < r0 + NA_WIN_R else 2 * NA_WIN_R - 1
                    for y in range(NA_UNION)])
    return u0, idx


def _na_kernel(q_ref, k_ref, v_ref, bt_ref, o_ref, fb_ref, v1_ref, nb_a, cx_a, nb_b, cx_b, *, ctx_len, grid_rows, need_ctx):
    n_groups = grid_rows // NA_GROUP

    @pl.when(pl.program_id(1) == 0)
    def _():
        for j in range(2):
            for pattern, g in enumerate((0, 1, n_groups - 1)):
                _, idx = _na_group_geometry(g, grid_rows)
                for a in range(NA_GROUP):
                    for p in range(NA_UNION // 2):
                        fb_ref[j, pattern, a * GRID_W:(a + 1) * GRID_W, 2 * p * GRID_W:(2 * p + 2) * GRID_W] = (
                            jnp.concatenate([bt_ref[j, idx[a][2 * p]], bt_ref[j, idx[a][2 * p + 1]]], axis=-1))

    _store_with_ones(v1_ref, v_ref[...])
    kc = k_ref[0:ctx_len, :]
    vc = v1_ref[0:ctx_len, :]

    def normalised(r):
        return r[:, 0:LANES] / r[:, LANES:2 * LANES]

    if need_ctx:
        qc = q_ref[0:ctx_len, :]
        lo, hi = _head_masks(ctx_len)
        outs = []
        for m in (lo, hi):
            s = _dot_nt(jnp.where(m, qc, 0), kc)
            e = jnp.exp2(s - jnp.max(s, axis=-1, keepdims=True))
            outs.append(normalised(_dot(e.astype(BF16), vc)))
        o_ref[0:ctx_len, :] = jnp.where(lo, outs[0], outs[1]).astype(o_ref.dtype)
    else:
        o_ref[0:ctx_len, :] = jnp.zeros((ctx_len, LANES), o_ref.dtype)

    gq = NA_GROUP * GRID_W
    uk = NA_UNION * GRID_W
    lo, hi = _head_masks(gq)

    def rows_of(g):
        u0 = jnp.clip(g * NA_GROUP - NA_WIN_R // 2, 0, grid_rows - NA_UNION)
        return pl.multiple_of(ctx_len + g * gq, gq), pl.multiple_of(ctx_len + u0 * GRID_W, GRID_W)

    def scores(g, nb_ref, cx_ref):
        q0, w0 = rows_of(g)
        pattern = jnp.where(g == 0, 0, jnp.where(g == n_groups - 1, 2, 1))
        q_g = q_ref[pl.ds(q0, gq), :]
        k_u = k_ref[pl.ds(w0, uk), :]
        for j, m in enumerate((lo, hi)):
            qj = jnp.where(m, q_g, 0)
            nb_ref[j] = _dot_nt(qj, k_u) + fb_ref[j, pattern]
            cx_ref[j] = _dot_nt(qj, kc)

    def finish(g, nb_ref, cx_ref):
        q0, w0 = rows_of(g)
        v_u = v1_ref[pl.ds(w0, uk), :]
        outs = []
        for j in range(2):
            s_nb = nb_ref[j]
            s_cx = cx_ref[j]
            mx = jnp.maximum(jnp.max(s_nb, axis=-1, keepdims=True), jnp.max(s_cx, axis=-1, keepdims=True))
            e_nb = jnp.exp2(s_nb - mx).astype(BF16)
            e_cx = jnp.exp2(s_cx - mx).astype(BF16)
            outs.append(normalised(_dot(e_cx, vc) + _dot(e_nb, v_u)))
        o_ref[pl.ds(q0, gq), :] = jnp.where(lo, outs[0], outs[1]).astype(o_ref.dtype)

    scores(0, nb_a, cx_a)

    def pair_body(t, carry):
        g = 2 * t
        scores(g + 1, nb_b, cx_b)
        finish(g, nb_a, cx_a)
        scores(g + 2, nb_a, cx_a)
        finish(g + 1, nb_b, cx_b)
        return carry

    lax.fori_loop(0, n_groups // 2 - 1, pair_body, 0)
    scores(n_groups - 1, nb_b, cx_b)
    finish(n_groups - 2, nb_a, cx_a)
    finish(n_groups - 1, nb_b, cx_b)


def _na_bias_table(rpb, grid_rows):
    n_groups = grid_rows // NA_GROUP
    assert grid_rows % (2 * NA_GROUP) == 0 and n_groups >= 3 and grid_rows >= NA_UNION
    for g in range(2, n_groups - 1):
        u1, idx1 = _na_group_geometry(1, grid_rows)
        ug, idxg = _na_group_geometry(g, grid_rows)
        assert idxg == idx1 and ug - g * NA_GROUP == u1 - NA_GROUP
    qcol = jnp.arange(GRID_W)
    kcol = jnp.arange(GRID_W)
    c0 = jnp.clip(qcol - NA_WIN_C // 2, 0, GRID_W - NA_WIN_C)
    col_ok = (kcol[None, :] >= c0[:, None]) & (kcol[None, :] < c0[:, None] + NA_WIN_C)
    rel = kcol[None, :] - qcol[:, None] + NA_WIN_C - 1
    onehot = ((rel[:, :, None] == jnp.arange(2 * NA_WIN_C - 1)[None, None, :]) & col_ok[:, :, None]).astype(F32)
    bt = jnp.einsum('hyd,qkd->hyqk', rpb.astype(F32), onehot, precision=lax.Precision.HIGHEST)
    bt = jnp.where(col_ok[None, None], bt * LOG2E, NEG_INF)
    return jnp.pad(bt, ((0, 0), (0, 1), (0, 0), (0, 0)), constant_values=NEG_INF)


def _na_attention(qkv, rpb, ctx_len, need_ctx):
    b, r, _ = qkv.shape
    grid_rows = (r - ctx_len) // GRID_W
    nblk = D_MODEL // LANES
    bt = _na_bias_table(rpb, grid_rows)
    kern = functools.partial(_na_kernel, ctx_len=ctx_len, grid_rows=grid_rows, need_ctx=need_ctx)
    return pl.pallas_call(
        kern,
        grid=(nblk, b),
        in_specs=[
            pl.BlockSpec((None, r, LANES), lambda hp, bi: (bi, 0, hp)),
            pl.BlockSpec((None, r, LANES), lambda hp, bi: (bi, 0, nblk + hp)),
            pl.BlockSpec((None, r, LANES), lambda hp, bi: (bi, 0, 2 * nblk + hp)),
            pl.BlockSpec((2, 2 * NA_WIN_R, GRID_W, GRID_W), lambda hp, bi: (hp, 0, 0, 0)),
        ],
        out_specs=pl.BlockSpec((None, r, LANES), lambda hp, bi: (bi, 0, hp)),
        out_shape=jax.ShapeDtypeStruct((b, r, D_MODEL), BF16),
        scratch_shapes=[pltpu.VMEM((2, 3, NA_GROUP * GRID_W, NA_UNION * GRID_W), F32), pltpu.VMEM((r, 2 * LANES), BF16)]
        + [pltpu.VMEM((2, NA_GROUP * GRID_W, NA_UNION * GRID_W), F32),
           pltpu.VMEM((2, NA_GROUP * GRID_W, ctx_len), F32)] * 2,
        compiler_params=_params(("arbitrary", "arbitrary")),
        name="na_attn",
    )(qkv, qkv, qkv, bt)


def _da_kernel(q_ref, k_ref, v_ref, lam_ref, g_ref, o_ref, v1_ref, s_a, s_b, *, ctx_len, lam_init):
    lp = lam_ref[...]
    lam = (jnp.exp(jnp.sum(lp[0:1] * lp[1:2], axis=-1, keepdims=True))
           - jnp.exp(jnp.sum(lp[2:3] * lp[3:4], axis=-1, keepdims=True)) + lam_init)
    tq = ROW_TILE
    n_tiles = (q_ref.shape[0] - ctx_len) // tq
    lo, hi = _head_masks(tq)

    _store_with_ones(v1_ref, v_ref[...])

    def finish(scores, n_keys, row0, rows):
        ps = []
        for s in scores:
            e = jnp.exp2(s - jnp.max(s, axis=-1, keepdims=True))
            r = _dot(e.astype(BF16), v1_ref[0:n_keys, :])
            ps.append(r[:, 0:LANES] / r[:, LANES:2 * LANES])
        o = ps[0] - lam * ps[1]
        o_ref[pl.ds(row0, rows), :] = (_rms(o) * g_ref[...] * (1.0 - lam_init)).astype(o_ref.dtype)

    qc = q_ref[0:ctx_len, :]
    lo_c, hi_c = _head_masks(ctx_len)
    finish([_dot_nt(jnp.where(m, qc, 0), k_ref[0:ctx_len, :]) for m in (lo_c, hi_c)], ctx_len, 0, ctx_len)

    def row_of(t):
        return pl.multiple_of(ctx_len + t * tq, tq)

    def scores(t, s_ref):
        q = q_ref[pl.ds(row_of(t), tq), :]
        for j, m in enumerate((lo, hi)):
            s_ref[j] = _dot_nt(jnp.where(m, q, 0), k_ref[...])

    scores(0, s_a)

    def pair_body(p, carry):
        t = 2 * p
        scores(t + 1, s_b)
        finish([s_a[0], s_a[1]], k_ref.shape[0], row_of(t), tq)
        scores(t + 2, s_a)
        finish([s_b[0], s_b[1]], k_ref.shape[0], row_of(t + 1), tq)
        return carry

    lax.fori_loop(0, n_tiles // 2 - 1, pair_body, 0)
    scores(n_tiles - 1, s_b)
    finish([s_a[0], s_a[1]], k_ref.shape[0], row_of(n_tiles - 2), tq)
    finish([s_b[0], s_b[1]], k_ref.shape[0], row_of(n_tiles - 1), tq)


def _da_attention(qkv, lam_params, subln_g, layer_idx, ctx_len):
    b, r, _ = qkv.shape
    assert (r - ctx_len) % (2 * ROW_TILE) == 0
    nblk = D_MODEL // LANES
    lam_init = 0.8 - 0.6 * math.exp(-0.3 * layer_idx)
    kern = functools.partial(_da_kernel, ctx_len=ctx_len, lam_init=lam_init)
    return pl.pallas_call(
        kern,
        grid=(nblk, b),
        in_specs=[
            pl.BlockSpec((None, r, LANES), lambda h, bi: (bi, 0, h)),
            pl.BlockSpec((None, r, LANES), lambda h, bi: (bi, 0, nblk + h)),
            pl.BlockSpec((None, r, LANES), lambda h, bi: (bi, 0, 2 * nblk + h)),
            pl.BlockSpec((4, HEAD_DIM), lambda h, bi: (0, 0)),
            pl.BlockSpec((1, 2 * HEAD_DIM), lambda h, bi: (0, 0)),
        ],
        out_specs=pl.BlockSpec((None, r, LANES), lambda h, bi: (bi, 0, h)),
        out_shape=jax.ShapeDtypeStruct((b, r, D_MODEL), BF16),
        scratch_shapes=[pltpu.VMEM((r, 2 * LANES), BF16), pltpu.VMEM((2, ROW_TILE, r), F32), pltpu.VMEM((2, ROW_TILE, r), F32)],
        compiler_params=_params(("arbitrary", "arbitrary")),
        name="da_attn",
    )(qkv, qkv, qkv, lam_params.astype(F32), subln_g.astype(F32).reshape(1, -1))


def _sw_kernel(sink_ref, q_ref, k_ref, v_ref, o_ref, kd_ref, vd_ref, sl_a, sc_a, sl_b, sc_b, *, ctx_len, seq, need_ctx):
    kvp = pl.program_id(0)
    band = Q_BLOCK + 2 * SW_WINDOW
    hpk = SW_Q_HEADS // SW_KV_HEADS

    k = k_ref[...]
    v = v_ref[...]
    for hh in range(2):
        sl = slice(hh * HEAD_DIM, (hh + 1) * HEAD_DIM)
        kd_ref[hh] = jnp.concatenate([k[:, sl], k[:, sl]], axis=-1)
        _store_with_ones(vd_ref.at[hh], jnp.concatenate([v[:, sl], v[:, sl]], axis=-1))

    n_blk = 2 * (hpk // 2)

    def stacked_q(row0, rows, c):
        lo, hi = _head_masks(rows)
        qv = q_ref[pl.ds(row0, rows), c * LANES:(c + 1) * LANES]
        return jnp.concatenate([jnp.where(lo, qv, 0), jnp.where(hi, qv, 0)], axis=0)

    def per_head(rows, c, col, fn):
        sinks = [sink_ref[kvp * 2 * hpk + 2 * c + j] * LOG2E for j in range(2)]
        return jnp.concatenate([fn(col[j * rows:(j + 1) * rows], sinks[j]) for j in range(2)], axis=0)

    def write_out(row0, rows, c, o):
        lo, _ = _head_masks(rows)
        o_ref[pl.ds(row0, rows), c * LANES:(c + 1) * LANES] = jnp.where(lo, o[0:rows], o[rows:2 * rows]).astype(o_ref.dtype)

    def softmax_pv(rows, c, parts):
        row_max = functools.reduce(jnp.maximum, [jnp.max(s, axis=-1, keepdims=True) for s, _ in parts])
        mx = per_head(rows, c, row_max, lambda m, sink: jnp.maximum(m, sink))
        r = sum(_dot(jnp.exp2(s - mx).astype(BF16), val) for s, val in parts)
        den = r[:, LANES:2 * LANES] + per_head(rows, c, mx, lambda m, sink: jnp.exp2(sink - m))
        return r[:, 0:LANES] / den

    if need_ctx:
        for c in range(n_blk):
            hh = c // (hpk // 2)
            s = _dot_nt(stacked_q(0, ctx_len, c), kd_ref[hh, 0:ctx_len, :])
            write_out(0, ctx_len, c, softmax_pv(ctx_len, c, [(s, vd_ref[hh, 0:ctx_len, :])]))
    else:
        o_ref[0:ctx_len, :] = jnp.zeros((ctx_len, o_ref.shape[-1]), o_ref.dtype)

    n_steps = seq // Q_BLOCK

    def rows_of(n):
        start = n * Q_BLOCK
        b0 = jnp.clip(start - SW_WINDOW, 0, seq - band)
        return start, b0, pl.multiple_of(ctx_len + start, Q_BLOCK), pl.multiple_of(ctx_len + b0, Q_BLOCK)

    def scores(n, sl_ref, sc_ref):
        start, b0, qrow, brow = rows_of(n)
        qpos = start + (lax.broadcasted_iota(jnp.int32, (2 * Q_BLOCK, band), 0) & (Q_BLOCK - 1))
        kpos = b0 + lax.broadcasted_iota(jnp.int32, (2 * Q_BLOCK, band), 1)
        in_band = jnp.abs(kpos - qpos) <= SW_WINDOW
        for c in range(n_blk):
            hh = c // (hpk // 2)
            qs = stacked_q(qrow, Q_BLOCK, c)
            sl_ref[c] = jnp.where(in_band, _dot_nt(qs, kd_ref[hh, pl.ds(brow, band), :]), NEG_INF)
            sc_ref[c] = _dot_nt(qs, kd_ref[hh, 0:ctx_len, :])

    def finish(n, sl_ref, sc_ref):
        _, _, qrow, brow = rows_of(n)
        for c in range(n_blk):
            hh = c // (hpk // 2)
            o = softmax_pv(Q_BLOCK, c, [(sc_ref[c], vd_ref[hh, 0:ctx_len, :]),
                                        (sl_ref[c], vd_ref[hh, pl.ds(brow, band), :])])
            write_out(qrow, Q_BLOCK, c, o)

    scores(0, sl_a, sc_a)

    def pair_body(t, carry):
        n = 2 * t
        scores(n + 1, sl_b, sc_b)
        finish(n, sl_a, sc_a)
        scores(n + 2, sl_a, sc_a)
        finish(n + 1, sl_b, sc_b)
        return carry

    lax.fori_loop(0, n_steps // 2 - 1, pair_body, 0)
    scores(n_steps - 1, sl_b, sc_b)
    finish(n_steps - 2, sl_a, sc_a)
    finish(n_steps - 1, sl_b, sc_b)


def _sw_attention(qkv, sinks, ctx_len, need_ctx):
    b, r, _ = qkv.shape
    seq = r - ctx_len
    assert seq % (2 * Q_BLOCK) == 0
    nq_blk = SW_Q_HEADS * HEAD_DIM // LANES
    slab = 2 * (SW_Q_HEADS // SW_KV_HEADS) * HEAD_DIM
    kern = functools.partial(_sw_kernel, ctx_len=ctx_len, seq=seq, need_ctx=need_ctx)
    n_kv_blk = SW_KV_HEADS * HEAD_DIM // LANES
    return pl.pallas_call(
        kern,
        grid=(n_kv_blk, b),
        in_specs=[
            pl.BlockSpec(memory_space=pltpu.SMEM),
            pl.BlockSpec((None, r, slab), lambda kp, bi: (bi, 0, kp)),
            pl.BlockSpec((None, r, LANES), lambda kp, bi: (bi, 0, nq_blk + kp)),
            pl.BlockSpec((None, r, LANES), lambda kp, bi: (bi, 0, nq_blk + n_kv_blk + kp)),
        ],
        out_specs=pl.BlockSpec((None, r, slab), lambda kp, bi: (bi, 0, kp)),
        out_shape=jax.ShapeDtypeStruct((b, r, D_MODEL), BF16),
        scratch_shapes=[pltpu.VMEM((2, r, LANES), BF16), pltpu.VMEM((2, r, 2 * LANES), BF16)]
        + [pltpu.VMEM((slab // LANES, 2 * Q_BLOCK, Q_BLOCK + 2 * SW_WINDOW), F32),
           pltpu.VMEM((slab // LANES, 2 * Q_BLOCK, ctx_len), F32)] * 2,
        compiler_params=_params(("arbitrary", "arbitrary")),
        name="sw_attn",
    )(sinks.astype(F32), qkv, qkv, qkv)


def _post_kernel(o_ref, x_ref, wo_ref, mod_ref, g_ref, wr_ref, br_ref, xn_ref, h_ref, rt_ref, *, ctx_tiles):
    d = x_ref.shape[-1]
    tm = x_ref.shape[0]
    row = _mod_row(pl.program_id(0), pl.program_id(1), ctx_tiles)
    gate1 = mod_ref[pl.ds(row, 1), 2 * d:3 * d]
    shift2 = mod_ref[pl.ds(row, 1), 3 * d:4 * d]
    scale2 = mod_ref[pl.ds(row, 1), 4 * d:5 * d]
    xn = x_ref[...] + gate1 * _dot(o_ref[...], wo_ref[...])
    xn_ref[...] = xn
    h = (_rms(xn) * g_ref[...]) * (1.0 + scale2) + shift2
    for j in range(SUBLANES):
        h_ref[pl.ds(j, tm, stride=SUBLANES), :] = h[:, j * LANES:(j + 1) * LANES]

    h_hi = h.astype(BF16)
    h_lo = (h - h_hi.astype(F32)).astype(BF16)
    hi_terms = _dot(h_hi, wr_ref[...])
    logits = (hi_terms[:, 0:LANES] + hi_terms[:, LANES:2 * LANES] + _dot(h_lo, wr_ref[:, 0:LANES])) + br_ref[...]

    lt = logits.T
    sub = lax.broadcasted_iota(jnp.int32, (SUBLANES, tm), 0).astype(F32)

    def first_argmax(vals, mx):
        return jnp.min(jnp.where(vals == mx, sub, float(SUBLANES)), axis=0, keepdims=True)

    gl = jnp.where(sub < N_GROUPS, lt[0:SUBLANES, :], -jnp.inf)
    ge = jnp.exp(gl - jnp.max(gl, axis=0, keepdims=True))
    gp = ge / jnp.sum(ge, axis=0, keepdims=True)
    g_p = jnp.max(gp, axis=0, keepdims=True)
    g_idx = first_argmax(gp, g_p)
    el = lt[SUBLANES:2 * SUBLANES, :]
    for g in range(1, N_GROUPS):
        el = jnp.where(g_idx == g, lt[(g + 1) * SUBLANES:(g + 2) * SUBLANES, :], el)
    v1 = jnp.max(el, axis=0, keepdims=True)
    i1 = first_argmax(el, v1)
    el2 = jnp.where(sub == i1, -jnp.inf, el)
    v2 = jnp.max(el2, axis=0, keepdims=True)
    i2 = first_argmax(el2, v2)
    e2 = jnp.exp(v2 - v1)
    w1 = g_p * (1.0 / (1.0 + e2))
    w2 = g_p * (e2 / (1.0 + e2))
    base = g_idx * EXPERTS_PER_GROUP
    rt_t = jnp.concatenate([base + i1, base + i2, w1, w2, jnp.zeros((LANES - 4, tm), F32)], axis=0)
    rt_ref[...] = rt_t.T[:, 0:SUBLANES]


def _post(o, x, wo, mod, g, wr, br, ctx_len):
    wr_hi = wr.astype(BF16)
    wr_cat = jnp.concatenate([wr_hi, (wr - wr_hi.astype(F32)).astype(BF16)], axis=1)
    b, r, d = x.shape
    nt = r // ROW_TILE
    kern = functools.partial(_post_kernel, ctx_tiles=ctx_len // ROW_TILE)
    return pl.pallas_call(
        kern,
        grid=(b, nt),
        in_specs=[
            pl.BlockSpec((None, ROW_TILE, d), lambda bi, i: (bi, i, 0)),
            pl.BlockSpec((None, ROW_TILE, d), lambda bi, i: (bi, i, 0)),
            pl.BlockSpec((d, d), lambda bi, i: (0, 0)),
            pl.BlockSpec((MOD_ROWS, 6 * d), lambda bi, i: (0, 0)),
            pl.BlockSpec((1, d), lambda bi, i: (0, 0)),
            pl.BlockSpec((d, 2 * LANES), lambda bi, i: (0, 0)),
            pl.BlockSpec((1, LANES), lambda bi, i: (0, 0)),
        ],
        out_specs=[
            pl.BlockSpec((None, ROW_TILE, d), lambda bi, i: (bi, i, 0)),
            pl.BlockSpec((ROW_TILE * SUBLANES, LANES), lambda bi, i: (bi * nt + i, 0)),
            pl.BlockSpec((ROW_TILE, SUBLANES), lambda bi, i: (bi * nt + i, 0)),
        ],
        out_shape=[
            jax.ShapeDtypeStruct((b, r, d), F32),
            jax.ShapeDtypeStruct((b * r * SUBLANES, LANES), F32),
            jax.ShapeDtypeStruct((b * r, SUBLANES), F32),
        ],
        compiler_params=_params(("arbitrary", "arbitrary")),
        name="post_attn",
    )(o, x, wo, mod, g, wr_cat, br)


DISPATCH_TOKENS = 512


def _row(ref, idx):
    return ref.at[pl.ds(pl.multiple_of(idx * SUBLANES, SUBLANES), SUBLANES), :]


def _dispatch_kernel(dest_ref, h_ref, xs_hbm, sem):
    def issue(t, carry):
        src = _row(h_ref, t)
        for k in range(2):
            pltpu.make_async_copy(src, _row(xs_hbm, dest_ref[0, 0, 2 * t + k]), sem).start(priority=k)
        return carry

    lax.fori_loop(0, DISPATCH_TOKENS, issue, 0)
    for _ in range(2):
        pltpu.make_async_copy(h_ref, xs_hbm.at[pl.ds(0, DISPATCH_TOKENS * SUBLANES), :], sem).wait()


def _dispatch(h_rows, dest):
    t = dest.shape[0] // 2
    steps = t // DISPATCH_TOKENS
    return pl.pallas_call(
        _dispatch_kernel,
        grid=(steps,),
        in_specs=[
            pl.BlockSpec((1, 1, 2 * DISPATCH_TOKENS), lambda i: (i, 0, 0), memory_space=pltpu.SMEM),
            pl.BlockSpec((DISPATCH_TOKENS * SUBLANES, LANES), lambda i: (i, 0)),
        ],
        out_specs=pl.BlockSpec(memory_space=pl.ANY),
        out_shape=jax.ShapeDtypeStruct((2 * t * SUBLANES, LANES), F32),
        scratch_shapes=[pltpu.SemaphoreType.DMA(())],
        compiler_params=_params(("arbitrary",)),
        name="moe_dispatch",
    )(dest.reshape(steps, 1, 2 * DISPATCH_TOKENS), h_rows)


def _experts_kernel(ve_ref, vt_ref, lo_ref, hi_ref, first_ref, valid_ref, change_ref, next_ref, slot_ref,
                    xs_ref, wg_hbm, wu_hbm, wd_hbm, ys_ref, wgu_s, wd_s, acc_s, stg_g, stg_u, stg_d, sem, *, layer):
    v = pl.program_id(0)
    tm = ROW_TILE
    f = stg_g.shape[-1]
    streams = ((wg_hbm, stg_g), (wu_hbm, stg_u), (wd_hbm, stg_d))

    def fetch(expert, slot):
        for src, dst in streams:
            pltpu.make_async_copy(src.at[layer, expert], dst.at[slot], sem.at[slot]).start()

    @pl.when(v == 0)
    def _():
        acc_s[...] = jnp.zeros(acc_s.shape, acc_s.dtype)
        fetch(ve_ref[0], 0)

    @pl.when(valid_ref[v] == 1)
    def _():
        @pl.when(change_ref[v] == 1)
        def _():
            slot = slot_ref[v]
            for src, dst in streams:
                pltpu.make_async_copy(src.at[layer, 0], dst.at[slot], sem.at[slot]).wait()
            wgu_s[:, 0:f] = stg_g[slot].astype(BF16)
            wgu_s[:, f:2 * f] = stg_u[slot].astype(BF16)
            wd_s[...] = stg_d[slot].astype(BF16)

            @pl.when(next_ref[v] >= 0)
            def _():
                fetch(next_ref[v], 1 - slot)

        x = jnp.concatenate([xs_ref[pl.ds(j, tm, stride=SUBLANES), :] for j in range(SUBLANES)], axis=-1)
        au = _dot(x.astype(BF16), wgu_s[...])
        rows = vt_ref[v] * tm + lax.broadcasted_iota(jnp.int32, (tm, 1), 0)
        mine = (rows >= lo_ref[v]) & (rows < hi_ref[v])
        mid = jnp.where(mine, jax.nn.silu(au[:, 0:f]) * au[:, f:2 * f], 0.0)
        y = _dot(mid.astype(BF16), wd_s[...])
        y = jnp.where(first_ref[v] == 1, y, acc_s[...] + y)
        acc_s[...] = y
        for j in range(SUBLANES):
            ys_ref[pl.ds(j, tm, stride=SUBLANES), :] = y[:, j * LANES:(j + 1) * LANES]


def _experts(xs, sched, w_gate, w_up, w_down, layer):
    rows = xs.shape[0] // SUBLANES
    d, f = w_gate.shape[-2:]
    n_visits = sched[0].shape[0]
    blk = (ROW_TILE * SUBLANES, LANES)
    tile_map = lambda v, ve, vt, *_: (vt[v], 0)
    grid_spec = pltpu.PrefetchScalarGridSpec(
        num_scalar_prefetch=len(sched),
        grid=(n_visits,),
        in_specs=[pl.BlockSpec(blk, tile_map)] + [pl.BlockSpec(memory_space=pl.ANY)] * 3,
        out_specs=pl.BlockSpec(blk, tile_map),
        scratch_shapes=[pltpu.VMEM((d, 2 * f), BF16), pltpu.VMEM((f, d), BF16), pltpu.VMEM((ROW_TILE, d), F32),
                        pltpu.VMEM((2, d, f), F32), pltpu.VMEM((2, d, f), F32), pltpu.VMEM((2, f, d), F32),
                        pltpu.SemaphoreType.DMA((2,))],
    )
    return pl.pallas_call(
        functools.partial(_experts_kernel, layer=layer),
        grid_spec=grid_spec,
        out_shape=jax.ShapeDtypeStruct((rows * SUBLANES, LANES), F32),
        compiler_params=_params(("arbitrary",)),
        name="moe_experts",
    )(*sched, xs, w_gate, w_up, w_down)


COMBINE_UNROLL = 4


def _combine_kernel(dest_ref, next_ref, rt_ref, x_ref, mod_ref, g_ref, *rest, ctx_tiles, tiles_per_batch, final, next_qkv):
    if final:
        ys_hbm, o_ref, buf, sem = rest
    else:
        qg_ref, qmod_ref, qw_ref, cos_ref, sin_ref, ys_hbm, o_ref, qkv_ref, buf, sem = rest
    tm = ROW_TILE
    d = x_ref.shape[-1]
    s = pl.program_id(0)
    slot = s % 2
    half_rows = tm * SUBLANES

    def gather(idx_ref, slot_):
        def issue(t4, carry):
            for u in range(COMBINE_UNROLL):
                t = t4 * COMBINE_UNROLL + u
                for k in range(2):
                    pltpu.make_async_copy(_row(ys_hbm, idx_ref[0, 0, 2 * t + k]),
                                          _row(buf.at[slot_], k * tm + t), sem.at[slot_]).start(priority=k)
            return carry

        lax.fori_loop(0, tm // COMBINE_UNROLL, issue, 0)

    @pl.when(s == 0)
    def _():
        gather(dest_ref, 0)

    for parity in range(2):
        @pl.when(jnp.logical_and(s + 1 < pl.num_programs(0), slot == parity))
        def _():
            gather(next_ref, 1 - parity)

    for k in range(2):
        part = pl.ds(k * half_rows, half_rows)
        pltpu.make_async_copy(ys_hbm.at[part, :], buf.at[slot].at[part, :], sem.at[slot]).wait()

    row = _mod_row(s // tiles_per_batch, s % tiles_per_batch, ctx_tiles)
    gate2 = mod_ref[pl.ds(row, 1), 5 * d:6 * d]
    rt = rt_ref[...]
    w = (rt[:, 2:3], rt[:, 3:4])
    rows = buf.at[slot]
    pieces = []
    for j in range(SUBLANES):
        y = (w[0] * rows[pl.ds(j, tm, stride=SUBLANES), :]
             + w[1] * rows[pl.ds(half_rows + j, tm, stride=SUBLANES), :])
        sl = slice(j * LANES, (j + 1) * LANES)
        pieces.append(x_ref[:, sl] + gate2[:, sl] * y)
    xn = jnp.concatenate(pieces, axis=-1)
    if final:
        o_ref[...] = _rms(xn) * g_ref[...]
    else:
        o_ref[...] = xn
        _qkv_tile(xn, row, qg_ref, qmod_ref, qw_ref, cos_ref, sin_ref, qkv_ref, *next_qkv)


def _combine(ys, dest, rt, x, mod, ctx_len, final_g=None, next_layer=None):
    b, r, d = x.shape
    t = b * r
    steps = t // ROW_TILE
    ct, tpb = ctx_len // ROW_TILE, r // ROW_TILE
    final = final_g is not None
    const = lambda i: (0, 0)
    if final:
        out_rows = t - b * ctx_len
        out_specs = pl.BlockSpec((ROW_TILE, d), lambda i: ((i // tpb) * (tpb - ct) + jnp.maximum(i % tpb - ct, 0), 0))
        out_shape = jax.ShapeDtypeStruct((out_rows, d), F32)
        extra_in, extra_specs, next_qkv = [], [], None
    else:
        out_rows = t
        final_g = jnp.zeros((1, d), F32)
        qg, qmod, qw, cos, sin, rope_cols, q_cols = next_layer
        n = qw.shape[1]
        next_qkv = (rope_cols, q_cols)
        extra_in = [qg, qmod, qw, cos, sin]
        extra_specs = [pl.BlockSpec((1, d), const), pl.BlockSpec((MOD_ROWS, 6 * d), const), pl.BlockSpec((d, n), const),
                       pl.BlockSpec((ROW_TILE, 2 * LANES), lambda i: (i % tpb, 0)),
                       pl.BlockSpec((ROW_TILE, 2 * LANES), lambda i: (i % tpb, 0))]
        out_specs = [pl.BlockSpec((ROW_TILE, d), lambda i: (i, 0)), pl.BlockSpec((ROW_TILE, n), lambda i: (i, 0))]
        out_shape = [jax.ShapeDtypeStruct((t, d), F32), jax.ShapeDtypeStruct((t, n), BF16)]
    kern = functools.partial(_combine_kernel, ctx_tiles=ct, tiles_per_batch=tpb, final=final, next_qkv=next_qkv)
    dest3 = dest.reshape(steps, 1, 2 * ROW_TILE)
    out = pl.pallas_call(
        kern,
        grid=(steps,),
        in_specs=[
            pl.BlockSpec((1, 1, 2 * ROW_TILE), lambda i: (i, 0, 0), memory_space=pltpu.SMEM),
            pl.BlockSpec((1, 1, 2 * ROW_TILE), lambda i: (jnp.minimum(i + 1, steps - 1), 0, 0), memory_space=pltpu.SMEM),
            pl.BlockSpec((ROW_TILE, SUBLANES), lambda i: (i, 0)),
            pl.BlockSpec((ROW_TILE, d), lambda i: (i, 0)),
            pl.BlockSpec((MOD_ROWS, 6 * d), const),
            pl.BlockSpec((1, d), const),
        ] + extra_specs + [pl.BlockSpec(memory_space=pl.ANY)],
        out_specs=out_specs,
        out_shape=out_shape,
        scratch_shapes=[pltpu.VMEM((2, 2 * ROW_TILE * SUBLANES, LANES), F32), pltpu.SemaphoreType.DMA((2,))],
        compiler_params=_params(("arbitrary",)),
        name="moe_combine",
    )(dest3, dest3, rt, x.reshape(t, d), mod, final_g, *extra_in, ys)
    if final:
        return out.reshape(b, out_rows // b, d), None
    return out[0].reshape(b, r, d), out[1].reshape(b, r, -1)


def _route_plan(rt, n_tiles):
    eid = rt[:, 0:2].astype(jnp.int32).reshape(-1)
    onehot = (eid[:, None] == jnp.arange(N_EXPERTS, dtype=jnp.int32)[None, :]).astype(jnp.int32)
    csum = jnp.cumsum(onehot, axis=0)
    counts = csum[-1]
    cstart = jnp.cumsum(counts) - counts
    dest = jnp.sum(onehot * (cstart[None, :] + csum - 1), axis=1).astype(jnp.int32)

    tm = ROW_TILE
    first_tile = cstart // tm
    last_tile = (cstart + counts - 1) // tm
    nvis = jnp.where(counts > 0, last_tile - first_tile + 1, 0)
    vend = jnp.cumsum(nvis)
    vstart = vend - nvis
    total = vend[-1]
    n_visits = n_tiles + N_EXPERTS - 1
    v = jnp.arange(n_visits, dtype=jnp.int32)
    valid = v < total
    vc = jnp.minimum(v, total - 1)
    ve = jnp.sum((vc[:, None] >= vend[None, :]).astype(jnp.int32), axis=1)
    oh_e = (ve[:, None] == jnp.arange(N_EXPERTS, dtype=jnp.int32)[None, :]).astype(jnp.int32)
    pick = lambda a: jnp.sum(oh_e * a[None, :], axis=1)
    vt = pick(first_tile) + vc - pick(vstart)
    lo = pick(cstart)
    hi = jnp.where(valid, lo + pick(counts), lo)
    first = jnp.concatenate([jnp.ones((1,), jnp.int32), (vt[1:] != vt[:-1]).astype(jnp.int32)])
    change = jnp.concatenate([jnp.ones((1,), jnp.int32), (ve[1:] != ve[:-1]).astype(jnp.int32)]) * valid
    e_ids = jnp.arange(N_EXPERTS, dtype=jnp.int32)
    used = jnp.where(counts > 0, e_ids, N_EXPERTS)
    following = jnp.concatenate([lax.cummin(used, reverse=True)[1:], jnp.full((1,), N_EXPERTS, jnp.int32)])
    nxt = pick(jnp.where(following < N_EXPERTS, following, -1))
    slot = (jnp.cumsum(change) - 1) % 2
    sched = tuple(a.astype(jnp.int32) for a in (ve, vt, lo, hi, first, valid, change, nxt, slot))
    return dest, sched


def kernel(x, c, ctx, c_ctx, ada_w, ada_b, norm_mix_g, norm_ffn_g, final_norm_g, na_w_qkv, na_rpb, na_w_o, da_w_qkv, da_lambda, da_subln_g, da_w_o, sw_w_qkv, sw_sinks, sw_w_o, moe_w_grp, moe_b_grp, moe_w_exp, moe_b_exp, moe_w_gate, moe_w_up, moe_w_down):
    b, seq, d = x.shape
    ctx_len = ctx.shape[1]
    depth = ada_w.shape[0]
    assert d == D_MODEL and ctx_len % ROW_TILE == 0 and seq % ROW_TILE == 0 and b <= MOD_ROWS // 2
    assert EXPERTS_PER_GROUP == SUBLANES and N_GROUPS <= SUBLANES
    assert (b * (ctx_len + seq)) % DISPATCH_TOKENS == 0

    half = MOD_ROWS // 2
    cond = jnp.concatenate([jnp.pad(c, ((0, half - b), (0, 0))), jnp.pad(c_ctx[None, :], ((0, half - 1), (0, 0)))], axis=0)
    mod_all = _adaln(cond, ada_w, ada_b)
    cos, sin = _rope_tables(seq, ctx_len)
    stream = jnp.concatenate([ctx, x], axis=1)
    n_tiles = 2 * b * (ctx_len + seq) // ROW_TILE

    def qkv_args(i):
        kind, j = i % N_MIXERS, i // N_MIXERS
        if kind == 0:
            w, rope_cols, q_cols = na_w_qkv[j], 0, NA_HEADS * HEAD_DIM
        elif kind == 1:
            nq = 2 * DA_HEADS * HEAD_DIM
            w, rope_cols, q_cols = da_w_qkv[j], 2 * nq, nq
        else:
            w, rope_cols, q_cols = sw_w_qkv[j], (SW_Q_HEADS + SW_KV_HEADS) * HEAD_DIM, SW_Q_HEADS * HEAD_DIM
        return norm_mix_g[i].reshape(1, d), mod_all[i], w.astype(BF16), cos, sin, rope_cols, q_cols

    qkv = _qkv(stream, *qkv_args(0), ctx_len)
    for i in range(depth):
        need_ctx = i < depth - 1
        kind, j = i % N_MIXERS, i // N_MIXERS
        mod = mod_all[i]
        if kind == 0:
            o = _na_attention(qkv, na_rpb[j], ctx_len, need_ctx)
            wo = na_w_o[j]
        elif kind == 1:
            o = _da_attention(qkv, da_lambda[j], da_subln_g[j], i, ctx_len)
            wo = da_w_o[j]
        else:
            o = _sw_attention(qkv, sw_sinks[j], ctx_len, need_ctx)
            wo = sw_w_o[j]

        gpad, pad = SUBLANES - N_GROUPS, LANES - SUBLANES - N_EXPERTS
        wr = jnp.concatenate([jnp.pad(moe_w_grp[i].astype(F32), ((0, 0), (0, gpad))),
                              jnp.pad(moe_w_exp[i].astype(F32), ((0, 0), (0, pad)))], axis=1)
        br = jnp.concatenate([jnp.pad(moe_b_grp[i].astype(F32), (0, gpad)),
                              jnp.pad(moe_b_exp[i].astype(F32), (0, pad))]).reshape(1, LANES)
        stream, h_rows, rt = _post(o, stream, wo.astype(BF16), mod, norm_ffn_g[i].reshape(1, d), wr, br, ctx_len)
        dest, sched = _route_plan(rt, n_tiles)
        xs = _dispatch(h_rows, dest)
        ys = _experts(xs, sched, moe_w_gate, moe_w_up, moe_w_down, i)
        if i == depth - 1:
            stream, qkv = _combine(ys, dest, rt, stream, mod, ctx_len, final_g=final_norm_g.reshape(1, d).astype(F32))
        else:
            stream, qkv = _combine(ys, dest, rt, stream, mod, ctx_len, next_layer=qkv_args(i + 1))

    return stream
```

```python
import functools
import math

import jax
import jax.numpy as jnp
from jax import lax
from jax.experimental import pallas as pl
from jax.experimental.pallas import tpu as pltpu

F32 = jnp.float32
BF16 = jnp.bfloat16

D_MODEL = 1024
HEAD_DIM = 64
GRID_W = 64
ROPE_AXIS_PAIRS = HEAD_DIM // 4
ROPE_BASE = 10000.0
NA_HEADS = D_MODEL // HEAD_DIM
NA_WIN_R = 8
NA_WIN_C = 16
NA_GROUP = 4
NA_UNION = NA_GROUP + NA_WIN_R
DA_HEADS = D_MODEL // (2 * HEAD_DIM)
SW_Q_HEADS = D_MODEL // HEAD_DIM
SW_KV_HEADS = 4
SW_WINDOW = 128
Q_BLOCK = 128
N_GROUPS = 4
EXPERTS_PER_GROUP = 8
N_EXPERTS = N_GROUPS * EXPERTS_PER_GROUP
D_EXPERT = 512
NORM_EPS = 1e-6
NEG_INF = -1e30
N_MIXERS = 3

LANES = 128
SUBLANES = 8
ROW_TILE = 256
MOD_ROWS = 16
VMEM_LIMIT = 56 * 1024 * 1024

LOG2E = math.log2(math.e)
Q_SCALE = HEAD_DIM ** -0.5 * LOG2E

_NT = (((1,), (1,)), ((), ()))


def _dot(a, b, **kw):
    return jnp.dot(a, b, preferred_element_type=F32, **kw)


def _dot_nt(a, b):
    return lax.dot_general(a, b, _NT, preferred_element_type=F32)


def _params(sem, vmem=VMEM_LIMIT):
    return pltpu.CompilerParams(dimension_semantics=sem, vmem_limit_bytes=vmem)


def _mod_row(b, tile, ctx_tiles):
    return jnp.where(tile < ctx_tiles, MOD_ROWS // 2, b)


def _rms(x):
    return x * lax.rsqrt(jnp.mean(x * x, axis=-1, keepdims=True) + NORM_EPS)


def _adaln_kernel(c_ref, w_ref, b_ref, o_ref):
    s = jax.nn.silu(c_ref[...])
    o_ref[...] = _dot(s, w_ref[...], precision=lax.Precision.HIGHEST) + b_ref[...]


def _adaln(cond, ada_w, ada_b):
    depth, d, n6 = ada_w.shape
    tn = 1536
    return pl.pallas_call(
        _adaln_kernel,
        grid=(depth, n6 // tn),
        in_specs=[
            pl.BlockSpec((MOD_ROWS, d), lambda l, n: (0, 0)),
            pl.BlockSpec((None, d, tn), lambda l, n: (l, 0, n)),
            pl.BlockSpec((None, 1, tn), lambda l, n: (l, 0, n)),
        ],
        out_specs=pl.BlockSpec((None, MOD_ROWS, tn), lambda l, n: (l, 0, n)),
        out_shape=jax.ShapeDtypeStruct((depth, MOD_ROWS, n6), F32),
        compiler_params=_params(("arbitrary", "arbitrary")),
        name="adaln",
    )(cond, ada_w, ada_b.reshape(depth, 1, n6))


def _qkv_tile(x, row, g_ref, mod_ref, w_ref, cos_ref, sin_ref, o_ref, rope_cols, q_cols):
    d = x.shape[-1]
    shift = mod_ref[pl.ds(row, 1), 0:d]
    scale = mod_ref[pl.ds(row, 1), d:2 * d]
    h = (_rms(x) * g_ref[...]) * (1.0 + scale) + shift
    acc = _dot(h.astype(BF16), w_ref[...])
    n = acc.shape[-1]
    if rope_cols:
        tm = acc.shape[0]
        first_half = (lax.broadcasted_iota(jnp.int32, (tm, LANES), 1) % (2 * ROPE_AXIS_PAIRS)) < ROPE_AXIS_PAIRS
        for c in range(rope_cols // LANES):
            tab = slice(LANES, 2 * LANES) if c * LANES < q_cols else slice(0, LANES)
            t = acc[:, c * LANES:(c + 1) * LANES]
            up = pltpu.roll(t, LANES - ROPE_AXIS_PAIRS, 1)
            dn = pltpu.roll(t, ROPE_AXIS_PAIRS, 1)
            o_ref[:, c * LANES:(c + 1) * LANES] = (
                t * cos_ref[:, tab] + jnp.where(first_half, up, dn) * sin_ref[:, tab]).astype(o_ref.dtype)
    else:
        o_ref[:, 0:q_cols] = (acc[:, 0:q_cols] * Q_SCALE).astype(o_ref.dtype)
        rope_cols = q_cols
    if rope_cols < n:
        o_ref[:, rope_cols:] = acc[:, rope_cols:].astype(o_ref.dtype)


def _qkv_kernel(x_ref, g_ref, mod_ref, w_ref, cos_ref, sin_ref, o_ref, *, rope_cols, q_cols, ctx_tiles):
    row = _mod_row(pl.program_id(0), pl.program_id(1), ctx_tiles)
    _qkv_tile(x_ref[...], row, g_ref, mod_ref, w_ref, cos_ref, sin_ref, o_ref, rope_cols, q_cols)


def _qkv(x, g, mod, w, cos, sin, rope_cols, q_cols, ctx_len):
    b, r, d = x.shape
    n = w.shape[1]
    kern = functools.partial(_qkv_kernel, rope_cols=rope_cols, q_cols=q_cols, ctx_tiles=ctx_len // ROW_TILE)
    return pl.pallas_call(
        kern,
        grid=(b, r // ROW_TILE),
        in_specs=[
            pl.BlockSpec((None, ROW_TILE, d), lambda bi, i: (bi, i, 0)),
            pl.BlockSpec((1, d), lambda bi, i: (0, 0)),
            pl.BlockSpec((MOD_ROWS, 6 * d), lambda bi, i: (0, 0)),
            pl.BlockSpec((d, n), lambda bi, i: (0, 0)),
            pl.BlockSpec((ROW_TILE, 2 * LANES), lambda bi, i: (i, 0)),
            pl.BlockSpec((ROW_TILE, 2 * LANES), lambda bi, i: (i, 0)),
        ],
        out_specs=pl.BlockSpec((None, ROW_TILE, n), lambda bi, i: (bi, i, 0)),
        out_shape=jax.ShapeDtypeStruct((b, r, n), BF16),
        compiler_params=_params(("arbitrary", "arbitrary")),
        name="qkv",
    )(x, g, mod, w, cos, sin)


def _rope_tables(seq, ctx_len):
    t = jnp.arange(seq, dtype=jnp.int32)
    row = (t // GRID_W).astype(F32)
    col = (t % GRID_W).astype(F32)
    inv = ROPE_BASE ** (-jnp.arange(ROPE_AXIS_PAIRS, dtype=F32) / ROPE_AXIS_PAIRS)
    ang_r = row[:, None] * inv[None, :]
    ang_c = col[:, None] * inv[None, :]
    ang = jnp.concatenate([ang_r, ang_r, ang_c, ang_c], axis=-1)
    sign = jnp.tile(jnp.concatenate([-jnp.ones((ROPE_AXIS_PAIRS,), F32), jnp.ones((ROPE_AXIS_PAIRS,), F32)]), 2)
    cos = jnp.concatenate([jnp.ones((ctx_len, HEAD_DIM), F32), jnp.cos(ang)], axis=0)
    sin = jnp.concatenate([jnp.zeros((ctx_len, HEAD_DIM), F32), jnp.sin(ang) * sign[None, :]], axis=0)
    cos = jnp.tile(cos, (1, LANES // HEAD_DIM))
    sin = jnp.tile(sin, (1, LANES // HEAD_DIM))
    return jnp.concatenate([cos, cos * Q_SCALE], axis=1), jnp.concatenate([sin, sin * Q_SCALE], axis=1)


def _store_with_ones(dst_ref, v):
    dst_ref[:, 0:LANES] = v
    dst_ref[:, LANES:2 * LANES] = jnp.ones(v.shape, v.dtype)


def _head_masks(rows):
    lane = lax.broadcasted_iota(jnp.int32, (rows, LANES), 1)
    lo = lane < HEAD_DIM
    return lo, jnp.logical_not(lo)


def _na_group_geometry(g, grid_rows):
    u0 = min(max(g * NA_GROUP - NA_WIN_R // 2, 0), grid_rows - NA_UNION)
    idx = []
    for a in range(NA_GROUP):
        r = g * NA_GROUP + a
        r0 = min(max(r - NA_WIN_R // 2, 0), grid_rows - NA_WIN_R)
        idx.append([u0 + y - r + NA_WIN_R - 1 if r0 <= u0 + y < r0 + NA_WIN_R else 2 * NA_WIN_R - 1
                    for y in range(NA_UNION)])
    return u0, idx


def _na_kernel(q_ref, k_ref, v_ref, bt_ref, o_ref, fb_ref, v1_ref, nb_a, cx_a, nb_b, cx_b, *, ctx_len, grid_rows, need_ctx):
    n_groups = grid_rows // NA_GROUP

    @pl.when(pl.program_id(1) == 0)
    def _():
        for j in range(2):
            for pattern, g in enumerate((0, 1, n_groups - 1)):
                _, idx = _na_group_geometry(g, grid_rows)
                for a in range(NA_GROUP):
                    for p in range(NA_UNION // 2):
                        fb_ref[j, pattern, a * GRID_W:(a + 1) * GRID_W, 2 * p * GRID_W:(2 * p + 2) * GRID_W] = (
                            jnp.concatenate([bt_ref[j, idx[a][2 * p]], bt_ref[j, idx[a][2 * p + 1]]], axis=-1))

    _store_with_ones(v1_ref, v_ref[...])
    kc = k_ref[0:ctx_len, :]
    vc = v1_ref[0:ctx_len, :]

    def normalised(r):
        return r[:, 0:LANES] / r[:, LANES:2 * LANES]

    if need_ctx:
        qc = q_ref[0:ctx_len, :]
        lo, hi = _head_masks(ctx_len)
        outs = []
        for m in (lo, hi):
            s = _dot_nt(jnp.where(m, qc, 0), kc)
            e = jnp.exp2(s - jnp.max(s, axis=-1, keepdims=True))
            outs.append(normalised(_dot(e.astype(BF16), vc)))
        o_ref[0:ctx_len, :] = jnp.where(lo, outs[0], outs[1]).astype(o_ref.dtype)
    else:
        o_ref[0:ctx_len, :] = jnp.zeros((ctx_len, LANES), o_ref.dtype)

    gq = NA_GROUP * GRID_W
    uk = NA_UNION * GRID_W
    lo, hi = _head_masks(gq)

    def rows_of(g):
        u0 = jnp.clip(g * NA_GROUP - NA_WIN_R // 2, 0, grid_rows - NA_UNION)
        return pl.multiple_of(ctx_len + g * gq, gq), pl.multiple_of(ctx_len + u0 * GRID_W, GRID_W)

    def scores(g, nb_ref, cx_ref):
        q0, w0 = rows_of(g)
        pattern = jnp.where(g == 0, 0, jnp.where(g == n_groups - 1, 2, 1))
        q_g = q_ref[pl.ds(q0, gq), :]
        k_u = k_ref[pl.ds(w0, uk), :]
        for j, m in enumerate((lo, hi)):
            qj = jnp.where(m, q_g, 0)
            nb_ref[j] = _dot_nt(qj, k_u) + fb_ref[j, pattern]
            cx_ref[j] = _dot_nt(qj, kc)

    def finish(g, nb_ref, cx_ref):
        q0, w0 = rows_of(g)
        v_u = v1_ref[pl.ds(w0, uk), :]
        outs = []
        for j in range(2):
            s_nb = nb_ref[j]
            s_cx = cx_ref[j]
            mx = jnp.maximum(jnp.max(s_nb, axis=-1, keepdims=True), jnp.max(s_cx, axis=-1, keepdims=True))
            e_nb = jnp.exp2(s_nb - mx).astype(BF16)
            e_cx = jnp.exp2(s_cx - mx).astype(BF16)
            outs.append(normalised(_dot(e_cx, vc) + _dot(e_nb, v_u)))
        o_ref[pl.ds(q0, gq), :] = jnp.where(lo, outs[0], outs[1]).astype(o_ref.dtype)

    scores(0, nb_a, cx_a)

    def pair_body(t, carry):
        g = 2 * t
        scores(g + 1, nb_b, cx_b)
        finish(g, nb_a, cx_a)
        scores(g + 2, nb_a, cx_a)
        finish(g + 1, nb_b, cx_b)
        return carry

    lax.fori_loop(0, n_groups // 2 - 1, pair_body, 0)
    scores(n_groups - 1, nb_b, cx_b)
    finish(n_groups - 2, nb_a, cx_a)
    finish(n_groups - 1, nb_b, cx_b)


def _na_bias_table(rpb, grid_rows):
    n_groups = grid_rows // NA_GROUP
    assert grid_rows % (2 * NA_GROUP) == 0 and n_groups >= 3 and grid_rows >= NA_UNION
    for g in range(2, n_groups - 1):
        u1, idx1 = _na_group_geometry(1, grid_rows)
        ug, idxg = _na_group_geometry(g, grid_rows)
        assert idxg == idx1 and ug - g * NA_GROUP == u1 - NA_GROUP
    qcol = jnp.arange(GRID_W)
    kcol = jnp.arange(GRID_W)
    c0 = jnp.clip(qcol - NA_WIN_C // 2, 0, GRID_W - NA_WIN_C)
    col_ok = (kcol[None, :] >= c0[:, None]) & (kcol[None, :] < c0[:, None] + NA_WIN_C)
    rel = kcol[None, :] - qcol[:, None] + NA_WIN_C - 1
    onehot = ((rel[:, :, None] == jnp.arange(2 * NA_WIN_C - 1)[None, None, :]) & col_ok[:, :, None]).astype(F32)
    bt = jnp.einsum('hyd,qkd->hyqk', rpb.astype(F32), onehot, precision=lax.Precision.HIGHEST)
    bt = jnp.where(col_ok[None, None], bt * LOG2E, NEG_INF)
    return jnp.pad(bt, ((0, 0), (0, 1), (0, 0), (0, 0)), constant_values=NEG_INF)


def _na_attention(qkv, rpb, ctx_len, need_ctx):
    b, r, _ = qkv.shape
    grid_rows = (r - ctx_len) // GRID_W
    nblk = D_MODEL // LANES
    bt = _na_bias_table(rpb, grid_rows)
    kern = functools.partial(_na_kernel, ctx_len=ctx_len, grid_rows=grid_rows, need_ctx=need_ctx)
    return pl.pallas_call(
        kern,
        grid=(nblk, b),
        in_specs=[
            pl.BlockSpec((None, r, LANES), lambda hp, bi: (bi, 0, hp)),
            pl.BlockSpec((None, r, LANES), lambda hp, bi: (bi, 0, nblk + hp)),
            pl.BlockSpec((None, r, LANES), lambda hp, bi: (bi, 0, 2 * nblk + hp)),
            pl.BlockSpec((2, 2 * NA_WIN_R, GRID_W, GRID_W), lambda hp, bi: (hp, 0, 0, 0)),
        ],
        out_specs=pl.BlockSpec((None, r, LANES), lambda hp, bi: (bi, 0, hp)),
        out_shape=jax.ShapeDtypeStruct((b, r, D_MODEL), BF16),
        scratch_shapes=[pltpu.VMEM((2, 3, NA_GROUP * GRID_W, NA_UNION * GRID_W), F32), pltpu.VMEM((r, 2 * LANES), BF16)]
        + [pltpu.VMEM((2, NA_GROUP * GRID_W, NA_UNION * GRID_W), F32),
           pltpu.VMEM((2, NA_GROUP * GRID_W, ctx_len), F32)] * 2,
        compiler_params=_params(("arbitrary", "arbitrary")),
        name="na_attn",
    )(qkv, qkv, qkv, bt)


def _da_kernel(q_ref, k_ref, v_ref, lam_ref, g_ref, o_ref, v1_ref, s_a, s_b, *, ctx_len, lam_init):
    lp = lam_ref[...]
    lam = (jnp.exp(jnp.sum(lp[0:1] * lp[1:2], axis=-1, keepdims=True))
           - jnp.exp(jnp.sum(lp[2:3] * lp[3:4], axis=-1, keepdims=True)) + lam_init)
    tq = ROW_TILE
    n_tiles = (q_ref.shape[0] - ctx_len) // tq
    lo, hi = _head_masks(tq)

    _store_with_ones(v1_ref, v_ref[...])

    def finish(scores, n_keys, row0, rows):
        ps = []
        for s in scores:
            e = jnp.exp2(s - jnp.max(s, axis=-1, keepdims=True))
            r = _dot(e.astype(BF16), v1_ref[0:n_keys, :])
            ps.append(r[:, 0:LANES] / r[:, LANES:2 * LANES])
        o = ps[0] - lam * ps[1]
        o_ref[pl.ds(row0, rows), :] = (_rms(o) * g_ref[...] * (1.0 - lam_init)).astype(o_ref.dtype)

    qc = q_ref[0:ctx_len, :]
    lo_c, hi_c = _head_masks(ctx_len)
    finish([_dot_nt(jnp.where(m, qc, 0), k_ref[0:ctx_len, :]) for m in (lo_c, hi_c)], ctx_len, 0, ctx_len)

    def row_of(t):
        return pl.multiple_of(ctx_len + t * tq, tq)

    def scores(t, s_ref):
        q = q_ref[pl.ds(row_of(t), tq), :]
        for j, m in enumerate((lo, hi)):
            s_ref[j] = _dot_nt(jnp.where(m, q, 0), k_ref[...])

    scores(0, s_a)

    def pair_body(p, carry):
        t = 2 * p
        scores(t + 1, s_b)
        finish([s_a[0], s_a[1]], k_ref.shape[0], row_of(t), tq)
        scores(t + 2, s_a)
        finish([s_b[0], s_b[1]], k_ref.shape[0], row_of(t + 1), tq)
        return carry

    lax.fori_loop(0, n_tiles // 2 - 1, pair_body, 0)
    scores(n_tiles - 1, s_b)
    finish([s_a[0], s_a[1]], k_ref.shape[0], row_of(n_tiles - 2), tq)
    finish([s_b[0], s_b[1]], k_ref.shape[0], row_of(n_tiles - 1), tq)


def _da_attention(qkv, lam_params, subln_g, layer_idx, ctx_len):
    b, r, _ = qkv.shape
    assert (r - ctx_len) % (2 * ROW_TILE) == 0
    nblk = D_MODEL // LANES
    lam_init = 0.8 - 0.6 * math.exp(-0.3 * layer_idx)
    kern = functools.partial(_da_kernel, ctx_len=ctx_len, lam_init=lam_init)
    return pl.pallas_call(
        kern,
        grid=(nblk, b),
        in_specs=[
            pl.BlockSpec((None, r, LANES), lambda h, bi: (bi, 0, h)),
            pl.BlockSpec((None, r, LANES), lambda h, bi: (bi, 0, nblk + h)),
            pl.BlockSpec((None, r, LANES), lambda h, bi: (bi, 0, 2 * nblk + h)),
            pl.BlockSpec((4, HEAD_DIM), lambda h, bi: (0, 0)),
            pl.BlockSpec((1, 2 * HEAD_DIM), lambda h, bi: (0, 0)),
        ],
        out_specs=pl.BlockSpec((None, r, LANES), lambda h, bi: (bi, 0, h)),
        out_shape=jax.ShapeDtypeStruct((b, r, D_MODEL), BF16),
        scratch_shapes=[pltpu.VMEM((r, 2 * LANES), BF16), pltpu.VMEM((2, ROW_TILE, r), F32), pltpu.VMEM((2, ROW_TILE, r), F32)],
        compiler_params=_params(("arbitrary", "arbitrary")),
        name="da_attn",
    )(qkv, qkv, qkv, lam_params.astype(F32), subln_g.astype(F32).reshape(1, -1))


def _sw_kernel(sink_ref, q_ref, k_ref, v_ref, o_ref, kd_ref, vd_ref, sl_a, sc_a, sl_b, sc_b, *, ctx_len, seq, need_ctx):
    kvp = pl.program_id(0)
    band = Q_BLOCK + 2 * SW_WINDOW
    hpk = SW_Q_HEADS // SW_KV_HEADS

    k = k_ref[...]
    v = v_ref[...]
    for hh in range(2):
        sl = slice(hh * HEAD_DIM, (hh + 1) * HEAD_DIM)
        kd_ref[hh] = jnp.concatenate([k[:, sl], k[:, sl]], axis=-1)
        _store_with_ones(vd_ref.at[hh], jnp.concatenate([v[:, sl], v[:, sl]], axis=-1))

    n_blk = 2 * (hpk // 2)

    def stacked_q(row0, rows, c):
        lo, hi = _head_masks(rows)
        qv = q_ref[pl.ds(row0, rows), c * LANES:(c + 1) * LANES]
        return jnp.concatenate([jnp.where(lo, qv, 0), jnp.where(hi, qv, 0)], axis=0)

    def per_head(rows, c, col, fn):
        sinks = [sink_ref[kvp * 2 * hpk + 2 * c + j] * LOG2E for j in range(2)]
        return jnp.concatenate([fn(col[j * rows:(j + 1) * rows], sinks[j]) for j in range(2)], axis=0)

    def write_out(row0, rows, c, o):
        lo, _ = _head_masks(rows)
        o_ref[pl.ds(row0, rows), c * LANES:(c + 1) * LANES] = jnp.where(lo, o[0:rows], o[rows:2 * rows]).astype(o_ref.dtype)

    def softmax_pv(rows, c, parts):
        row_max = functools.reduce(jnp.maximum, [jnp.max(s, axis=-1, keepdims=True) for s, _ in parts])
        mx = per_head(rows, c, row_max, lambda m, sink: jnp.maximum(m, sink))
        r = sum(_dot(jnp.exp2(s - mx).astype(BF16), val) for s, val in parts)
        den = r[:, LANES:2 * LANES] + per_head(rows, c, mx, lambda m, sink: jnp.exp2(sink - m))
        return r[:, 0:LANES] / den

    if need_ctx:
        for c in range(n_blk):
            hh = c // (hpk // 2)
            s = _dot_nt(stacked_q(0, ctx_len, c), kd_ref[hh, 0:ctx_len, :])
            write_out(0, ctx_len, c, softmax_pv(ctx_len, c, [(s, vd_ref[hh, 0:ctx_len, :])]))
    else:
        o_ref[0:ctx_len, :] = jnp.zeros((ctx_len, o_ref.shape[-1]), o_ref.dtype)

    n_steps = seq // Q_BLOCK

    def rows_of(n):
        start = n * Q_BLOCK
        b0 = jnp.clip(start - SW_WINDOW, 0, seq - band)
        return start, b0, pl.multiple_of(ctx_len + start, Q_BLOCK), pl.multiple_of(ctx_len + b0, Q_BLOCK)

    def scores(n, sl_ref, sc_ref):
        start, b0, qrow, brow = rows_of(n)
        qpos = start + (lax.broadcasted_iota(jnp.int32, (2 * Q_BLOCK, band), 0) & (Q_BLOCK - 1))
        kpos = b0 + lax.broadcasted_iota(jnp.int32, (2 * Q_BLOCK, band), 1)
        in_band = jnp.abs(kpos - qpos) <= SW_WINDOW
        for c in range(n_blk):
            hh = c // (hpk // 2)
            qs = stacked_q(qrow, Q_BLOCK, c)
            sl_ref[c] = jnp.where(in_band, _dot_nt(qs, kd_ref[hh, pl.ds(brow, band), :]), NEG_INF)
            sc_ref[c] = _dot_nt(qs, kd_ref[hh, 0:ctx_len, :])

    def finish(n, sl_ref, sc_ref):
        _, _, qrow, brow = rows_of(n)
        for c in range(n_blk):
            hh = c // (hpk // 2)
            o = softmax_pv(Q_BLOCK, c, [(sc_ref[c], vd_ref[hh, 0:ctx_len, :]),
                                        (sl_ref[c], vd_ref[hh, pl.ds(brow, band), :])])
            write_out(qrow, Q_BLOCK, c, o)

    scores(0, sl_a, sc_a)

    def pair_body(t, carry):
        n = 2 * t
        scores(n + 1, sl_b, sc_b)
        finish(n, sl_a, sc_a)
        scores(n + 2, sl_a, sc_a)
        finish(n + 1, sl_b, sc_b)
        return carry

    lax.fori_loop(0, n_steps // 2 - 1, pair_body, 0)
    scores(n_steps - 1, sl_b, sc_b)
    finish(n_steps - 2, sl_a, sc_a)
    finish(n_steps - 1, sl_b, sc_b)


def _sw_attention(qkv, sinks, ctx_len, need_ctx):
    b, r, _ = qkv.shape
    seq = r - ctx_len
    assert seq % (2 * Q_BLOCK) == 0
    nq_blk = SW_Q_HEADS * HEAD_DIM // LANES
    slab = 2 * (SW_Q_HEADS // SW_KV_HEADS) * HEAD_DIM
    kern = functools.partial(_sw_kernel, ctx_len=ctx_len, seq=seq, need_ctx=need_ctx)
    n_kv_blk = SW_KV_HEADS * HEAD_DIM // LANES
    return pl.pallas_call(
        kern,
        grid=(n_kv_blk, b),
        in_specs=[
            pl.BlockSpec(memory_space=pltpu.SMEM),
            pl.BlockSpec((None, r, slab), lambda kp, bi: (bi, 0, kp)),
            pl.BlockSpec((None, r, LANES), lambda kp, bi: (bi, 0, nq_blk + kp)),
            pl.BlockSpec((None, r, LANES), lambda kp, bi: (bi, 0, nq_blk + n_kv_blk + kp)),
        ],
        out_specs=pl.BlockSpec((None, r, slab), lambda kp, bi: (bi, 0, kp)),
        out_shape=jax.ShapeDtypeStruct((b, r, D_MODEL), BF16),
        scratch_shapes=[pltpu.VMEM((2, r, LANES), BF16), pltpu.VMEM((2, r, 2 * LANES), BF16)]
        + [pltpu.VMEM((slab // LANES, 2 * Q_BLOCK, Q_BLOCK + 2 * SW_WINDOW), F32),
           pltpu.VMEM((slab // LANES, 2 * Q_BLOCK, ctx_len), F32)] * 2,
        compiler_params=_params(("arbitrary", "arbitrary")),
        name="sw_attn",
    )(sinks.astype(F32), qkv, qkv, qkv)


POST_SUBTILES = 2


def _post_kernel(o_ref, x_ref, wo_ref, mod_ref, g_ref, wr_ref, br_ref, xn_ref, h_ref, rt_ref, *, ctx_tiles, tiles_per_batch):
    for sub in range(POST_SUBTILES):
        tile = pl.program_id(0) * POST_SUBTILES + sub
        row = _mod_row(tile // tiles_per_batch, tile % tiles_per_batch, ctx_tiles)
        rows = slice(sub * ROW_TILE, (sub + 1) * ROW_TILE)
        _post_tile(o_ref.at[rows], x_ref.at[rows], wo_ref, mod_ref, g_ref, wr_ref, br_ref, xn_ref.at[rows],
                   h_ref.at[sub * ROW_TILE * SUBLANES:(sub + 1) * ROW_TILE * SUBLANES], rt_ref.at[rows], row)


def _post_tile(o_ref, x_ref, wo_ref, mod_ref, g_ref, wr_ref, br_ref, xn_ref, h_ref, rt_ref, row):
    d = x_ref.shape[-1]
    tm = x_ref.shape[0]
    gate1 = mod_ref[pl.ds(row, 1), 2 * d:3 * d]
    shift2 = mod_ref[pl.ds(row, 1), 3 * d:4 * d]
    scale2 = mod_ref[pl.ds(row, 1), 4 * d:5 * d]
    xn = x_ref[...] + gate1 * _dot(o_ref[...], wo_ref[...])
    xn_ref[...] = xn
    h = (_rms(xn) * g_ref[...]) * (1.0 + scale2) + shift2
    for j in range(SUBLANES):
        h_ref[pl.ds(j, tm, stride=SUBLANES), :] = h[:, j * LANES:(j + 1) * LANES]

    h_hi = h.astype(BF16)
    h_lo = (h - h_hi.astype(F32)).astype(BF16)
    hi_terms = _dot(h_hi, wr_ref[...])
    logits = (hi_terms[:, 0:LANES] + hi_terms[:, LANES:2 * LANES] + _dot(h_lo, wr_ref[:, 0:LANES])) + br_ref[...]

    lt = logits.T
    sub = lax.broadcasted_iota(jnp.int32, (SUBLANES, tm), 0).astype(F32)

    def first_argmax(vals, mx):
        return jnp.min(jnp.where(vals == mx, sub, float(SUBLANES)), axis=0, keepdims=True)

    gl = jnp.where(sub < N_GROUPS, lt[0:SUBLANES, :], -jnp.inf)
    ge = jnp.exp(gl - jnp.max(gl, axis=0, keepdims=True))
    gp = ge / jnp.sum(ge, axis=0, keepdims=True)
    g_p = jnp.max(gp, axis=0, keepdims=True)
    g_idx = first_argmax(gp, g_p)
    el = lt[SUBLANES:2 * SUBLANES, :]
    for g in range(1, N_GROUPS):
        el = jnp.where(g_idx == g, lt[(g + 1) * SUBLANES:(g + 2) * SUBLANES, :], el)
    v1 = jnp.max(el, axis=0, keepdims=True)
    i1 = first_argmax(el, v1)
    el2 = jnp.where(sub == i1, -jnp.inf, el)
    v2 = jnp.max(el2, axis=0, keepdims=True)
    i2 = first_argmax(el2, v2)
    e2 = jnp.exp(v2 - v1)
    w1 = g_p * (1.0 / (1.0 + e2))
    w2 = g_p * (e2 / (1.0 + e2))
    base = g_idx * EXPERTS_PER_GROUP
    rt_t = jnp.concatenate([base + i1, base + i2, w1, w2, jnp.zeros((LANES - 4, tm), F32)], axis=0)
    rt_ref[...] = rt_t.T[:, 0:SUBLANES]


def _post(o, x, wo, mod, g, wr, br, ctx_len):
    wr_hi = wr.astype(BF16)
    wr_cat = jnp.concatenate([wr_hi, (wr - wr_hi.astype(F32)).astype(BF16)], axis=1)
    b, r, d = x.shape
    t = b * r
    step_rows = POST_SUBTILES * ROW_TILE
    assert t % step_rows == 0
    kern = functools.partial(_post_kernel, ctx_tiles=ctx_len // ROW_TILE, tiles_per_batch=r // ROW_TILE)
    const = lambda i: (0, 0)
    rows = lambda i: (i, 0)
    xn, h_rows, rt = pl.pallas_call(
        kern,
        grid=(t // step_rows,),
        in_specs=[
            pl.BlockSpec((step_rows, d), rows),
            pl.BlockSpec((step_rows, d), rows),
            pl.BlockSpec((d, d), const),
            pl.BlockSpec((MOD_ROWS, 6 * d), const),
            pl.BlockSpec((1, d), const),
            pl.BlockSpec((d, 2 * LANES), const),
            pl.BlockSpec((1, LANES), const),
        ],
        out_specs=[
            pl.BlockSpec((step_rows, d), rows),
            pl.BlockSpec((step_rows * SUBLANES, LANES), rows),
            pl.BlockSpec((step_rows, SUBLANES), rows),
        ],
        out_shape=[
            jax.ShapeDtypeStruct((t, d), F32),
            jax.ShapeDtypeStruct((t * SUBLANES, LANES), F32),
            jax.ShapeDtypeStruct((t, SUBLANES), F32),
        ],
        compiler_params=_params(("arbitrary",)),
        name="post_attn",
    )(o.reshape(t, d), x.reshape(t, d), wo, mod, g, wr_cat, br)
    return xn.reshape(b, r, d), h_rows, rt


DISPATCH_TOKENS = 512


def _row(ref, idx):
    return ref.at[pl.ds(pl.multiple_of(idx * SUBLANES, SUBLANES), SUBLANES), :]


def _dispatch_kernel(dest_ref, h_ref, xs_hbm, sem):
    def issue(t, carry):
        src = _row(h_ref, t)
        for k in range(2):
            pltpu.make_async_copy(src, _row(xs_hbm, dest_ref[0, 0, 2 * t + k]), sem).start(priority=k)
        return carry

    lax.fori_loop(0, DISPATCH_TOKENS, issue, 0)
    for _ in range(2):
        pltpu.make_async_copy(h_ref, xs_hbm.at[pl.ds(0, DISPATCH_TOKENS * SUBLANES), :], sem).wait()


def _dispatch(h_rows, dest):
    t = dest.shape[0] // 2
    steps = t // DISPATCH_TOKENS
    return pl.pallas_call(
        _dispatch_kernel,
        grid=(steps,),
        in_specs=[
            pl.BlockSpec((1, 1, 2 * DISPATCH_TOKENS), lambda i: (i, 0, 0), memory_space=pltpu.SMEM),
            pl.BlockSpec((DISPATCH_TOKENS * SUBLANES, LANES), lambda i: (i, 0)),
        ],
        out_specs=pl.BlockSpec(memory_space=pl.ANY),
        out_shape=jax.ShapeDtypeStruct((2 * t * SUBLANES, LANES), F32),
        scratch_shapes=[pltpu.SemaphoreType.DMA(())],
        compiler_params=_params(("arbitrary",)),
        name="moe_dispatch",
    )(dest.reshape(steps, 1, 2 * DISPATCH_TOKENS), h_rows)


def _experts_kernel(ve_ref, vt_ref, lo_ref, hi_ref, first_ref, valid_ref, change_ref, next_ref, slot_ref,
                    xs_ref, wg_hbm, wu_hbm, wd_hbm, ys_ref, wgu_s, wd_s, acc_s, stg_g, stg_u, stg_d, sem, *, layer):
    v = pl.program_id(0)
    tm = ROW_TILE
    f = stg_g.shape[-1]
    streams = ((wg_hbm, stg_g), (wu_hbm, stg_u), (wd_hbm, stg_d))

    def fetch(expert, slot):
        for src, dst in streams:
            pltpu.make_async_copy(src.at[layer, expert], dst.at[slot], sem.at[slot]).start()

    @pl.when(v == 0)
    def _():
        acc_s[...] = jnp.zeros(acc_s.shape, acc_s.dtype)
        fetch(ve_ref[0], 0)

    @pl.when(valid_ref[v] == 1)
    def _():
        @pl.when(change_ref[v] == 1)
        def _():
            slot = slot_ref[v]
            for src, dst in streams:
                pltpu.make_async_copy(src.at[layer, 0], dst.at[slot], sem.at[slot]).wait()
            wgu_s[:, 0:f] = stg_g[slot].astype(BF16)
            wgu_s[:, f:2 * f] = stg_u[slot].astype(BF16)
            wd_s[...] = stg_d[slot].astype(BF16)

            @pl.when(next_ref[v] >= 0)
            def _():
                fetch(next_ref[v], 1 - slot)

        x = jnp.concatenate([xs_ref[pl.ds(j, tm, stride=SUBLANES), :] for j in range(SUBLANES)], axis=-1)
        au = _dot(x.astype(BF16), wgu_s[...])
        rows = vt_ref[v] * tm + lax.broadcasted_iota(jnp.int32, (tm, 1), 0)
        mine = (rows >= lo_ref[v]) & (rows < hi_ref[v])
        mid = jnp.where(mine, jax.nn.silu(au[:, 0:f]) * au[:, f:2 * f], 0.0)
        y = _dot(mid.astype(BF16), wd_s[...])
        y = jnp.where(first_ref[v] == 1, y, acc_s[...] + y)
        acc_s[...] = y
        for j in range(SUBLANES):
            ys_ref[pl.ds(j, tm, stride=SUBLANES), :] = y[:, j * LANES:(j + 1) * LANES]


def _experts(xs, sched, w_gate, w_up, w_down, layer):
    rows = xs.shape[0] // SUBLANES
    d, f = w_gate.shape[-2:]
    n_visits = sched[0].shape[0]
    blk = (ROW_TILE * SUBLANES, LANES)
    tile_map = lambda v, ve, vt, *_: (vt[v], 0)
    grid_spec = pltpu.PrefetchScalarGridSpec(
        num_scalar_prefetch=len(sched),
        grid=(n_visits,),
        in_specs=[pl.BlockSpec(blk, tile_map)] + [pl.BlockSpec(memory_space=pl.ANY)] * 3,
        out_specs=pl.BlockSpec(blk, tile_map),
        scratch_shapes=[pltpu.VMEM((d, 2 * f), BF16), pltpu.VMEM((f, d), BF16), pltpu.VMEM((ROW_TILE, d), F32),
                        pltpu.VMEM((2, d, f), F32), pltpu.VMEM((2, d, f), F32), pltpu.VMEM((2, f, d), F32),
                        pltpu.SemaphoreType.DMA((2,))],
    )
    return pl.pallas_call(
        functools.partial(_experts_kernel, layer=layer),
        grid_spec=grid_spec,
        out_shape=jax.ShapeDtypeStruct((rows * SUBLANES, LANES), F32),
        compiler_params=_params(("arbitrary",)),
        name="moe_experts",
    )(*sched, xs, w_gate, w_up, w_down)


COMBINE_UNROLL = 4


def _combine_kernel(dest_ref, next_ref, rt_ref, x_ref, mod_ref, g_ref, *rest, ctx_tiles, tiles_per_batch, final, next_qkv):
    if final:
        ys_hbm, o_ref, buf, sem = rest
    else:
        qg_ref, qmod_ref, qw_ref, cos_ref, sin_ref, ys_hbm, o_ref, qkv_ref, buf, sem = rest
    tm = ROW_TILE
    d = x_ref.shape[-1]
    s = pl.program_id(0)
    slot = s % 2
    half_rows = tm * SUBLANES

    def gather(idx_ref, slot_):
        def issue(t4, carry):
            for u in range(COMBINE_UNROLL):
                t = t4 * COMBINE_UNROLL + u
                for k in range(2):
                    pltpu.make_async_copy(_row(ys_hbm, idx_ref[0, 0, 2 * t + k]),
                                          _row(buf.at[slot_], k * tm + t), sem.at[slot_]).start(priority=k)
            return carry

        lax.fori_loop(0, tm // COMBINE_UNROLL, issue, 0)

    @pl.when(s == 0)
    def _():
        gather(dest_ref, 0)

    for parity in range(2):
        @pl.when(jnp.logical_and(s + 1 < pl.num_programs(0), slot == parity))
        def _():
            gather(next_ref, 1 - parity)

    for k in range(2):
        part = pl.ds(k * half_rows, half_rows)
        pltpu.make_async_copy(ys_hbm.at[part, :], buf.at[slot].at[part, :], sem.at[slot]).wait()

    row = _mod_row(s // tiles_per_batch, s % tiles_per_batch, ctx_tiles)
    gate2 = mod_ref[pl.ds(row, 1), 5 * d:6 * d]
    rt = rt_ref[...]
    w = (rt[:, 2:3], rt[:, 3:4])
    rows = buf.at[slot]
    pieces = []
    for j in range(SUBLANES):
        y = (w[0] * rows[pl.ds(j, tm, stride=SUBLANES), :]
             + w[1] * rows[pl.ds(half_rows + j, tm, stride=SUBLANES), :])
        sl = slice(j * LANES, (j + 1) * LANES)
        pieces.append(x_ref[:, sl] + gate2[:, sl] * y)
    xn = jnp.concatenate(pieces, axis=-1)
    if final:
        o_ref[...] = _rms(xn) * g_ref[...]
    else:
        o_ref[...] = xn
        _qkv_tile(xn, row, qg_ref, qmod_ref, qw_ref, cos_ref, sin_ref, qkv_ref, *next_qkv)


def _combine(ys, dest, rt, x, mod, ctx_len, final_g=None, next_layer=None):
    b, r, d = x.shape
    t = b * r
    steps = t // ROW_TILE
    ct, tpb = ctx_len // ROW_TILE, r // ROW_TILE
    final = final_g is not None
    const = lambda i: (0, 0)
    if final:
        out_rows = t - b * ctx_len
        out_specs = pl.BlockSpec((ROW_TILE, d), lambda i: ((i // tpb) * (tpb - ct) + jnp.maximum(i % tpb - ct, 0), 0))
        out_shape = jax.ShapeDtypeStruct((out_rows, d), F32)
        extra_in, extra_specs, next_qkv = [], [], None
    else:
        out_rows = t
        final_g = jnp.zeros((1, d), F32)
        qg, qmod, qw, cos, sin, rope_cols, q_cols = next_layer
        n = qw.shape[1]
        next_qkv = (rope_cols, q_cols)
        extra_in = [qg, qmod, qw, cos, sin]
        extra_specs = [pl.BlockSpec((1, d), const), pl.BlockSpec((MOD_ROWS, 6 * d), const), pl.BlockSpec((d, n), const),
                       pl.BlockSpec((ROW_TILE, 2 * LANES), lambda i: (i % tpb, 0)),
                       pl.BlockSpec((ROW_TILE, 2 * LANES), lambda i: (i % tpb, 0))]
        out_specs = [pl.BlockSpec((ROW_TILE, d), lambda i: (i, 0)), pl.BlockSpec((ROW_TILE, n), lambda i: (i, 0))]
        out_shape = [jax.ShapeDtypeStruct((t, d), F32), jax.ShapeDtypeStruct((t, n), BF16)]
    kern = functools.partial(_combine_kernel, ctx_tiles=ct, tiles_per_batch=tpb, final=final, next_qkv=next_qkv)
    dest3 = dest.reshape(steps, 1, 2 * ROW_TILE)
    out = pl.pallas_call(
        kern,
        grid=(steps,),
        in_specs=[
            pl.BlockSpec((1, 1, 2 * ROW_TILE), lambda i: (i, 0, 0), memory_space=pltpu.SMEM),
            pl.BlockSpec((1, 1, 2 * ROW_TILE), lambda i: (jnp.minimum(i + 1, steps - 1), 0, 0), memory_space=pltpu.SMEM),
            pl.BlockSpec((ROW_TILE, SUBLANES), lambda i: (i, 0)),
            pl.BlockSpec((ROW_TILE, d), lambda i: (i, 0)),
            pl.BlockSpec((MOD_ROWS, 6 * d), const),
            pl.BlockSpec((1, d), const),
        ] + extra_specs + [pl.BlockSpec(memory_space=pl.ANY)],
        out_specs=out_specs,
        out_shape=out_shape,
        scratch_shapes=[pltpu.VMEM((2, 2 * ROW_TILE * SUBLANES, LANES), F32), pltpu.SemaphoreType.DMA((2,))],
        compiler_params=_params(("arbitrary",)),
        name="moe_combine",
    )(dest3, dest3, rt, x.reshape(t, d), mod, final_g, *extra_in, ys)
    if final:
        return out.reshape(b, out_rows // b, d), None
    return out[0].reshape(b, r, d), out[1].reshape(b, r, -1)


def _route_plan(rt, n_tiles):
    eid = rt[:, 0:2].astype(jnp.int32).reshape(-1)
    onehot = (eid[:, None] == jnp.arange(N_EXPERTS, dtype=jnp.int32)[None, :]).astype(jnp.int32)
    csum = jnp.cumsum(onehot, axis=0)
    counts = csum[-1]
    cstart = jnp.cumsum(counts) - counts
    dest = jnp.sum(onehot * (cstart[None, :] + csum - 1), axis=1).astype(jnp.int32)

    tm = ROW_TILE
    first_tile = cstart // tm
    last_tile = (cstart + counts - 1) // tm
    nvis = jnp.where(counts > 0, last_tile - first_tile + 1, 0)
    vend = jnp.cumsum(nvis)
    vstart = vend - nvis
    total = vend[-1]
    n_visits = n_tiles + N_EXPERTS - 1
    v = jnp.arange(n_visits, dtype=jnp.int32)
    valid = v < total
    vc = jnp.minimum(v, total - 1)
    ve = jnp.sum((vc[:, None] >= vend[None, :]).astype(jnp.int32), axis=1)
    oh_e = (ve[:, None] == jnp.arange(N_EXPERTS, dtype=jnp.int32)[None, :]).astype(jnp.int32)
    pick = lambda a: jnp.sum(oh_e * a[None, :], axis=1)
    vt = pick(first_tile) + vc - pick(vstart)
    lo = pick(cstart)
    hi = jnp.where(valid, lo + pick(counts), lo)
    first = jnp.concatenate([jnp.ones((1,), jnp.int32), (vt[1:] != vt[:-1]).astype(jnp.int32)])
    change = jnp.concatenate([jnp.ones((1,), jnp.int32), (ve[1:] != ve[:-1]).astype(jnp.int32)]) * valid
    e_ids = jnp.arange(N_EXPERTS, dtype=jnp.int32)
    used = jnp.where(counts > 0, e_ids, N_EXPERTS)
    following = jnp.concatenate([lax.cummin(used, reverse=True)[1:], jnp.full((1,), N_EXPERTS, jnp.int32)])
    nxt = pick(jnp.where(following < N_EXPERTS, following, -1))
    slot = (jnp.cumsum(change) - 1) % 2
    sched = tuple(a.astype(jnp.int32) for a in (ve, vt, lo, hi, first, valid, change, nxt, slot))
    return dest, sched


def kernel(x, c, ctx, c_ctx, ada_w, ada_b, norm_mix_g, norm_ffn_g, final_norm_g, na_w_qkv, na_rpb, na_w_o, da_w_qkv, da_lambda, da_subln_g, da_w_o, sw_w_qkv, sw_sinks, sw_w_o, moe_w_grp, moe_b_grp, moe_w_exp, moe_b_exp, moe_w_gate, moe_w_up, moe_w_down):
    b, seq, d = x.shape
    ctx_len = ctx.shape[1]
    depth = ada_w.shape[0]
    assert d == D_MODEL and ctx_len % ROW_TILE == 0 and seq % ROW_TILE == 0 and b <= MOD_ROWS // 2
    assert EXPERTS_PER_GROUP == SUBLANES and N_GROUPS <= SUBLANES
    assert (b * (ctx_len + seq)) % DISPATCH_TOKENS == 0

    half = MOD_ROWS // 2
    cond = jnp.concatenate([jnp.pad(c, ((0, half - b), (0, 0))), jnp.pad(c_ctx[None, :], ((0, half - 1), (0, 0)))], axis=0)
    mod_all = _adaln(cond, ada_w, ada_b)
    cos, sin = _rope_tables(seq, ctx_len)
    stream = jnp.concatenate([ctx, x], axis=1)
    n_tiles = 2 * b * (ctx_len + seq) // ROW_TILE

    def qkv_args(i):
        kind, j = i % N_MIXERS, i // N_MIXERS
        if kind == 0:
            w, rope_cols, q_cols = na_w_qkv[j], 0, NA_HEADS * HEAD_DIM
        elif kind == 1:
            nq = 2 * DA_HEADS * HEAD_DIM
            w, rope_cols, q_cols = da_w_qkv[j], 2 * nq, nq
        else:
            w, rope_cols, q_cols = sw_w_qkv[j], (SW_Q_HEADS + SW_KV_HEADS) * HEAD_DIM, SW_Q_HEADS * HEAD_DIM
        return norm_mix_g[i].reshape(1, d), mod_all[i], w.astype(BF16), cos, sin, rope_cols, q_cols

    qkv = _qkv(stream, *qkv_args(0), ctx_len)
    for i in range(depth):
        need_ctx = i < depth - 1
        kind, j = i % N_MIXERS, i // N_MIXERS
        mod = mod_all[i]
        if kind == 0:
            o = _na_attention(qkv, na_rpb[j], ctx_len, need_ctx)
            wo = na_w_o[j]
        elif kind == 1:
            o = _da_attention(qkv, da_lambda[j], da_subln_g[j], i, ctx_len)
            wo = da_w_o[j]
        else:
            o = _sw_attention(qkv, sw_sinks[j], ctx_len, need_ctx)
            wo = sw_w_o[j]

        gpad, pad = SUBLANES - N_GROUPS, LANES - SUBLANES - N_EXPERTS
        wr = jnp.concatenate([jnp.pad(moe_w_grp[i].astype(F32), ((0, 0), (0, gpad))),
                              jnp.pad(moe_w_exp[i].astype(F32), ((0, 0), (0, pad)))], axis=1)
        br = jnp.concatenate([jnp.pad(moe_b_grp[i].astype(F32), (0, gpad)),
                              jnp.pad(moe_b_exp[i].astype(F32), (0, pad))]).reshape(1, LANES)
        stream, h_rows, rt = _post(o, stream, wo.astype(BF16), mod, norm_ffn_g[i].reshape(1, d), wr, br, ctx_len)
        dest, sched = _route_plan(rt, n_tiles)
        xs = _dispatch(h_rows, dest)
        ys = _experts(xs, sched, moe_w_gate, moe_w_up, moe_w_down, i)
        if i == depth - 1:
            stream, qkv = _combine(ys, dest, rt, stream, mod, ctx_len, final_g=final_norm_g.reshape(1, d).astype(F32))
        else:
            stream, qkv = _combine(ys, dest, rt, stream, mod, ctx_len, next_layer=qkv_args(i + 1))

    return stream
```

```python
import functools
import math

import jax
import jax.numpy as jnp
from jax import lax
from jax.experimental import pallas as pl
from jax.experimental.pallas import tpu as pltpu

F32 = jnp.float32
BF16 = jnp.bfloat16

D_MODEL = 1024
HEAD_DIM = 64
GRID_W = 64
ROPE_AXIS_PAIRS = HEAD_DIM // 4
ROPE_BASE = 10000.0
NA_HEADS = D_MODEL // HEAD_DIM
NA_WIN_R = 8
NA_WIN_C = 16
NA_GROUP = 4
NA_UNION = NA_GROUP + NA_WIN_R
DA_HEADS = D_MODEL // (2 * HEAD_DIM)
SW_Q_HEADS = D_MODEL // HEAD_DIM
SW_KV_HEADS = 4
SW_WINDOW = 128
Q_BLOCK = 128
N_GROUPS = 4
EXPERTS_PER_GROUP = 8
N_EXPERTS = N_GROUPS * EXPERTS_PER_GROUP
D_EXPERT = 512
NORM_EPS = 1e-6
NEG_INF = -1e30
N_MIXERS = 3

LANES = 128
SUBLANES = 8
ROW_TILE = 256
MOD_ROWS = 16
VMEM_LIMIT = 56 * 1024 * 1024

LOG2E = math.log2(math.e)
Q_SCALE = HEAD_DIM ** -0.5 * LOG2E

_NT = (((1,), (1,)), ((), ()))


def _dot(a, b, **kw):
    return jnp.dot(a, b, preferred_element_type=F32, **kw)


def _dot_nt(a, b):
    return lax.dot_general(a, b, _NT, preferred_element_type=F32)


def _params(sem, vmem=VMEM_LIMIT):
    return pltpu.CompilerParams(dimension_semantics=sem, vmem_limit_bytes=vmem)


def _mod_row(b, tile, ctx_tiles):
    return jnp.where(tile < ctx_tiles, MOD_ROWS // 2, b)


def _rms(x):
    return x * lax.rsqrt(jnp.mean(x * x, axis=-1, keepdims=True) + NORM_EPS)


def _adaln_kernel(c_ref, w_ref, b_ref, o_ref):
    s = jax.nn.silu(c_ref[...])
    o_ref[...] = _dot(s, w_ref[...], precision=lax.Precision.HIGHEST) + b_ref[...]


def _adaln(cond, ada_w, ada_b):
    depth, d, n6 = ada_w.shape
    tn = 1536
    return pl.pallas_call(
        _adaln_kernel,
        grid=(depth, n6 // tn),
        in_specs=[
            pl.BlockSpec((MOD_ROWS, d), lambda l, n: (0, 0)),
            pl.BlockSpec((None, d, tn), lambda l, n: (l, 0, n)),
            pl.BlockSpec((None, 1, tn), lambda l, n: (l, 0, n)),
        ],
        out_specs=pl.BlockSpec((None, MOD_ROWS, tn), lambda l, n: (l, 0, n)),
        out_shape=jax.ShapeDtypeStruct((depth, MOD_ROWS, n6), F32),
        compiler_params=_params(("arbitrary", "arbitrary")),
        name="adaln",
    )(cond, ada_w, ada_b.reshape(depth, 1, n6))


def _qkv_tile(x, row, g_ref, mod_ref, w_ref, cos_ref, sin_ref, o_ref, rope_cols, q_cols):
    d = x.shape[-1]
    shift = mod_ref[pl.ds(row, 1), 0:d]
    scale = mod_ref[pl.ds(row, 1), d:2 * d]
    h = (_rms(x) * g_ref[...]) * (1.0 + scale) + shift
    acc = _dot(h.astype(BF16), w_ref[...])
    n = acc.shape[-1]
    if rope_cols:
        tm = acc.shape[0]
        first_half = (lax.broadcasted_iota(jnp.int32, (tm, LANES), 1) % (2 * ROPE_AXIS_PAIRS)) < ROPE_AXIS_PAIRS
        for c in range(rope_cols // LANES):
            tab = slice(LANES, 2 * LANES) if c * LANES < q_cols else slice(0, LANES)
            t = acc[:, c * LANES:(c + 1) * LANES]
            up = pltpu.roll(t, LANES - ROPE_AXIS_PAIRS, 1)
            dn = pltpu.roll(t, ROPE_AXIS_PAIRS, 1)
            o_ref[:, c * LANES:(c + 1) * LANES] = (
                t * cos_ref[:, tab] + jnp.where(first_half, up, dn) * sin_ref[:, tab]).astype(o_ref.dtype)
    else:
        o_ref[:, 0:q_cols] = (acc[:, 0:q_cols] * Q_SCALE).astype(o_ref.dtype)
        rope_cols = q_cols
    if rope_cols < n:
        o_ref[:, rope_cols:] = acc[:, rope_cols:].astype(o_ref.dtype)


def _qkv_kernel(x_ref, g_ref, mod_ref, w_ref, cos_ref, sin_ref, o_ref, *, rope_cols, q_cols, ctx_tiles):
    row = _mod_row(pl.program_id(0), pl.program_id(1), ctx_tiles)
    _qkv_tile(x_ref[...], row, g_ref, mod_ref, w_ref, cos_ref, sin_ref, o_ref, rope_cols, q_cols)


def _qkv(x, g, mod, w, cos, sin, rope_cols, q_cols, ctx_len):
    b, r, d = x.shape
    n = w.shape[1]
    kern = functools.partial(_qkv_kernel, rope_cols=rope_cols, q_cols=q_cols, ctx_tiles=ctx_len // ROW_TILE)
    return pl.pallas_call(
        kern,
        grid=(b, r // ROW_TILE),
        in_specs=[
            pl.BlockSpec((None, ROW_TILE, d), lambda bi, i: (bi, i, 0)),
            pl.BlockSpec((1, d), lambda bi, i: (0, 0)),
            pl.BlockSpec((MOD_ROWS, 6 * d), lambda bi, i: (0, 0)),
            pl.BlockSpec((d, n), lambda bi, i: (0, 0)),
            pl.BlockSpec((ROW_TILE, 2 * LANES), lambda bi, i: (i, 0)),
            pl.BlockSpec((ROW_TILE, 2 * LANES), lambda bi, i: (i, 0)),
        ],
        out_specs=pl.BlockSpec((None, ROW_TILE, n), lambda bi, i: (bi, i, 0)),
        out_shape=jax.ShapeDtypeStruct((b, r, n), BF16),
        compiler_params=_params(("arbitrary", "arbitrary")),
        name="qkv",
    )(x, g, mod, w, cos, sin)


def _rope_tables(seq, ctx_len):
    t = jnp.arange(seq, dtype=jnp.int32)
    row = (t // GRID_W).astype(F32)
    col = (t % GRID_W).astype(F32)
    inv = ROPE_BASE ** (-jnp.arange(ROPE_AXIS_PAIRS, dtype=F32) / ROPE_AXIS_PAIRS)
    ang_r = row[:, None] * inv[None, :]
    ang_c = col[:, None] * inv[None, :]
    ang = jnp.concatenate([ang_r, ang_r, ang_c, ang_c], axis=-1)
    sign = jnp.tile(jnp.concatenate([-jnp.ones((ROPE_AXIS_PAIRS,), F32), jnp.ones((ROPE_AXIS_PAIRS,), F32)]), 2)
    cos = jnp.concatenate([jnp.ones((ctx_len, HEAD_DIM), F32), jnp.cos(ang)], axis=0)
    sin = jnp.concatenate([jnp.zeros((ctx_len, HEAD_DIM), F32), jnp.sin(ang) * sign[None, :]], axis=0)
    cos = jnp.tile(cos, (1, LANES // HEAD_DIM))
    sin = jnp.tile(sin, (1, LANES // HEAD_DIM))
    return jnp.concatenate([cos, cos * Q_SCALE], axis=1), jnp.concatenate([sin, sin * Q_SCALE], axis=1)


def _store_with_ones(dst_ref, v):
    dst_ref[:, 0:LANES] = v
    dst_ref[:, LANES:2 * LANES] = jnp.ones(v.shape, v.dtype)


def _head_masks(rows):
    lane = lax.broadcasted_iota(jnp.int32, (rows, LANES), 1)
    lo = lane < HEAD_DIM
    return lo, jnp.logical_not(lo)


def _na_group_geometry(g, grid_rows):
    u0 = min(max(g * NA_GROUP - NA_WIN_R // 2, 0), grid_rows - NA_UNION)
    idx = []
    for a in range(NA_GROUP):
        r = g * NA_GROUP + a
        r0 = min(max(r - NA_WIN_R // 2, 0), grid_rows - NA_WIN_R)
        idx.append([u0 + y - r + NA_WIN_R - 1 if r0 <= u0 + y < r0 + NA_WIN_R else 2 * NA_WIN_R - 1
                    for y in range(NA_UNION)])
    return u0, idx


def _na_kernel(q_ref, k_ref, v_ref, bt_ref, o_ref, fb_ref, v1_ref, nb_a, cx_a, nb_b, cx_b, *, ctx_len, grid_rows, need_ctx):
    n_groups = grid_rows // NA_GROUP

    @pl.when(pl.program_id(1) == 0)
    def _():
        for j in range(2):
            for pattern, g in enumerate((0, 1, n_groups - 1)):
                _, idx = _na_group_geometry(g, grid_rows)
                for a in range(NA_GROUP):
                    for p in range(NA_UNION // 2):
                        fb_ref[j, pattern, a * GRID_W:(a + 1) * GRID_W, 2 * p * GRID_W:(2 * p + 2) * GRID_W] = (
                            jnp.concatenate([bt_ref[j, idx[a][2 * p]], bt_ref[j, idx[a][2 * p + 1]]], axis=-1))

    _store_with_ones(v1_ref, v_ref[...])
    kc = k_ref[0:ctx_len, :]
    vc = v1_ref[0:ctx_len, :]

    def normalised(r):
        return r[:, 0:LANES] / r[:, LANES:2 * LANES]

    if need_ctx:
        qc = q_ref[0:ctx_len, :]
        lo, hi = _head_masks(ctx_len)
        outs = []
        for m in (lo, hi):
            s = _dot_nt(jnp.where(m, qc, 0), kc)
            e = jnp.exp2(s - jnp.max(s, axis=-1, keepdims=True))
            outs.append(normalised(_dot(e.astype(BF16), vc)))
        o_ref[0:ctx_len, :] = jnp.where(lo, outs[0], outs[1]).astype(o_ref.dtype)
    else:
        o_ref[0:ctx_len, :] = jnp.zeros((ctx_len, LANES), o_ref.dtype)

    gq = NA_GROUP * GRID_W
    uk = NA_UNION * GRID_W
    lo, hi = _head_masks(gq)

    def rows_of(g):
        u0 = jnp.clip(g * NA_GROUP - NA_WIN_R // 2, 0, grid_rows - NA_UNION)
        return pl.multiple_of(ctx_len + g * gq, gq), pl.multiple_of(ctx_len + u0 * GRID_W, GRID_W)

    def scores(g, nb_ref, cx_ref):
        q0, w0 = rows_of(g)
        pattern = jnp.where(g == 0, 0, jnp.where(g == n_groups - 1, 2, 1))
        q_g = q_ref[pl.ds(q0, gq), :]
        k_u = k_ref[pl.ds(w0, uk), :]
        for j, m in enumerate((lo, hi)):
            qj = jnp.where(m, q_g, 0)
            nb_ref[j] = _dot_nt(qj, k_u) + fb_ref[j, pattern]
            cx_ref[j] = _dot_nt(qj, kc)

    def finish(g, nb_ref, cx_ref):
        q0, w0 = rows_of(g)
        v_u = v1_ref[pl.ds(w0, uk), :]
        outs = []
        for j in range(2):
            s_nb = nb_ref[j]
            s_cx = cx_ref[j]
            mx = jnp.maximum(jnp.max(s_nb, axis=-1, keepdims=True), jnp.max(s_cx, axis=-1, keepdims=True))
            e_nb = jnp.exp2(s_nb - mx).astype(BF16)
            e_cx = jnp.exp2(s_cx - mx).astype(BF16)
            outs.append(normalised(_dot(e_cx, vc) + _dot(e_nb, v_u)))
        o_ref[pl.ds(q0, gq), :] = jnp.where(lo, outs[0], outs[1]).astype(o_ref.dtype)

    scores(0, nb_a, cx_a)

    def pair_body(t, carry):
        g = 2 * t
        scores(g + 1, nb_b, cx_b)
        finish(g, nb_a, cx_a)
        scores(g + 2, nb_a, cx_a)
        finish(g + 1, nb_b, cx_b)
        return carry

    lax.fori_loop(0, n_groups // 2 - 1, pair_body, 0)
    scores(n_groups - 1, nb_b, cx_b)
    finish(n_groups - 2, nb_a, cx_a)
    finish(n_groups - 1, nb_b, cx_b)


def _na_bias_table(rpb, grid_rows):
    n_groups = grid_rows // NA_GROUP
    assert grid_rows % (2 * NA_GROUP) == 0 and n_groups >= 3 and grid_rows >= NA_UNION
    for g in range(2, n_groups - 1):
        u1, idx1 = _na_group_geometry(1, grid_rows)
        ug, idxg = _na_group_geometry(g, grid_rows)
        assert idxg == idx1 and ug - g * NA_GROUP == u1 - NA_GROUP
    qcol = jnp.arange(GRID_W)
    kcol = jnp.arange(GRID_W)
    c0 = jnp.clip(qcol - NA_WIN_C // 2, 0, GRID_W - NA_WIN_C)
    col_ok = (kcol[None, :] >= c0[:, None]) & (kcol[None, :] < c0[:, None] + NA_WIN_C)
    rel = kcol[None, :] - qcol[:, None] + NA_WIN_C - 1
    onehot = ((rel[:, :, None] == jnp.arange(2 * NA_WIN_C - 1)[None, None, :]) & col_ok[:, :, None]).astype(F32)
    bt = jnp.einsum('hyd,qkd->hyqk', rpb.astype(F32), onehot, precision=lax.Precision.HIGHEST)
    bt = jnp.where(col_ok[None, None], bt * LOG2E, NEG_INF)
    return jnp.pad(bt, ((0, 0), (0, 1), (0, 0), (0, 0)), constant_values=NEG_INF)


def _na_attention(qkv, rpb, ctx_len, need_ctx):
    b, r, _ = qkv.shape
    grid_rows = (r - ctx_len) // GRID_W
    nblk = D_MODEL // LANES
    bt = _na_bias_table(rpb, grid_rows)
    kern = functools.partial(_na_kernel, ctx_len=ctx_len, grid_rows=grid_rows, need_ctx=need_ctx)
    return pl.pallas_call(
        kern,
        grid=(nblk, b),
        in_specs=[
            pl.BlockSpec((None, r, LANES), lambda hp, bi: (bi, 0, hp)),
            pl.BlockSpec((None, r, LANES), lambda hp, bi: (bi, 0, nblk + hp)),
            pl.BlockSpec((None, r, LANES), lambda hp, bi: (bi, 0, 2 * nblk + hp)),
            pl.BlockSpec((2, 2 * NA_WIN_R, GRID_W, GRID_W), lambda hp, bi: (hp, 0, 0, 0)),
        ],
        out_specs=pl.BlockSpec((None, r, LANES), lambda hp, bi: (bi, 0, hp)),
        out_shape=jax.ShapeDtypeStruct((b, r, D_MODEL), BF16),
        scratch_shapes=[pltpu.VMEM((2, 3, NA_GROUP * GRID_W, NA_UNION * GRID_W), F32), pltpu.VMEM((r, 2 * LANES), BF16)]
        + [pltpu.VMEM((2, NA_GROUP * GRID_W, NA_UNION * GRID_W), F32),
           pltpu.VMEM((2, NA_GROUP * GRID_W, ctx_len), F32)] * 2,
        compiler_params=_params(("arbitrary", "arbitrary")),
        name="na_attn",
    )(qkv, qkv, qkv, bt)


def _da_kernel(q_ref, k_ref, v_ref, lam_ref, g_ref, o_ref, v1_ref, s_a, s_b, *, ctx_len, lam_init):
    lp = lam_ref[...]
    lam = (jnp.exp(jnp.sum(lp[0:1] * lp[1:2], axis=-1, keepdims=True))
           - jnp.exp(jnp.sum(lp[2:3] * lp[3:4], axis=-1, keepdims=True)) + lam_init)
    tq = ROW_TILE
    n_tiles = (q_ref.shape[0] - ctx_len) // tq
    lo, hi = _head_masks(tq)

    _store_with_ones(v1_ref, v_ref[...])

    def finish(scores, n_keys, row0, rows):
        ps = []
        for s in scores:
            e = jnp.exp2(s - jnp.max(s, axis=-1, keepdims=True))
            r = _dot(e.astype(BF16), v1_ref[0:n_keys, :])
            ps.append(r[:, 0:LANES] / r[:, LANES:2 * LANES])
        o = ps[0] - lam * ps[1]
        o_ref[pl.ds(row0, rows), :] = (_rms(o) * g_ref[...] * (1.0 - lam_init)).astype(o_ref.dtype)

    qc = q_ref[0:ctx_len, :]
    lo_c, hi_c = _head_masks(ctx_len)
    finish([_dot_nt(jnp.where(m, qc, 0), k_ref[0:ctx_len, :]) for m in (lo_c, hi_c)], ctx_len, 0, ctx_len)

    def row_of(t):
        return pl.multiple_of(ctx_len + t * tq, tq)

    def scores(t, s_ref):
        q = q_ref[pl.ds(row_of(t), tq), :]
        for j, m in enumerate((lo, hi)):
            s_ref[j] = _dot_nt(jnp.where(m, q, 0), k_ref[...])

    scores(0, s_a)

    def pair_body(p, carry):
        t = 2 * p
        scores(t + 1, s_b)
        finish([s_a[0], s_a[1]], k_ref.shape[0], row_of(t), tq)
        scores(t + 2, s_a)
        finish([s_b[0], s_b[1]], k_ref.shape[0], row_of(t + 1), tq)
        return carry

    lax.fori_loop(0, n_tiles // 2 - 1, pair_body, 0)
    scores(n_tiles - 1, s_b)
    finish([s_a[0], s_a[1]], k_ref.shape[0], row_of(n_tiles - 2), tq)
    finish([s_b[0], s_b[1]], k_ref.shape[0], row_of(n_tiles - 1), tq)


def _da_attention(qkv, lam_params, subln_g, layer_idx, ctx_len):
    b, r, _ = qkv.shape
    assert (r - ctx_len) % (2 * ROW_TILE) == 0
    nblk = D_MODEL // LANES
    lam_init = 0.8 - 0.6 * math.exp(-0.3 * layer_idx)
    kern = functools.partial(_da_kernel, ctx_len=ctx_len, lam_init=lam_init)
    return pl.pallas_call(
        kern,
        grid=(nblk, b),
        in_specs=[
            pl.BlockSpec((None, r, LANES), lambda h, bi: (bi, 0, h)),
            pl.BlockSpec((None, r, LANES), lambda h, bi: (bi, 0, nblk + h)),
            pl.BlockSpec((None, r, LANES), lambda h, bi: (bi, 0, 2 * nblk + h)),
            pl.BlockSpec((4, HEAD_DIM), lambda h, bi: (0, 0)),
            pl.BlockSpec((1, 2 * HEAD_DIM), lambda h, bi: (0, 0)),
        ],
        out_specs=pl.BlockSpec((None, r, LANES), lambda h, bi: (bi, 0, h)),
        out_shape=jax.ShapeDtypeStruct((b, r, D_MODEL), BF16),
        scratch_shapes=[pltpu.VMEM((r, 2 * LANES), BF16), pltpu.VMEM((2, ROW_TILE, r), F32), pltpu.VMEM((2, ROW_TILE, r), F32)],
        compiler_params=_params(("arbitrary", "arbitrary")),
        name="da_attn",
    )(qkv, qkv, qkv, lam_params.astype(F32), subln_g.astype(F32).reshape(1, -1))


def _sw_kernel(sink_ref, q_ref, k_ref, v_ref, o_ref, kd_ref, vd_ref, sl_a, sc_a, sl_b, sc_b, *, ctx_len, seq, need_ctx):
    kvp = pl.program_id(0)
    band = Q_BLOCK + 2 * SW_WINDOW
    hpk = SW_Q_HEADS // SW_KV_HEADS

    k = k_ref[...]
    v = v_ref[...]
    for hh in range(2):
        sl = slice(hh * HEAD_DIM, (hh + 1) * HEAD_DIM)
        kd_ref[hh] = jnp.concatenate([k[:, sl], k[:, sl]], axis=-1)
        _store_with_ones(vd_ref.at[hh], jnp.concatenate([v[:, sl], v[:, sl]], axis=-1))

    n_blk = 2 * (hpk // 2)

    def stacked_q(row0, rows, c):
        lo, hi = _head_masks(rows)
        qv = q_ref[pl.ds(row0, rows), c * LANES:(c + 1) * LANES]
        return jnp.concatenate([jnp.where(lo, qv, 0), jnp.where(hi, qv, 0)], axis=0)

    def per_head(rows, c, col, fn):
        sinks = [sink_ref[kvp * 2 * hpk + 2 * c + j] * LOG2E for j in range(2)]
        return jnp.concatenate([fn(col[j * rows:(j + 1) * rows], sinks[j]) for j in range(2)], axis=0)

    def write_out(row0, rows, c, o):
        lo, _ = _head_masks(rows)
        o_ref[pl.ds(row0, rows), c * LANES:(c + 1) * LANES] = jnp.where(lo, o[0:rows], o[rows:2 * rows]).astype(o_ref.dtype)

    def softmax_pv(rows, c, parts):
        row_max = functools.reduce(jnp.maximum, [jnp.max(s, axis=-1, keepdims=True) for s, _ in parts])
        mx = per_head(rows, c, row_max, lambda m, sink: jnp.maximum(m, sink))
        r = sum(_dot(jnp.exp2(s - mx).astype(BF16), val) for s, val in parts)
        den = r[:, LANES:2 * LANES] + per_head(rows, c, mx, lambda m, sink: jnp.exp2(sink - m))
        return r[:, 0:LANES] / den

    if need_ctx:
        for c in range(n_blk):
            hh = c // (hpk // 2)
            s = _dot_nt(stacked_q(0, ctx_len, c), kd_ref[hh, 0:ctx_len, :])
            write_out(0, ctx_len, c, softmax_pv(ctx_len, c, [(s, vd_ref[hh, 0:ctx_len, :])]))
    else:
        o_ref[0:ctx_len, :] = jnp.zeros((ctx_len, o_ref.shape[-1]), o_ref.dtype)

    n_steps = seq // Q_BLOCK

    def rows_of(n):
        start = n * Q_BLOCK
        b0 = jnp.clip(start - SW_WINDOW, 0, seq - band)
        return start, b0, pl.multiple_of(ctx_len + start, Q_BLOCK), pl.multiple_of(ctx_len + b0, Q_BLOCK)

    def scores(n, sl_ref, sc_ref):
        start, b0, qrow, brow = rows_of(n)
        qpos = start + (lax.broadcasted_iota(jnp.int32, (2 * Q_BLOCK, band), 0) & (Q_BLOCK - 1))
        kpos = b0 + lax.broadcasted_iota(jnp.int32, (2 * Q_BLOCK, band), 1)
        in_band = jnp.abs(kpos - qpos) <= SW_WINDOW
        for c in range(n_blk):
            hh = c // (hpk // 2)
            qs = stacked_q(qrow, Q_BLOCK, c)
            sl_ref[c] = jnp.where(in_band, _dot_nt(qs, kd_ref[hh, pl.ds(brow, band), :]), NEG_INF)
            sc_ref[c] = _dot_nt(qs, kd_ref[hh, 0:ctx_len, :])

    def finish(n, sl_ref, sc_ref):
        _, _, qrow, brow = rows_of(n)
        for c in range(n_blk):
            hh = c // (hpk // 2)
            o = softmax_pv(Q_BLOCK, c, [(sc_ref[c], vd_ref[hh, 0:ctx_len, :]),
                                        (sl_ref[c], vd_ref[hh, pl.ds(brow, band), :])])
            write_out(qrow, Q_BLOCK, c, o)

    scores(0, sl_a, sc_a)

    def pair_body(t, carry):
        n = 2 * t
        scores(n + 1, sl_b, sc_b)
        finish(n, sl_a, sc_a)
        scores(n + 2, sl_a, sc_a)
        finish(n + 1, sl_b, sc_b)
        return carry

    lax.fori_loop(0, n_steps // 2 - 1, pair_body, 0)
    scores(n_steps - 1, sl_b, sc_b)
    finish(n_steps - 2, sl_a, sc_a)
    finish(n_steps - 1, sl_b, sc_b)


def _sw_attention(qkv, sinks, ctx_len, need_ctx):
    b, r, _ = qkv.shape
    seq = r - ctx_len
    assert seq % (2 * Q_BLOCK) == 0
    nq_blk = SW_Q_HEADS * HEAD_DIM // LANES
    slab = 2 * (SW_Q_HEADS // SW_KV_HEADS) * HEAD_DIM
    kern = functools.partial(_sw_kernel, ctx_len=ctx_len, seq=seq, need_ctx=need_ctx)
    n_kv_blk = SW_KV_HEADS * HEAD_DIM // LANES
    return pl.pallas_call(
        kern,
        grid=(n_kv_blk, b),
        in_specs=[
            pl.BlockSpec(memory_space=pltpu.SMEM),
            pl.BlockSpec((None, r, slab), lambda kp, bi: (bi, 0, kp)),
            pl.BlockSpec((None, r, LANES), lambda kp, bi: (bi, 0, nq_blk + kp)),
            pl.BlockSpec((None, r, LANES), lambda kp, bi: (bi, 0, nq_blk + n_kv_blk + kp)),
        ],
        out_specs=pl.BlockSpec((None, r, slab), lambda kp, bi: (bi, 0, kp)),
        out_shape=jax.ShapeDtypeStruct((b, r, D_MODEL), BF16),
        scratch_shapes=[pltpu.VMEM((2, r, LANES), BF16), pltpu.VMEM((2, r, 2 * LANES), BF16)]
        + [pltpu.VMEM((slab // LANES, 2 * Q_BLOCK, Q_BLOCK + 2 * SW_WINDOW), F32),
           pltpu.VMEM((slab // LANES, 2 * Q_BLOCK, ctx_len), F32)] * 2,
        compiler_params=_params(("arbitrary", "arbitrary")),
        name="sw_attn",
    )(sinks.astype(F32), qkv, qkv, qkv)


POST_SUBTILES = 2


def _post_kernel(o_ref, x_ref, wo_ref, mod_ref, g_ref, wr_ref, br_ref, xn_ref, h_ref, rt_ref, *, ctx_tiles, tiles_per_batch):
    for sub in range(POST_SUBTILES):
        tile = pl.program_id(0) * POST_SUBTILES + sub
        row = _mod_row(tile // tiles_per_batch, tile % tiles_per_batch, ctx_tiles)
        rows = slice(sub * ROW_TILE, (sub + 1) * ROW_TILE)
        _post_tile(o_ref.at[rows], x_ref.at[rows], wo_ref, mod_ref, g_ref, wr_ref, br_ref, xn_ref.at[rows],
                   h_ref.at[sub * ROW_TILE * SUBLANES:(sub + 1) * ROW_TILE * SUBLANES], rt_ref.at[rows], row)


def _post_tile(o_ref, x_ref, wo_ref, mod_ref, g_ref, wr_ref, br_ref, xn_ref, h_ref, rt_ref, row):
    d = x_ref.shape[-1]
    tm = x_ref.shape[0]
    gate1 = mod_ref[pl.ds(row, 1), 2 * d:3 * d]
    shift2 = mod_ref[pl.ds(row, 1), 3 * d:4 * d]
    scale2 = mod_ref[pl.ds(row, 1), 4 * d:5 * d]
    xn = x_ref[...] + gate1 * _dot(o_ref[...], wo_ref[...])
    xn_ref[...] = xn
    h = (_rms(xn) * g_ref[...]) * (1.0 + scale2) + shift2
    for j in range(SUBLANES):
        h_ref[pl.ds(j, tm, stride=SUBLANES), :] = h[:, j * LANES:(j + 1) * LANES]

    h_hi = h.astype(BF16)
    h_lo = (h - h_hi.astype(F32)).astype(BF16)
    hi_terms = _dot(h_hi, wr_ref[...])
    logits = (hi_terms[:, 0:LANES] + hi_terms[:, LANES:2 * LANES] + _dot(h_lo, wr_ref[:, 0:LANES])) + br_ref[...]

    lt = logits.T
    sub = lax.broadcasted_iota(jnp.int32, (SUBLANES, tm), 0).astype(F32)

    def first_argmax(vals, mx):
        return jnp.min(jnp.where(vals == mx, sub, float(SUBLANES)), axis=0, keepdims=True)

    gl = jnp.where(sub < N_GROUPS, lt[0:SUBLANES, :], -jnp.inf)
    ge = jnp.exp(gl - jnp.max(gl, axis=0, keepdims=True))
    gp = ge / jnp.sum(ge, axis=0, keepdims=True)
    g_p = jnp.max(gp, axis=0, keepdims=True)
    g_idx = first_argmax(gp, g_p)
    el = lt[SUBLANES:2 * SUBLANES, :]
    for g in range(1, N_GROUPS):
        el = jnp.where(g_idx == g, lt[(g + 1) * SUBLANES:(g + 2) * SUBLANES, :], el)
    v1 = jnp.max(el, axis=0, keepdims=True)
    i1 = first_argmax(el, v1)
    el2 = jnp.where(sub == i1, -jnp.inf, el)
    v2 = jnp.max(el2, axis=0, keepdims=True)
    i2 = first_argmax(el2, v2)
    e2 = jnp.exp(v2 - v1)
    w1 = g_p * (1.0 / (1.0 + e2))
    w2 = g_p * (e2 / (1.0 + e2))
    base = g_idx * EXPERTS_PER_GROUP
    rt_t = jnp.concatenate([base + i1, base + i2, w1, w2, jnp.zeros((LANES - 4, tm), F32)], axis=0)
    rt_ref[...] = rt_t.T[:, 0:SUBLANES]


def _post(o, x, wo, mod, g, wr, br, ctx_len):
    wr_hi = wr.astype(BF16)
    wr_cat = jnp.concatenate([wr_hi, (wr - wr_hi.astype(F32)).astype(BF16)], axis=1)
    b, r, d = x.shape
    t = b * r
    step_rows = POST_SUBTILES * ROW_TILE
    assert t % step_rows == 0
    kern = functools.partial(_post_kernel, ctx_tiles=ctx_len // ROW_TILE, tiles_per_batch=r // ROW_TILE)
    const = lambda i: (0, 0)
    rows = lambda i: (i, 0)
    xn, h_rows, rt = pl.pallas_call(
        kern,
        grid=(t // step_rows,),
        in_specs=[
            pl.BlockSpec((step_rows, d), rows),
            pl.BlockSpec((step_rows, d), rows),
            pl.BlockSpec((d, d), const),
            pl.BlockSpec((MOD_ROWS, 6 * d), const),
            pl.BlockSpec((1, d), const),
            pl.BlockSpec((d, 2 * LANES), const),
            pl.BlockSpec((1, LANES), const),
        ],
        out_specs=[
            pl.BlockSpec((step_rows, d), rows),
            pl.BlockSpec((step_rows * SUBLANES, LANES), rows),
            pl.BlockSpec((step_rows, SUBLANES), rows),
        ],
        out_shape=[
            jax.ShapeDtypeStruct((t, d), F32),
            jax.ShapeDtypeStruct((t * SUBLANES, LANES), F32),
            jax.ShapeDtypeStruct((t, SUBLANES), F32),
        ],
        compiler_params=_params(("arbitrary",)),
        name="post_attn",
    )(o.reshape(t, d), x.reshape(t, d), wo, mod, g, wr_cat, br)
    return xn.reshape(b, r, d), h_rows, rt


DISPATCH_TOKENS = 512


def _row(ref, idx):
    return ref.at[pl.ds(pl.multiple_of(idx * SUBLANES, SUBLANES), SUBLANES), :]


def _dispatch_kernel(dest_ref, h_ref, xs_hbm, sem):
    def issue(t, carry):
        src = _row(h_ref, t)
        for k in range(2):
            pltpu.make_async_copy(src, _row(xs_hbm, dest_ref[0, 0, 2 * t + k]), sem).start(priority=k)
        return carry

    lax.fori_loop(0, DISPATCH_TOKENS, issue, 0)
    for _ in range(2):
        pltpu.make_async_copy(h_ref, xs_hbm.at[pl.ds(0, DISPATCH_TOKENS * SUBLANES), :], sem).wait()


def _dispatch(h_rows, dest):
    t = dest.shape[0] // 2
    steps = t // DISPATCH_TOKENS
    return pl.pallas_call(
        _dispatch_kernel,
        grid=(steps,),
        in_specs=[
            pl.BlockSpec((1, 1, 2 * DISPATCH_TOKENS), lambda i: (i, 0, 0), memory_space=pltpu.SMEM),
            pl.BlockSpec((DISPATCH_TOKENS * SUBLANES, LANES), lambda i: (i, 0)),
        ],
        out_specs=pl.BlockSpec(memory_space=pl.ANY),
        out_shape=jax.ShapeDtypeStruct((2 * t * SUBLANES, LANES), F32),
        scratch_shapes=[pltpu.SemaphoreType.DMA(())],
        compiler_params=_params(("arbitrary",)),
        name="moe_dispatch",
    )(dest.reshape(steps, 1, 2 * DISPATCH_TOKENS), h_rows)


def _experts_kernel(ve_ref, vt_ref, lo_ref, hi_ref, first_ref, valid_ref, change_ref, next_ref, slot_ref,
                    xs_ref, wg_hbm, wu_hbm, wd_hbm, ys_ref, wgu_s, wd_s, acc_s, stg_g, stg_u, stg_d, sem, *, layer):
    v = pl.program_id(0)
    tm = ROW_TILE
    f = stg_g.shape[-1]
    streams = ((wg_hbm, stg_g), (wu_hbm, stg_u), (wd_hbm, stg_d))

    def fetch(expert, slot):
        for src, dst in streams:
            pltpu.make_async_copy(src.at[layer, expert], dst.at[slot], sem.at[slot]).start()

    @pl.when(v == 0)
    def _():
        acc_s[...] = jnp.zeros(acc_s.shape, acc_s.dtype)
        fetch(ve_ref[0], 0)

    @pl.when(valid_ref[v] == 1)
    def _():
        @pl.when(change_ref[v] == 1)
        def _():
            slot = slot_ref[v]
            for src, dst in streams:
                pltpu.make_async_copy(src.at[layer, 0], dst.at[slot], sem.at[slot]).wait()
            wgu_s[:, 0:f] = stg_g[slot].astype(BF16)
            wgu_s[:, f:2 * f] = stg_u[slot].astype(BF16)
            wd_s[...] = stg_d[slot].astype(BF16)

            @pl.when(next_ref[v] >= 0)
            def _():
                fetch(next_ref[v], 1 - slot)

        x = jnp.concatenate([xs_ref[pl.ds(j, tm, stride=SUBLANES), :] for j in range(SUBLANES)], axis=-1)
        au = _dot(x.astype(BF16), wgu_s[...])
        rows = vt_ref[v] * tm + lax.broadcasted_iota(jnp.int32, (tm, 1), 0)
        mine = (rows >= lo_ref[v]) & (rows < hi_ref[v])
        mid = jnp.where(mine, jax.nn.silu(au[:, 0:f]) * au[:, f:2 * f], 0.0)
        y = _dot(mid.astype(BF16), wd_s[...])
        y = jnp.where(first_ref[v] == 1, y, acc_s[...] + y)
        acc_s[...] = y
        for j in range(SUBLANES):
            ys_ref[pl.ds(j, tm, stride=SUBLANES), :] = y[:, j * LANES:(j + 1) * LANES]


def _experts(xs, sched, w_gate, w_up, w_down, layer):
    rows = xs.shape[0] // SUBLANES
    d, f = w_gate.shape[-2:]
    n_visits = sched[0].shape[0]
    blk = (ROW_TILE * SUBLANES, LANES)
    tile_map = lambda v, ve, vt, *_: (vt[v], 0)
    grid_spec = pltpu.PrefetchScalarGridSpec(
        num_scalar_prefetch=len(sched),
        grid=(n_visits,),
        in_specs=[pl.BlockSpec(blk, tile_map)] + [pl.BlockSpec(memory_space=pl.ANY)] * 3,
        out_specs=pl.BlockSpec(blk, tile_map),
        scratch_shapes=[pltpu.VMEM((d, 2 * f), BF16), pltpu.VMEM((f, d), BF16), pltpu.VMEM((ROW_TILE, d), F32),
                        pltpu.VMEM((2, d, f), F32), pltpu.VMEM((2, d, f), F32), pltpu.VMEM((2, f, d), F32),
                        pltpu.SemaphoreType.DMA((2,))],
    )
    return pl.pallas_call(
        functools.partial(_experts_kernel, layer=layer),
        grid_spec=grid_spec,
        out_shape=jax.ShapeDtypeStruct((rows * SUBLANES, LANES), F32),
        compiler_params=_params(("arbitrary",)),
        name="moe_experts",
    )(*sched, xs, w_gate, w_up, w_down)


COMBINE_UNROLL = 4
COMBINE_SUBTILES = 2


def _combine_kernel(dest_ref, next_ref, rt_ref, x_ref, mod_ref, g_ref, *rest, ctx_tiles, tiles_per_batch, final, next_qkv, n_sub):
    if final:
        ys_hbm, o_ref, buf, sem = rest
    else:
        qg_ref, qmod_ref, qw_ref = rest[0:3]
        tables = rest[3:3 + 2 * n_sub]
        ys_hbm, o_ref, qkv_ref, buf, sem = rest[3 + 2 * n_sub:]
    tm = ROW_TILE
    tokens = n_sub * tm
    d = x_ref.shape[-1]
    s = pl.program_id(0)
    slot = s % 2
    half_rows = tokens * SUBLANES

    def gather(idx_ref, slot_):
        def issue(t4, carry):
            for u in range(COMBINE_UNROLL):
                t = t4 * COMBINE_UNROLL + u
                for k in range(2):
                    pltpu.make_async_copy(_row(ys_hbm, idx_ref[0, 0, 2 * t + k]),
                                          _row(buf.at[slot_], k * tokens + t), sem.at[slot_]).start(priority=k)
            return carry

        lax.fori_loop(0, tokens // COMBINE_UNROLL, issue, 0)

    @pl.when(s == 0)
    def _():
        gather(dest_ref, 0)

    for parity in range(2):
        @pl.when(jnp.logical_and(s + 1 < pl.num_programs(0), slot == parity))
        def _():
            gather(next_ref, 1 - parity)

    for k in range(2):
        part = pl.ds(k * half_rows, half_rows)
        pltpu.make_async_copy(ys_hbm.at[part, :], buf.at[slot].at[part, :], sem.at[slot]).wait()

    rows = buf.at[slot]
    for sub in range(n_sub):
        tile = s * n_sub + sub
        row = _mod_row(tile // tiles_per_batch, tile % tiles_per_batch, ctx_tiles)
        gate2 = mod_ref[pl.ds(row, 1), 5 * d:6 * d]
        toks = slice(sub * tm, (sub + 1) * tm)
        rt = rt_ref[toks, :]
        w = (rt[:, 2:3], rt[:, 3:4])
        base = sub * tm * SUBLANES
        pieces = []
        for j in range(SUBLANES):
            y = (w[0] * rows[pl.ds(base + j, tm, stride=SUBLANES), :]
                 + w[1] * rows[pl.ds(half_rows + base + j, tm, stride=SUBLANES), :])
            sl = slice(j * LANES, (j + 1) * LANES)
            pieces.append(x_ref[toks, sl] + gate2[:, sl] * y)
        xn = jnp.concatenate(pieces, axis=-1)
        if final:
            o_ref[toks, :] = _rms(xn) * g_ref[...]
        else:
            o_ref[toks, :] = xn
            _qkv_tile(xn, row, qg_ref, qmod_ref, qw_ref, tables[2 * sub], tables[2 * sub + 1], qkv_ref.at[toks], *next_qkv)


def _combine(ys, dest, rt, x, mod, ctx_len, final_g=None, next_layer=None):
    b, r, d = x.shape
    t = b * r
    ct, tpb = ctx_len // ROW_TILE, r // ROW_TILE
    final = final_g is not None
    n_sub = 1 if final else COMBINE_SUBTILES
    step_rows = n_sub * ROW_TILE
    assert t % step_rows == 0
    steps = t // step_rows
    const = lambda i: (0, 0)
    rows = lambda i: (i, 0)
    if final:
        out_rows = t - b * ctx_len
        out_specs = pl.BlockSpec((ROW_TILE, d), lambda i: ((i // tpb) * (tpb - ct) + jnp.maximum(i % tpb - ct, 0), 0))
        out_shape = jax.ShapeDtypeStruct((out_rows, d), F32)
        extra_in, extra_specs, next_qkv = [], [], None
    else:
        out_rows = t
        final_g = jnp.zeros((1, d), F32)
        qg, qmod, qw, cos, sin, rope_cols, q_cols = next_layer
        n = qw.shape[1]
        next_qkv = (rope_cols, q_cols)
        extra_in = [qg, qmod, qw]
        extra_specs = [pl.BlockSpec((1, d), const), pl.BlockSpec((MOD_ROWS, 6 * d), const), pl.BlockSpec((d, n), const)]
        for sub in range(n_sub):
            table_map = lambda i, sub=sub: ((i * n_sub + sub) % tpb, 0)
            extra_in += [cos, sin]
            extra_specs += [pl.BlockSpec((ROW_TILE, 2 * LANES), table_map)] * 2
        out_specs = [pl.BlockSpec((step_rows, d), rows), pl.BlockSpec((step_rows, n), rows)]
        out_shape = [jax.ShapeDtypeStruct((t, d), F32), jax.ShapeDtypeStruct((t, n), BF16)]
    kern = functools.partial(_combine_kernel, ctx_tiles=ct, tiles_per_batch=tpb, final=final, next_qkv=next_qkv, n_sub=n_sub)
    dest3 = dest.reshape(steps, 1, 2 * step_rows)
    out = pl.pallas_call(
        kern,
        grid=(steps,),
        in_specs=[
            pl.BlockSpec((1, 1, 2 * step_rows), lambda i: (i, 0, 0), memory_space=pltpu.SMEM),
            pl.BlockSpec((1, 1, 2 * step_rows), lambda i: (jnp.minimum(i + 1, steps - 1), 0, 0), memory_space=pltpu.SMEM),
            pl.BlockSpec((step_rows, SUBLANES), rows),
            pl.BlockSpec((step_rows, d), rows),
            pl.BlockSpec((MOD_ROWS, 6 * d), const),
            pl.BlockSpec((1, d), const),
        ] + extra_specs + [pl.BlockSpec(memory_space=pl.ANY)],
        out_specs=out_specs,
        out_shape=out_shape,
        scratch_shapes=[pltpu.VMEM((2, 2 * step_rows * SUBLANES, LANES), F32), pltpu.SemaphoreType.DMA((2,))],
        compiler_params=_params(("arbitrary",)),
        name="moe_combine",
    )(dest3, dest3, rt, x.reshape(t, d), mod, final_g, *extra_in, ys)
    if final:
        return out.reshape(b, out_rows // b, d), None
    return out[0].reshape(b, r, d), out[1].reshape(b, r, -1)


def _route_plan(rt, n_tiles):
    eid = rt[:, 0:2].astype(jnp.int32).reshape(-1)
    onehot = (eid[:, None] == jnp.arange(N_EXPERTS, dtype=jnp.int32)[None, :]).astype(jnp.int32)
    csum = jnp.cumsum(onehot, axis=0)
    counts = csum[-1]
    cstart = jnp.cumsum(counts) - counts
    dest = jnp.sum(onehot * (cstart[None, :] + csum - 1), axis=1).astype(jnp.int32)

    tm = ROW_TILE
    first_tile = cstart // tm
    last_tile = (cstart + counts - 1) // tm
    nvis = jnp.where(counts > 0, last_tile - first_tile + 1, 0)
    vend = jnp.cumsum(nvis)
    vstart = vend - nvis
    total = vend[-1]
    n_visits = n_tiles + N_EXPERTS - 1
    v = jnp.arange(n_visits, dtype=jnp.int32)
    valid = v < total
    vc = jnp.minimum(v, total - 1)
    ve = jnp.sum((vc[:, None] >= vend[None, :]).astype(jnp.int32), axis=1)
    oh_e = (ve[:, None] == jnp.arange(N_EXPERTS, dtype=jnp.int32)[None, :]).astype(jnp.int32)
    pick = lambda a: jnp.sum(oh_e * a[None, :], axis=1)
    vt = pick(first_tile) + vc - pick(vstart)
    lo = pick(cstart)
    hi = jnp.where(valid, lo + pick(counts), lo)
    first = jnp.concatenate([jnp.ones((1,), jnp.int32), (vt[1:] != vt[:-1]).astype(jnp.int32)])
    change = jnp.concatenate([jnp.ones((1,), jnp.int32), (ve[1:] != ve[:-1]).astype(jnp.int32)]) * valid
    e_ids = jnp.arange(N_EXPERTS, dtype=jnp.int32)
    used = jnp.where(counts > 0, e_ids, N_EXPERTS)
    following = jnp.concatenate([lax.cummin(used, reverse=True)[1:], jnp.full((1,), N_EXPERTS, jnp.int32)])
    nxt = pick(jnp.where(following < N_EXPERTS, following, -1))
    slot = (jnp.cumsum(change) - 1) % 2
    sched = tuple(a.astype(jnp.int32) for a in (ve, vt, lo, hi, first, valid, change, nxt, slot))
    return dest, sched


def kernel(x, c, ctx, c_ctx, ada_w, ada_b, norm_mix_g, norm_ffn_g, final_norm_g, na_w_qkv, na_rpb, na_w_o, da_w_qkv, da_lambda, da_subln_g, da_w_o, sw_w_qkv, sw_sinks, sw_w_o, moe_w_grp, moe_b_grp, moe_w_exp, moe_b_exp, moe_w_gate, moe_w_up, moe_w_down):
    b, seq, d = x.shape
    ctx_len = ctx.shape[1]
    depth = ada_w.shape[0]
    assert d == D_MODEL and ctx_len % ROW_TILE == 0 and seq % ROW_TILE == 0 and b <= MOD_ROWS // 2
    assert EXPERTS_PER_GROUP == SUBLANES and N_GROUPS <= SUBLANES
    assert (b * (ctx_len + seq)) % DISPATCH_TOKENS == 0

    half = MOD_ROWS // 2
    cond = jnp.concatenate([jnp.pad(c, ((0, half - b), (0, 0))), jnp.pad(c_ctx[None, :], ((0, half - 1), (0, 0)))], axis=0)
    mod_all = _adaln(cond, ada_w, ada_b)
    cos, sin = _rope_tables(seq, ctx_len)
    stream = jnp.concatenate([ctx, x], axis=1)
    n_tiles = 2 * b * (ctx_len + seq) // ROW_TILE

    def qkv_args(i):
        kind, j = i % N_MIXERS, i // N_MIXERS
        if kind == 0:
            w, rope_cols, q_cols = na_w_qkv[j], 0, NA_HEADS * HEAD_DIM
        elif kind == 1:
            nq = 2 * DA_HEADS * HEAD_DIM
            w, rope_cols, q_cols = da_w_qkv[j], 2 * nq, nq
        else:
            w, rope_cols, q_cols = sw_w_qkv[j], (SW_Q_HEADS + SW_KV_HEADS) * HEAD_DIM, SW_Q_HEADS * HEAD_DIM
        return norm_mix_g[i].reshape(1, d), mod_all[i], w.astype(BF16), cos, sin, rope_cols, q_cols

    qkv = _qkv(stream, *qkv_args(0), ctx_len)
    for i in range(depth):
        need_ctx = i < depth - 1
        kind, j = i % N_MIXERS, i // N_MIXERS
        mod = mod_all[i]
        if kind == 0:
            o = _na_attention(qkv, na_rpb[j], ctx_len, need_ctx)
            wo = na_w_o[j]
        elif kind == 1:
            o = _da_attention(qkv, da_lambda[j], da_subln_g[j], i, ctx_len)
            wo = da_w_o[j]
        else:
            o = _sw_attention(qkv, sw_sinks[j], ctx_len, need_ctx)
            wo = sw_w_o[j]

        gpad, pad = SUBLANES - N_GROUPS, LANES - SUBLANES - N_EXPERTS
        wr = jnp.concatenate([jnp.pad(moe_w_grp[i].astype(F32), ((0, 0), (0, gpad))),
                              jnp.pad(moe_w_exp[i].astype(F32), ((0, 0), (0, pad)))], axis=1)
        br = jnp.concatenate([jnp.pad(moe_b_grp[i].astype(F32), (0, gpad)),
                              jnp.pad(moe_b_exp[i].astype(F32), (0, pad))]).reshape(1, LANES)
        stream, h_rows, rt = _post(o, stream, wo.astype(BF16), mod, norm_ffn_g[i].reshape(1, d), wr, br, ctx_len)
        dest, sched = _route_plan(rt, n_tiles)
        xs = _dispatch(h_rows, dest)
        ys = _experts(xs, sched, moe_w_gate, moe_w_up, moe_w_down, i)
        if i == depth - 1:
            stream, qkv = _combine(ys, dest, rt, stream, mod, ctx_len, final_g=final_norm_g.reshape(1, d).astype(F32))
        else:
            stream, qkv = _combine(ys, dest, rt, stream, mod, ctx_len, next_layer=qkv_args(i + 1))

    return stream
```

```python
import functools
import math

import jax
import jax.numpy as jnp
from jax import lax
from jax.experimental import pallas as pl
from jax.experimental.pallas import tpu as pltpu

F32 = jnp.float32
BF16 = jnp.bfloat16

D_MODEL = 1024
HEAD_DIM = 64
GRID_W = 64
ROPE_AXIS_PAIRS = HEAD_DIM // 4
ROPE_BASE = 10000.0
NA_HEADS = D_MODEL // HEAD_DIM
NA_WIN_R = 8
NA_WIN_C = 16
NA_GROUP = 4
NA_UNION = NA_GROUP + NA_WIN_R
DA_HEADS = D_MODEL // (2 * HEAD_DIM)
SW_Q_HEADS = D_MODEL // HEAD_DIM
SW_KV_HEADS = 4
SW_WINDOW = 128
Q_BLOCK = 128
N_GROUPS = 4
EXPERTS_PER_GROUP = 8
N_EXPERTS = N_GROUPS * EXPERTS_PER_GROUP
D_EXPERT = 512
NORM_EPS = 1e-6
NEG_INF = -1e30
N_MIXERS = 3

LANES = 128
SUBLANES = 8
ROW_TILE = 256
MOD_ROWS = 16
VMEM_LIMIT = 56 * 1024 * 1024

LOG2E = math.log2(math.e)
Q_SCALE = HEAD_DIM ** -0.5 * LOG2E

_NT = (((1,), (1,)), ((), ()))


def _dot(a, b, **kw):
    return jnp.dot(a, b, preferred_element_type=F32, **kw)


def _dot_nt(a, b):
    return lax.dot_general(a, b, _NT, preferred_element_type=F32)


def _params(sem, vmem=VMEM_LIMIT):
    return pltpu.CompilerParams(dimension_semantics=sem, vmem_limit_bytes=vmem)


def _mod_row(b, tile, ctx_tiles):
    return jnp.where(tile < ctx_tiles, MOD_ROWS // 2, b)


def _rms(x):
    return x * lax.rsqrt(jnp.mean(x * x, axis=-1, keepdims=True) + NORM_EPS)


def _adaln_kernel(c_ref, w_ref, b_ref, o_ref):
    s = jax.nn.silu(c_ref[...])
    o_ref[...] = _dot(s, w_ref[...], precision=lax.Precision.HIGHEST) + b_ref[...]


def _adaln(cond, ada_w, ada_b):
    depth, d, n6 = ada_w.shape
    tn = 1536
    return pl.pallas_call(
        _adaln_kernel,
        grid=(depth, n6 // tn),
        in_specs=[
            pl.BlockSpec((MOD_ROWS, d), lambda l, n: (0, 0)),
            pl.BlockSpec((None, d, tn), lambda l, n: (l, 0, n)),
            pl.BlockSpec((None, 1, tn), lambda l, n: (l, 0, n)),
        ],
        out_specs=pl.BlockSpec((None, MOD_ROWS, tn), lambda l, n: (l, 0, n)),
        out_shape=jax.ShapeDtypeStruct((depth, MOD_ROWS, n6), F32),
        compiler_params=_params(("arbitrary", "arbitrary")),
        name="adaln",
    )(cond, ada_w, ada_b.reshape(depth, 1, n6))


def _qkv_tile(x, row, g_ref, mod_ref, w_ref, cos_ref, sin_ref, o_ref, rope_cols, q_cols):
    d = x.shape[-1]
    shift = mod_ref[pl.ds(row, 1), 0:d]
    scale = mod_ref[pl.ds(row, 1), d:2 * d]
    h = (_rms(x) * g_ref[...]) * (1.0 + scale) + shift
    acc = _dot(h.astype(BF16), w_ref[...])
    n = acc.shape[-1]
    if rope_cols:
        tm = acc.shape[0]
        first_half = (lax.broadcasted_iota(jnp.int32, (tm, LANES), 1) % (2 * ROPE_AXIS_PAIRS)) < ROPE_AXIS_PAIRS
        for c in range(rope_cols // LANES):
            tab = slice(LANES, 2 * LANES) if c * LANES < q_cols else slice(0, LANES)
            t = acc[:, c * LANES:(c + 1) * LANES]
            up = pltpu.roll(t, LANES - ROPE_AXIS_PAIRS, 1)
            dn = pltpu.roll(t, ROPE_AXIS_PAIRS, 1)
            o_ref[:, c * LANES:(c + 1) * LANES] = (
                t * cos_ref[:, tab] + jnp.where(first_half, up, dn) * sin_ref[:, tab]).astype(o_ref.dtype)
    else:
        o_ref[:, 0:q_cols] = (acc[:, 0:q_cols] * Q_SCALE).astype(o_ref.dtype)
        rope_cols = q_cols
    if rope_cols < n:
        o_ref[:, rope_cols:] = acc[:, rope_cols:].astype(o_ref.dtype)


def _qkv_kernel(x_ref, g_ref, mod_ref, w_ref, cos_ref, sin_ref, o_ref, *, rope_cols, q_cols, ctx_tiles):
    row = _mod_row(pl.program_id(0), pl.program_id(1), ctx_tiles)
    _qkv_tile(x_ref[...], row, g_ref, mod_ref, w_ref, cos_ref, sin_ref, o_ref, rope_cols, q_cols)


def _qkv(x, g, mod, w, cos, sin, rope_cols, q_cols, ctx_len):
    b, r, d = x.shape
    n = w.shape[1]
    kern = functools.partial(_qkv_kernel, rope_cols=rope_cols, q_cols=q_cols, ctx_tiles=ctx_len // ROW_TILE)
    return pl.pallas_call(
        kern,
        grid=(b, r // ROW_TILE),
        in_specs=[
            pl.BlockSpec((None, ROW_TILE, d), lambda bi, i: (bi, i, 0)),
            pl.BlockSpec((1, d), lambda bi, i: (0, 0)),
            pl.BlockSpec((MOD_ROWS, 6 * d), lambda bi, i: (0, 0)),
            pl.BlockSpec((d, n), lambda bi, i: (0, 0)),
            pl.BlockSpec((ROW_TILE, 2 * LANES), lambda bi, i: (i, 0)),
            pl.BlockSpec((ROW_TILE, 2 * LANES), lambda bi, i: (i, 0)),
        ],
        out_specs=pl.BlockSpec((None, ROW_TILE, n), lambda bi, i: (bi, i, 0)),
        out_shape=jax.ShapeDtypeStruct((b, r, n), BF16),
        compiler_params=_params(("arbitrary", "arbitrary")),
        name="qkv",
    )(x, g, mod, w, cos, sin)


def _rope_tables(seq, ctx_len):
    t = jnp.arange(seq, dtype=jnp.int32)
    row = (t // GRID_W).astype(F32)
    col = (t % GRID_W).astype(F32)
    inv = ROPE_BASE ** (-jnp.arange(ROPE_AXIS_PAIRS, dtype=F32) / ROPE_AXIS_PAIRS)
    ang_r = row[:, None] * inv[None, :]
    ang_c = col[:, None] * inv[None, :]
    ang = jnp.concatenate([ang_r, ang_r, ang_c, ang_c], axis=-1)
    sign = jnp.tile(jnp.concatenate([-jnp.ones((ROPE_AXIS_PAIRS,), F32), jnp.ones((ROPE_AXIS_PAIRS,), F32)]), 2)
    cos = jnp.concatenate([jnp.ones((ctx_len, HEAD_DIM), F32), jnp.cos(ang)], axis=0)
    sin = jnp.concatenate([jnp.zeros((ctx_len, HEAD_DIM), F32), jnp.sin(ang) * sign[None, :]], axis=0)
    cos = jnp.tile(cos, (1, LANES // HEAD_DIM))
    sin = jnp.tile(sin, (1, LANES // HEAD_DIM))
    return jnp.concatenate([cos, cos * Q_SCALE], axis=1), jnp.concatenate([sin, sin * Q_SCALE], axis=1)


def _store_with_ones(dst_ref, v):
    dst_ref[:, 0:LANES] = v
    dst_ref[:, LANES:2 * LANES] = jnp.ones(v.shape, v.dtype)


def _head_masks(rows):
    lane = lax.broadcasted_iota(jnp.int32, (rows, LANES), 1)
    lo = lane < HEAD_DIM
    return lo, jnp.logical_not(lo)


def _na_group_geometry(g, grid_rows):
    u0 = min(max(g * NA_GROUP - NA_WIN_R // 2, 0), grid_rows - NA_UNION)
    idx = []
    for a in range(NA_GROUP):
        r = g * NA_GROUP + a
        r0 = min(max(r - NA_WIN_R // 2, 0), grid_rows - NA_WIN_R)
        idx.append([u0 + y - r + NA_WIN_R - 1 if r0 <= u0 + y < r0 + NA_WIN_R else 2 * NA_WIN_R - 1
                    for y in range(NA_UNION)])
    return u0, idx


def _na_kernel(q_ref, k_ref, v_ref, bt_ref, o_ref, fb_ref, v1_ref, nb_a, cx_a, nb_b, cx_b, *, ctx_len, grid_rows, need_ctx):
    n_groups = grid_rows // NA_GROUP

    @pl.when(pl.program_id(1) == 0)
    def _():
        for j in range(2):
            for pattern, g in enumerate((0, 1, n_groups - 1)):
                _, idx = _na_group_geometry(g, grid_rows)
                for a in range(NA_GROUP):
                    for p in range(NA_UNION // 2):
                        fb_ref[j, pattern, a * GRID_W:(a + 1) * GRID_W, 2 * p * GRID_W:(2 * p + 2) * GRID_W] = (
                            jnp.concatenate([bt_ref[j, idx[a][2 * p]], bt_ref[j, idx[a][2 * p + 1]]], axis=-1))

    _store_with_ones(v1_ref, v_ref[...])
    kc = k_ref[0:ctx_len, :]
    vc = v1_ref[0:ctx_len, :]

    def normalised(r):
        return r[:, 0:LANES] / r[:, LANES:2 * LANES]

    if need_ctx:
        qc = q_ref[0:ctx_len, :]
        lo, hi = _head_masks(ctx_len)
        outs = []
        for m in (lo, hi):
            s = _dot_nt(jnp.where(m, qc, 0), kc)
            e = jnp.exp2(s - jnp.max(s, axis=-1, keepdims=True))
            outs.append(normalised(_dot(e.astype(BF16), vc)))
        o_ref[0:ctx_len, :] = jnp.where(lo, outs[0], outs[1]).astype(o_ref.dtype)
    else:
        o_ref[0:ctx_len, :] = jnp.zeros((ctx_len, LANES), o_ref.dtype)

    gq = NA_GROUP * GRID_W
    uk = NA_UNION * GRID_W
    lo, hi = _head_masks(gq)

    def rows_of(g):
        u0 = jnp.clip(g * NA_GROUP - NA_WIN_R // 2, 0, grid_rows - NA_UNION)
        return pl.multiple_of(ctx_len + g * gq, gq), pl.multiple_of(ctx_len + u0 * GRID_W, GRID_W)

    def scores(g, nb_ref, cx_ref):
        q0, w0 = rows_of(g)
        pattern = jnp.where(g == 0, 0, jnp.where(g == n_groups - 1, 2, 1))
        q_g = q_ref[pl.ds(q0, gq), :]
        k_u = k_ref[pl.ds(w0, uk), :]
        for j, m in enumerate((lo, hi)):
            qj = jnp.where(m, q_g, 0)
            nb_ref[j] = _dot_nt(qj, k_u) + fb_ref[j, pattern]
            cx_ref[j] = _dot_nt(qj, kc)

    def finish(g, nb_ref, cx_ref):
        q0, w0 = rows_of(g)
        v_u = v1_ref[pl.ds(w0, uk), :]
        outs = []
        for j in range(2):
            s_nb = nb_ref[j]
            s_cx = cx_ref[j]
            mx = jnp.maximum(jnp.max(s_nb, axis=-1, keepdims=True), jnp.max(s_cx, axis=-1, keepdims=True))
            e_nb = jnp.exp2(s_nb - mx).astype(BF16)
            e_cx = jnp.exp2(s_cx - mx).astype(BF16)
            outs.append(normalised(_dot(e_cx, vc) + _dot(e_nb, v_u)))
        o_ref[pl.ds(q0, gq), :] = jnp.where(lo, outs[0], outs[1]).astype(o_ref.dtype)

    scores(0, nb_a, cx_a)

    def pair_body(t, carry):
        g = 2 * t
        scores(g + 1, nb_b, cx_b)
        finish(g, nb_a, cx_a)
        scores(g + 2, nb_a, cx_a)
        finish(g + 1, nb_b, cx_b)
        return carry

    lax.fori_loop(0, n_groups // 2 - 1, pair_body, 0)
    scores(n_groups - 1, nb_b, cx_b)
    finish(n_groups - 2, nb_a, cx_a)
    finish(n_groups - 1, nb_b, cx_b)


def _na_bias_table(rpb, grid_rows):
    n_groups = grid_rows // NA_GROUP
    assert grid_rows % (2 * NA_GROUP) == 0 and n_groups >= 3 and grid_rows >= NA_UNION
    for g in range(2, n_groups - 1):
        u1, idx1 = _na_group_geometry(1, grid_rows)
        ug, idxg = _na_group_geometry(g, grid_rows)
        assert idxg == idx1 and ug - g * NA_GROUP == u1 - NA_GROUP
    qcol = jnp.arange(GRID_W)
    kcol = jnp.arange(GRID_W)
    c0 = jnp.clip(qcol - NA_WIN_C // 2, 0, GRID_W - NA_WIN_C)
    col_ok = (kcol[None, :] >= c0[:, None]) & (kcol[None, :] < c0[:, None] + NA_WIN_C)
    rel = kcol[None, :] - qcol[:, None] + NA_WIN_C - 1
    onehot = ((rel[:, :, None] == jnp.arange(2 * NA_WIN_C - 1)[None, None, :]) & col_ok[:, :, None]).astype(F32)
    bt = jnp.einsum('hyd,qkd->hyqk', rpb.astype(F32), onehot, precision=lax.Precision.HIGHEST)
    bt = jnp.where(col_ok[None, None], bt * LOG2E, NEG_INF)
    return jnp.pad(bt, ((0, 0), (0, 1), (0, 0), (0, 0)), constant_values=NEG_INF)


def _na_attention(qkv, rpb, ctx_len, need_ctx):
    b, r, _ = qkv.shape
    grid_rows = (r - ctx_len) // GRID_W
    nblk = D_MODEL // LANES
    bt = _na_bias_table(rpb, grid_rows)
    kern = functools.partial(_na_kernel, ctx_len=ctx_len, grid_rows=grid_rows, need_ctx=need_ctx)
    return pl.pallas_call(
        kern,
        grid=(nblk, b),
        in_specs=[
            pl.BlockSpec((None, r, LANES), lambda hp, bi: (bi, 0, hp)),
            pl.BlockSpec((None, r, LANES), lambda hp, bi: (bi, 0, nblk + hp)),
            pl.BlockSpec((None, r, LANES), lambda hp, bi: (bi, 0, 2 * nblk + hp)),
            pl.BlockSpec((2, 2 * NA_WIN_R, GRID_W, GRID_W), lambda hp, bi: (hp, 0, 0, 0)),
        ],
        out_specs=pl.BlockSpec((None, r, LANES), lambda hp, bi: (bi, 0, hp)),
        out_shape=jax.ShapeDtypeStruct((b, r, D_MODEL), BF16),
        scratch_shapes=[pltpu.VMEM((2, 3, NA_GROUP * GRID_W, NA_UNION * GRID_W), F32), pltpu.VMEM((r, 2 * LANES), BF16)]
        + [pltpu.VMEM((2, NA_GROUP * GRID_W, NA_UNION * GRID_W), F32),
           pltpu.VMEM((2, NA_GROUP * GRID_W, ctx_len), F32)] * 2,
        compiler_params=_params(("arbitrary", "arbitrary")),
        name="na_attn",
    )(qkv, qkv, qkv, bt)


def _da_kernel(q_ref, k_ref, v_ref, lam_ref, g_ref, o_ref, v1_ref, s_a, s_b, *, ctx_len, lam_init):
    lp = lam_ref[...]
    lam = (jnp.exp(jnp.sum(lp[0:1] * lp[1:2], axis=-1, keepdims=True))
           - jnp.exp(jnp.sum(lp[2:3] * lp[3:4], axis=-1, keepdims=True)) + lam_init)
    tq = ROW_TILE
    n_tiles = (q_ref.shape[0] - ctx_len) // tq
    lo, hi = _head_masks(tq)

    _store_with_ones(v1_ref, v_ref[...])

    def finish(scores, n_keys, row0, rows):
        ps = []
        for s in scores:
            e = jnp.exp2(s - jnp.max(s, axis=-1, keepdims=True))
            r = _dot(e.astype(BF16), v1_ref[0:n_keys, :])
            ps.append(r[:, 0:LANES] / r[:, LANES:2 * LANES])
        o = ps[0] - lam * ps[1]
        o_ref[pl.ds(row0, rows), :] = (_rms(o) * g_ref[...] * (1.0 - lam_init)).astype(o_ref.dtype)

    qc = q_ref[0:ctx_len, :]
    lo_c, hi_c = _head_masks(ctx_len)
    finish([_dot_nt(jnp.where(m, qc, 0), k_ref[0:ctx_len, :]) for m in (lo_c, hi_c)], ctx_len, 0, ctx_len)

    def row_of(t):
        return pl.multiple_of(ctx_len + t * tq, tq)

    def scores(t, s_ref):
        q = q_ref[pl.ds(row_of(t), tq), :]
        for j, m in enumerate((lo, hi)):
            s_ref[j] = _dot_nt(jnp.where(m, q, 0), k_ref[...])

    scores(0, s_a)

    def pair_body(p, carry):
        t = 2 * p
        scores(t + 1, s_b)
        finish([s_a[0], s_a[1]], k_ref.shape[0], row_of(t), tq)
        scores(t + 2, s_a)
        finish([s_b[0], s_b[1]], k_ref.shape[0], row_of(t + 1), tq)
        return carry

    lax.fori_loop(0, n_tiles // 2 - 1, pair_body, 0)
    scores(n_tiles - 1, s_b)
    finish([s_a[0], s_a[1]], k_ref.shape[0], row_of(n_tiles - 2), tq)
    finish([s_b[0], s_b[1]], k_ref.shape[0], row_of(n_tiles - 1), tq)


def _da_attention(qkv, lam_params, subln_g, layer_idx, ctx_len):
    b, r, _ = qkv.shape
    assert (r - ctx_len) % (2 * ROW_TILE) == 0
    nblk = D_MODEL // LANES
    lam_init = 0.8 - 0.6 * math.exp(-0.3 * layer_idx)
    kern = functools.partial(_da_kernel, ctx_len=ctx_len, lam_init=lam_init)
    return pl.pallas_call(
        kern,
        grid=(nblk, b),
        in_specs=[
            pl.BlockSpec((None, r, LANES), lambda h, bi: (bi, 0, h)),
            pl.BlockSpec((None, r, LANES), lambda h, bi: (bi, 0, nblk + h)),
            pl.BlockSpec((None, r, LANES), lambda h, bi: (bi, 0, 2 * nblk + h)),
            pl.BlockSpec((4, HEAD_DIM), lambda h, bi: (0, 0)),
            pl.BlockSpec((1, 2 * HEAD_DIM), lambda h, bi: (0, 0)),
        ],
        out_specs=pl.BlockSpec((None, r, LANES), lambda h, bi: (bi, 0, h)),
        out_shape=jax.ShapeDtypeStruct((b, r, D_MODEL), BF16),
        scratch_shapes=[pltpu.VMEM((r, 2 * LANES), BF16), pltpu.VMEM((2, ROW_TILE, r), F32), pltpu.VMEM((2, ROW_TILE, r), F32)],
        compiler_params=_params(("arbitrary", "arbitrary")),
        name="da_attn",
    )(qkv, qkv, qkv, lam_params.astype(F32), subln_g.astype(F32).reshape(1, -1))


def _sw_kernel(sink_ref, q_ref, k_ref, v_ref, o_ref, kd_ref, vd_ref, sl_a, sc_a, sl_b, sc_b, *, ctx_len, seq, need_ctx):
    kvp = pl.program_id(0)
    band = Q_BLOCK + 2 * SW_WINDOW
    hpk = SW_Q_HEADS // SW_KV_HEADS

    k = k_ref[...]
    v = v_ref[...]
    for hh in range(2):
        sl = slice(hh * HEAD_DIM, (hh + 1) * HEAD_DIM)
        kd_ref[hh] = jnp.concatenate([k[:, sl], k[:, sl]], axis=-1)
        _store_with_ones(vd_ref.at[hh], jnp.concatenate([v[:, sl], v[:, sl]], axis=-1))

    n_blk = 2 * (hpk // 2)

    def stacked_q(row0, rows, c):
        lo, hi = _head_masks(rows)
        qv = q_ref[pl.ds(row0, rows), c * LANES:(c + 1) * LANES]
        return jnp.concatenate([jnp.where(lo, qv, 0), jnp.where(hi, qv, 0)], axis=0)

    def per_head(rows, c, col, fn):
        sinks = [sink_ref[kvp * 2 * hpk + 2 * c + j] * LOG2E for j in range(2)]
        return jnp.concatenate([fn(col[j * rows:(j + 1) * rows], sinks[j]) for j in range(2)], axis=0)

    def write_out(row0, rows, c, o):
        lo, _ = _head_masks(rows)
        o_ref[pl.ds(row0, rows), c * LANES:(c + 1) * LANES] = jnp.where(lo, o[0:rows], o[rows:2 * rows]).astype(o_ref.dtype)

    def softmax_pv(rows, c, parts):
        row_max = functools.reduce(jnp.maximum, [jnp.max(s, axis=-1, keepdims=True) for s, _ in parts])
        mx = per_head(rows, c, row_max, lambda m, sink: jnp.maximum(m, sink))
        r = sum(_dot(jnp.exp2(s - mx).astype(BF16), val) for s, val in parts)
        den = r[:, LANES:2 * LANES] + per_head(rows, c, mx, lambda m, sink: jnp.exp2(sink - m))
        return r[:, 0:LANES] / den

    if need_ctx:
        for c in range(n_blk):
            hh = c // (hpk // 2)
            s = _dot_nt(stacked_q(0, ctx_len, c), kd_ref[hh, 0:ctx_len, :])
            write_out(0, ctx_len, c, softmax_pv(ctx_len, c, [(s, vd_ref[hh, 0:ctx_len, :])]))
    else:
        o_ref[0:ctx_len, :] = jnp.zeros((ctx_len, o_ref.shape[-1]), o_ref.dtype)

    n_steps = seq // Q_BLOCK

    def rows_of(n):
        start = n * Q_BLOCK
        b0 = jnp.clip(start - SW_WINDOW, 0, seq - band)
        return start, b0, pl.multiple_of(ctx_len + start, Q_BLOCK), pl.multiple_of(ctx_len + b0, Q_BLOCK)

    def scores(n, sl_ref, sc_ref):
        start, b0, qrow, brow = rows_of(n)
        qpos = start + (lax.broadcasted_iota(jnp.int32, (2 * Q_BLOCK, band), 0) & (Q_BLOCK - 1))
        kpos = b0 + lax.broadcasted_iota(jnp.int32, (2 * Q_BLOCK, band), 1)
        in_band = jnp.abs(kpos - qpos) <= SW_WINDOW
        for c in range(n_blk):
            hh = c // (hpk // 2)
            qs = stacked_q(qrow, Q_BLOCK, c)
            sl_ref[c] = jnp.where(in_band, _dot_nt(qs, kd_ref[hh, pl.ds(brow, band), :]), NEG_INF)
            sc_ref[c] = _dot_nt(qs, kd_ref[hh, 0:ctx_len, :])

    def finish(n, sl_ref, sc_ref):
        _, _, qrow, brow = rows_of(n)
        for c in range(n_blk):
            hh = c // (hpk // 2)
            o = softmax_pv(Q_BLOCK, c, [(sc_ref[c], vd_ref[hh, 0:ctx_len, :]),
                                        (sl_ref[c], vd_ref[hh, pl.ds(brow, band), :])])
            write_out(qrow, Q_BLOCK, c, o)

    scores(0, sl_a, sc_a)

    def pair_body(t, carry):
        n = 2 * t
        scores(n + 1, sl_b, sc_b)
        finish(n, sl_a, sc_a)
        scores(n + 2, sl_a, sc_a)
        finish(n + 1, sl_b, sc_b)
        return carry

    lax.fori_loop(0, n_steps // 2 - 1, pair_body, 0)
    scores(n_steps - 1, sl_b, sc_b)
    finish(n_steps - 2, sl_a, sc_a)
    finish(n_steps - 1, sl_b, sc_b)


def _sw_attention(qkv, sinks, ctx_len, need_ctx):
    b, r, _ = qkv.shape
    seq = r - ctx_len
    assert seq % (2 * Q_BLOCK) == 0
    nq_blk = SW_Q_HEADS * HEAD_DIM // LANES
    slab = 2 * (SW_Q_HEADS // SW_KV_HEADS) * HEAD_DIM
    kern = functools.partial(_sw_kernel, ctx_len=ctx_len, seq=seq, need_ctx=need_ctx)
    n_kv_blk = SW_KV_HEADS * HEAD_DIM // LANES
    return pl.pallas_call(
        kern,
        grid=(n_kv_blk, b),
        in_specs=[
            pl.BlockSpec(memory_space=pltpu.SMEM),
            pl.BlockSpec((None, r, slab), lambda kp, bi: (bi, 0, kp)),
            pl.BlockSpec((None, r, LANES), lambda kp, bi: (bi, 0, nq_blk + kp)),
            pl.BlockSpec((None, r, LANES), lambda kp, bi: (bi, 0, nq_blk + n_kv_blk + kp)),
        ],
        out_specs=pl.BlockSpec((None, r, slab), lambda kp, bi: (bi, 0, kp)),
        out_shape=jax.ShapeDtypeStruct((b, r, D_MODEL), BF16),
        scratch_shapes=[pltpu.VMEM((2, r, LANES), BF16), pltpu.VMEM((2, r, 2 * LANES), BF16)]
        + [pltpu.VMEM((slab // LANES, 2 * Q_BLOCK, Q_BLOCK + 2 * SW_WINDOW), F32),
           pltpu.VMEM((slab // LANES, 2 * Q_BLOCK, ctx_len), F32)] * 2,
        compiler_params=_params(("arbitrary", "arbitrary")),
        name="sw_attn",
    )(sinks.astype(F32), qkv, qkv, qkv)


POST_SUBTILES = 4


def _post_kernel(o_ref, x_ref, wo_ref, mod_ref, g_ref, wr_ref, br_ref, xn_ref, h_ref, rt_ref, *, ctx_tiles, tiles_per_batch):
    for sub in range(POST_SUBTILES):
        tile = pl.program_id(0) * POST_SUBTILES + sub
        row = _mod_row(tile // tiles_per_batch, tile % tiles_per_batch, ctx_tiles)
        rows = slice(sub * ROW_TILE, (sub + 1) * ROW_TILE)
        _post_tile(o_ref.at[rows], x_ref.at[rows], wo_ref, mod_ref, g_ref, wr_ref, br_ref, xn_ref.at[rows],
                   h_ref.at[sub * ROW_TILE * SUBLANES:(sub + 1) * ROW_TILE * SUBLANES], rt_ref.at[rows], row)


def _post_tile(o_ref, x_ref, wo_ref, mod_ref, g_ref, wr_ref, br_ref, xn_ref, h_ref, rt_ref, row):
    d = x_ref.shape[-1]
    tm = x_ref.shape[0]
    gate1 = mod_ref[pl.ds(row, 1), 2 * d:3 * d]
    shift2 = mod_ref[pl.ds(row, 1), 3 * d:4 * d]
    scale2 = mod_ref[pl.ds(row, 1), 4 * d:5 * d]
    xn = x_ref[...] + gate1 * _dot(o_ref[...], wo_ref[...])
    xn_ref[...] = xn
    h = (_rms(xn) * g_ref[...]) * (1.0 + scale2) + shift2
    for j in range(SUBLANES):
        h_ref[pl.ds(j, tm, stride=SUBLANES), :] = h[:, j * LANES:(j + 1) * LANES]

    h_hi = h.astype(BF16)
    h_lo = (h - h_hi.astype(F32)).astype(BF16)
    hi_terms = _dot(h_hi, wr_ref[...])
    logits = (hi_terms[:, 0:LANES] + hi_terms[:, LANES:2 * LANES] + _dot(h_lo, wr_ref[:, 0:LANES])) + br_ref[...]

    lt = logits.T
    sub = lax.broadcasted_iota(jnp.int32, (SUBLANES, tm), 0).astype(F32)

    def first_argmax(vals, mx):
        return jnp.min(jnp.where(vals == mx, sub, float(SUBLANES)), axis=0, keepdims=True)

    gl = jnp.where(sub < N_GROUPS, lt[0:SUBLANES, :], -jnp.inf)
    ge = jnp.exp(gl - jnp.max(gl, axis=0, keepdims=True))
    gp = ge / jnp.sum(ge, axis=0, keepdims=True)
    g_p = jnp.max(gp, axis=0, keepdims=True)
    g_idx = first_argmax(gp, g_p)
    el = lt[SUBLANES:2 * SUBLANES, :]
    for g in range(1, N_GROUPS):
        el = jnp.where(g_idx == g, lt[(g + 1) * SUBLANES:(g + 2) * SUBLANES, :], el)
    v1 = jnp.max(el, axis=0, keepdims=True)
    i1 = first_argmax(el, v1)
    el2 = jnp.where(sub == i1, -jnp.inf, el)
    v2 = jnp.max(el2, axis=0, keepdims=True)
    i2 = first_argmax(el2, v2)
    e2 = jnp.exp(v2 - v1)
    w1 = g_p * (1.0 / (1.0 + e2))
    w2 = g_p * (e2 / (1.0 + e2))
    base = g_idx * EXPERTS_PER_GROUP
    rt_t = jnp.concatenate([base + i1, base + i2, w1, w2, jnp.zeros((LANES - 4, tm), F32)], axis=0)
    rt_ref[...] = rt_t.T[:, 0:SUBLANES]


def _post(o, x, wo, mod, g, wr, br, ctx_len):
    wr_hi = wr.astype(BF16)
    wr_cat = jnp.concatenate([wr_hi, (wr - wr_hi.astype(F32)).astype(BF16)], axis=1)
    b, r, d = x.shape
    t = b * r
    step_rows = POST_SUBTILES * ROW_TILE
    assert t % step_rows == 0
    kern = functools.partial(_post_kernel, ctx_tiles=ctx_len // ROW_TILE, tiles_per_batch=r // ROW_TILE)
    const = lambda i: (0, 0)
    rows = lambda i: (i, 0)
    xn, h_rows, rt = pl.pallas_call(
        kern,
        grid=(t // step_rows,),
        in_specs=[
            pl.BlockSpec((step_rows, d), rows),
            pl.BlockSpec((step_rows, d), rows),
            pl.BlockSpec((d, d), const),
            pl.BlockSpec((MOD_ROWS, 6 * d), const),
            pl.BlockSpec((1, d), const),
            pl.BlockSpec((d, 2 * LANES), const),
            pl.BlockSpec((1, LANES), const),
        ],
        out_specs=[
            pl.BlockSpec((step_rows, d), rows),
            pl.BlockSpec((step_rows * SUBLANES, LANES), rows),
            pl.BlockSpec((step_rows, SUBLANES), rows),
        ],
        out_shape=[
            jax.ShapeDtypeStruct((t, d), F32),
            jax.ShapeDtypeStruct((t * SUBLANES, LANES), F32),
            jax.ShapeDtypeStruct((t, SUBLANES), F32),
        ],
        compiler_params=_params(("arbitrary",)),
        name="post_attn",
    )(o.reshape(t, d), x.reshape(t, d), wo, mod, g, wr_cat, br)
    return xn.reshape(b, r, d), h_rows, rt


DISPATCH_TOKENS = 512


def _row(ref, idx):
    return ref.at[pl.ds(pl.multiple_of(idx * SUBLANES, SUBLANES), SUBLANES), :]


def _dispatch_kernel(dest_ref, h_ref, xs_hbm, sem):
    def issue(t, carry):
        src = _row(h_ref, t)
        for k in range(2):
            pltpu.make_async_copy(src, _row(xs_hbm, dest_ref[0, 0, 2 * t + k]), sem).start(priority=k)
        return carry

    lax.fori_loop(0, DISPATCH_TOKENS, issue, 0)
    for _ in range(2):
        pltpu.make_async_copy(h_ref, xs_hbm.at[pl.ds(0, DISPATCH_TOKENS * SUBLANES), :], sem).wait()


def _dispatch(h_rows, dest):
    t = dest.shape[0] // 2
    steps = t // DISPATCH_TOKENS
    return pl.pallas_call(
        _dispatch_kernel,
        grid=(steps,),
        in_specs=[
            pl.BlockSpec((1, 1, 2 * DISPATCH_TOKENS), lambda i: (i, 0, 0), memory_space=pltpu.SMEM),
            pl.BlockSpec((DISPATCH_TOKENS * SUBLANES, LANES), lambda i: (i, 0)),
        ],
        out_specs=pl.BlockSpec(memory_space=pl.ANY),
        out_shape=jax.ShapeDtypeStruct((2 * t * SUBLANES, LANES), F32),
        scratch_shapes=[pltpu.SemaphoreType.DMA(())],
        compiler_params=_params(("arbitrary",)),
        name="moe_dispatch",
    )(dest.reshape(steps, 1, 2 * DISPATCH_TOKENS), h_rows)


def _experts_kernel(ve_ref, vt_ref, lo_ref, hi_ref, first_ref, valid_ref, change_ref, next_ref, slot_ref,
                    xs_ref, wg_hbm, wu_hbm, wd_hbm, ys_ref, wgu_s, wd_s, acc_s, stg_g, stg_u, stg_d, sem, *, layer):
    v = pl.program_id(0)
    tm = ROW_TILE
    f = stg_g.shape[-1]
    streams = ((wg_hbm, stg_g), (wu_hbm, stg_u), (wd_hbm, stg_d))

    def fetch(expert, slot):
        for src, dst in streams:
            pltpu.make_async_copy(src.at[layer, expert], dst.at[slot], sem.at[slot]).start()

    @pl.when(v == 0)
    def _():
        acc_s[...] = jnp.zeros(acc_s.shape, acc_s.dtype)
        fetch(ve_ref[0], 0)

    @pl.when(valid_ref[v] == 1)
    def _():
        @pl.when(change_ref[v] == 1)
        def _():
            slot = slot_ref[v]
            for src, dst in streams:
                pltpu.make_async_copy(src.at[layer, 0], dst.at[slot], sem.at[slot]).wait()
            wgu_s[:, 0:f] = stg_g[slot].astype(BF16)
            wgu_s[:, f:2 * f] = stg_u[slot].astype(BF16)
            wd_s[...] = stg_d[slot].astype(BF16)

            @pl.when(next_ref[v] >= 0)
            def _():
                fetch(next_ref[v], 1 - slot)

        x = jnp.concatenate([xs_ref[pl.ds(j, tm, stride=SUBLANES), :] for j in range(SUBLANES)], axis=-1)
        au = _dot(x.astype(BF16), wgu_s[...])
        rows = vt_ref[v] * tm + lax.broadcasted_iota(jnp.int32, (tm, 1), 0)
        mine = (rows >= lo_ref[v]) & (rows < hi_ref[v])
        mid = jnp.where(mine, jax.nn.silu(au[:, 0:f]) * au[:, f:2 * f], 0.0)
        y = _dot(mid.astype(BF16), wd_s[...])
        y = jnp.where(first_ref[v] == 1, y, acc_s[...] + y)
        acc_s[...] = y
        for j in range(SUBLANES):
            ys_ref[pl.ds(j, tm, stride=SUBLANES), :] = y[:, j * LANES:(j + 1) * LANES]


def _experts(xs, sched, w_gate, w_up, w_down, layer):
    rows = xs.shape[0] // SUBLANES
    d, f = w_gate.shape[-2:]
    n_visits = sched[0].shape[0]
    blk = (ROW_TILE * SUBLANES, LANES)
    tile_map = lambda v, ve, vt, *_: (vt[v], 0)
    grid_spec = pltpu.PrefetchScalarGridSpec(
        num_scalar_prefetch=len(sched),
        grid=(n_visits,),
        in_specs=[pl.BlockSpec(blk, tile_map)] + [pl.BlockSpec(memory_space=pl.ANY)] * 3,
        out_specs=pl.BlockSpec(blk, tile_map),
        scratch_shapes=[pltpu.VMEM((d, 2 * f), BF16), pltpu.VMEM((f, d), BF16), pltpu.VMEM((ROW_TILE, d), F32),
                        pltpu.VMEM((2, d, f), F32), pltpu.VMEM((2, d, f), F32), pltpu.VMEM((2, f, d), F32),
                        pltpu.SemaphoreType.DMA((2,))],
    )
    return pl.pallas_call(
        functools.partial(_experts_kernel, layer=layer),
        grid_spec=grid_spec,
        out_shape=jax.ShapeDtypeStruct((rows * SUBLANES, LANES), F32),
        compiler_params=_params(("arbitrary",)),
        name="moe_experts",
    )(*sched, xs, w_gate, w_up, w_down)


COMBINE_UNROLL = 8


def _combine_kernel(dest_ref, next_ref, rt_ref, x_ref, mod_ref, g_ref, *rest, ctx_tiles, tiles_per_batch, final, next_qkv):
    if final:
        ys_hbm, o_ref, buf, sem = rest
    else:
        qg_ref, qmod_ref, qw_ref, cos_ref, sin_ref, ys_hbm, o_ref, qkv_ref, buf, sem = rest
    tm = ROW_TILE
    d = x_ref.shape[-1]
    s = pl.program_id(0)
    slot = s % 2
    half_rows = tm * SUBLANES

    def gather(idx_ref, slot_):
        def issue(t4, carry):
            for u in range(COMBINE_UNROLL):
                t = t4 * COMBINE_UNROLL + u
                for k in range(2):
                    pltpu.make_async_copy(_row(ys_hbm, idx_ref[0, 0, 2 * t + k]),
                                          _row(buf.at[slot_], k * tm + t), sem.at[slot_]).start(priority=k)
            return carry

        lax.fori_loop(0, tm // COMBINE_UNROLL, issue, 0)

    @pl.when(s == 0)
    def _():
        gather(dest_ref, 0)

    for parity in range(2):
        @pl.when(jnp.logical_and(s + 1 < pl.num_programs(0), slot == parity))
        def _():
            gather(next_ref, 1 - parity)

    for k in range(2):
        part = pl.ds(k * half_rows, half_rows)
        pltpu.make_async_copy(ys_hbm.at[part, :], buf.at[slot].at[part, :], sem.at[slot]).wait()

    row = _mod_row(s // tiles_per_batch, s % tiles_per_batch, ctx_tiles)
    gate2 = mod_ref[pl.ds(row, 1), 5 * d:6 * d]
    rt = rt_ref[...]
    w = (rt[:, 2:3], rt[:, 3:4])
    rows = buf.at[slot]
    pieces = []
    for j in range(SUBLANES):
        y = (w[0] * rows[pl.ds(j, tm, stride=SUBLANES), :]
             + w[1] * rows[pl.ds(half_rows + j, tm, stride=SUBLANES), :])
        sl = slice(j * LANES, (j + 1) * LANES)
        pieces.append(x_ref[:, sl] + gate2[:, sl] * y)
    xn = jnp.concatenate(pieces, axis=-1)
    if final:
        o_ref[...] = _rms(xn) * g_ref[...]
    else:
        o_ref[...] = xn
        _qkv_tile(xn, row, qg_ref, qmod_ref, qw_ref, cos_ref, sin_ref, qkv_ref, *next_qkv)


def _combine(ys, dest, rt, x, mod, ctx_len, final_g=None, next_layer=None):
    b, r, d = x.shape
    t = b * r
    steps = t // ROW_TILE
    ct, tpb = ctx_len // ROW_TILE, r // ROW_TILE
    final = final_g is not None
    const = lambda i: (0, 0)
    if final:
        out_rows = t - b * ctx_len
        out_specs = pl.BlockSpec((ROW_TILE, d), lambda i: ((i // tpb) * (tpb - ct) + jnp.maximum(i % tpb - ct, 0), 0))
        out_shape = jax.ShapeDtypeStruct((out_rows, d), F32)
        extra_in, extra_specs, next_qkv = [], [], None
    else:
        out_rows = t
        final_g = jnp.zeros((1, d), F32)
        qg, qmod, qw, cos, sin, rope_cols, q_cols = next_layer
        n = qw.shape[1]
        next_qkv = (rope_cols, q_cols)
        extra_in = [qg, qmod, qw, cos, sin]
        extra_specs = [pl.BlockSpec((1, d), const), pl.BlockSpec((MOD_ROWS, 6 * d), const), pl.BlockSpec((d, n), const),
                       pl.BlockSpec((ROW_TILE, 2 * LANES), lambda i: (i % tpb, 0)),
                       pl.BlockSpec((ROW_TILE, 2 * LANES), lambda i: (i % tpb, 0))]
        out_specs = [pl.BlockSpec((ROW_TILE, d), lambda i: (i, 0)), pl.BlockSpec((ROW_TILE, n), lambda i: (i, 0))]
        out_shape = [jax.ShapeDtypeStruct((t, d), F32), jax.ShapeDtypeStruct((t, n), BF16)]
    kern = functools.partial(_combine_kernel, ctx_tiles=ct, tiles_per_batch=tpb, final=final, next_qkv=next_qkv)
    dest3 = dest.reshape(steps, 1, 2 * ROW_TILE)
    out = pl.pallas_call(
        kern,
        grid=(steps,),
        in_specs=[
            pl.BlockSpec((1, 1, 2 * ROW_TILE), lambda i: (i, 0, 0), memory_space=pltpu.SMEM),
            pl.BlockSpec((1, 1, 2 * ROW_TILE), lambda i: (jnp.minimum(i + 1, steps - 1), 0, 0), memory_space=pltpu.SMEM),
            pl.BlockSpec((ROW_TILE, SUBLANES), lambda i: (i, 0)),
            pl.BlockSpec((ROW_TILE, d), lambda i: (i, 0)),
            pl.BlockSpec((MOD_ROWS, 6 * d), const),
            pl.BlockSpec((1, d), const),
        ] + extra_specs + [pl.BlockSpec(memory_space=pl.ANY)],
        out_specs=out_specs,
        out_shape=out_shape,
        scratch_shapes=[pltpu.VMEM((2, 2 * ROW_TILE * SUBLANES, LANES), F32), pltpu.SemaphoreType.DMA((2,))],
        compiler_params=_params(("arbitrary",)),
        name="moe_combine",
    )(dest3, dest3, rt, x.reshape(t, d), mod, final_g, *extra_in, ys)
    if final:
        return out.reshape(b, out_rows // b, d), None
    return out[0].reshape(b, r, d), out[1].reshape(b, r, -1)


def _route_plan(rt, n_tiles):
    eid = rt[:, 0:2].astype(jnp.int32).reshape(-1)
    onehot = (eid[:, None] == jnp.arange(N_EXPERTS, dtype=jnp.int32)[None, :]).astype(jnp.int32)
    csum = jnp.cumsum(onehot, axis=0)
    counts = csum[-1]
    cstart = jnp.cumsum(counts) - counts
    dest = jnp.sum(onehot * (cstart[None, :] + csum - 1), axis=1).astype(jnp.int32)

    tm = ROW_TILE
    first_tile = cstart // tm
    last_tile = (cstart + counts - 1) // tm
    nvis = jnp.where(counts > 0, last_tile - first_tile + 1, 0)
    vend = jnp.cumsum(nvis)
    vstart = vend - nvis
    total = vend[-1]
    n_visits = n_tiles + N_EXPERTS - 1
    v = jnp.arange(n_visits, dtype=jnp.int32)
    valid = v < total
    vc = jnp.minimum(v, total - 1)
    ve = jnp.sum((vc[:, None] >= vend[None, :]).astype(jnp.int32), axis=1)
    oh_e = (ve[:, None] == jnp.arange(N_EXPERTS, dtype=jnp.int32)[None, :]).astype(jnp.int32)
    pick = lambda a: jnp.sum(oh_e * a[None, :], axis=1)
    vt = pick(first_tile) + vc - pick(vstart)
    lo = pick(cstart)
    hi = jnp.where(valid, lo + pick(counts), lo)
    first = jnp.concatenate([jnp.ones((1,), jnp.int32), (vt[1:] != vt[:-1]).astype(jnp.int32)])
    change = jnp.concatenate([jnp.ones((1,), jnp.int32), (ve[1:] != ve[:-1]).astype(jnp.int32)]) * valid
    e_ids = jnp.arange(N_EXPERTS, dtype=jnp.int32)
    used = jnp.where(counts > 0, e_ids, N_EXPERTS)
    following = jnp.concatenate([lax.cummin(used, reverse=True)[1:], jnp.full((1,), N_EXPERTS, jnp.int32)])
    nxt = pick(jnp.where(following < N_EXPERTS, following, -1))
    slot = (jnp.cumsum(change) - 1) % 2
    sched = tuple(a.astype(jnp.int32) for a in (ve, vt, lo, hi, first, valid, change, nxt, slot))
    return dest, sched


def kernel(x, c, ctx, c_ctx, ada_w, ada_b, norm_mix_g, norm_ffn_g, final_norm_g, na_w_qkv, na_rpb, na_w_o, da_w_qkv, da_lambda, da_subln_g, da_w_o, sw_w_qkv, sw_sinks, sw_w_o, moe_w_grp, moe_b_grp, moe_w_exp, moe_b_exp, moe_w_gate, moe_w_up, moe_w_down):
    b, seq, d = x.shape
    ctx_len = ctx.shape[1]
    depth = ada_w.shape[0]
    assert d == D_MODEL and ctx_len % ROW_TILE == 0 and seq % ROW_TILE == 0 and b <= MOD_ROWS // 2
    assert EXPERTS_PER_GROUP == SUBLANES and N_GROUPS <= SUBLANES
    assert (b * (ctx_len + seq)) % DISPATCH_TOKENS == 0

    half = MOD_ROWS // 2
    cond = jnp.concatenate([jnp.pad(c, ((0, half - b), (0, 0))), jnp.pad(c_ctx[None, :], ((0, half - 1), (0, 0)))], axis=0)
    mod_all = _adaln(cond, ada_w, ada_b)
    cos, sin = _rope_tables(seq, ctx_len)
    stream = jnp.concatenate([ctx, x], axis=1)
    n_tiles = 2 * b * (ctx_len + seq) // ROW_TILE

    def qkv_args(i):
        kind, j = i % N_MIXERS, i // N_MIXERS
        if kind == 0:
            w, rope_cols, q_cols = na_w_qkv[j], 0, NA_HEADS * HEAD_DIM
        elif kind == 1:
            nq = 2 * DA_HEADS * HEAD_DIM
            w, rope_cols, q_cols = da_w_qkv[j], 2 * nq, nq
        else:
            w, rope_cols, q_cols = sw_w_qkv[j], (SW_Q_HEADS + SW_KV_HEADS) * HEAD_DIM, SW_Q_HEADS * HEAD_DIM
        return norm_mix_g[i].reshape(1, d), mod_all[i], w.astype(BF16), cos, sin, rope_cols, q_cols

    qkv = _qkv(stream, *qkv_args(0), ctx_len)
    for i in range(depth):
        need_ctx = i < depth - 1
        kind, j = i % N_MIXERS, i // N_MIXERS
        mod = mod_all[i]
        if kind == 0:
            o = _na_attention(qkv, na_rpb[j], ctx_len, need_ctx)
            wo = na_w_o[j]
        elif kind == 1:
            o = _da_attention(qkv, da_lambda[j], da_subln_g[j], i, ctx_len)
            wo = da_w_o[j]
        else:
            o = _sw_attention(qkv, sw_sinks[j], ctx_len, need_ctx)
            wo = sw_w_o[j]

        gpad, pad = SUBLANES - N_GROUPS, LANES - SUBLANES - N_EXPERTS
        wr = jnp.concatenate([jnp.pad(moe_w_grp[i].astype(F32), ((0, 0), (0, gpad))),
                              jnp.pad(moe_w_exp[i].astype(F32), ((0, 0), (0, pad)))], axis=1)
        br = jnp.concatenate([jnp.pad(moe_b_grp[i].astype(F32), (0, gpad)),
                              jnp.pad(moe_b_exp[i].astype(F32), (0, pad))]).reshape(1, LANES)
        stream, h_rows, rt = _post(o, stream, wo.astype(BF16), mod, norm_ffn_g[i].reshape(1, d), wr, br, ctx_len)
        dest, sched = _route_plan(rt, n_tiles)
        xs = _dispatch(h_rows, dest)
        ys = _experts(xs, sched, moe_w_gate, moe_w_up, moe_w_down, i)
        if i == depth - 1:
            stream, qkv = _combine(ys, dest, rt, stream, mod, ctx_len, final_g=final_norm_g.reshape(1, d).astype(F32))
        else:
            stream, qkv = _combine(ys, dest, rt, stream, mod, ctx_len, next_layer=qkv_args(i + 1))

    return stream
```
